```python
import jax, jax.numpy as jnp
from jax import lax
import numpy as np

D_MODEL = 2048
BATCH = 8
SEQ = 8192
DEPTH = 2

D_MIX = D_MODEL
D_A = D_MIX // 4
D_B = D_MIX // 4
D_C = D_MIX // 4
D_D = D_MIX // 4
GDN_HEAD_DIM = 128
GDN_HEADS = D_A // GDN_HEAD_DIM
GDN_CONV = 4
GDN_CHUNK = 64
GDN_CHUNK_LOG2 = 6
LRU_BLOCKS = 8
LRU_BLOCK_DIM = D_B // LRU_BLOCKS
LRU_CONV = 4
LRU_C = 8.0
SGU_GROUPS = 4
SGU_GROUP_DIM = D_C // SGU_GROUPS
SGU_CHUNK = 128
SCONV_WIDTH = 3
D_FF = (D_MODEL * 11) // 4
FFN_CONV = 3
IN_SIZES = (D_A, D_A, D_A, D_A, GDN_HEADS, GDN_HEADS, D_B, D_B, 2 * D_C, D_D, D_D, D_D)
N_IN = 4 * D_A + 2 * GDN_HEADS + 2 * D_B + 2 * D_C + 3 * D_D
EPS = 1e-6

kernel_name = "hybrid_parallel_head_groups_gdn_rglru_sgu_shortconv"


def _split(t, sizes):
    idx, acc = [], 0
    for s in sizes[:-1]:
        acc += s
        idx.append(acc)
    return jnp.split(t, idx, axis=-1)


def rms_norm(x, w):
    xf = x.astype(jnp.float32)
    y = xf * lax.rsqrt(jnp.mean(xf * xf, axis=-1, keepdims=True) + EPS)
    return (y * w.astype(jnp.float32)).astype(x.dtype)


def layer_norm(x, w, b):
    xf = x.astype(jnp.float32)
    mu = jnp.mean(xf, axis=-1, keepdims=True)
    xc = xf - mu
    y = xc * lax.rsqrt(jnp.mean(xc * xc, axis=-1, keepdims=True) + EPS)
    return y * w.astype(jnp.float32) + b.astype(jnp.float32)


def causal_dwconv(x, w, b=None):
    K, C = w.shape
    y = lax.conv_general_dilated(x, w[:, None, :].astype(x.dtype), window_strides=(1,),
                                 padding=[(K - 1, 0)], dimension_numbers=('NWC', 'WIO', 'NWC'),
                                 feature_group_count=C)
    if b is not None:
        y = y + b.astype(x.dtype)
    return y


def l2norm(t):
    return t * lax.rsqrt(jnp.sum(t * t, axis=-1, keepdims=True) + EPS)


def chunk_gated_delta_rule(q, k, v, g, beta):
    Bn, S, H, Dk = q.shape
    Dv = v.shape[-1]
    nC = S // GDN_CHUNK
    C = GDN_CHUNK
    q = l2norm(q) * (Dk ** -0.5)
    k = l2norm(k)

    def chunks(t):
        return t.reshape(Bn, nC, C, H, -1).transpose(0, 3, 1, 2, 4)

    qc, kc, vc = chunks(q), chunks(k), chunks(v)
    gc = g.reshape(Bn, nC, C, H).transpose(0, 3, 1, 2)
    bc = beta.reshape(Bn, nC, C, H).transpose(0, 3, 1, 2)
    gcum = jnp.cumsum(gc, axis=-1)
    causal = jnp.tril(jnp.ones((C, C), dtype=bool))
    strict = jnp.tril(jnp.ones((C, C), dtype=bool), -1)
    diff = gcum[..., :, None] - gcum[..., None, :]
    decay = jnp.where(causal, jnp.exp(jnp.where(causal, diff, 0.0)), 0.0)
    kb = kc * bc[..., None]
    M = jnp.where(strict, jnp.einsum('bhncd,bhnsd->bhncs', kb, kc) * decay, 0.0)
    N = -M
    T = jnp.eye(C, dtype=jnp.float32) + N
    P = N
    for _ in range(GDN_CHUNK_LOG2 - 1):
        P = jnp.einsum('bhnij,bhnjk->bhnik', P, P)
        T = T + jnp.einsum('bhnij,bhnjk->bhnik', T, P)
    w = jnp.einsum('bhncs,bhnsd->bhncd', T, kb * jnp.exp(gcum)[..., None])
    u = jnp.einsum('bhncs,bhnsd->bhncd', T, vc * bc[..., None])
    attn = jnp.where(causal, jnp.einsum('bhncd,bhnsd->bhncs', qc, kc) * decay, 0.0)
    q_g = qc * jnp.exp(gcum)[..., None]
    k_g = kc * jnp.exp(gcum[..., -1:] - gcum)[..., None]
    g_last = jnp.exp(gcum[..., -1])

    def step(state, inp):
        q_i, a_i, u_i, w_i, k_i, gl_i = inp
        v_new = u_i - jnp.einsum('bhck,bhkv->bhcv', w_i, state)
        o_i = jnp.einsum('bhck,bhkv->bhcv', q_i, state) + jnp.einsum('bhcs,bhsv->bhcv', a_i, v_new)
        state = state * gl_i[..., None, None] + jnp.einsum('bhck,bhcv->bhkv', k_i, v_new)
        return state, o_i

    xs = tuple(jnp.moveaxis(t, 2, 0) for t in (q_g, attn, u, w, k_g, g_last))
    s0 = jnp.zeros((Bn, H, Dk, Dv), jnp.float32)
    _, o = lax.scan(step, s0, xs)
    return o.transpose(1, 0, 3, 2, 4).reshape(Bn, S, H, Dv)


def gdn_mixer(q, k, v, z, b, a, conv_w, a_log, dt_bias, norm_w):
    Bn, S, _ = q.shape
    qkv = jax.nn.silu(causal_dwconv(jnp.concatenate([q, k, v], axis=-1), conv_w))
    q, k, v = jnp.split(qkv, 3, axis=-1)

    def heads(t):
        return t.reshape(Bn, S, GDN_HEADS, GDN_HEAD_DIM).astype(jnp.float32)

    beta = jax.nn.sigmoid(b.astype(jnp.float32))
    g = -jnp.exp(a_log.astype(jnp.float32)) * jax.nn.softplus(a.astype(jnp.float32) + dt_bias.astype(jnp.float32))
    o = chunk_gated_delta_rule(heads(q), heads(k), heads(v), g, beta)
    o = o * lax.rsqrt(jnp.mean(o * o, axis=-1, keepdims=True) + EPS) * norm_w.astype(jnp.float32) * jax.nn.silu(heads(z))
    return o.reshape(Bn, S, D_A).astype(z.dtype)


def lru_scan(a, b):
    def combine(l, r):
        return (l[0] * r[0], r[0] * l[1] + r[1])
    _, h = lax.associative_scan(combine, (a, b), axis=1)
    return h


def rglru_mixer(xb, gate, conv_w, conv_b, wa, ba, wx, bx, lam):
    Bn, S, _ = xb.shape
    xc = causal_dwconv(xb, conv_w, conv_b).astype(jnp.float32)
    xblk = xc.reshape(Bn, S, LRU_BLOCKS, LRU_BLOCK_DIM)
    r = jax.nn.sigmoid(jnp.einsum('bshi,hij->bshj', xblk, wa.astype(jnp.float32)) + ba.astype(jnp.float32)).reshape(Bn, S, D_B)
    i = jax.nn.sigmoid(jnp.einsum('bshi,hij->bshj', xblk, wx.astype(jnp.float32)) + bx.astype(jnp.float32)).reshape(Bn, S, D_B)
    log_a = -LRU_C * r * jax.nn.softplus(-lam.astype(jnp.float32))
    a = jnp.exp(log_a)
    mult = jnp.sqrt(-jnp.expm1(2.0 * log_a))
    h = lru_scan(a, mult * (i * xc))
    return (h * jax.nn.gelu(gate.astype(jnp.float32))).astype(xb.dtype)


def sgu_mixer(uv, ln_w, ln_b, ws, bs):
    Bn, S, _ = uv.shape
    uvf = jax.nn.gelu(uv.astype(jnp.float32))
    u, v = jnp.split(uvf, 2, axis=-1)
    v = layer_norm(v, ln_w, ln_b)
    v = v.reshape(Bn, S // SGU_CHUNK, SGU_CHUNK, SGU_GROUPS, SGU_GROUP_DIM)
    mask = jnp.tril(jnp.ones((SGU_CHUNK, SGU_CHUNK), dtype=bool))
    wsm = jnp.where(mask, ws.astype(jnp.float32), 0.0)
    v = jnp.einsum('gts,bnsgd->bntgd', wsm, v) + bs.astype(jnp.float32).T[:, :, None]
    return (u * v.reshape(Bn, S, D_C)).astype(uv.dtype)


def short_conv_mixer(bg, cg, hh, conv_w):
    return bg * causal_dwconv(cg * hh, conv_w)


def conv_ffn(h, up, conv_w, conv_b, down):
    hid = causal_dwconv(h @ up.astype(h.dtype), conv_w, conv_b)
    gate, val = jnp.split(hid, 2, axis=-1)
    return (jax.nn.gelu(gate) * val) @ down.astype(h.dtype)


def _fwd_setup_inputs(seed: int = 0) -> dict:
    key = jax.random.key(seed)
    ks = iter(jax.random.split(key, 48))
    L = DEPTH
    f32 = jnp.float32

    def nrm(shape, scale):
        return jax.random.normal(next(ks), shape, f32) * scale

    def gain(shape):
        return 1.0 + 0.1 * jax.random.normal(next(ks), shape, f32)

    def unif(shape, lo, hi):
        return jax.random.uniform(next(ks), shape, f32, lo, hi)

    s = unif((L, D_B), 0.9, 0.999) ** (1.0 / LRU_C)
    lru_lambda = jnp.log(s) - jnp.log1p(-s)
    dt = jnp.exp(unif((L, GDN_HEADS), float(np.log(1e-3)), float(np.log(1e-1))))
    gdn_dt_bias = dt + jnp.log(-jnp.expm1(-dt))
    return {
        "x": nrm((BATCH, SEQ, D_MODEL), 1.0),
        "pre_mix_norm": gain((L, D_MODEL)),
        "w_in": nrm((L, D_MODEL, N_IN), D_MODEL ** -0.5),
        "gdn_conv_w": nrm((L, GDN_CONV, 3 * D_A), GDN_CONV ** -0.5),
        "gdn_a_log": jnp.log(unif((L, GDN_HEADS), 1.0, 16.0)),
        "gdn_dt_bias": gdn_dt_bias,
        "gdn_norm_w": gain((L, GDN_HEAD_DIM)),
        "lru_conv_w": nrm((L, LRU_CONV, D_B), LRU_CONV ** -0.5),
        "lru_conv_b": nrm((L, D_B), 0.01),
        "lru_wa": nrm((L, LRU_BLOCKS, LRU_BLOCK_DIM, LRU_BLOCK_DIM), LRU_BLOCK_DIM ** -0.5),
        "lru_ba": nrm((L, LRU_BLOCKS, LRU_BLOCK_DIM), 0.01),
        "lru_wx": nrm((L, LRU_BLOCKS, LRU_BLOCK_DIM, LRU_BLOCK_DIM), LRU_BLOCK_DIM ** -0.5),
        "lru_bx": nrm((L, LRU_BLOCKS, LRU_BLOCK_DIM), 0.01),
        "lru_lambda": lru_lambda,
        "sgu_ln_w": gain((L, D_C)),
        "sgu_ln_b": nrm((L, D_C), 0.01),
        "sgu_ws": nrm((L, SGU_GROUPS, SGU_CHUNK, SGU_CHUNK), SGU_CHUNK ** -0.5),
        "sgu_b": gain((L, SGU_GROUPS, SGU_CHUNK)),
        "sconv_w": nrm((L, SCONV_WIDTH, D_D), SCONV_WIDTH ** -0.5),
        "grp_norm_w": gain((L, 3, D_B)),
        "w_out": nrm((L, D_MIX, D_MODEL), D_MIX ** -0.5),
        "post_mix_norm": gain((L, D_MODEL)),
        "pre_ffn_norm": gain((L, D_MODEL)),
        "ffn_up": nrm((L, D_MODEL, 2 * D_FF), D_MODEL ** -0.5),
        "ffn_conv_w": nrm((L, FFN_CONV, 2 * D_FF), FFN_CONV ** -0.5),
        "ffn_conv_b": nrm((L, 2 * D_FF), 0.01),
        "ffn_down": nrm((L, D_FF, D_MODEL), D_FF ** -0.5),
        "post_ffn_norm": gain((L, D_MODEL)),
    }


def _fwd_reference(x, pre_mix_norm, w_in, gdn_conv_w, gdn_a_log, gdn_dt_bias, gdn_norm_w,
              lru_conv_w, lru_conv_b, lru_wa, lru_ba, lru_wx, lru_bx, lru_lambda,
              sgu_ln_w, sgu_ln_b, sgu_ws, sgu_b, sconv_w, grp_norm_w, w_out,
              post_mix_norm, pre_ffn_norm, ffn_up, ffn_conv_w, ffn_conv_b, ffn_down,
              post_ffn_norm):
    for l in range(DEPTH):
        h = rms_norm(x, pre_mix_norm[l])
        p = h @ w_in[l].astype(h.dtype)
        (q, k, v, z, b_gdn, a_gdn, lru_x, lru_gate, sgu_uv, sc_b, sc_c, sc_h) = _split(p, IN_SIZES)
        y_a = gdn_mixer(q, k, v, z, b_gdn, a_gdn, gdn_conv_w[l], gdn_a_log[l], gdn_dt_bias[l], gdn_norm_w[l])
        y_b = rms_norm(rglru_mixer(lru_x, lru_gate, lru_conv_w[l], lru_conv_b[l], lru_wa[l], lru_ba[l],
                                   lru_wx[l], lru_bx[l], lru_lambda[l]), grp_norm_w[l, 0])
        y_c = rms_norm(sgu_mixer(sgu_uv, sgu_ln_w[l], sgu_ln_b[l], sgu_ws[l], sgu_b[l]), grp_norm_w[l, 1])
        y_d = rms_norm(short_conv_mixer(sc_b, sc_c, sc_h, sconv_w[l]), grp_norm_w[l, 2])
        y = jnp.concatenate([y_a, y_b, y_c, y_d], axis=-1) @ w_out[l].astype(h.dtype)
        x = x + rms_norm(y, post_mix_norm[l])
        h = rms_norm(x, pre_ffn_norm[l])
        y = conv_ffn(h, ffn_up[l], ffn_conv_w[l], ffn_conv_b[l], ffn_down[l])
        x = x + rms_norm(y, post_ffn_norm[l])
    return x


import jax as _jax
import jax.numpy as _jnp

TWIN_FORMAT = 'train_step'
FWD_PARAMS = ['x', 'pre_mix_norm', 'w_in', 'gdn_conv_w', 'gdn_a_log', 'gdn_dt_bias', 'gdn_norm_w', 'lru_conv_w', 'lru_conv_b', 'lru_wa', 'lru_ba', 'lru_wx', 'lru_bx', 'lru_lambda', 'sgu_ln_w', 'sgu_ln_b', 'sgu_ws', 'sgu_b', 'sconv_w', 'grp_norm_w', 'w_out', 'post_mix_norm', 'pre_ffn_norm', 'ffn_up', 'ffn_conv_w', 'ffn_conv_b', 'ffn_down', 'post_ffn_norm']
TWIN_WEIGHTS = ['pre_mix_norm', 'w_in', 'gdn_conv_w', 'gdn_a_log', 'gdn_dt_bias', 'gdn_norm_w', 'lru_conv_w', 'lru_conv_b', 'lru_wa', 'lru_ba', 'lru_wx', 'lru_bx', 'lru_lambda', 'sgu_ln_w', 'sgu_ln_b', 'sgu_ws', 'sgu_b', 'sconv_w', 'grp_norm_w', 'w_out', 'post_mix_norm', 'pre_ffn_norm', 'ffn_up', 'ffn_conv_w', 'ffn_conv_b', 'ffn_down', 'post_ffn_norm']
TWIN_DIFF_INPUT = 'x'
TWIN_INPUTS = ['x', 'pre_mix_norm', 'w_in', 'gdn_conv_w', 'gdn_a_log', 'gdn_dt_bias', 'gdn_norm_w', 'lru_conv_w', 'lru_conv_b', 'lru_wa', 'lru_ba', 'lru_wx', 'lru_bx', 'lru_lambda', 'sgu_ln_w', 'sgu_ln_b', 'sgu_ws', 'sgu_b', 'sconv_w', 'grp_norm_w', 'w_out', 'post_mix_norm', 'pre_ffn_norm', 'ffn_up', 'ffn_conv_w', 'ffn_conv_b', 'ffn_down', 'post_ffn_norm', 'loss_target', 'm_pre_mix_norm', 'm_w_in', 'm_gdn_conv_w', 'm_gdn_a_log', 'm_gdn_dt_bias', 'm_gdn_norm_w', 'm_lru_conv_w', 'm_lru_conv_b', 'm_lru_wa', 'm_lru_ba', 'm_lru_wx', 'm_lru_bx', 'm_lru_lambda', 'm_sgu_ln_w', 'm_sgu_ln_b', 'm_sgu_ws', 'm_sgu_b', 'm_sconv_w', 'm_grp_norm_w', 'm_w_out', 'm_post_mix_norm', 'm_pre_ffn_norm', 'm_ffn_up', 'm_ffn_conv_w', 'm_ffn_conv_b', 'm_ffn_down', 'm_post_ffn_norm', 'v_pre_mix_norm', 'v_w_in', 'v_gdn_conv_w', 'v_gdn_a_log', 'v_gdn_dt_bias', 'v_gdn_norm_w', 'v_lru_conv_w', 'v_lru_conv_b', 'v_lru_wa', 'v_lru_ba', 'v_lru_wx', 'v_lru_bx', 'v_lru_lambda', 'v_sgu_ln_w', 'v_sgu_ln_b', 'v_sgu_ws', 'v_sgu_b', 'v_sconv_w', 'v_grp_norm_w', 'v_w_out', 'v_post_mix_norm', 'v_pre_ffn_norm', 'v_ffn_up', 'v_ffn_conv_w', 'v_ffn_conv_b', 'v_ffn_down', 'v_post_ffn_norm']
TWIN_OUTPUTS = ['loss', 'grad_x', 'grad_pre_mix_norm', 'grad_w_in', 'grad_gdn_conv_w', 'grad_gdn_a_log', 'grad_gdn_dt_bias', 'grad_gdn_norm_w', 'grad_lru_conv_w', 'grad_lru_conv_b', 'grad_lru_wa', 'grad_lru_ba', 'grad_lru_wx', 'grad_lru_bx', 'grad_lru_lambda', 'grad_sgu_ln_w', 'grad_sgu_ln_b', 'grad_sgu_ws', 'grad_sgu_b', 'grad_sconv_w', 'grad_grp_norm_w', 'grad_w_out', 'grad_post_mix_norm', 'grad_pre_ffn_norm', 'grad_ffn_up', 'grad_ffn_conv_w', 'grad_ffn_conv_b', 'grad_ffn_down', 'grad_post_ffn_norm', 'delta_pre_mix_norm', 'delta_w_in', 'delta_gdn_conv_w', 'delta_gdn_a_log', 'delta_gdn_dt_bias', 'delta_gdn_norm_w', 'delta_lru_conv_w', 'delta_lru_conv_b', 'delta_lru_wa', 'delta_lru_ba', 'delta_lru_wx', 'delta_lru_bx', 'delta_lru_lambda', 'delta_sgu_ln_w', 'delta_sgu_ln_b', 'delta_sgu_ws', 'delta_sgu_b', 'delta_sconv_w', 'delta_grp_norm_w', 'delta_w_out', 'delta_post_mix_norm', 'delta_pre_ffn_norm', 'delta_ffn_up', 'delta_ffn_conv_w', 'delta_ffn_conv_b', 'delta_ffn_down', 'delta_post_ffn_norm', 'new_m_pre_mix_norm', 'new_m_w_in', 'new_m_gdn_conv_w', 'new_m_gdn_a_log', 'new_m_gdn_dt_bias', 'new_m_gdn_norm_w', 'new_m_lru_conv_w', 'new_m_lru_conv_b', 'new_m_lru_wa', 'new_m_lru_ba', 'new_m_lru_wx', 'new_m_lru_bx', 'new_m_lru_lambda', 'new_m_sgu_ln_w', 'new_m_sgu_ln_b', 'new_m_sgu_ws', 'new_m_sgu_b', 'new_m_sconv_w', 'new_m_grp_norm_w', 'new_m_w_out', 'new_m_post_mix_norm', 'new_m_pre_ffn_norm', 'new_m_ffn_up', 'new_m_ffn_conv_w', 'new_m_ffn_conv_b', 'new_m_ffn_down', 'new_m_post_ffn_norm', 'new_v_pre_mix_norm', 'new_v_w_in', 'new_v_gdn_conv_w', 'new_v_gdn_a_log', 'new_v_gdn_dt_bias', 'new_v_gdn_norm_w', 'new_v_lru_conv_w', 'new_v_lru_conv_b', 'new_v_lru_wa', 'new_v_lru_ba', 'new_v_lru_wx', 'new_v_lru_bx', 'new_v_lru_lambda', 'new_v_sgu_ln_w', 'new_v_sgu_ln_b', 'new_v_sgu_ws', 'new_v_sgu_b', 'new_v_sconv_w', 'new_v_grp_norm_w', 'new_v_w_out', 'new_v_post_mix_norm', 'new_v_pre_ffn_norm', 'new_v_ffn_up', 'new_v_ffn_conv_w', 'new_v_ffn_conv_b', 'new_v_ffn_down', 'new_v_post_ffn_norm']
TWIN_LEAF_KINDS = {'loss': 'loss', 'grad_x': 'grad_x', 'grad_pre_mix_norm': 'grad_w', 'grad_w_in': 'grad_w', 'grad_gdn_conv_w': 'grad_w', 'grad_gdn_a_log': 'grad_w', 'grad_gdn_dt_bias': 'grad_w', 'grad_gdn_norm_w': 'grad_w', 'grad_lru_conv_w': 'grad_w', 'grad_lru_conv_b': 'grad_w', 'grad_lru_wa': 'grad_w', 'grad_lru_ba': 'grad_w', 'grad_lru_wx': 'grad_w', 'grad_lru_bx': 'grad_w', 'grad_lru_lambda': 'grad_w', 'grad_sgu_ln_w': 'grad_w', 'grad_sgu_ln_b': 'grad_w', 'grad_sgu_ws': 'grad_w', 'grad_sgu_b': 'grad_w', 'grad_sconv_w': 'grad_w', 'grad_grp_norm_w': 'grad_w', 'grad_w_out': 'grad_w', 'grad_post_mix_norm': 'grad_w', 'grad_pre_ffn_norm': 'grad_w', 'grad_ffn_up': 'grad_w', 'grad_ffn_conv_w': 'grad_w', 'grad_ffn_conv_b': 'grad_w', 'grad_ffn_down': 'grad_w', 'grad_post_ffn_norm': 'grad_w', 'delta_pre_mix_norm': 'delta_w', 'delta_w_in': 'delta_w', 'delta_gdn_conv_w': 'delta_w', 'delta_gdn_a_log': 'delta_w', 'delta_gdn_dt_bias': 'delta_w', 'delta_gdn_norm_w': 'delta_w', 'delta_lru_conv_w': 'delta_w', 'delta_lru_conv_b': 'delta_w', 'delta_lru_wa': 'delta_w', 'delta_lru_ba': 'delta_w', 'delta_lru_wx': 'delta_w', 'delta_lru_bx': 'delta_w', 'delta_lru_lambda': 'delta_w', 'delta_sgu_ln_w': 'delta_w', 'delta_sgu_ln_b': 'delta_w', 'delta_sgu_ws': 'delta_w', 'delta_sgu_b': 'delta_w', 'delta_sconv_w': 'delta_w', 'delta_grp_norm_w': 'delta_w', 'delta_w_out': 'delta_w', 'delta_post_mix_norm': 'delta_w', 'delta_pre_ffn_norm': 'delta_w', 'delta_ffn_up': 'delta_w', 'delta_ffn_conv_w': 'delta_w', 'delta_ffn_conv_b': 'delta_w', 'delta_ffn_down': 'delta_w', 'delta_post_ffn_norm': 'delta_w', 'new_m_pre_mix_norm': 'new_m', 'new_m_w_in': 'new_m', 'new_m_gdn_conv_w': 'new_m', 'new_m_gdn_a_log': 'new_m', 'new_m_gdn_dt_bias': 'new_m', 'new_m_gdn_norm_w': 'new_m', 'new_m_lru_conv_w': 'new_m', 'new_m_lru_conv_b': 'new_m', 'new_m_lru_wa': 'new_m', 'new_m_lru_ba': 'new_m', 'new_m_lru_wx': 'new_m', 'new_m_lru_bx': 'new_m', 'new_m_lru_lambda': 'new_m', 'new_m_sgu_ln_w': 'new_m', 'new_m_sgu_ln_b': 'new_m', 'new_m_sgu_ws': 'new_m', 'new_m_sgu_b': 'new_m', 'new_m_sconv_w': 'new_m', 'new_m_grp_norm_w': 'new_m', 'new_m_w_out': 'new_m', 'new_m_post_mix_norm': 'new_m', 'new_m_pre_ffn_norm': 'new_m', 'new_m_ffn_up': 'new_m', 'new_m_ffn_conv_w': 'new_m', 'new_m_ffn_conv_b': 'new_m', 'new_m_ffn_down': 'new_m', 'new_m_post_ffn_norm': 'new_m', 'new_v_pre_mix_norm': 'new_v', 'new_v_w_in': 'new_v', 'new_v_gdn_conv_w': 'new_v', 'new_v_gdn_a_log': 'new_v', 'new_v_gdn_dt_bias': 'new_v', 'new_v_gdn_norm_w': 'new_v', 'new_v_lru_conv_w': 'new_v', 'new_v_lru_conv_b': 'new_v', 'new_v_lru_wa': 'new_v', 'new_v_lru_ba': 'new_v', 'new_v_lru_wx': 'new_v', 'new_v_lru_bx': 'new_v', 'new_v_lru_lambda': 'new_v', 'new_v_sgu_ln_w': 'new_v', 'new_v_sgu_ln_b': 'new_v', 'new_v_sgu_ws': 'new_v', 'new_v_sgu_b': 'new_v', 'new_v_sconv_w': 'new_v', 'new_v_grp_norm_w': 'new_v', 'new_v_w_out': 'new_v', 'new_v_post_mix_norm': 'new_v', 'new_v_pre_ffn_norm': 'new_v', 'new_v_ffn_up': 'new_v', 'new_v_ffn_conv_w': 'new_v', 'new_v_ffn_conv_b': 'new_v', 'new_v_ffn_down': 'new_v', 'new_v_post_ffn_norm': 'new_v'}


def _forward(args):
    return _fwd_reference(*[args[k] for k in FWD_PARAMS])


def _output_shape():
    def fwd():
        inp = _fwd_setup_inputs(0)
        return _fwd_reference(*[inp[k] for k in FWD_PARAMS])
    out = _jax.eval_shape(fwd)
    return out.shape, out.dtype

N_MICROBATCH = 1
ADAM_LR = 0.001
ADAM_B1 = 0.9
ADAM_B2 = 0.999
ADAM_EPS = 1e-08
ADAM_WD = 0.01
ADAM_STEP = 10
PER_EXAMPLE_BATCH_AXIS = {'x': 0, 'loss_target': 0}
SHARED_INPUTS = []
_WEIGHT_DTYPES = {'pre_mix_norm': _jnp.float32, 'w_in': _jnp.float32, 'gdn_conv_w': _jnp.float32, 'gdn_a_log': _jnp.float32, 'gdn_dt_bias': _jnp.float32, 'gdn_norm_w': _jnp.float32, 'lru_conv_w': _jnp.float32, 'lru_conv_b': _jnp.float32, 'lru_wa': _jnp.float32, 'lru_ba': _jnp.float32, 'lru_wx': _jnp.float32, 'lru_bx': _jnp.float32, 'lru_lambda': _jnp.float32, 'sgu_ln_w': _jnp.float32, 'sgu_ln_b': _jnp.float32, 'sgu_ws': _jnp.float32, 'sgu_b': _jnp.float32, 'sconv_w': _jnp.float32, 'grp_norm_w': _jnp.float32, 'w_out': _jnp.float32, 'post_mix_norm': _jnp.float32, 'pre_ffn_norm': _jnp.float32, 'ffn_up': _jnp.float32, 'ffn_conv_w': _jnp.float32, 'ffn_conv_b': _jnp.float32, 'ffn_down': _jnp.float32, 'post_ffn_norm': _jnp.float32}
MOMENT_SCALE = {'pre_mix_norm': 1.180674e+00, 'w_in': 6.924754e-01, 'gdn_conv_w': 5.665532e-01, 'gdn_a_log': 1.216224e+00, 'gdn_dt_bias': 1.185424e+00, 'gdn_norm_w': 2.717327e+00, 'lru_conv_w': 3.028563e+00, 'lru_conv_b': 3.810416e+01, 'lru_wa': 1.271326e+00, 'lru_ba': 9.428839e-01, 'lru_wx': 2.361297e+00, 'lru_bx': 9.031111e-01, 'lru_lambda': 1.529297e+00, 'sgu_ln_w': 3.437210e-01, 'sgu_ln_b': 3.503199e-01, 'sgu_ws': 3.216431e-01, 'sgu_b': 5.788735e-01, 'sconv_w': 6.120047e-01, 'grp_norm_w': 3.172656e+00, 'w_out': 2.855499e+00, 'post_mix_norm': 3.244749e+01, 'pre_ffn_norm': 1.028203e+00, 'ffn_up': 4.392663e-01, 'ffn_conv_w': 5.017865e-01, 'ffn_conv_b': 2.914207e+00, 'ffn_down': 9.791667e-01, 'post_ffn_norm': 3.219350e+01}


def _to_microbatches(a, axis):
    t = _jnp.moveaxis(a, axis, 0)
    t = t.reshape((N_MICROBATCH, t.shape[0] // N_MICROBATCH) + t.shape[1:])
    return _jnp.moveaxis(t, 1, axis + 1)


def setup_inputs(seed: int = 0) -> dict:
    inp = _fwd_setup_inputs(seed)
    key = _jax.random.fold_in(_jax.random.key(seed), 7919)
    shape, _ = _output_shape()
    out = dict(inp)
    out["loss_target"] = _jax.random.normal(_jax.random.fold_in(key, 0), shape, _jnp.float32)
    for i, name in enumerate(TWIN_WEIGHTS):
        w = inp[name].astype(_jnp.float32)
        if MOMENT_SCALE is None:
            s = _jnp.sqrt(_jnp.mean(_jnp.square(w)) + 1e-30)
        else:
            s = MOMENT_SCALE[name]
        km, kv = _jax.random.split(_jax.random.fold_in(key, i + 1))
        out[name] = w
        out["m_" + name] = s * _jax.random.normal(km, w.shape, _jnp.float32)
        out["v_" + name] = (s * s) * _jax.random.uniform(kv, w.shape, _jnp.float32, 0.5, 1.5)
    if N_MICROBATCH > 1:
        for name, axis in PER_EXAMPLE_BATCH_AXIS.items():
            out[name] = _to_microbatches(out[name], axis)
    return {'x': out['x'], 'pre_mix_norm': out['pre_mix_norm'], 'w_in': out['w_in'], 'gdn_conv_w': out['gdn_conv_w'], 'gdn_a_log': out['gdn_a_log'], 'gdn_dt_bias': out['gdn_dt_bias'], 'gdn_norm_w': out['gdn_norm_w'], 'lru_conv_w': out['lru_conv_w'], 'lru_conv_b': out['lru_conv_b'], 'lru_wa': out['lru_wa'], 'lru_ba': out['lru_ba'], 'lru_wx': out['lru_wx'], 'lru_bx': out['lru_bx'], 'lru_lambda': out['lru_lambda'], 'sgu_ln_w': out['sgu_ln_w'], 'sgu_ln_b': out['sgu_ln_b'], 'sgu_ws': out['sgu_ws'], 'sgu_b': out['sgu_b'], 'sconv_w': out['sconv_w'], 'grp_norm_w': out['grp_norm_w'], 'w_out': out['w_out'], 'post_mix_norm': out['post_mix_norm'], 'pre_ffn_norm': out['pre_ffn_norm'], 'ffn_up': out['ffn_up'], 'ffn_conv_w': out['ffn_conv_w'], 'ffn_conv_b': out['ffn_conv_b'], 'ffn_down': out['ffn_down'], 'post_ffn_norm': out['post_ffn_norm'], 'loss_target': out['loss_target'], 'm_pre_mix_norm': out['m_pre_mix_norm'], 'm_w_in': out['m_w_in'], 'm_gdn_conv_w': out['m_gdn_conv_w'], 'm_gdn_a_log': out['m_gdn_a_log'], 'm_gdn_dt_bias': out['m_gdn_dt_bias'], 'm_gdn_norm_w': out['m_gdn_norm_w'], 'm_lru_conv_w': out['m_lru_conv_w'], 'm_lru_conv_b': out['m_lru_conv_b'], 'm_lru_wa': out['m_lru_wa'], 'm_lru_ba': out['m_lru_ba'], 'm_lru_wx': out['m_lru_wx'], 'm_lru_bx': out['m_lru_bx'], 'm_lru_lambda': out['m_lru_lambda'], 'm_sgu_ln_w': out['m_sgu_ln_w'], 'm_sgu_ln_b': out['m_sgu_ln_b'], 'm_sgu_ws': out['m_sgu_ws'], 'm_sgu_b': out['m_sgu_b'], 'm_sconv_w': out['m_sconv_w'], 'm_grp_norm_w': out['m_grp_norm_w'], 'm_w_out': out['m_w_out'], 'm_post_mix_norm': out['m_post_mix_norm'], 'm_pre_ffn_norm': out['m_pre_ffn_norm'], 'm_ffn_up': out['m_ffn_up'], 'm_ffn_conv_w': out['m_ffn_conv_w'], 'm_ffn_conv_b': out['m_ffn_conv_b'], 'm_ffn_down': out['m_ffn_down'], 'm_post_ffn_norm': out['m_post_ffn_norm'], 'v_pre_mix_norm': out['v_pre_mix_norm'], 'v_w_in': out['v_w_in'], 'v_gdn_conv_w': out['v_gdn_conv_w'], 'v_gdn_a_log': out['v_gdn_a_log'], 'v_gdn_dt_bias': out['v_gdn_dt_bias'], 'v_gdn_norm_w': out['v_gdn_norm_w'], 'v_lru_conv_w': out['v_lru_conv_w'], 'v_lru_conv_b': out['v_lru_conv_b'], 'v_lru_wa': out['v_lru_wa'], 'v_lru_ba': out['v_lru_ba'], 'v_lru_wx': out['v_lru_wx'], 'v_lru_bx': out['v_lru_bx'], 'v_lru_lambda': out['v_lru_lambda'], 'v_sgu_ln_w': out['v_sgu_ln_w'], 'v_sgu_ln_b': out['v_sgu_ln_b'], 'v_sgu_ws': out['v_sgu_ws'], 'v_sgu_b': out['v_sgu_b'], 'v_sconv_w': out['v_sconv_w'], 'v_grp_norm_w': out['v_grp_norm_w'], 'v_w_out': out['v_w_out'], 'v_post_mix_norm': out['v_post_mix_norm'], 'v_pre_ffn_norm': out['v_pre_ffn_norm'], 'v_ffn_up': out['v_ffn_up'], 'v_ffn_conv_w': out['v_ffn_conv_w'], 'v_ffn_conv_b': out['v_ffn_conv_b'], 'v_ffn_down': out['v_ffn_down'], 'v_post_ffn_norm': out['v_post_ffn_norm']}


def _loss(weights, diff, rest, loss_target):
    with _jax.named_scope("forward"):
        args = {**rest, TWIN_DIFF_INPUT: diff, **{k: w.astype(_WEIGHT_DTYPES[k]) for k, w in weights.items()}}
        y = _forward(args)
    with _jax.named_scope("loss_head"):
        err = _jnp.square(y.astype(_jnp.float32) - loss_target)
        return 0.5 * _jnp.sum(_jnp.mean(err, axis=-1)) if err.ndim else 0.5 * err


def _adamw(w, g, m, v):
    m = ADAM_B1 * m + (1.0 - ADAM_B1) * g
    v = ADAM_B2 * v + (1.0 - ADAM_B2) * _jnp.square(g)
    m_hat = m / (1.0 - ADAM_B1 ** ADAM_STEP)
    v_hat = v / (1.0 - ADAM_B2 ** ADAM_STEP)
    delta = -ADAM_LR * (m_hat / (_jnp.sqrt(v_hat) + ADAM_EPS) + ADAM_WD * w)
    return delta, m, v


def reference(x, pre_mix_norm, w_in, gdn_conv_w, gdn_a_log, gdn_dt_bias, gdn_norm_w, lru_conv_w, lru_conv_b, lru_wa, lru_ba, lru_wx, lru_bx, lru_lambda, sgu_ln_w, sgu_ln_b, sgu_ws, sgu_b, sconv_w, grp_norm_w, w_out, post_mix_norm, pre_ffn_norm, ffn_up, ffn_conv_w, ffn_conv_b, ffn_down, post_ffn_norm, loss_target, m_pre_mix_norm, m_w_in, m_gdn_conv_w, m_gdn_a_log, m_gdn_dt_bias, m_gdn_norm_w, m_lru_conv_w, m_lru_conv_b, m_lru_wa, m_lru_ba, m_lru_wx, m_lru_bx, m_lru_lambda, m_sgu_ln_w, m_sgu_ln_b, m_sgu_ws, m_sgu_b, m_sconv_w, m_grp_norm_w, m_w_out, m_post_mix_norm, m_pre_ffn_norm, m_ffn_up, m_ffn_conv_w, m_ffn_conv_b, m_ffn_down, m_post_ffn_norm, v_pre_mix_norm, v_w_in, v_gdn_conv_w, v_gdn_a_log, v_gdn_dt_bias, v_gdn_norm_w, v_lru_conv_w, v_lru_conv_b, v_lru_wa, v_lru_ba, v_lru_wx, v_lru_bx, v_lru_lambda, v_sgu_ln_w, v_sgu_ln_b, v_sgu_ws, v_sgu_b, v_sconv_w, v_grp_norm_w, v_w_out, v_post_mix_norm, v_pre_ffn_norm, v_ffn_up, v_ffn_conv_w, v_ffn_conv_b, v_ffn_down, v_post_ffn_norm):
    given = dict(x=x, pre_mix_norm=pre_mix_norm, w_in=w_in, gdn_conv_w=gdn_conv_w, gdn_a_log=gdn_a_log, gdn_dt_bias=gdn_dt_bias, gdn_norm_w=gdn_norm_w, lru_conv_w=lru_conv_w, lru_conv_b=lru_conv_b, lru_wa=lru_wa, lru_ba=lru_ba, lru_wx=lru_wx, lru_bx=lru_bx, lru_lambda=lru_lambda, sgu_ln_w=sgu_ln_w, sgu_ln_b=sgu_ln_b, sgu_ws=sgu_ws, sgu_b=sgu_b, sconv_w=sconv_w, grp_norm_w=grp_norm_w, w_out=w_out, post_mix_norm=post_mix_norm, pre_ffn_norm=pre_ffn_norm, ffn_up=ffn_up, ffn_conv_w=ffn_conv_w, ffn_conv_b=ffn_conv_b, ffn_down=ffn_down, post_ffn_norm=post_ffn_norm, loss_target=loss_target, m_pre_mix_norm=m_pre_mix_norm, m_w_in=m_w_in, m_gdn_conv_w=m_gdn_conv_w, m_gdn_a_log=m_gdn_a_log, m_gdn_dt_bias=m_gdn_dt_bias, m_gdn_norm_w=m_gdn_norm_w, m_lru_conv_w=m_lru_conv_w, m_lru_conv_b=m_lru_conv_b, m_lru_wa=m_lru_wa, m_lru_ba=m_lru_ba, m_lru_wx=m_lru_wx, m_lru_bx=m_lru_bx, m_lru_lambda=m_lru_lambda, m_sgu_ln_w=m_sgu_ln_w, m_sgu_ln_b=m_sgu_ln_b, m_sgu_ws=m_sgu_ws, m_sgu_b=m_sgu_b, m_sconv_w=m_sconv_w, m_grp_norm_w=m_grp_norm_w, m_w_out=m_w_out, m_post_mix_norm=m_post_mix_norm, m_pre_ffn_norm=m_pre_ffn_norm, m_ffn_up=m_ffn_up, m_ffn_conv_w=m_ffn_conv_w, m_ffn_conv_b=m_ffn_conv_b, m_ffn_down=m_ffn_down, m_post_ffn_norm=m_post_ffn_norm, v_pre_mix_norm=v_pre_mix_norm, v_w_in=v_w_in, v_gdn_conv_w=v_gdn_conv_w, v_gdn_a_log=v_gdn_a_log, v_gdn_dt_bias=v_gdn_dt_bias, v_gdn_norm_w=v_gdn_norm_w, v_lru_conv_w=v_lru_conv_w, v_lru_conv_b=v_lru_conv_b, v_lru_wa=v_lru_wa, v_lru_ba=v_lru_ba, v_lru_wx=v_lru_wx, v_lru_bx=v_lru_bx, v_lru_lambda=v_lru_lambda, v_sgu_ln_w=v_sgu_ln_w, v_sgu_ln_b=v_sgu_ln_b, v_sgu_ws=v_sgu_ws, v_sgu_b=v_sgu_b, v_sconv_w=v_sconv_w, v_grp_norm_w=v_grp_norm_w, v_w_out=v_w_out, v_post_mix_norm=v_post_mix_norm, v_pre_ffn_norm=v_pre_ffn_norm, v_ffn_up=v_ffn_up, v_ffn_conv_w=v_ffn_conv_w, v_ffn_conv_b=v_ffn_conv_b, v_ffn_down=v_ffn_down, v_post_ffn_norm=v_post_ffn_norm)
    weights = {n: given[n] for n in TWIN_WEIGHTS}
    shared = {n: given[n] for n in SHARED_INPUTS}
    per_example = {n: given[n] for n in ['x']}
    grad_fn = _jax.value_and_grad(_loss, argnums=(0, 1))

    def one_microbatch(ex, loss_target):
        ex = dict(ex)
        diff = ex.pop(TWIN_DIFF_INPUT)
        return grad_fn(weights, diff, {**shared, **ex}, loss_target)

    if N_MICROBATCH == 1:
        loss, (grad_w, grad_x) = one_microbatch(per_example, given["loss_target"])
    else:
        def body(carry, xs):
            loss_sum, grad_sum = carry
            l_k, (gw_k, gx_k) = one_microbatch(xs[0], xs[1])
            with _jax.named_scope("update"):
                return (loss_sum + l_k, _jax.tree.map(_jnp.add, grad_sum, gw_k)), gx_k

        init = (_jnp.zeros((), _jnp.float32), _jax.tree.map(_jnp.zeros_like, weights))
        (loss, grad_w), grad_x = _jax.lax.scan(body, init, (per_example, given["loss_target"]))
    with _jax.named_scope("update"):
        delta_w, new_m, new_v = {}, {}, {}
        for n in TWIN_WEIGHTS:
            delta_w[n], new_m[n], new_v[n] = _adamw(weights[n], grad_w[n], given["m_" + n], given["v_" + n])
    return (loss, grad_x, *[grad_w[n] for n in TWIN_WEIGHTS], *[delta_w[n] for n in TWIN_WEIGHTS],
            *[new_m[n] for n in TWIN_WEIGHTS], *[new_v[n] for n in TWIN_WEIGHTS])
```

```python
import functools
import math

import jax
import jax.numpy as jnp
from jax import lax
from jax.experimental import pallas as pl
from jax.experimental.pallas import tpu as pltpu

F32 = jnp.float32
BF = jnp.bfloat16
MESH_ID = pl.DeviceIdType.MESH

EPS = 1e-6
DEPTH = 2
D = 2048
DG = 512
HEADS = 4
HD = 128
CHUNK = 64
LRU_C = 8.0
D_FF = 5632
N_IN = 5640
NP = 5760
OFF_Q, OFF_Z, OFF_LX, OFF_LG, OFF_UV, OFF_SB, OFF_SC, OFF_SH, OFF_BA = 0, 1536, 2048, 2560, 3072, 4096, 4608, 5120, 5632

ADAM_LR, ADAM_B1, ADAM_B2, ADAM_EPS, ADAM_WD, ADAM_STEP = 0.001, 0.9, 0.999, 1e-08, 0.01, 10

HALO = 8


def _mk_bdot(ca, cb):
    na, nb = 1 - ca, 1 - cb

    def dg(x, y, cx, cy):
        return lax.dot_general(x.astype(BF), y.astype(BF), (((cx,), (cy,)), ((), ())), preferred_element_type=F32)

    @jax.custom_vjp
    def f(a, b):
        return dg(a, b, ca, cb)

    def fwd(a, b):
        return dg(a, b, ca, cb), (a, b)

    def bwd(res, g):
        a, b = res
        da = dg(g, b, 1, nb) if ca == 1 else dg(b, g, nb, 1)
        db = dg(a, g, na, 0) if cb == 0 else dg(g, a, 0, na)
        return da.astype(a.dtype), db.astype(b.dtype)

    f.defvjp(fwd, bwd)
    return f


_bdot = _mk_bdot(1, 0)
_bdot_nt = _mk_bdot(1, 1)
_bdot_tn = _mk_bdot(0, 0)


def _fdot(a, b, ca=1, cb=0):
    return lax.dot_general(a, b, (((ca,), (cb,)), ((), ())), precision=lax.Precision.HIGHEST, preferred_element_type=F32)


def _sigmoid(x):
    return 1.0 / (1.0 + jnp.exp(-x))


def _silu(x):
    return x * _sigmoid(x)


def _gelu(x):
    return 0.5 * x * (1.0 + jnp.tanh(0.7978845608028654 * (x + 0.044715 * (x * x * x))))


def _log1p(z):
    u = 1.0 + z
    d = u - 1.0
    return jnp.where(d == 0.0, z, jnp.log(u) * (z / jnp.where(d == 0.0, 1.0, d)))


def _softplus(x):
    return jnp.maximum(x, 0.0) + _log1p(jnp.exp(-jnp.abs(x)))


def _neg_expm1(y):
    t = jnp.tanh(0.5 * y)
    return -2.0 * t / (1.0 - t)


def _rms(x, w):
    return x * lax.rsqrt(jnp.mean(x * x, axis=-1, keepdims=True) + EPS) * w


def _rows(x):
    return lax.broadcasted_iota(jnp.int32, x.shape, 0)


def _mk_shift():
    @functools.partial(jax.custom_vjp, nondiff_argnums=(1,))
    def shift(xx, s):
        n = xx.shape[0]
        return pltpu.roll(xx, s, axis=0)[HALO:n] if s else xx[HALO:n]

    def fwd(xx, s):
        return shift(xx, s), None

    def bwd(s, _, g):
        ext = jnp.concatenate([g, jnp.zeros((HALO, g.shape[1]), g.dtype)], axis=0)
        return (pltpu.roll(ext, HALO - s, axis=0),)

    shift.defvjp(fwd, bwd)
    return shift


_shift = _mk_shift()


def _causal_conv(xx, w):
    K = w.shape[0]
    y = _shift(xx, K - 1) * w[0:1, :]
    for k in range(1, K):
        y = y + _shift(xx, K - 1 - k) * w[k:k + 1, :]
    return y


def _f_pre(x, w):
    return (_rms(x, w),)


def _f_post_pre(x, y, w_post, w_pre):
    x1 = x + _rms(y, w_post)
    return x1, _rms(x1, w_pre)


def _f_post(x, y, w_post):
    return (x + _rms(y, w_post),)


def _heads(fn, *xs):
    return jnp.concatenate([fn(*[x[:, h * HD:(h + 1) * HD] for x in xs]) for h in range(HEADS)], axis=1)


def _l2n(t):
    return t * lax.rsqrt(jnp.sum(t * t, axis=-1, keepdims=True) + EPS)


def _f_gdn_pre(qkv, ba, conv_w, alog, dtb):
    c = _silu(_causal_conv(qkv, conv_w))
    q = _heads(lambda t: _l2n(t) * (HD ** -0.5), c[:, 0:DG])
    k = _heads(_l2n, c[:, DG:2 * DG])
    v = c[:, 2 * DG:3 * DG]
    lane = lax.broadcasted_iota(jnp.int32, ba.shape, 1)
    beta = _sigmoid(ba)
    g = -jnp.exp(alog) * _softplus(ba + dtb)
    gb = jnp.where(lane < HEADS, beta, jnp.where(lane < 2 * HEADS, g, 0.0))
    return q, k, v, gb


def _f_gdn_chunk(h, q, k, v, gb, s0):
    C = q.shape[0]
    r = lax.broadcasted_iota(jnp.int32, (C, C), 0)
    c = lax.broadcasted_iota(jnp.int32, (C, C), 1)
    causal = r >= c
    strict = r > c
    lane = lax.broadcasted_iota(jnp.int32, gb.shape, 1)
    gcum_c = _fdot(causal.astype(F32), gb)
    gcum_r = _fdot(gb, (r <= c).astype(F32), 0, 0)
    sub = lax.broadcasted_iota(jnp.int32, gcum_r.shape, 0)
    gcol = jnp.sum(jnp.where(lane == HEADS + h, gcum_c, 0.0), axis=1, keepdims=True)
    grow = jnp.sum(jnp.where(sub == HEADS + h, gcum_r, 0.0), axis=0, keepdims=True)
    bcol = jnp.sum(jnp.where(lane == h, gb, 0.0), axis=1, keepdims=True)
    decay = jnp.where(causal, jnp.exp(jnp.where(causal, gcol - grow, 0.0)), 0.0)
    kb = k * bcol
    m = jnp.where(strict, _bdot_nt(kb, k) * decay, 0.0)
    n = -m
    t = (r == c).astype(F32) + n
    p = n
    for _ in range(5):
        p = _bdot(p, p)
        t = t + _bdot(t, p)
    eg = jnp.exp(gcol)
    w = _bdot(t, kb * eg)
    u = _bdot(t, v * bcol)
    attn = jnp.where(causal, _bdot_nt(q, k) * decay, 0.0)
    glast = jnp.sum(jnp.where(_rows(gcol) == C - 1, gcol, 0.0), axis=0, keepdims=True)
    k_g = k * jnp.exp(glast - gcol)
    v_new = u - _bdot(w, s0)
    o = _bdot(q * eg, s0) + _bdot(attn, v_new)
    s1 = s0 * jnp.exp(glast) + _bdot_tn(k_g, v_new)
    return o, s1


def _f_gdn_post(o, z, nw):
    return (_heads(lambda a, b: _rms(a, nw) * _silu(b), o, z),)


def _f_lru_ab(lx, conv_w, conv_b, wa, ba, wx, bx, lam):
    xc = _causal_conv(lx, conv_w) + conv_b
    r = _sigmoid(_bdot(xc, wa) + ba)
    i = _sigmoid(_bdot(xc, wx) + bx)
    log_a = -LRU_C * r * _softplus(-lam)
    a = jnp.exp(log_a)
    mult = jnp.sqrt(_neg_expm1(2.0 * log_a))
    return a, mult * (i * xc)


def _f_lru_post(hs, gate, gw):
    return (_rms(hs * _gelu(gate), gw),)


def _f_sgu(uv, ln_w, ln_b, ws, bst, gw):
    tm = uv.shape[0]
    uvf = _gelu(uv)
    u, v = uvf[:, 0:DG], uvf[:, DG:2 * DG]
    mu = jnp.mean(v, axis=-1, keepdims=True)
    vc = v - mu
    v = vc * lax.rsqrt(jnp.mean(vc * vc, axis=-1, keepdims=True) + EPS) * ln_w + ln_b
    lane = lax.broadcasted_iota(jnp.int32, bst.shape, 1)
    tril = lax.broadcasted_iota(jnp.int32, (128, 128), 0) >= lax.broadcasted_iota(jnp.int32, (128, 128), 1)
    wsm = [jnp.where(tril, ws[g * 128:(g + 1) * 128, :], 0.0) for g in range(4)]
    bias = [jnp.sum(jnp.where(lane == g, bst, 0.0), axis=1, keepdims=True) for g in range(4)]
    out = []
    for n in range(tm // 128):
        vn = v[n * 128:(n + 1) * 128, :]
        gs = [_bdot(wsm[g], vn[:, g * 128:(g + 1) * 128]) + bias[g] for g in range(4)]
        out.append(jnp.concatenate(gs, axis=1))
    vo = jnp.concatenate(out, axis=0) if len(out) > 1 else out[0]
    return (_rms(u * vo, gw),)


def _f_sconv(sb, sc, sh, conv_w, gw):
    return (_rms(sb * _causal_conv(sc * sh, conv_w), gw),)


def _f_ffn_act(ug, uv, wg, wv, bg, bv):
    return (_gelu(_causal_conv(ug, wg) + bg) * (_causal_conv(uv, wv) + bv),)


def _row_specs(rows, consts, tm, ncol, tile_of):
    specs, args = [], []
    for arr, cb, w, halo in rows:
        specs.append(pl.BlockSpec((tm, w), lambda j, i, cb=cb: (tile_of(i), cb + j)))
        args.append(arr)
        if halo:
            specs.append(pl.BlockSpec((HALO, w), lambda j, i, cb=cb: (jnp.maximum(tile_of(i) * (tm // HALO) - 1, 0), cb + j)))
            args.append(arr)
    for arr, tiled in consts:
        r, c = arr.shape
        specs.append(pl.BlockSpec((r, c // ncol), lambda j, i: (0, j)) if tiled else pl.BlockSpec((r, c), lambda j, i: (0, 0)))
        args.append(arr)
    return specs, args


def _load_rows(refs, rows, consts, tile):
    k, vals = 0, []
    for _arr, _cb, _w, halo in rows:
        t = refs[k][...].astype(F32)
        k += 1
        if halo:
            hl = jnp.where(tile > 0, refs[k][...].astype(F32), 0.0)
            k += 1
            t = jnp.concatenate([hl, t], axis=0)
        vals.append(t)
    for _ in consts:
        vals.append(refs[k][...])
        k += 1
    return vals, k


def _rowwise(fn, rows, consts, outs, *, tm, name, ncol=1):
    n = rows[0][0].shape[0]
    nt = n // tm
    in_specs, args = _row_specs(rows, consts, tm, ncol, lambda i: i)

    def body(*refs):
        vals, k = _load_rows(refs, rows, consts, pl.program_id(1))
        for o_ref, r in zip(refs[k:], fn(*vals)):
            o_ref[...] = r.astype(o_ref.dtype)

    res = pl.pallas_call(
        body, grid=(ncol, nt), in_specs=in_specs,
        out_specs=[pl.BlockSpec((tm, w), lambda j, i: (i, j)) for w, _ in outs],
        out_shape=[jax.ShapeDtypeStruct((n, w * ncol), dt) for w, dt in outs],
        name=name, compiler_params=pltpu.CompilerParams(dimension_semantics=("arbitrary", "arbitrary")),
    )(*args)
    return res


def _rowwise_vjp(fn, rows, consts, cots, *, tm, name, ncol=1):
    n = rows[0][0].shape[0]
    nt = n // tm
    rows4 = [r[:4] for r in rows]
    consts2 = [c[:2] for c in consts]
    in_specs, args = _row_specs(rows4, consts2, tm, ncol, lambda i: nt - 1 - i)
    for arr, cb, w in cots:
        in_specs.append(pl.BlockSpec((tm, w), lambda j, i, cb=cb: (nt - 1 - i, cb + j)))
        args.append(arr)
    out_specs, out_shape, scratch = [], [], []
    for arr, cb, w, halo, gdt in rows:
        if gdt is not None:
            out_specs.append(pl.BlockSpec((tm, w), lambda j, i: (nt - 1 - i, j)))
            out_shape.append(jax.ShapeDtypeStruct((n, w * ncol), gdt))
            if halo:
                scratch.append(pltpu.VMEM((HALO, w), F32))
    for arr, tiled, want in consts:
        if want:
            r, c = arr.shape
            out_specs.append(pl.BlockSpec((r, c // ncol), lambda j, i: (0, j)) if tiled else pl.BlockSpec((r, c), lambda j, i: (0, 0)))
            out_shape.append(jax.ShapeDtypeStruct((r, c), F32))
    n_in = len(in_specs)
    n_out = len(out_specs)

    def body(*refs):
        j, i = pl.program_id(0), pl.program_id(1)
        tile = nt - 1 - i
        vals, k = _load_rows(refs, rows4, consts2, tile)
        cvals = [refs[k + q][...].astype(F32) for q in range(len(cots))]
        outs = refs[n_in:n_in + n_out]
        carries = refs[n_in + n_out:]
        _, vjp = jax.vjp(fn, *vals)
        g = vjp(tuple(cvals))
        o, cidx = 0, 0
        for q, (arr, cb, w, halo, gdt) in enumerate(rows):
            if gdt is None:
                continue
            if halo:
                ge = g[q]
                main = ge[HALO:]
                carry = carries[cidx]
                cidx += 1
                tail = main[tm - HALO:] + jnp.where(i > 0, carry[...], 0.0)
                outs[o][0:tm - HALO, :] = main[0:tm - HALO].astype(gdt)
                outs[o][tm - HALO:tm, :] = tail.astype(gdt)
                carry[...] = ge[0:HALO]
            else:
                outs[o][...] = g[q].astype(gdt)
            o += 1
        for q, (arr, tiled, want) in enumerate(consts):
            if not want:
                continue
            first = (i == 0) if tiled else jnp.logical_and(i == 0, j == 0)
            acc = outs[o]
            gq = g[len(rows) + q].astype(F32)

            @pl.when(first)
            def _(acc=acc, gq=gq):
                acc[...] = gq

            @pl.when(jnp.logical_not(first))
            def _(acc=acc, gq=gq):
                acc[...] += gq

            o += 1

    res = pl.pallas_call(
        body, grid=(ncol, nt), in_specs=in_specs, out_specs=out_specs, out_shape=out_shape, scratch_shapes=scratch,
        name=name, compiler_params=pltpu.CompilerParams(dimension_semantics=("arbitrary", "arbitrary")),
    )(*args)
    nrow = sum(1 for r in rows if r[4] is not None)
    return list(res[:nrow]), list(res[nrow:])


def _matmul(a, b, mode, out_dtype, *, tm, tn, tk, name):
    if mode == "tn":
        K, M = a.shape
        N = b.shape[1]
        a_spec = pl.BlockSpec((tk, tm), lambda i, j, k: (k, i))
        b_spec = pl.BlockSpec((tk, tn), lambda i, j, k: (k, j))
        dims = (((0,), (0,)), ((), ()))
    elif mode == "nt":
        M, K = a.shape
        N = b.shape[0]
        a_spec = pl.BlockSpec((tm, tk), lambda i, j, k: (i, k))
        b_spec = pl.BlockSpec((tn, tk), lambda i, j, k: (j, k))
        dims = (((1,), (1,)), ((), ()))
    else:
        M, K = a.shape
        N = b.shape[1]
        a_spec = pl.BlockSpec((tm, tk), lambda i, j, k: (i, k))
        b_spec = pl.BlockSpec((tk, tn), lambda i, j, k: (k, j))
        dims = (((1,), (0,)), ((), ()))
    assert M % tm == 0 and N % tn == 0 and K % tk == 0, (name, M, N, K, tm, tn, tk)
    nk = K // tk

    def body(a_ref, b_ref, o_ref, acc_ref):
        k = pl.program_id(2)
        part = lax.dot_general(a_ref[...].astype(BF), b_ref[...].astype(BF), dims, preferred_element_type=F32)
        if nk == 1:
            o_ref[...] = part.astype(o_ref.dtype)
        else:
            @pl.when(k == 0)
            def _():
                acc_ref[...] = part

            @pl.when(k > 0)
            def _():
                acc_ref[...] += part

            @pl.when(k == nk - 1)
            def _():
                o_ref[...] = acc_ref[...].astype(o_ref.dtype)

    return pl.pallas_call(
        body, grid=(M // tm, N // tn, nk), in_specs=[a_spec, b_spec],
        out_specs=pl.BlockSpec((tm, tn), lambda i, j, k: (i, j)),
        out_shape=jax.ShapeDtypeStruct((M, N), out_dtype),
        scratch_shapes=[pltpu.VMEM((tm, tn) if nk > 1 else (8, 128), F32)],
        name=name, compiler_params=pltpu.CompilerParams(dimension_semantics=("parallel", "parallel", "arbitrary")),
    )(a, b)


def _scan_fwd(a, b, *, tm, name):
    n, c = a.shape
    nt = n // tm

    def body(a_ref, b_ref, h_ref, carry):
        i = pl.program_id(0)
        av, bv = a_ref[...], b_ref[...]
        row = _rows(av)
        d = 1
        while d < tm:
            a_s = jnp.where(row >= d, pltpu.roll(av, d, axis=0), 1.0)
            b_s = jnp.where(row >= d, pltpu.roll(bv, d, axis=0), 0.0)
            bv = av * b_s + bv
            av = av * a_s
            d *= 2
        h = bv + av * jnp.where(i > 0, carry[HALO - 1:HALO, :], 0.0)
        h_ref[...] = h
        carry[...] = h[tm - HALO:tm]

    return pl.pallas_call(
        body, grid=(nt,), in_specs=[pl.BlockSpec((tm, c), lambda i: (i, 0))] * 2,
        out_specs=pl.BlockSpec((tm, c), lambda i: (i, 0)), out_shape=jax.ShapeDtypeStruct((n, c), F32),
        scratch_shapes=[pltpu.VMEM((HALO, c), F32)], name=name,
        compiler_params=pltpu.CompilerParams(dimension_semantics=("arbitrary",)),
    )(a, b)


def _scan_bwd(a, h, dh, *, tm, name):
    n, c = a.shape
    nt = n // tm
    tb = tm // HALO

    def body(a_ref, an_ref, h_ref, hp_ref, dh_ref, da_ref, db_ref, carry):
        i = pl.program_id(0)
        tile = nt - 1 - i
        av, hv, g = a_ref[...], h_ref[...], dh_ref[...]
        row = _rows(av)
        a_next = jnp.where(tile < nt - 1, an_ref[0:1, :], 0.0)
        au = jnp.where(row < tm - 1, pltpu.roll(av, tm - 1, axis=0), a_next)
        d = 1
        while d < tm:
            a_s = jnp.where(row < tm - d, pltpu.roll(au, tm - d, axis=0), 1.0)
            g_s = jnp.where(row < tm - d, pltpu.roll(g, tm - d, axis=0), 0.0)
            g = au * g_s + g
            au = au * a_s
            d *= 2
        g = g + au * jnp.where(i > 0, carry[0:1, :], 0.0)
        h_prev = jnp.where(row >= 1, pltpu.roll(hv, 1, axis=0), jnp.where(tile > 0, hp_ref[HALO - 1:HALO, :], 0.0))
        db_ref[...] = g
        da_ref[...] = g * h_prev
        carry[...] = g[0:HALO]

    cur = pl.BlockSpec((tm, c), lambda i: (nt - 1 - i, 0))
    nxt = pl.BlockSpec((HALO, c), lambda i: (jnp.minimum((nt - i) * tb, n // HALO - 1), 0))
    prv = pl.BlockSpec((HALO, c), lambda i: (jnp.maximum((nt - 1 - i) * tb - 1, 0), 0))
    return pl.pallas_call(
        body, grid=(nt,), in_specs=[cur, nxt, cur, prv, cur], out_specs=[cur, cur],
        out_shape=[jax.ShapeDtypeStruct((n, c), F32)] * 2, scratch_shapes=[pltpu.VMEM((HALO, c), F32)], name=name,
        compiler_params=pltpu.CompilerParams(dimension_semantics=("arbitrary",)),
    )(a, a, h, h, dh)


def _gdn_fwd(q, k, v, gb, *, name):
    n = q.shape[0]
    nc = n // CHUNK

    def body(q_ref, k_ref, v_ref, gb_ref, o_ref, s_ref, state):
        @pl.when(pl.program_id(0) == 0)
        def _():
            state[...] = jnp.zeros_like(state)

        gbv = gb_ref[...]
        for h in range(HEADS):
            cs = slice(h * HD, (h + 1) * HD)
            s0 = state[cs, :]
            s_ref[0, cs, :] = s0
            o, s1 = _f_gdn_chunk(h, q_ref[:, cs], k_ref[:, cs], v_ref[:, cs], gbv, s0)
            o_ref[:, cs] = o
            state[cs, :] = s1

    row = pl.BlockSpec((CHUNK, DG), lambda i: (i, 0))
    return pl.pallas_call(
        body, grid=(nc,), in_specs=[row, row, row, pl.BlockSpec((CHUNK, 128), lambda i: (i, 0))],
        out_specs=[row, pl.BlockSpec((1, DG, HD), lambda i: (i, 0, 0))],
        out_shape=[jax.ShapeDtypeStruct((n, DG), F32), jax.ShapeDtypeStruct((nc, DG, HD), F32)],
        scratch_shapes=[pltpu.VMEM((DG, HD), F32)], name=name,
        compiler_params=pltpu.CompilerParams(dimension_semantics=("arbitrary",)),
    )(q, k, v, gb)


def _gdn_bwd(q, k, v, gb, s_all, do, *, name):
    n = q.shape[0]
    nc = n // CHUNK

    def body(q_ref, k_ref, v_ref, gb_ref, s_ref, do_ref, dq_ref, dk_ref, dv_ref, dgb_ref, dstate):
        @pl.when(pl.program_id(0) == 0)
        def _():
            dstate[...] = jnp.zeros_like(dstate)

        gbv = gb_ref[...]
        dgb = jnp.zeros_like(gbv)
        for h in range(HEADS):
            cs = slice(h * HD, (h + 1) * HD)
            _, vjp = jax.vjp(functools.partial(_f_gdn_chunk, h), q_ref[:, cs], k_ref[:, cs], v_ref[:, cs], gbv, s_ref[0, cs, :])
            dq, dk, dv, dg, ds = vjp((do_ref[:, cs], dstate[cs, :]))
            dq_ref[:, cs] = dq
            dk_ref[:, cs] = dk
            dv_ref[:, cs] = dv
            dstate[cs, :] = ds
            dgb = dgb + dg
        dgb_ref[...] = dgb

    row = pl.BlockSpec((CHUNK, DG), lambda i: (nc - 1 - i, 0))
    gsp = pl.BlockSpec((CHUNK, 128), lambda i: (nc - 1 - i, 0))
    return pl.pallas_call(
        body, grid=(nc,), in_specs=[row, row, row, gsp, pl.BlockSpec((1, DG, HD), lambda i: (nc - 1 - i, 0, 0)), row],
        out_specs=[row, row, row, gsp],
        out_shape=[jax.ShapeDtypeStruct((n, DG), F32)] * 3 + [jax.ShapeDtypeStruct((n, 128), F32)],
        scratch_shapes=[pltpu.VMEM((DG, HD), F32)], name=name,
        compiler_params=pltpu.CompilerParams(dimension_semantics=("arbitrary",)),
    )(q, k, v, gb, s_all, do)


def _loss_head(x1, y2, w, tgt, *, tm, name):
    n, c = x1.shape

    def body(x_ref, y_ref, w_ref, t_ref, l_ref, d_ref):
        err = x_ref[...] + _rms(y_ref[...], w_ref[...]) - t_ref[...]
        part = jnp.sum(jnp.sum(err * err, axis=1, keepdims=True), axis=0, keepdims=True) * (0.5 / c)

        @pl.when(pl.program_id(0) == 0)
        def _():
            l_ref[...] = jnp.zeros_like(l_ref)

        l_ref[...] += part
        d_ref[...] = err * (1.0 / c)

    row = pl.BlockSpec((tm, c), lambda i: (i, 0))
    return pl.pallas_call(
        body, grid=(n // tm,), in_specs=[row, row, pl.BlockSpec((1, c), lambda i: (0, 0)), row],
        out_specs=[pl.BlockSpec((8, 128), lambda i: (0, 0)), row],
        out_shape=[jax.ShapeDtypeStruct((8, 128), F32), jax.ShapeDtypeStruct((n, c), F32)], name=name,
        compiler_params=pltpu.CompilerParams(dimension_semantics=("arbitrary",)),
    )(x1, y2, w, tgt)


def _sum_parts(own, recv, *, tr, name):
    r, c = own.shape
    p = recv.shape[0]

    def body(o_ref, r_ref, s_ref):
        s = o_ref[...]
        for q in range(p):
            s = s + r_ref[q].astype(F32)
        s_ref[...] = s

    return pl.pallas_call(
        body, grid=(r // tr,), in_specs=[pl.BlockSpec((tr, c), lambda i: (i, 0)), pl.BlockSpec((p, tr, c), lambda i: (0, i, 0))],
        out_specs=pl.BlockSpec((tr, c), lambda i: (i, 0)), out_shape=jax.ShapeDtypeStruct((r, c), F32), name=name,
        compiler_params=pltpu.CompilerParams(dimension_semantics=("parallel",)),
    )(own, recv)


def _sum_slots(buf, *, name):
    p, r, c = buf.shape

    def body(b_ref, s_ref):
        s = b_ref[0]
        for q in range(1, p):
            s = s + b_ref[q]
        s_ref[...] = s

    return pl.pallas_call(body, out_shape=jax.ShapeDtypeStruct((r, c), F32), name=name)(buf)


def _adamw(w, m, v, gs, *, tr, name):
    r, c = w.shape
    ng = len(gs)
    c1 = 1.0 - ADAM_B1 ** ADAM_STEP
    c2 = 1.0 - ADAM_B2 ** ADAM_STEP

    def body(*refs):
        w_ref, m_ref, v_ref = refs[:3]
        g = refs[3][...]
        for q in range(1, ng):
            g = g + refs[3 + q][...]
        g_ref, d_ref, nm_ref, nv_ref = refs[3 + ng:]
        nm = ADAM_B1 * m_ref[...] + (1.0 - ADAM_B1) * g
        nv = ADAM_B2 * v_ref[...] + (1.0 - ADAM_B2) * (g * g)
        g_ref[...] = g
        d_ref[...] = -ADAM_LR * ((nm / c1) / (jnp.sqrt(nv / c2) + ADAM_EPS) + ADAM_WD * w_ref[...])
        nm_ref[...] = nm
        nv_ref[...] = nv

    blk = pl.BlockSpec((tr, c), lambda i: (i, 0))
    return pl.pallas_call(
        body, grid=(r // tr,), in_specs=[blk] * (3 + ng), out_specs=[blk] * 4,
        out_shape=[jax.ShapeDtypeStruct((r, c), F32)] * 4, name=name,
        compiler_params=pltpu.CompilerParams(dimension_semantics=("parallel",)),
    )(w, m, v, *gs)


_ANY = pl.BlockSpec(memory_space=pl.ANY)
_CHIP_REL = ((1, 0), (0, 1), (1, 1))
_DEV_REL = tuple((r >> 2 & 1, r >> 1 & 1, r & 1) for r in range(1, 8))


def _flip(v, f):
    return 1 - v if f else v


def _gather_chips(shards, *, name):
    na = len(shards)

    def body(*refs):
        ins, outs = refs[:na], refs[na:2 * na]
        send, recv, lsem = refs[2 * na:]
        x, y, c = lax.axis_index("x"), lax.axis_index("y"), lax.axis_index("c")
        me = 2 * x + y
        started = []
        for a in range(na):
            loc = pltpu.make_async_copy(ins[a], outs[a].at[me], lsem.at[a])
            loc.start()
            started.append(loc)
        remote = []
        for a in range(na):
            for k, (fx, fy) in enumerate(_CHIP_REL):
                px, py = _flip(x, fx), _flip(y, fy)
                cp = pltpu.make_async_remote_copy(
                    src_ref=ins[a], dst_ref=outs[a].at[me], send_sem=send.at[3 * a + k], recv_sem=recv.at[3 * a + k],
                    device_id=(px, py, c), device_id_type=MESH_ID)
                cp.start()
                arrive = pltpu.make_async_remote_copy(
                    src_ref=ins[a], dst_ref=outs[a].at[2 * px + py], send_sem=send.at[3 * a + k], recv_sem=recv.at[3 * a + k],
                    device_id=(px, py, c), device_id_type=MESH_ID)
                remote.append((cp, arrive))
        for cp, arrive in remote:
            arrive.wait_recv()
        for cp, arrive in remote:
            cp.wait_send()
        for loc in started:
            loc.wait()

    return pl.pallas_call(
        body, in_specs=[_ANY] * na, out_specs=[_ANY] * na,
        out_shape=[jax.ShapeDtypeStruct((4,) + s.shape, s.dtype) for s in shards],
        scratch_shapes=[pltpu.SemaphoreType.DMA((3 * na,)), pltpu.SemaphoreType.DMA((3 * na,)), pltpu.SemaphoreType.DMA((na,))],
        name=name,
    )(*shards)


def _scatter_chips(blocks, *, name):
    na = len(blocks)

    def body(*refs):
        ins, outs = refs[:na], refs[na:2 * na]
        send, recv = refs[2 * na:]
        x, y, c = lax.axis_index("x"), lax.axis_index("y"), lax.axis_index("c")
        copies = []
        for a in range(na):
            for k, (fx, fy) in enumerate(_CHIP_REL):
                px, py = _flip(x, fx), _flip(y, fy)
                cp = pltpu.make_async_remote_copy(
                    src_ref=ins[a].at[2 * px + py], dst_ref=outs[a].at[k], send_sem=send.at[3 * a + k], recv_sem=recv.at[3 * a + k],
                    device_id=(px, py, c), device_id_type=MESH_ID)
                cp.start()
                copies.append(cp)
        for cp in copies:
            cp.wait_recv()
        for cp in copies:
            cp.wait_send()

    return pl.pallas_call(
        body, in_specs=[_ANY] * na, out_specs=[_ANY] * na,
        out_shape=[jax.ShapeDtypeStruct((3,) + b.shape[1:], b.dtype) for b in blocks],
        scratch_shapes=[pltpu.SemaphoreType.DMA((3 * na,)), pltpu.SemaphoreType.DMA((3 * na,))], name=name,
    )(*blocks)


def _swap_cores(arrs, *, name):
    na = len(arrs)

    def body(*refs):
        ins, outs = refs[:na], refs[na:2 * na]
        send, recv = refs[2 * na:]
        sib = (lax.axis_index("x"), lax.axis_index("y"), 1 - lax.axis_index("c"))
        copies = []
        for a in range(na):
            cp = pltpu.make_async_remote_copy(src_ref=ins[a], dst_ref=outs[a], send_sem=send.at[a], recv_sem=recv.at[a],
                                              device_id=sib, device_id_type=MESH_ID)
            cp.start()
            copies.append(cp)
        for cp in copies:
            cp.wait_recv()
        for cp in copies:
            cp.wait_send()

    return pl.pallas_call(
        body, in_specs=[_ANY] * na, out_specs=[_ANY] * na,
        out_shape=[jax.ShapeDtypeStruct(a.shape, a.dtype) for a in arrs],
        scratch_shapes=[pltpu.SemaphoreType.DMA((na,)), pltpu.SemaphoreType.DMA((na,))], name=name,
    )(*arrs)


def _gather_devices(buf, *, name):
    def body(in_ref, out_ref, send, recv, lsem):
        x, y, c = lax.axis_index("x"), lax.axis_index("y"), lax.axis_index("c")
        me = 4 * x + 2 * y + c
        loc = pltpu.make_async_copy(in_ref, out_ref.at[me], lsem)
        loc.start()
        pairs = []
        for k, (fx, fy, fc) in enumerate(_DEV_REL):
            px, py, pc = _flip(x, fx), _flip(y, fy), _flip(c, fc)
            cp = pltpu.make_async_remote_copy(src_ref=in_ref, dst_ref=out_ref.at[me], send_sem=send.at[k], recv_sem=recv.at[k],
                                              device_id=(px, py, pc), device_id_type=MESH_ID)
            cp.start()
            arrive = pltpu.make_async_remote_copy(src_ref=in_ref, dst_ref=out_ref.at[4 * px + 2 * py + pc], send_sem=send.at[k],
                                                  recv_sem=recv.at[k], device_id=(px, py, pc), device_id_type=MESH_ID)
            pairs.append((cp, arrive))
        for cp, arrive in pairs:
            arrive.wait_recv()
        for cp, arrive in pairs:
            cp.wait_send()
        loc.wait()

    return pl.pallas_call(
        body, in_specs=[_ANY], out_specs=_ANY, out_shape=jax.ShapeDtypeStruct((8,) + buf.shape, buf.dtype),
        scratch_shapes=[pltpu.SemaphoreType.DMA((7,)), pltpu.SemaphoreType.DMA((7,)), pltpu.SemaphoreType.DMA(())], name=name,
    )(buf)


def _pad_cols(w):
    z = jnp.zeros(w.shape[:-1] + (NP - N_IN,), w.dtype)
    return jnp.concatenate([w[..., 0:2048], w[..., 2056:N_IN], w[..., 2048:2056], z], axis=-1)


def _unpad_cols(w):
    return jnp.concatenate([w[..., 0:2048], w[..., OFF_BA:OFF_BA + 8], w[..., 2048:OFF_BA]], axis=-1)


def _lanes(v, off):
    return jnp.pad(v.reshape(1, -1), ((0, 0), (off, 128 - off - v.size)))


def _block_diag(w):
    eye = jnp.eye(8, dtype=w.dtype)
    return (w[:, :, None, :] * eye[:, None, :, None]).reshape(DG, DG)


def _diag_blocks(w):
    return jnp.stack([w[h * 64:(h + 1) * 64, h * 64:(h + 1) * 64] for h in range(8)])


def _mixer_params(p):
    return dict(
        gdn_conv_w=p["gdn_conv_w"], alog=_lanes(p["gdn_a_log"], HEADS), dtb=_lanes(p["gdn_dt_bias"], HEADS),
        gdn_nw=p["gdn_norm_w"].reshape(1, HD),
        lru_conv_w=p["lru_conv_w"], lru_conv_b=p["lru_conv_b"].reshape(1, DG),
        wa=_block_diag(p["lru_wa"]), ba=p["lru_ba"].reshape(1, DG), wx=_block_diag(p["lru_wx"]), bx=p["lru_bx"].reshape(1, DG),
        lam=p["lru_lambda"].reshape(1, DG),
        ln_w=p["sgu_ln_w"].reshape(1, DG), ln_b=p["sgu_ln_b"].reshape(1, DG), ws=p["sgu_ws"].reshape(DG, 128),
        bst=jnp.pad(p["sgu_b"].T, ((0, 0), (0, 124))),
        sconv_w=p["sconv_w"], gw0=p["grp_norm_w"][0:1], gw1=p["grp_norm_w"][1:2], gw2=p["grp_norm_w"][2:3],
    )


def _mixer_param_grads(g):
    return dict(
        gdn_conv_w=g["gdn_conv_w"], gdn_a_log=g["alog"][0, HEADS:2 * HEADS], gdn_dt_bias=g["dtb"][0, HEADS:2 * HEADS],
        gdn_norm_w=g["gdn_nw"][0],
        lru_conv_w=g["lru_conv_w"], lru_conv_b=g["lru_conv_b"][0],
        lru_wa=_diag_blocks(g["wa"]), lru_ba=g["ba"].reshape(8, 64), lru_wx=_diag_blocks(g["wx"]), lru_bx=g["bx"].reshape(8, 64),
        lru_lambda=g["lam"][0],
        sgu_ln_w=g["ln_w"][0], sgu_ln_b=g["ln_b"][0], sgu_ws=g["ws"].reshape(4, 128, 128), sgu_b=g["bst"][:, 0:4].T,
        sconv_w=g["sconv_w"], grp_norm_w=jnp.concatenate([g["gw0"], g["gw1"], g["gw2"]], axis=0),
    )


TM_MIX = 256


def _mixers_fwd(p, mp, tag=""):
    c = lambda *names: [(mp[n], False) for n in names]
    q, k, v, gb = _rowwise(_f_gdn_pre, [(p, 0, 1536, True), (p, OFF_BA // 128, 128, False)], c("gdn_conv_w", "alog", "dtb"),
                           [(DG, F32)] * 3 + [(128, F32)], tm=TM_MIX, name="gdn_pre" + tag)
    o, s_all = _gdn_fwd(q, k, v, gb, name="gdn_chunks" + tag)
    y_a, = _rowwise(_f_gdn_post, [(o, 0, DG, False), (p, OFF_Z // DG, DG, False)], c("gdn_nw"), [(DG, BF)], tm=TM_MIX, name="gdn_post" + tag)
    a, b = _rowwise(_f_lru_ab, [(p, OFF_LX // DG, DG, True)], c("lru_conv_w", "lru_conv_b", "wa", "ba", "wx", "bx", "lam"),
                    [(DG, F32)] * 2, tm=TM_MIX, name="lru_ab" + tag)
    hs = _scan_fwd(a, b, tm=TM_MIX, name="lru_scan" + tag)
    y_b, = _rowwise(_f_lru_post, [(hs, 0, DG, False), (p, OFF_LG // DG, DG, False)], c("gw0"), [(DG, BF)], tm=TM_MIX, name="lru_post" + tag)
    y_c, = _rowwise(_f_sgu, [(p, OFF_UV // 1024, 1024, False)], c("ln_w", "ln_b", "ws", "bst", "gw1"), [(DG, BF)], tm=TM_MIX, name="sgu" + tag)
    y_d, = _rowwise(_f_sconv, [(p, OFF_SB // DG, DG, False), (p, OFF_SC // DG, DG, True), (p, OFF_SH // DG, DG, True)],
                    c("sconv_w", "gw2"), [(DG, BF)], tm=TM_MIX, name="sconv" + tag)
    return jnp.concatenate([y_a, y_b, y_c, y_d], axis=1), (q, k, v, gb, o, s_all, a, hs)


def _mixers_bwd(p, mp, saved, dy, tag=""):
    q, k, v, gb, o, s_all, a, hs = saved
    c = lambda *names: [(mp[n], False, True) for n in names]
    g = {}

    (do, dz), (g["gdn_nw"],) = _rowwise_vjp(
        _f_gdn_post, [(o, 0, DG, False, F32), (p, OFF_Z // DG, DG, False, BF)], c("gdn_nw"), [(dy, 0, DG)], tm=TM_MIX, name="gdn_post_b" + tag)
    dq, dk, dv, dgb = _gdn_bwd(q, k, v, gb, s_all, do, name="gdn_chunks_b" + tag)
    (dqkv, dba), (g["gdn_conv_w"], g["alog"], g["dtb"]) = _rowwise_vjp(
        _f_gdn_pre, [(p, 0, 1536, True, BF), (p, OFF_BA // 128, 128, False, BF)], c("gdn_conv_w", "alog", "dtb"),
        [(dq, 0, DG), (dk, 0, DG), (dv, 0, DG), (dgb, 0, 128)], tm=TM_MIX, name="gdn_pre_b" + tag)

    (dhs, dgate), (g["gw0"],) = _rowwise_vjp(
        _f_lru_post, [(hs, 0, DG, False, F32), (p, OFF_LG // DG, DG, False, BF)], c("gw0"), [(dy, 1, DG)], tm=TM_MIX, name="lru_post_b" + tag)
    da, db = _scan_bwd(a, hs, dhs, tm=TM_MIX, name="lru_scan_b" + tag)
    (dlx,), (g["lru_conv_w"], g["lru_conv_b"], g["wa"], g["ba"], g["wx"], g["bx"], g["lam"]) = _rowwise_vjp(
        _f_lru_ab, [(p, OFF_LX // DG, DG, True, BF)], c("lru_conv_w", "lru_conv_b", "wa", "ba", "wx", "bx", "lam"),
        [(da, 0, DG), (db, 0, DG)], tm=TM_MIX, name="lru_ab_b" + tag)

    (duv,), (g["ln_w"], g["ln_b"], g["ws"], g["bst"], g["gw1"]) = _rowwise_vjp(
        _f_sgu, [(p, OFF_UV // 1024, 1024, False, BF)], c("ln_w", "ln_b", "ws", "bst", "gw1"), [(dy, 2, DG)], tm=TM_MIX, name="sgu_b" + tag)

    (dsb, dsc, dsh), (g["sconv_w"], g["gw2"]) = _rowwise_vjp(
        _f_sconv, [(p, OFF_SB // DG, DG, False, BF), (p, OFF_SC // DG, DG, True, BF), (p, OFF_SH // DG, DG, True, BF)],
        c("sconv_w", "gw2"), [(dy, 3, DG)], tm=TM_MIX, name="sconv_b" + tag)

    dp = jnp.concatenate([dqkv, dz, dlx, dgate, duv, dsb, dsc, dsh, dba], axis=1)
    return dp, g


WEIGHTS = ("pre_mix_norm", "w_in", "gdn_conv_w", "gdn_a_log", "gdn_dt_bias", "gdn_norm_w", "lru_conv_w", "lru_conv_b", "lru_wa",
           "lru_ba", "lru_wx", "lru_bx", "lru_lambda", "sgu_ln_w", "sgu_ln_b", "sgu_ws", "sgu_b", "sconv_w", "grp_norm_w", "w_out",
           "post_mix_norm", "pre_ffn_norm", "ffn_up", "ffn_conv_w", "ffn_conv_b", "ffn_down", "post_ffn_norm")
BIG = ("w_in", "ffn_up", "w_out", "ffn_down")
CHIP_SHARDED_SMALL = ("gdn_conv_w", "lru_conv_w", "sconv_w", "grp_norm_w", "ffn_conv_w")
MIXER_PARAMS = ("gdn_conv_w", "gdn_a_log", "gdn_dt_bias", "gdn_norm_w", "lru_conv_w", "lru_conv_b", "lru_wa", "lru_ba", "lru_wx",
                "lru_bx", "lru_lambda", "sgu_ln_w", "sgu_ln_b", "sgu_ws", "sgu_b", "sconv_w", "grp_norm_w")
TM_ROW = 256
N_FF_TILES = 11
FF_TILE = D_FF // N_FF_TILES
PACK_ROWS = 256


def _pack(arrs):
    flat = jnp.concatenate([a.reshape(-1) for a in arrs])
    n = flat.size
    rows = -(-n // (128 * PACK_ROWS)) * PACK_ROWS
    return jnp.pad(flat, (0, rows * 128 - n)).reshape(rows, 128)


def _unpack(buf, shapes):
    flat = buf.reshape(-1)
    out, off = [], 0
    for s in shapes:
        n = math.prod(s)
        out.append(flat[off:off + n].reshape(s))
        off += n
    return out


def _cols_to_blocks(w):
    l, r, c4 = w.shape
    return w.reshape(l, r, 4, c4 // 4).transpose(2, 0, 1, 3).reshape(4, l * r, c4 // 4)


def _rows_to_blocks(w):
    l, r4, c = w.shape
    return w.reshape(l, 4, r4 // 4, c).transpose(1, 0, 2, 3).reshape(4, l * (r4 // 4), c)


def _layer_fwd(l, xs, h, wt, sp, mp):
    t = str(l)
    p = _matmul(h, wt["w_in"][l], "nn", F32, tm=512, tn=1152, tk=D, name="mm_in" + t)
    ycat, saved = _mixers_fwd(p, mp, tag=t)
    y = _matmul(ycat, wt["w_out"][l], "nn", F32, tm=512, tn=1024, tk=D, name="mm_out" + t)
    x1, h2 = _rowwise(_f_post_pre, [(xs, 0, D, False), (y, 0, D, False)], [(sp["post_mix_norm"][l], False), (sp["pre_ffn_norm"][l], False)],
                      [(D, F32), (D, BF)], tm=TM_ROW, name="post_mix" + t)
    u = _matmul(h2, wt["ffn_up"][l], "nn", F32, tm=512, tn=1024, tk=D, name="mm_up" + t)
    act, = _rowwise(_f_ffn_act, [(u, 0, FF_TILE, True), (u, N_FF_TILES, FF_TILE, True)], [(c, True) for c in sp["ffn_conv"][l]],
                    [(FF_TILE, BF)], tm=TM_ROW, ncol=N_FF_TILES, name="ffn_act" + t)
    y2 = _matmul(act, wt["ffn_down"][l], "nn", F32, tm=512, tn=1024, tk=1408, name="mm_down" + t)
    return dict(xs=xs, h=h, p=p, saved=saved, ycat=ycat, y=y, x1=x1, h2=h2, u=u, act=act, y2=y2)


def _layer_bwd(l, a, dx1, dy2, wt, sp, mp):
    t = str(l)
    g = {}
    dact = _matmul(dy2, wt["ffn_down"][l], "nt", F32, tm=512, tn=1408, tk=D, name="mm_down_dx" + t)
    g["ffn_down"] = _matmul(a["act"], dy2, "tn", F32, tm=1408, tn=1024, tk=512, name="mm_down_dw" + t)
    (dug, duv), gc = _rowwise_vjp(
        _f_ffn_act, [(a["u"], 0, FF_TILE, True, BF), (a["u"], N_FF_TILES, FF_TILE, True, BF)], [(c, True, True) for c in sp["ffn_conv"][l]],
        [(dact, 0, FF_TILE)], tm=TM_ROW, ncol=N_FF_TILES, name="ffn_act_b" + t)
    g["ffn_conv_w"] = jnp.concatenate([gc[0], gc[1]], axis=1)
    g["ffn_conv_b"] = jnp.concatenate([gc[2], gc[3]], axis=1)[0]
    du = jnp.concatenate([dug, duv], axis=1)
    dh2 = _matmul(du, wt["ffn_up"][l], "nt", F32, tm=512, tn=1024, tk=2816, name="mm_up_dx" + t)
    g["ffn_up"] = _matmul(a["h2"], du, "tn", F32, tm=1024, tn=1408, tk=512, name="mm_up_dw" + t)
    (dxs, dy), (gpm, gpf) = _rowwise_vjp(
        _f_post_pre, [(a["xs"], 0, D, False, F32), (a["y"], 0, D, False, BF)],
        [(sp["post_mix_norm"][l], False, True), (sp["pre_ffn_norm"][l], False, True)], [(dx1, 0, D), (dh2, 0, D)], tm=TM_ROW, name="post_mix_b" + t)
    g["post_mix_norm"], g["pre_ffn_norm"] = gpm[0], gpf[0]
    dycat = _matmul(dy, wt["w_out"][l], "nt", F32, tm=512, tn=1024, tk=D, name="mm_out_dx" + t)
    g["w_out"] = _matmul(a["ycat"], dy, "tn", F32, tm=1024, tn=1024, tk=512, name="mm_out_dw" + t)
    dp, gm = _mixers_bwd(a["p"], mp, a["saved"], dycat, tag=t)
    g.update(_mixer_param_grads(gm))
    dh = _matmul(dp, wt["w_in"][l], "nt", F32, tm=512, tn=1024, tk=1152, name="mm_in_dx" + t)
    g["w_in"] = _matmul(a["h"], dp, "tn", F32, tm=1024, tn=1152, tk=512, name="mm_in_dw" + t)
    return dxs, dh, g


def kernel(x, pre_mix_norm, w_in, gdn_conv_w, gdn_a_log, gdn_dt_bias, gdn_norm_w, lru_conv_w, lru_conv_b, lru_wa, lru_ba, lru_wx, lru_bx, lru_lambda, sgu_ln_w, sgu_ln_b, sgu_ws, sgu_b, sconv_w, grp_norm_w, w_out, post_mix_norm, pre_ffn_norm, ffn_up, ffn_conv_w, ffn_conv_b, ffn_down, post_ffn_norm, loss_target, m_pre_mix_norm, m_w_in, m_gdn_conv_w, m_gdn_a_log, m_gdn_dt_bias, m_gdn_norm_w, m_lru_conv_w, m_lru_conv_b, m_lru_wa, m_lru_ba, m_lru_wx, m_lru_bx, m_lru_lambda, m_sgu_ln_w, m_sgu_ln_b, m_sgu_ws, m_sgu_b, m_sconv_w, m_grp_norm_w, m_w_out, m_post_mix_norm, m_pre_ffn_norm, m_ffn_up, m_ffn_conv_w, m_ffn_conv_b, m_ffn_down, m_post_ffn_norm, v_pre_mix_norm, v_w_in, v_gdn_conv_w, v_gdn_a_log, v_gdn_dt_bias, v_gdn_norm_w, v_lru_conv_w, v_lru_conv_b, v_lru_wa, v_lru_ba, v_lru_wx, v_lru_bx, v_lru_lambda, v_sgu_ln_w, v_sgu_ln_b, v_sgu_ws, v_sgu_b, v_sconv_w, v_grp_norm_w, v_w_out, v_post_mix_norm, v_pre_ffn_norm, v_ffn_up, v_ffn_conv_w, v_ffn_conv_b, v_ffn_down, v_post_ffn_norm):
    given = dict(locals())
    me = 2 * lax.axis_index("x") + lax.axis_index("y")
    xs0, tgt = x[0], loss_target[0]

    small_sh = [given[n] for n in CHIP_SHARDED_SMALL]
    got = _gather_chips([w_in.astype(BF), ffn_up.astype(BF), w_out.astype(BF), ffn_down.astype(BF), _pack(small_sh)], name="gather_weights")
    wt = {
        "w_in": _pad_cols(got[0].transpose(1, 2, 0, 3).reshape(DEPTH, D, N_IN)),
        "ffn_up": got[1].transpose(1, 2, 0, 3).reshape(DEPTH, D, 2 * D_FF),
        "w_out": got[2].transpose(1, 0, 2, 3).reshape(DEPTH, D, D),
        "ffn_down": got[3].transpose(1, 0, 2, 3).reshape(DEPTH, D_FF, D),
    }
    full = {n: given[n] for n in WEIGHTS if n not in BIG and n not in CHIP_SHARDED_SMALL}
    per_chip = [_unpack(got[4][j], [s.shape for s in small_sh]) for j in range(4)]
    parts = [jnp.stack([per_chip[j][i] for j in range(4)]) for i in range(len(small_sh))]
    for n, pj in zip(CHIP_SHARDED_SMALL, parts):
        full[n] = pj.transpose(1, 2, 0, 3).reshape(pj.shape[1], pj.shape[2], 4 * pj.shape[3])
    sp = {n: [full[n][l:l + 1] for l in range(DEPTH)] for n in ("pre_mix_norm", "post_mix_norm", "pre_ffn_norm", "post_ffn_norm")}
    sp["ffn_conv"] = [[full["ffn_conv_w"][l][:, :D_FF], full["ffn_conv_w"][l][:, D_FF:], full["ffn_conv_b"][l:l + 1, :D_FF],
                       full["ffn_conv_b"][l:l + 1, D_FF:]] for l in range(DEPTH)]
    mps = [_mixer_params({n: full[n][l] for n in MIXER_PARAMS}) for l in range(DEPTH)]

    h, = _rowwise(_f_pre, [(xs0, 0, D, False)], [(sp["pre_mix_norm"][0], False)], [(D, BF)], tm=TM_ROW, name="pre_mix0")
    a0 = _layer_fwd(0, xs0, h, wt, sp, mps[0])
    xs1, h1 = _rowwise(_f_post_pre, [(a0["x1"], 0, D, False), (a0["y2"], 0, D, False)],
                       [(sp["post_ffn_norm"][0], False), (sp["pre_mix_norm"][1], False)], [(D, F32), (D, BF)], tm=TM_ROW, name="post_ffn0")
    a1 = _layer_fwd(1, xs1, h1, wt, sp, mps[1])
    lacc, dxo = _loss_head(a1["x1"], a1["y2"], sp["post_ffn_norm"][1], tgt, tm=TM_ROW, name="loss_head")

    gl = [None, None]
    (dx1, dy2), (gpf1,) = _rowwise_vjp(_f_post, [(a1["x1"], 0, D, False, F32), (a1["y2"], 0, D, False, BF)],
                                      [(sp["post_ffn_norm"][1], False, True)], [(dxo, 0, D)], tm=TM_ROW, name="post_ffn1_b")
    dxs1, dh1, gl[1] = _layer_bwd(1, a1, dx1, dy2, wt, sp, mps[1])
    gl[1]["post_ffn_norm"] = gpf1[0]
    (dx1, dy2), (gpf0, gpm1) = _rowwise_vjp(
        _f_post_pre, [(a0["x1"], 0, D, False, F32), (a0["y2"], 0, D, False, BF)],
        [(sp["post_ffn_norm"][0], False, True), (sp["pre_mix_norm"][1], False, True)], [(dxs1, 0, D), (dh1, 0, D)], tm=TM_ROW, name="post_ffn0_b")
    gl[1]["pre_mix_norm"] = gpm1[0]
    dxs0, dh0, gl[0] = _layer_bwd(0, a0, dx1, dy2, wt, sp, mps[0])
    gl[0]["post_ffn_norm"] = gpf0[0]
    (grad_x,), (gpm0,) = _rowwise_vjp(lambda xv, w: (xv, _rms(xv, w)), [(xs0, 0, D, False, F32)], [(sp["pre_mix_norm"][0], False, True)],
                                     [(dxs0, 0, D), (dh0, 0, D)], tm=TM_ROW, name="pre_mix0_b")
    gl[0]["pre_mix_norm"] = gpm0[0]
    gfull = {n: jnp.stack([gl[0][n], gl[1][n]]) for n in WEIGHTS}

    small = [n for n in WEIGHTS if n not in BIG]
    tot = _sum_slots(_gather_devices(_pack([gfull[n] for n in small] + [lacc[0, 0:1]]), name="gather_small_grads"), name="sum_small_grads")
    red = dict(zip(small + ["loss"], _unpack(tot, [gfull[n].shape for n in small] + [(1,)])))
    for n in CHIP_SHARDED_SMALL:
        cb = given[n].shape[-1]
        red[n] = lax.dynamic_slice_in_dim(red[n], me * cb, cb, axis=red[n].ndim - 1)
    shapes = [given[n].shape for n in small]
    res = _adamw(_pack([given[n] for n in small]), _pack([given["m_" + n] for n in small]), _pack([given["v_" + n] for n in small]),
                 [_pack([red[n] for n in small])], tr=PACK_ROWS, name="adamw_small")
    outs = {kind: dict(zip(small, _unpack(r, shapes))) for kind, r in zip(("grad", "delta", "new_m", "new_v"), res)}

    blocks = {"w_in": _cols_to_blocks(_unpad_cols(gfull["w_in"])), "ffn_up": _cols_to_blocks(gfull["ffn_up"]),
              "w_out": _rows_to_blocks(gfull["w_out"]), "ffn_down": _rows_to_blocks(gfull["ffn_down"])}
    recv = _scatter_chips([blocks[n].astype(BF) for n in BIG], name="scatter_grads")
    sums = [_sum_parts(lax.dynamic_index_in_dim(blocks[n], me, 0, keepdims=False), r, tr=256, name="sum_grads_" + n) for n, r in zip(BIG, recv)]
    other = _swap_cores(sums, name="swap_grad_sums")
    for n, s, o in zip(BIG, sums, other):
        shp = given[n].shape
        two = lambda z: z.reshape(shp[0] * shp[1], shp[2])
        r4 = _adamw(two(given[n]), two(given["m_" + n]), two(given["v_" + n]), [s, o], tr=256, name="adamw_" + n)
        for kind, r in zip(("grad", "delta", "new_m", "new_v"), r4):
            outs[kind][n] = r.reshape(shp)

    return (red["loss"][0], grad_x[None], *[outs[k][n] for k in ("grad", "delta", "new_m", "new_v") for n in WEIGHTS])
```

```python
import functools
import math

import jax
import jax.numpy as jnp
from jax import lax
from jax.experimental import pallas as pl
from jax.experimental.pallas import tpu as pltpu

F32 = jnp.float32
BF = jnp.bfloat16
MESH_ID = pl.DeviceIdType.MESH

EPS = 1e-6
DEPTH = 2
D = 2048
DG = 512
HEADS = 4
HD = 128
CHUNK = 64
LRU_C = 8.0
D_FF = 5632
N_IN = 5640
NP = 5760
OFF_Q, OFF_Z, OFF_LX, OFF_LG, OFF_UV, OFF_SB, OFF_SC, OFF_SH, OFF_BA = 0, 1536, 2048, 2560, 3072, 4096, 4608, 5120, 5632

ADAM_LR, ADAM_B1, ADAM_B2, ADAM_EPS, ADAM_WD, ADAM_STEP = 0.001, 0.9, 0.999, 1e-08, 0.01, 10

HALO = 8


def _mk_bdot(ca, cb):
    na, nb = 1 - ca, 1 - cb

    def dg(x, y, cx, cy):
        return lax.dot_general(x.astype(BF), y.astype(BF), (((cx,), (cy,)), ((), ())), preferred_element_type=F32)

    @jax.custom_vjp
    def f(a, b):
        return dg(a, b, ca, cb)

    def fwd(a, b):
        return dg(a, b, ca, cb), (a, b)

    def bwd(res, g):
        a, b = res
        da = dg(g, b, 1, nb) if ca == 1 else dg(b, g, nb, 1)
        db = dg(a, g, na, 0) if cb == 0 else dg(g, a, 0, na)
        return da.astype(a.dtype), db.astype(b.dtype)

    f.defvjp(fwd, bwd)
    return f


_bdot = _mk_bdot(1, 0)
_bdot_nt = _mk_bdot(1, 1)
_bdot_tn = _mk_bdot(0, 0)


def _fdot(a, b, ca=1, cb=0):
    return lax.dot_general(a, b, (((ca,), (cb,)), ((), ())), precision=lax.Precision.HIGHEST, preferred_element_type=F32)


def _sigmoid(x):
    return 1.0 / (1.0 + jnp.exp(-x))


def _silu(x):
    return x * _sigmoid(x)


def _gelu(x):
    return 0.5 * x * (1.0 + jnp.tanh(0.7978845608028654 * (x + 0.044715 * (x * x * x))))


def _log1p(z):
    u = 1.0 + z
    d = u - 1.0
    return jnp.where(d == 0.0, z, jnp.log(u) * (z / jnp.where(d == 0.0, 1.0, d)))


def _softplus(x):
    return jnp.maximum(x, 0.0) + _log1p(jnp.exp(-jnp.abs(x)))


def _neg_expm1(y):
    t = jnp.tanh(0.5 * y)
    return -2.0 * t / (1.0 - t)


def _rms(x, w):
    return x * lax.rsqrt(jnp.mean(x * x, axis=-1, keepdims=True) + EPS) * w


def _rows(x):
    return lax.broadcasted_iota(jnp.int32, x.shape, 0)


def _mk_shift():
    @functools.partial(jax.custom_vjp, nondiff_argnums=(1,))
    def shift(xx, s):
        n = xx.shape[0]
        return pltpu.roll(xx, s, axis=0)[HALO:n] if s else xx[HALO:n]

    def fwd(xx, s):
        return shift(xx, s), None

    def bwd(s, _, g):
        ext = jnp.concatenate([g, jnp.zeros((HALO, g.shape[1]), g.dtype)], axis=0)
        return (pltpu.roll(ext, HALO - s, axis=0),)

    shift.defvjp(fwd, bwd)
    return shift


_shift = _mk_shift()


def _causal_conv(xx, w):
    K = w.shape[0]
    y = _shift(xx, K - 1) * w[0:1, :]
    for k in range(1, K):
        y = y + _shift(xx, K - 1 - k) * w[k:k + 1, :]
    return y


def _f_pre(x, w):
    return (_rms(x, w),)


def _f_post_pre(x, y, w_post, w_pre):
    x1 = x + _rms(y, w_post)
    return x1, _rms(x1, w_pre)


def _f_post(x, y, w_post):
    return (x + _rms(y, w_post),)


def _heads(fn, *xs):
    return jnp.concatenate([fn(*[x[:, h * HD:(h + 1) * HD] for x in xs]) for h in range(HEADS)], axis=1)


def _l2n(t):
    return t * lax.rsqrt(jnp.sum(t * t, axis=-1, keepdims=True) + EPS)


def _f_gdn_pre(qkv, ba, conv_w, alog, dtb):
    c = _silu(_causal_conv(qkv, conv_w))
    q = _heads(lambda t: _l2n(t) * (HD ** -0.5), c[:, 0:DG])
    k = _heads(_l2n, c[:, DG:2 * DG])
    v = c[:, 2 * DG:3 * DG]
    lane = lax.broadcasted_iota(jnp.int32, ba.shape, 1)
    beta = _sigmoid(ba)
    g = -jnp.exp(alog) * _softplus(ba + dtb)
    gb = jnp.where(lane < HEADS, beta, jnp.where(lane < 2 * HEADS, g, 0.0))
    return q, k, v, gb


def _f_gdn_chunk(h, q, k, v, gb, s0):
    C = q.shape[0]
    r = lax.broadcasted_iota(jnp.int32, (C, C), 0)
    c = lax.broadcasted_iota(jnp.int32, (C, C), 1)
    causal = r >= c
    strict = r > c
    lane = lax.broadcasted_iota(jnp.int32, gb.shape, 1)
    gcum_c = _fdot(causal.astype(F32), gb)
    gcum_r = _fdot(gb, (r <= c).astype(F32), 0, 0)
    sub = lax.broadcasted_iota(jnp.int32, gcum_r.shape, 0)
    gcol = jnp.sum(jnp.where(lane == HEADS + h, gcum_c, 0.0), axis=1, keepdims=True)
    grow = jnp.sum(jnp.where(sub == HEADS + h, gcum_r, 0.0), axis=0, keepdims=True)
    bcol = jnp.sum(jnp.where(lane == h, gb, 0.0), axis=1, keepdims=True)
    decay = jnp.where(causal, jnp.exp(jnp.where(causal, gcol - grow, 0.0)), 0.0)
    kb = k * bcol
    m = jnp.where(strict, _bdot_nt(kb, k) * decay, 0.0)
    n = -m
    t = (r == c).astype(F32) + n
    p = n
    for _ in range(5):
        p = _bdot(p, p)
        t = t + _bdot(t, p)
    eg = jnp.exp(gcol)
    w = _bdot(t, kb * eg)
    u = _bdot(t, v * bcol)
    attn = jnp.where(causal, _bdot_nt(q, k) * decay, 0.0)
    glast = jnp.sum(jnp.where(_rows(gcol) == C - 1, gcol, 0.0), axis=0, keepdims=True)
    k_g = k * jnp.exp(glast - gcol)
    v_new = u - _bdot(w, s0)
    o = _bdot(q * eg, s0) + _bdot(attn, v_new)
    s1 = s0 * jnp.exp(glast) + _bdot_tn(k_g, v_new)
    return o, s1


def _f_gdn_post(o, z, nw):
    return (_heads(lambda a, b: _rms(a, nw) * _silu(b), o, z),)


def _f_lru_ab(lx, conv_w, conv_b, wa, ba, wx, bx, lam):
    xc = _causal_conv(lx, conv_w) + conv_b
    r = _sigmoid(_bdot(xc, wa) + ba)
    i = _sigmoid(_bdot(xc, wx) + bx)
    log_a = -LRU_C * r * _softplus(-lam)
    a = jnp.exp(log_a)
    mult = jnp.sqrt(_neg_expm1(2.0 * log_a))
    return a, mult * (i * xc)


def _f_lru_post(hs, gate, gw):
    return (_rms(hs * _gelu(gate), gw),)


def _f_sgu(uv, ln_w, ln_b, ws, bst, gw):
    tm = uv.shape[0]
    uvf = _gelu(uv)
    u, v = uvf[:, 0:DG], uvf[:, DG:2 * DG]
    mu = jnp.mean(v, axis=-1, keepdims=True)
    vc = v - mu
    v = vc * lax.rsqrt(jnp.mean(vc * vc, axis=-1, keepdims=True) + EPS) * ln_w + ln_b
    lane = lax.broadcasted_iota(jnp.int32, bst.shape, 1)
    tril = lax.broadcasted_iota(jnp.int32, (128, 128), 0) >= lax.broadcasted_iota(jnp.int32, (128, 128), 1)
    wsm = [jnp.where(tril, ws[g * 128:(g + 1) * 128, :], 0.0) for g in range(4)]
    bias = [jnp.sum(jnp.where(lane == g, bst, 0.0), axis=1, keepdims=True) for g in range(4)]
    out = []
    for n in range(tm // 128):
        vn = v[n * 128:(n + 1) * 128, :]
        gs = [_bdot(wsm[g], vn[:, g * 128:(g + 1) * 128]) + bias[g] for g in range(4)]
        out.append(jnp.concatenate(gs, axis=1))
    vo = jnp.concatenate(out, axis=0) if len(out) > 1 else out[0]
    return (_rms(u * vo, gw),)


def _f_sconv(sb, sc, sh, conv_w, gw):
    return (_rms(sb * _causal_conv(sc * sh, conv_w), gw),)


def _f_ffn_act(ug, uv, wg, wv, bg, bv):
    return (_gelu(_causal_conv(ug, wg) + bg) * (_causal_conv(uv, wv) + bv),)


def _row_specs(rows, consts, tm, ncol, tile_of):
    specs, args = [], []
    for arr, cb, w, halo in rows:
        specs.append(pl.BlockSpec((tm, w), lambda j, i, cb=cb: (tile_of(i), cb + j)))
        args.append(arr)
        if halo:
            specs.append(pl.BlockSpec((HALO, w), lambda j, i, cb=cb: (jnp.maximum(tile_of(i) * (tm // HALO) - 1, 0), cb + j)))
            args.append(arr)
    for arr, tiled in consts:
        r, c = arr.shape
        specs.append(pl.BlockSpec((r, c // ncol), lambda j, i: (0, j)) if tiled else pl.BlockSpec((r, c), lambda j, i: (0, 0)))
        args.append(arr)
    return specs, args


def _load_rows(refs, rows, consts, tile):
    k, vals = 0, []
    for _arr, _cb, _w, halo in rows:
        t = refs[k][...].astype(F32)
        k += 1
        if halo:
            hl = jnp.where(tile > 0, refs[k][...].astype(F32), 0.0)
            k += 1
            t = jnp.concatenate([hl, t], axis=0)
        vals.append(t)
    for _ in consts:
        vals.append(refs[k][...])
        k += 1
    return vals, k


def _rowwise(fn, rows, consts, outs, *, tm, name, ncol=1):
    n = rows[0][0].shape[0]
    nt = n // tm
    in_specs, args = _row_specs(rows, consts, tm, ncol, lambda i: i)

    def body(*refs):
        vals, k = _load_rows(refs, rows, consts, pl.program_id(1))
        for o_ref, r in zip(refs[k:], fn(*vals)):
            o_ref[...] = r.astype(o_ref.dtype)

    res = pl.pallas_call(
        body, grid=(ncol, nt), in_specs=in_specs,
        out_specs=[pl.BlockSpec((tm, w), lambda j, i: (i, j)) for w, _ in outs],
        out_shape=[jax.ShapeDtypeStruct((n, w * ncol), dt) for w, dt in outs],
        name=name, compiler_params=pltpu.CompilerParams(dimension_semantics=("arbitrary", "arbitrary")),
    )(*args)
    return res


def _rowwise_vjp(fn, rows, consts, cots, *, tm, name, ncol=1):
    n = rows[0][0].shape[0]
    nt = n // tm
    rows4 = [r[:4] for r in rows]
    consts2 = [c[:2] for c in consts]
    in_specs, args = _row_specs(rows4, consts2, tm, ncol, lambda i: nt - 1 - i)
    for arr, cb, w in cots:
        in_specs.append(pl.BlockSpec((tm, w), lambda j, i, cb=cb: (nt - 1 - i, cb + j)))
        args.append(arr)
    out_specs, out_shape, scratch = [], [], []
    for arr, cb, w, halo, gdt in rows:
        if gdt is not None:
            out_specs.append(pl.BlockSpec((tm, w), lambda j, i: (nt - 1 - i, j)))
            out_shape.append(jax.ShapeDtypeStruct((n, w * ncol), gdt))
            if halo:
                scratch.append(pltpu.VMEM((HALO, w), F32))
    for arr, tiled, want in consts:
        if want:
            r, c = arr.shape
            out_specs.append(pl.BlockSpec((r, c // ncol), lambda j, i: (0, j)) if tiled else pl.BlockSpec((r, c), lambda j, i: (0, 0)))
            out_shape.append(jax.ShapeDtypeStruct((r, c), F32))
    n_in = len(in_specs)
    n_out = len(out_specs)

    def body(*refs):
        j, i = pl.program_id(0), pl.program_id(1)
        tile = nt - 1 - i
        vals, k = _load_rows(refs, rows4, consts2, tile)
        cvals = [refs[k + q][...].astype(F32) for q in range(len(cots))]
        outs = refs[n_in:n_in + n_out]
        carries = refs[n_in + n_out:]
        _, vjp = jax.vjp(fn, *vals)
        g = vjp(tuple(cvals))
        o, cidx = 0, 0
        for q, (arr, cb, w, halo, gdt) in enumerate(rows):
            if gdt is None:
                continue
            if halo:
                ge = g[q]
                main = ge[HALO:]
                carry = carries[cidx]
                cidx += 1
                tail = main[tm - HALO:] + jnp.where(i > 0, carry[...], 0.0)
                outs[o][0:tm - HALO, :] = main[0:tm - HALO].astype(gdt)
                outs[o][tm - HALO:tm, :] = tail.astype(gdt)
                carry[...] = ge[0:HALO]
            else:
                outs[o][...] = g[q].astype(gdt)
            o += 1
        for q, (arr, tiled, want) in enumerate(consts):
            if not want:
                continue
            first = (i == 0) if tiled else jnp.logical_and(i == 0, j == 0)
            acc = outs[o]
            gq = g[len(rows) + q].astype(F32)

            @pl.when(first)
            def _(acc=acc, gq=gq):
                acc[...] = gq

            @pl.when(jnp.logical_not(first))
            def _(acc=acc, gq=gq):
                acc[...] += gq

            o += 1

    res = pl.pallas_call(
        body, grid=(ncol, nt), in_specs=in_specs, out_specs=out_specs, out_shape=out_shape, scratch_shapes=scratch,
        name=name, compiler_params=pltpu.CompilerParams(dimension_semantics=("arbitrary", "arbitrary")),
    )(*args)
    nrow = sum(1 for r in rows if r[4] is not None)
    return list(res[:nrow]), list(res[nrow:])


_ANY = pl.BlockSpec(memory_space=pl.ANY)
_CHIP_REL = ((1, 0), (0, 1), (1, 1))
_DEV_REL = tuple((r >> 2 & 1, r >> 1 & 1, r & 1) for r in range(1, 8))


def _flip(v, f):
    return 1 - v if f else v


class _GatherChips:
    def __init__(self, shards):
        self.arrs = list(shards)
        n = len(self.arrs)
        self.out_shape = [jax.ShapeDtypeStruct((4,) + s.shape, s.dtype) for s in self.arrs]
        self.sems = [pltpu.SemaphoreType.DMA((3 * n,)), pltpu.SemaphoreType.DMA((3 * n,)), pltpu.SemaphoreType.DMA((n,))]

    def _copies(self, ins, outs, sems, arriving):
        send, recv, lsem = sems
        x, y, c = lax.axis_index("x"), lax.axis_index("y"), lax.axis_index("c")
        me = 2 * x + y
        if arriving:
            local = []
        else:
            local = [pltpu.make_async_copy(ins[a], outs[a].at[me], lsem.at[a]) for a in range(len(ins))]
        remote = []
        for a in range(len(ins)):
            for k, (fx, fy) in enumerate(_CHIP_REL):
                px, py = _flip(x, fx), _flip(y, fy)
                remote.append(pltpu.make_async_remote_copy(
                    src_ref=ins[a], dst_ref=outs[a].at[2 * px + py if arriving else me], send_sem=send.at[3 * a + k],
                    recv_sem=recv.at[3 * a + k], device_id=(px, py, c), device_id_type=MESH_ID))
        return local, remote

    def issue(self, ins, outs, sems):
        local, push = self._copies(ins, outs, sems, False)
        for cp in local + push:
            cp.start()

    def finish(self, ins, outs, sems):
        for cp in self._copies(ins, outs, sems, True)[1]:
            cp.wait_recv()
        local, push = self._copies(ins, outs, sems, False)
        for cp in push:
            cp.wait_send()
        for cp in local:
            cp.wait()


class _ScatterChips:
    def __init__(self, blocks):
        self.arrs = list(blocks)
        n = len(self.arrs)
        self.out_shape = [jax.ShapeDtypeStruct((3,) + b.shape[1:], b.dtype) for b in self.arrs]
        self.sems = [pltpu.SemaphoreType.DMA((3 * n,)), pltpu.SemaphoreType.DMA((3 * n,))]

    def _copies(self, ins, outs, sems):
        send, recv = sems
        x, y, c = lax.axis_index("x"), lax.axis_index("y"), lax.axis_index("c")
        copies = []
        for a in range(len(ins)):
            for k, (fx, fy) in enumerate(_CHIP_REL):
                px, py = _flip(x, fx), _flip(y, fy)
                copies.append(pltpu.make_async_remote_copy(
                    src_ref=ins[a].at[2 * px + py], dst_ref=outs[a].at[k], send_sem=send.at[3 * a + k], recv_sem=recv.at[3 * a + k],
                    device_id=(px, py, c), device_id_type=MESH_ID))
        return copies

    def issue(self, ins, outs, sems):
        for cp in self._copies(ins, outs, sems):
            cp.start()

    def finish(self, ins, outs, sems):
        copies = self._copies(ins, outs, sems)
        for cp in copies:
            cp.wait_recv()
        for cp in copies:
            cp.wait_send()


def _exchange(xch, *, name):
    ni, no = len(xch.arrs), len(xch.out_shape)

    def body(*refs):
        ins, outs, sems = refs[:ni], refs[ni:ni + no], refs[ni + no:]
        xch.issue(ins, outs, sems)
        xch.finish(ins, outs, sems)

    return pl.pallas_call(body, in_specs=[_ANY] * ni, out_specs=[_ANY] * no, out_shape=xch.out_shape, scratch_shapes=xch.sems, name=name)(*xch.arrs)


def _call(body, *, grid, in_specs, out_specs, out_shape, scratch, name, args, xch=None):
    params = pltpu.CompilerParams(dimension_semantics=("arbitrary",) * len(grid))
    if xch is None:
        res = pl.pallas_call(body, grid=grid, in_specs=in_specs, out_specs=out_specs, out_shape=out_shape, scratch_shapes=scratch,
                             name=name, compiler_params=params)(*args)
        return list(res), []
    n_in, n_out, n_sc = len(in_specs), len(out_specs), len(scratch)
    xi, xo = len(xch.arrs), len(xch.out_shape)

    def wrapped(*refs):
        ins, refs = refs[:n_in], refs[n_in:]
        xin, refs = refs[:xi], refs[xi:]
        outs, refs = refs[:n_out], refs[n_out:]
        xout, refs = refs[:xo], refs[xo:]
        sc, sems = refs[:n_sc], refs[n_sc:]
        first = functools.reduce(jnp.logical_and, [pl.program_id(d) == 0 for d in range(len(grid))])
        last = functools.reduce(jnp.logical_and, [pl.program_id(d) == grid[d] - 1 for d in range(len(grid))])

        @pl.when(first)
        def _():
            xch.issue(xin, xout, sems)

        body(*ins, *outs, *sc)

        @pl.when(last)
        def _():
            xch.finish(xin, xout, sems)

    res = pl.pallas_call(
        wrapped, grid=grid, in_specs=list(in_specs) + [_ANY] * xi, out_specs=list(out_specs) + [_ANY] * xo,
        out_shape=list(out_shape) + xch.out_shape, scratch_shapes=list(scratch) + xch.sems, name=name, compiler_params=params,
    )(*args, *xch.arrs)
    return list(res[:n_out]), list(res[n_out:])


def _matmul(a, b, mode, out_dtype, *, tm, tn, tk, name, b_blocks=False, out_blocks=False, xch=None):
    if b_blocks:
        _, br, bc4 = b.shape
        b2 = (br, 4 * bc4)
    else:
        b2 = b.shape

    def bspec(shape, index):
        if not b_blocks:
            return pl.BlockSpec(shape, index)
        per = bc4 // shape[1]

        def blocked(i, j, k):
            r, c = index(i, j, k)
            return (c // per, r, c % per)

        return pl.BlockSpec((None,) + shape, blocked)

    tm = min(tm, a.shape[1] if mode == "tn" else a.shape[0])
    tk = min(tk, a.shape[0] if mode == "tn" else a.shape[1])
    if mode == "tn":
        K, M = a.shape
        N = b2[1]
        a_spec = pl.BlockSpec((tk, tm), lambda i, j, k: (k, i))
        b_spec = bspec((tk, tn), lambda i, j, k: (k, j))
        dims = (((0,), (0,)), ((), ()))
    elif mode == "nt":
        M, K = a.shape
        N = b2[0]
        a_spec = pl.BlockSpec((tm, tk), lambda i, j, k: (i, k))
        b_spec = bspec((tn, tk), lambda i, j, k: (j, k))
        dims = (((1,), (1,)), ((), ()))
    else:
        M, K = a.shape
        N = b2[1]
        a_spec = pl.BlockSpec((tm, tk), lambda i, j, k: (i, k))
        b_spec = bspec((tk, tn), lambda i, j, k: (k, j))
        dims = (((1,), (0,)), ((), ()))
    assert M % tm == 0 and N % tn == 0 and K % tk == 0, (name, M, N, K, tm, tn, tk)
    nk = K // tk
    if out_blocks:
        per_o = (N // 4) // tn
        o_spec = pl.BlockSpec((None, tm, tn), lambda i, j, k: (j // per_o, i, j % per_o))
        o_shape = jax.ShapeDtypeStruct((4, M, N // 4), out_dtype)
    else:
        o_spec = pl.BlockSpec((tm, tn), lambda i, j, k: (i, j))
        o_shape = jax.ShapeDtypeStruct((M, N), out_dtype)

    def body(a_ref, b_ref, o_ref, acc_ref):
        k = pl.program_id(2)
        part = lax.dot_general(a_ref[...].astype(BF), b_ref[...].astype(BF), dims, preferred_element_type=F32)
        if nk == 1:
            o_ref[...] = part.astype(o_ref.dtype)
        else:
            @pl.when(k == 0)
            def _():
                acc_ref[...] = part

            @pl.when(k > 0)
            def _():
                acc_ref[...] += part

            @pl.when(k == nk - 1)
            def _():
                o_ref[...] = acc_ref[...].astype(o_ref.dtype)

    res, xres = _call(body, grid=(M // tm, N // tn, nk), in_specs=[a_spec, b_spec], out_specs=[o_spec], out_shape=[o_shape],
                      scratch=[pltpu.VMEM((tm, tn) if nk > 1 else (8, 128), F32)], name=name, args=(a, b), xch=xch)
    return res[0], xres


def _scan_fwd(a, b, *, tm, name):
    n, c = a.shape
    nt = n // tm

    def body(a_ref, b_ref, h_ref, carry):
        i = pl.program_id(0)
        av, bv = a_ref[...], b_ref[...]
        row = _rows(av)
        d = 1
        while d < tm:
            a_s = jnp.where(row >= d, pltpu.roll(av, d, axis=0), 1.0)
            b_s = jnp.where(row >= d, pltpu.roll(bv, d, axis=0), 0.0)
            bv = av * b_s + bv
            av = av * a_s
            d *= 2
        h = bv + av * jnp.where(i > 0, carry[HALO - 1:HALO, :], 0.0)
        h_ref[...] = h
        carry[...] = h[tm - HALO:tm]

    return pl.pallas_call(
        body, grid=(nt,), in_specs=[pl.BlockSpec((tm, c), lambda i: (i, 0))] * 2,
        out_specs=pl.BlockSpec((tm, c), lambda i: (i, 0)), out_shape=jax.ShapeDtypeStruct((n, c), F32),
        scratch_shapes=[pltpu.VMEM((HALO, c), F32)], name=name,
        compiler_params=pltpu.CompilerParams(dimension_semantics=("arbitrary",)),
    )(a, b)


def _scan_bwd(a, h, dh, *, tm, name):
    n, c = a.shape
    nt = n // tm
    tb = tm // HALO

    def body(a_ref, an_ref, h_ref, hp_ref, dh_ref, da_ref, db_ref, carry):
        i = pl.program_id(0)
        tile = nt - 1 - i
        av, hv, g = a_ref[...], h_ref[...], dh_ref[...]
        row = _rows(av)
        a_next = jnp.where(tile < nt - 1, an_ref[0:1, :], 0.0)
        au = jnp.where(row < tm - 1, pltpu.roll(av, tm - 1, axis=0), a_next)
        d = 1
        while d < tm:
            a_s = jnp.where(row < tm - d, pltpu.roll(au, tm - d, axis=0), 1.0)
            g_s = jnp.where(row < tm - d, pltpu.roll(g, tm - d, axis=0), 0.0)
            g = au * g_s + g
            au = au * a_s
            d *= 2
        g = g + au * jnp.where(i > 0, carry[0:1, :], 0.0)
        h_prev = jnp.where(row >= 1, pltpu.roll(hv, 1, axis=0), jnp.where(tile > 0, hp_ref[HALO - 1:HALO, :], 0.0))
        db_ref[...] = g
        da_ref[...] = g * h_prev
        carry[...] = g[0:HALO]

    cur = pl.BlockSpec((tm, c), lambda i: (nt - 1 - i, 0))
    nxt = pl.BlockSpec((HALO, c), lambda i: (jnp.minimum((nt - i) * tb, n // HALO - 1), 0))
    prv = pl.BlockSpec((HALO, c), lambda i: (jnp.maximum((nt - 1 - i) * tb - 1, 0), 0))
    return pl.pallas_call(
        body, grid=(nt,), in_specs=[cur, nxt, cur, prv, cur], out_specs=[cur, cur],
        out_shape=[jax.ShapeDtypeStruct((n, c), F32)] * 2, scratch_shapes=[pltpu.VMEM((HALO, c), F32)], name=name,
        compiler_params=pltpu.CompilerParams(dimension_semantics=("arbitrary",)),
    )(a, a, h, h, dh)


def _gdn_fwd(q, k, v, gb, *, name, xch=None):
    n = q.shape[0]
    nc = n // CHUNK

    def body(q_ref, k_ref, v_ref, gb_ref, o_ref, s_ref, state):
        @pl.when(pl.program_id(0) == 0)
        def _():
            state[...] = jnp.zeros_like(state)

        gbv = gb_ref[...]
        for h in range(HEADS):
            cs = slice(h * HD, (h + 1) * HD)
            s0 = state[cs, :]
            s_ref[0, cs, :] = s0
            o, s1 = _f_gdn_chunk(h, q_ref[:, cs], k_ref[:, cs], v_ref[:, cs], gbv, s0)
            o_ref[:, cs] = o
            state[cs, :] = s1

    row = pl.BlockSpec((CHUNK, DG), lambda i: (i, 0))
    return _call(
        body, grid=(nc,), in_specs=[row, row, row, pl.BlockSpec((CHUNK, 128), lambda i: (i, 0))],
        out_specs=[row, pl.BlockSpec((1, DG, HD), lambda i: (i, 0, 0))],
        out_shape=[jax.ShapeDtypeStruct((n, DG), F32), jax.ShapeDtypeStruct((nc, DG, HD), F32)],
        scratch=[pltpu.VMEM((DG, HD), F32)], name=name, args=(q, k, v, gb), xch=xch)


def _gdn_bwd(q, k, v, gb, s_all, do, *, name, xch=None):
    n = q.shape[0]
    nc = n // CHUNK

    def body(q_ref, k_ref, v_ref, gb_ref, s_ref, do_ref, dq_ref, dk_ref, dv_ref, dgb_ref, dstate):
        @pl.when(pl.program_id(0) == 0)
        def _():
            dstate[...] = jnp.zeros_like(dstate)

        gbv = gb_ref[...]
        dgb = jnp.zeros_like(gbv)
        for h in range(HEADS):
            cs = slice(h * HD, (h + 1) * HD)
            _, vjp = jax.vjp(functools.partial(_f_gdn_chunk, h), q_ref[:, cs], k_ref[:, cs], v_ref[:, cs], gbv, s_ref[0, cs, :])
            dq, dk, dv, dg, ds = vjp((do_ref[:, cs], dstate[cs, :]))
            dq_ref[:, cs] = dq
            dk_ref[:, cs] = dk
            dv_ref[:, cs] = dv
            dstate[cs, :] = ds
            dgb = dgb + dg
        dgb_ref[...] = dgb

    row = pl.BlockSpec((CHUNK, DG), lambda i: (nc - 1 - i, 0))
    gsp = pl.BlockSpec((CHUNK, 128), lambda i: (nc - 1 - i, 0))
    return _call(
        body, grid=(nc,), in_specs=[row, row, row, gsp, pl.BlockSpec((1, DG, HD), lambda i: (nc - 1 - i, 0, 0)), row],
        out_specs=[row, row, row, gsp],
        out_shape=[jax.ShapeDtypeStruct((n, DG), F32)] * 3 + [jax.ShapeDtypeStruct((n, 128), F32)],
        scratch=[pltpu.VMEM((DG, HD), F32)], name=name, args=(q, k, v, gb, s_all, do), xch=xch)


def _loss_head(x1, y2, w, tgt, *, tm, name):
    n, c = x1.shape

    def body(x_ref, y_ref, w_ref, t_ref, l_ref, d_ref):
        err = x_ref[...] + _rms(y_ref[...], w_ref[...]) - t_ref[...]
        part = jnp.sum(jnp.sum(err * err, axis=1, keepdims=True), axis=0, keepdims=True) * (0.5 / c)

        @pl.when(pl.program_id(0) == 0)
        def _():
            l_ref[...] = jnp.zeros_like(l_ref)

        l_ref[...] += part
        d_ref[...] = err * (1.0 / c)

    row = pl.BlockSpec((tm, c), lambda i: (i, 0))
    return pl.pallas_call(
        body, grid=(n // tm,), in_specs=[row, row, pl.BlockSpec((1, c), lambda i: (0, 0)), row],
        out_specs=[pl.BlockSpec((8, 128), lambda i: (0, 0)), row],
        out_shape=[jax.ShapeDtypeStruct((8, 128), F32), jax.ShapeDtypeStruct((n, c), F32)], name=name,
        compiler_params=pltpu.CompilerParams(dimension_semantics=("arbitrary",)),
    )(x1, y2, w, tgt)


def _sum_parts(own, recv, *, tr, name):
    r, c = own.shape
    p = recv.shape[0]

    def body(o_ref, r_ref, s_ref):
        s = o_ref[...]
        for q in range(p):
            s = s + r_ref[q].astype(F32)
        s_ref[...] = s

    return pl.pallas_call(
        body, grid=(r // tr,), in_specs=[pl.BlockSpec((tr, c), lambda i: (i, 0)), pl.BlockSpec((p, tr, c), lambda i: (0, i, 0))],
        out_specs=pl.BlockSpec((tr, c), lambda i: (i, 0)), out_shape=jax.ShapeDtypeStruct((r, c), F32), name=name,
        compiler_params=pltpu.CompilerParams(dimension_semantics=("parallel",)),
    )(own, recv)


def _sum_slots(buf, *, name):
    p, r, c = buf.shape

    def body(b_ref, s_ref):
        s = b_ref[0]
        for q in range(1, p):
            s = s + b_ref[q]
        s_ref[...] = s

    return pl.pallas_call(body, out_shape=jax.ShapeDtypeStruct((r, c), F32), name=name)(buf)


def _adamw(w, m, v, gs, *, tr, name):
    r, c = w.shape
    ngrp = len(gs)
    nterm = len(gs[0])
    per = r // ngrp // tr
    c1 = 1.0 - ADAM_B1 ** ADAM_STEP
    c2 = 1.0 - ADAM_B2 ** ADAM_STEP

    def body(*refs):
        w_ref, m_ref, v_ref = refs[:3]
        g_refs = refs[3:3 + ngrp * nterm]
        g_ref, d_ref, nm_ref, nv_ref = refs[3 + ngrp * nterm:]
        grp = pl.program_id(0) // per
        g = None
        for q in range(ngrp):
            gq = g_refs[q * nterm][...]
            for t in range(1, nterm):
                gq = gq + g_refs[q * nterm + t][...]
            g = gq if g is None else jnp.where(grp == q, gq, g)
        nm = ADAM_B1 * m_ref[...] + (1.0 - ADAM_B1) * g
        nv = ADAM_B2 * v_ref[...] + (1.0 - ADAM_B2) * (g * g)
        g_ref[...] = g
        d_ref[...] = -ADAM_LR * ((nm / c1) / (jnp.sqrt(nv / c2) + ADAM_EPS) + ADAM_WD * w_ref[...])
        nm_ref[...] = nm
        nv_ref[...] = nv

    blk = pl.BlockSpec((tr, c), lambda i: (i, 0))
    g_specs = [pl.BlockSpec((tr, c), lambda i, q=q: (jnp.clip(i - q * per, 0, per - 1), 0)) for q in range(ngrp) for _ in range(nterm)]
    return pl.pallas_call(
        body, grid=(r // tr,), in_specs=[blk] * 3 + g_specs, out_specs=[blk] * 4,
        out_shape=[jax.ShapeDtypeStruct((r, c), F32)] * 4, name=name,
        compiler_params=pltpu.CompilerParams(dimension_semantics=("arbitrary",)),
    )(w, m, v, *[t for grp in gs for t in grp])


def _swap_cores(arrs, *, name):
    na = len(arrs)

    def body(*refs):
        ins, outs = refs[:na], refs[na:2 * na]
        send, recv = refs[2 * na:]
        sib = (lax.axis_index("x"), lax.axis_index("y"), 1 - lax.axis_index("c"))
        copies = []
        for a in range(na):
            cp = pltpu.make_async_remote_copy(src_ref=ins[a], dst_ref=outs[a], send_sem=send.at[a], recv_sem=recv.at[a],
                                              device_id=sib, device_id_type=MESH_ID)
            cp.start()
            copies.append(cp)
        for cp in copies:
            cp.wait_recv()
        for cp in copies:
            cp.wait_send()

    return pl.pallas_call(
        body, in_specs=[_ANY] * na, out_specs=[_ANY] * na,
        out_shape=[jax.ShapeDtypeStruct(a.shape, a.dtype) for a in arrs],
        scratch_shapes=[pltpu.SemaphoreType.DMA((na,)), pltpu.SemaphoreType.DMA((na,))], name=name,
    )(*arrs)


def _gather_devices(buf, *, name):
    def body(in_ref, out_ref, send, recv, lsem):
        x, y, c = lax.axis_index("x"), lax.axis_index("y"), lax.axis_index("c")
        me = 4 * x + 2 * y + c
        loc = pltpu.make_async_copy(in_ref, out_ref.at[me], lsem)
        loc.start()
        pairs = []
        for k, (fx, fy, fc) in enumerate(_DEV_REL):
            px, py, pc = _flip(x, fx), _flip(y, fy), _flip(c, fc)
            cp = pltpu.make_async_remote_copy(src_ref=in_ref, dst_ref=out_ref.at[me], send_sem=send.at[k], recv_sem=recv.at[k],
                                              device_id=(px, py, pc), device_id_type=MESH_ID)
            cp.start()
            arrive = pltpu.make_async_remote_copy(src_ref=in_ref, dst_ref=out_ref.at[4 * px + 2 * py + pc], send_sem=send.at[k],
                                                  recv_sem=recv.at[k], device_id=(px, py, pc), device_id_type=MESH_ID)
            pairs.append((cp, arrive))
        for cp, arrive in pairs:
            arrive.wait_recv()
        for cp, arrive in pairs:
            cp.wait_send()
        loc.wait()

    return pl.pallas_call(
        body, in_specs=[_ANY], out_specs=_ANY, out_shape=jax.ShapeDtypeStruct((8,) + buf.shape, buf.dtype),
        scratch_shapes=[pltpu.SemaphoreType.DMA((7,)), pltpu.SemaphoreType.DMA((7,)), pltpu.SemaphoreType.DMA(())], name=name,
    )(buf)


def _pad_cols(w):
    z = jnp.zeros(w.shape[:-1] + (NP - N_IN,), w.dtype)
    return jnp.concatenate([w[..., 0:2048], w[..., 2056:N_IN], w[..., 2048:2056], z], axis=-1)


def _unpad_cols(w):
    return jnp.concatenate([w[..., 0:2048], w[..., OFF_BA:OFF_BA + 8], w[..., 2048:OFF_BA]], axis=-1)


def _lanes(v, off):
    return jnp.pad(v.reshape(1, -1), ((0, 0), (off, 128 - off - v.size)))


def _block_diag(w):
    eye = jnp.eye(8, dtype=w.dtype)
    return (w[:, :, None, :] * eye[:, None, :, None]).reshape(DG, DG)


def _diag_blocks(w):
    return jnp.stack([w[h * 64:(h + 1) * 64, h * 64:(h + 1) * 64] for h in range(8)])


def _mixer_params(p):
    return dict(
        gdn_conv_w=p["gdn_conv_w"], alog=_lanes(p["gdn_a_log"], HEADS), dtb=_lanes(p["gdn_dt_bias"], HEADS),
        gdn_nw=p["gdn_norm_w"].reshape(1, HD),
        lru_conv_w=p["lru_conv_w"], lru_conv_b=p["lru_conv_b"].reshape(1, DG),
        wa=_block_diag(p["lru_wa"]), ba=p["lru_ba"].reshape(1, DG), wx=_block_diag(p["lru_wx"]), bx=p["lru_bx"].reshape(1, DG),
        lam=p["lru_lambda"].reshape(1, DG),
        ln_w=p["sgu_ln_w"].reshape(1, DG), ln_b=p["sgu_ln_b"].reshape(1, DG), ws=p["sgu_ws"].reshape(DG, 128),
        bst=jnp.pad(p["sgu_b"].T, ((0, 0), (0, 124))),
        sconv_w=p["sconv_w"], gw0=p["grp_norm_w"][0:1], gw1=p["grp_norm_w"][1:2], gw2=p["grp_norm_w"][2:3],
    )


def _mixer_param_grads(g):
    return dict(
        gdn_conv_w=g["gdn_conv_w"], gdn_a_log=g["alog"][0, HEADS:2 * HEADS], gdn_dt_bias=g["dtb"][0, HEADS:2 * HEADS],
        gdn_norm_w=g["gdn_nw"][0],
        lru_conv_w=g["lru_conv_w"], lru_conv_b=g["lru_conv_b"][0],
        lru_wa=_diag_blocks(g["wa"]), lru_ba=g["ba"].reshape(8, 64), lru_wx=_diag_blocks(g["wx"]), lru_bx=g["bx"].reshape(8, 64),
        lru_lambda=g["lam"][0],
        sgu_ln_w=g["ln_w"][0], sgu_ln_b=g["ln_b"][0], sgu_ws=g["ws"].reshape(4, 128, 128), sgu_b=g["bst"][:, 0:4].T,
        sconv_w=g["sconv_w"], grp_norm_w=jnp.concatenate([g["gw0"], g["gw1"], g["gw2"]], axis=0),
    )


TM_MIX = 256


def _mixers_fwd(p, mp, tag="", xch=None):
    c = lambda *names: [(mp[n], False) for n in names]
    q, k, v, gb = _rowwise(_f_gdn_pre, [(p, 0, 1536, True), (p, OFF_BA // 128, 128, False)], c("gdn_conv_w", "alog", "dtb"),
                           [(DG, F32)] * 3 + [(128, F32)], tm=TM_MIX, name="gdn_pre" + tag)
    (o, s_all), xres = _gdn_fwd(q, k, v, gb, name="gdn_chunks" + tag, xch=xch)
    y_a, = _rowwise(_f_gdn_post, [(o, 0, DG, False), (p, OFF_Z // DG, DG, False)], c("gdn_nw"), [(DG, BF)], tm=TM_MIX, name="gdn_post" + tag)
    a, b = _rowwise(_f_lru_ab, [(p, OFF_LX // DG, DG, True)], c("lru_conv_w", "lru_conv_b", "wa", "ba", "wx", "bx", "lam"),
                    [(DG, F32)] * 2, tm=TM_MIX, name="lru_ab" + tag)
    hs = _scan_fwd(a, b, tm=TM_MIX, name="lru_scan" + tag)
    y_b, = _rowwise(_f_lru_post, [(hs, 0, DG, False), (p, OFF_LG // DG, DG, False)], c("gw0"), [(DG, BF)], tm=TM_MIX, name="lru_post" + tag)
    y_c, = _rowwise(_f_sgu, [(p, OFF_UV // 1024, 1024, False)], c("ln_w", "ln_b", "ws", "bst", "gw1"), [(DG, BF)], tm=TM_MIX, name="sgu" + tag)
    y_d, = _rowwise(_f_sconv, [(p, OFF_SB // DG, DG, False), (p, OFF_SC // DG, DG, True), (p, OFF_SH // DG, DG, True)],
                    c("sconv_w", "gw2"), [(DG, BF)], tm=TM_MIX, name="sconv" + tag)
    return jnp.concatenate([y_a, y_b, y_c, y_d], axis=1), (q, k, v, gb, o, s_all, a, hs), xres


def _mixers_bwd(p, mp, saved, dy, tag="", xch=None):
    q, k, v, gb, o, s_all, a, hs = saved
    c = lambda *names: [(mp[n], False, True) for n in names]
    g = {}

    (do, dz), (g["gdn_nw"],) = _rowwise_vjp(
        _f_gdn_post, [(o, 0, DG, False, F32), (p, OFF_Z // DG, DG, False, BF)], c("gdn_nw"), [(dy, 0, DG)], tm=TM_MIX, name="gdn_post_b" + tag)
    (dq, dk, dv, dgb), xres = _gdn_bwd(q, k, v, gb, s_all, do, name="gdn_chunks_b" + tag, xch=xch)
    (dqkv, dba), (g["gdn_conv_w"], g["alog"], g["dtb"]) = _rowwise_vjp(
        _f_gdn_pre, [(p, 0, 1536, True, BF), (p, OFF_BA // 128, 128, False, BF)], c("gdn_conv_w", "alog", "dtb"),
        [(dq, 0, DG), (dk, 0, DG), (dv, 0, DG), (dgb, 0, 128)], tm=TM_MIX, name="gdn_pre_b" + tag)

    (dhs, dgate), (g["gw0"],) = _rowwise_vjp(
        _f_lru_post, [(hs, 0, DG, False, F32), (p, OFF_LG // DG, DG, False, BF)], c("gw0"), [(dy, 1, DG)], tm=TM_MIX, name="lru_post_b" + tag)
    da, db = _scan_bwd(a, hs, dhs, tm=TM_MIX, name="lru_scan_b" + tag)
    (dlx,), (g["lru_conv_w"], g["lru_conv_b"], g["wa"], g["ba"], g["wx"], g["bx"], g["lam"]) = _rowwise_vjp(
        _f_lru_ab, [(p, OFF_LX // DG, DG, True, BF)], c("lru_conv_w", "lru_conv_b", "wa", "ba", "wx", "bx", "lam"),
        [(da, 0, DG), (db, 0, DG)], tm=TM_MIX, name="lru_ab_b" + tag)

    (duv,), (g["ln_w"], g["ln_b"], g["ws"], g["bst"], g["gw1"]) = _rowwise_vjp(
        _f_sgu, [(p, OFF_UV // 1024, 1024, False, BF)], c("ln_w", "ln_b", "ws", "bst", "gw1"), [(dy, 2, DG)], tm=TM_MIX, name="sgu_b" + tag)

    (dsb, dsc, dsh), (g["sconv_w"], g["gw2"]) = _rowwise_vjp(
        _f_sconv, [(p, OFF_SB // DG, DG, False, BF), (p, OFF_SC // DG, DG, True, BF), (p, OFF_SH // DG, DG, True, BF)],
        c("sconv_w", "gw2"), [(dy, 3, DG)], tm=TM_MIX, name="sconv_b" + tag)

    dp = jnp.concatenate([dqkv, dz, dlx, dgate, duv, dsb, dsc, dsh, dba], axis=1)
    return dp, g, xres


WEIGHTS = ("pre_mix_norm", "w_in", "gdn_conv_w", "gdn_a_log", "gdn_dt_bias", "gdn_norm_w", "lru_conv_w", "lru_conv_b", "lru_wa",
           "lru_ba", "lru_wx", "lru_bx", "lru_lambda", "sgu_ln_w", "sgu_ln_b", "sgu_ws", "sgu_b", "sconv_w", "grp_norm_w", "w_out",
           "post_mix_norm", "pre_ffn_norm", "ffn_up", "ffn_conv_w", "ffn_conv_b", "ffn_down", "post_ffn_norm")
BIG = ("w_in", "ffn_up", "w_out", "ffn_down")
CHIP_SHARDED_SMALL = ("gdn_conv_w", "lru_conv_w", "sconv_w", "grp_norm_w", "ffn_conv_w")
MIXER_PARAMS = ("gdn_conv_w", "gdn_a_log", "gdn_dt_bias", "gdn_norm_w", "lru_conv_w", "lru_conv_b", "lru_wa", "lru_ba", "lru_wx",
                "lru_bx", "lru_lambda", "sgu_ln_w", "sgu_ln_b", "sgu_ws", "sgu_b", "sconv_w", "grp_norm_w")
TM_ROW = 256
N_FF_TILES = 11
FF_TILE = D_FF // N_FF_TILES
PACK_ROWS = 256


def _pack(arrs):
    flat = jnp.concatenate([a.reshape(-1) for a in arrs])
    n = flat.size
    rows = -(-n // (128 * PACK_ROWS)) * PACK_ROWS
    return jnp.pad(flat, (0, rows * 128 - n)).reshape(rows, 128)


def _unpack(buf, shapes):
    flat = buf.reshape(-1)
    out, off = [], 0
    for s in shapes:
        n = math.prod(s)
        out.append(flat[off:off + n].reshape(s))
        off += n
    return out


def _row_tile(rows, cols):
    return 256 if rows % 256 == 0 and cols <= 1024 else 128


def _w_in_full(got):
    return _pad_cols(got.transpose(1, 0, 2).reshape(D, N_IN))


def _w_in_blocks(dw):
    return _unpad_cols(dw).reshape(D, 4, N_IN // 4).transpose(1, 0, 2)


def _layer_fwd(l, xs, h, w_in_l, shard, sp, mp, next_w_in):
    t = str(l)
    p, (w_out_g,) = _matmul(h, w_in_l, "nn", F32, tm=1024, tn=1152, tk=D, name="mm_in" + t, xch=_GatherChips([shard["w_out"][l]]))
    ycat, saved, (w_up,) = _mixers_fwd(p, mp, tag=t, xch=_GatherChips([shard["ffn_up"][l]]))
    w_out_l = w_out_g.reshape(D, D)
    y, _ = _matmul(ycat, w_out_l, "nn", F32, tm=1024, tn=1024, tk=D, name="mm_out" + t)
    x1, h2 = _rowwise(_f_post_pre, [(xs, 0, D, False), (y, 0, D, False)], [(sp["post_mix_norm"][l], False), (sp["pre_ffn_norm"][l], False)],
                      [(D, F32), (D, BF)], tm=TM_ROW, name="post_mix" + t)
    u, (w_dn_g,) = _matmul(h2, w_up, "nn", F32, tm=1024, tn=1408, tk=D, name="mm_up" + t, b_blocks=True,
                           xch=_GatherChips([shard["ffn_down"][l]]))
    act, = _rowwise(_f_ffn_act, [(u, 0, FF_TILE, True), (u, N_FF_TILES, FF_TILE, True)], [(c, True) for c in sp["ffn_conv"][l]],
                    [(FF_TILE, BF)], tm=TM_ROW, ncol=N_FF_TILES, name="ffn_act" + t)
    w_dn_l = w_dn_g.reshape(D_FF, D)
    y2, nxt = _matmul(act, w_dn_l, "nn", F32, tm=1024, tn=1024, tk=1408, name="mm_down" + t,
                      xch=None if next_w_in is None else _GatherChips([next_w_in]))
    keep = dict(xs=xs, h=h, p=p, saved=saved, ycat=ycat, y=y, x1=x1, h2=h2, u=u, act=act, y2=y2,
                w_in=w_in_l, w_out=w_out_l, w_up=w_up, w_dn=w_dn_l)
    return keep, (nxt[0] if nxt else None)


def _layer_bwd(l, a, dx1, dy2, sp, mp, above):
    t = str(l)
    g = {}
    dact, r_above = _matmul(dy2, a["w_dn"], "nt", F32, tm=1024, tn=1408, tk=D, name="mm_down_dx" + t,
                            xch=None if above is None else _ScatterChips([above]))
    dw_dn, _ = _matmul(a["act"], dy2, "tn", F32, tm=1408, tn=1024, tk=1024, name="mm_down_dw" + t)
    dw_dn = dw_dn.reshape(4, D_FF // 4, D)
    (dug, duv), gc = _rowwise_vjp(
        _f_ffn_act, [(a["u"], 0, FF_TILE, True, BF), (a["u"], N_FF_TILES, FF_TILE, True, BF)], [(c, True, True) for c in sp["ffn_conv"][l]],
        [(dact, 0, FF_TILE)], tm=TM_ROW, ncol=N_FF_TILES, name="ffn_act_b" + t)
    g["ffn_conv_w"] = jnp.concatenate([gc[0], gc[1]], axis=1)
    g["ffn_conv_b"] = jnp.concatenate([gc[2], gc[3]], axis=1)[0]
    du = jnp.concatenate([dug, duv], axis=1)
    dh2, (r_dn,) = _matmul(du, a["w_up"], "nt", F32, tm=1024, tn=1024, tk=2816, name="mm_up_dx" + t, b_blocks=True,
                           xch=_ScatterChips([dw_dn.astype(BF)]))
    dw_up, _ = _matmul(a["h2"], du, "tn", F32, tm=1024, tn=1408, tk=1024, name="mm_up_dw" + t, out_blocks=True)
    (dxs, dy), (gpm, gpf) = _rowwise_vjp(
        _f_post_pre, [(a["xs"], 0, D, False, F32), (a["y"], 0, D, False, BF)],
        [(sp["post_mix_norm"][l], False, True), (sp["pre_ffn_norm"][l], False, True)], [(dx1, 0, D), (dh2, 0, D)], tm=TM_ROW, name="post_mix_b" + t)
    g["post_mix_norm"], g["pre_ffn_norm"] = gpm[0], gpf[0]
    dycat, _ = _matmul(dy, a["w_out"], "nt", F32, tm=1024, tn=1024, tk=D, name="mm_out_dx" + t)
    dw_out, _ = _matmul(a["ycat"], dy, "tn", F32, tm=1024, tn=1024, tk=1024, name="mm_out_dw" + t)
    dw_out = dw_out.reshape(4, D // 4, D)
    dp, gm, (r_up, r_out) = _mixers_bwd(a["p"], mp, a["saved"], dycat, tag=t, xch=_ScatterChips([dw_up.astype(BF), dw_out.astype(BF)]))
    g.update(_mixer_param_grads(gm))
    dh, _ = _matmul(dp, a["w_in"], "nt", F32, tm=1024, tn=1024, tk=1920, name="mm_in_dx" + t)
    dw_in, _ = _matmul(a["h"], dp, "tn", F32, tm=1024, tn=1152, tk=1024, name="mm_in_dw" + t)
    big = {"ffn_down": (dw_dn, r_dn), "ffn_up": (dw_up, r_up), "w_out": (dw_out, r_out)}
    return dxs, dh, g, big, _w_in_blocks(dw_in), (r_above[0] if r_above else None)


def kernel(x, pre_mix_norm, w_in, gdn_conv_w, gdn_a_log, gdn_dt_bias, gdn_norm_w, lru_conv_w, lru_conv_b, lru_wa, lru_ba, lru_wx, lru_bx, lru_lambda, sgu_ln_w, sgu_ln_b, sgu_ws, sgu_b, sconv_w, grp_norm_w, w_out, post_mix_norm, pre_ffn_norm, ffn_up, ffn_conv_w, ffn_conv_b, ffn_down, post_ffn_norm, loss_target, m_pre_mix_norm, m_w_in, m_gdn_conv_w, m_gdn_a_log, m_gdn_dt_bias, m_gdn_norm_w, m_lru_conv_w, m_lru_conv_b, m_lru_wa, m_lru_ba, m_lru_wx, m_lru_bx, m_lru_lambda, m_sgu_ln_w, m_sgu_ln_b, m_sgu_ws, m_sgu_b, m_sconv_w, m_grp_norm_w, m_w_out, m_post_mix_norm, m_pre_ffn_norm, m_ffn_up, m_ffn_conv_w, m_ffn_conv_b, m_ffn_down, m_post_ffn_norm, v_pre_mix_norm, v_w_in, v_gdn_conv_w, v_gdn_a_log, v_gdn_dt_bias, v_gdn_norm_w, v_lru_conv_w, v_lru_conv_b, v_lru_wa, v_lru_ba, v_lru_wx, v_lru_bx, v_lru_lambda, v_sgu_ln_w, v_sgu_ln_b, v_sgu_ws, v_sgu_b, v_sconv_w, v_grp_norm_w, v_w_out, v_post_mix_norm, v_pre_ffn_norm, v_ffn_up, v_ffn_conv_w, v_ffn_conv_b, v_ffn_down, v_post_ffn_norm):
    given = dict(locals())
    me = 2 * lax.axis_index("x") + lax.axis_index("y")
    xs0, tgt = x[0], loss_target[0]

    small_sh = [given[n] for n in CHIP_SHARDED_SMALL]
    shard = {n: [given[n][l].astype(BF) for l in range(DEPTH)] for n in BIG}
    got = _exchange(_GatherChips([shard["w_in"][0], _pack(small_sh)]), name="gather_first")
    full = {n: given[n] for n in WEIGHTS if n not in BIG and n not in CHIP_SHARDED_SMALL}
    per_chip = [_unpack(got[1][j], [s.shape for s in small_sh]) for j in range(4)]
    parts = [jnp.stack([per_chip[j][i] for j in range(4)]) for i in range(len(small_sh))]
    for n, pj in zip(CHIP_SHARDED_SMALL, parts):
        full[n] = pj.transpose(1, 2, 0, 3).reshape(pj.shape[1], pj.shape[2], 4 * pj.shape[3])
    sp = {n: [full[n][l:l + 1] for l in range(DEPTH)] for n in ("pre_mix_norm", "post_mix_norm", "pre_ffn_norm", "post_ffn_norm")}
    sp["ffn_conv"] = [[full["ffn_conv_w"][l][:, :D_FF], full["ffn_conv_w"][l][:, D_FF:], full["ffn_conv_b"][l:l + 1, :D_FF],
                       full["ffn_conv_b"][l:l + 1, D_FF:]] for l in range(DEPTH)]
    mps = [_mixer_params({n: full[n][l] for n in MIXER_PARAMS}) for l in range(DEPTH)]

    h, = _rowwise(_f_pre, [(xs0, 0, D, False)], [(sp["pre_mix_norm"][0], False)], [(D, BF)], tm=TM_ROW, name="pre_mix0")
    a0, w_in1 = _layer_fwd(0, xs0, h, _w_in_full(got[0]), shard, sp, mps[0], shard["w_in"][1])
    xs1, h1 = _rowwise(_f_post_pre, [(a0["x1"], 0, D, False), (a0["y2"], 0, D, False)],
                       [(sp["post_ffn_norm"][0], False), (sp["pre_mix_norm"][1], False)], [(D, F32), (D, BF)], tm=TM_ROW, name="post_ffn0")
    a1, _ = _layer_fwd(1, xs1, h1, _w_in_full(w_in1), shard, sp, mps[1], None)
    lacc, dxo = _loss_head(a1["x1"], a1["y2"], sp["post_ffn_norm"][1], tgt, tm=TM_ROW, name="loss_head")

    gl = [None, None]
    (dx1, dy2), (gpf1,) = _rowwise_vjp(_f_post, [(a1["x1"], 0, D, False, F32), (a1["y2"], 0, D, False, BF)],
                                      [(sp["post_ffn_norm"][1], False, True)], [(dxo, 0, D)], tm=TM_ROW, name="post_ffn1_b")
    big = [None, None]
    dxs1, dh1, gl[1], big[1], dw_in1, _ = _layer_bwd(1, a1, dx1, dy2, sp, mps[1], None)
    gl[1]["post_ffn_norm"] = gpf1[0]
    (dx1, dy2), (gpf0, gpm1) = _rowwise_vjp(
        _f_post_pre, [(a0["x1"], 0, D, False, F32), (a0["y2"], 0, D, False, BF)],
        [(sp["post_ffn_norm"][0], False, True), (sp["pre_mix_norm"][1], False, True)], [(dxs1, 0, D), (dh1, 0, D)], tm=TM_ROW, name="post_ffn0_b")
    gl[1]["pre_mix_norm"] = gpm1[0]
    dxs0, dh0, gl[0], big[0], dw_in0, r_in1 = _layer_bwd(0, a0, dx1, dy2, sp, mps[0], dw_in1.astype(BF))
    big[1]["w_in"] = (dw_in1, r_in1)
    gl[0]["post_ffn_norm"] = gpf0[0]
    (grad_x,), (gpm0,) = _rowwise_vjp(lambda xv, w: (xv, _rms(xv, w)), [(xs0, 0, D, False, F32)], [(sp["pre_mix_norm"][0], False, True)],
                                     [(dxs0, 0, D), (dh0, 0, D)], tm=TM_ROW, name="pre_mix0_b")
    gl[0]["pre_mix_norm"] = gpm0[0]
    big[0]["w_in"] = (dw_in0, _exchange(_ScatterChips([dw_in0.astype(BF)]), name="scatter_last")[0])

    small = [n for n in WEIGHTS if n not in BIG]
    gfull = {n: jnp.stack([gl[0][n], gl[1][n]]) for n in small}
    tot = _sum_slots(_gather_devices(_pack([gfull[n] for n in small] + [lacc[0, 0:1]]), name="gather_small_grads"), name="sum_small_grads")
    red = dict(zip(small + ["loss"], _unpack(tot, [gfull[n].shape for n in small] + [(1,)])))
    for n in CHIP_SHARDED_SMALL:
        cb = given[n].shape[-1]
        red[n] = lax.dynamic_slice_in_dim(red[n], me * cb, cb, axis=red[n].ndim - 1)
    shapes = [given[n].shape for n in small]
    res = _adamw(_pack([given[n] for n in small]), _pack([given["m_" + n] for n in small]), _pack([given["v_" + n] for n in small]),
                 [[_pack([red[n] for n in small])]], tr=PACK_ROWS, name="adamw_small")
    outs = {kind: dict(zip(small, _unpack(r, shapes))) for kind, r in zip(("grad", "delta", "new_m", "new_v"), res)}

    keys = [(n, l) for n in BIG for l in range(DEPTH)]
    sums = []
    for n, l in keys:
        blocks, recv = big[l][n]
        own = lax.dynamic_index_in_dim(blocks, me, 0, keepdims=False)
        sums.append(_sum_parts(own, recv, tr=_row_tile(*own.shape), name="sum_grads_%s%d" % (n, l)))
    other = dict(zip(keys, _swap_cores(sums, name="swap_grad_sums")))
    sums = dict(zip(keys, sums))
    for n in BIG:
        shp = given[n].shape
        two = lambda z: z.reshape(shp[0] * shp[1], shp[2])
        r4 = _adamw(two(given[n]), two(given["m_" + n]), two(given["v_" + n]), [[sums[n, l], other[n, l]] for l in range(DEPTH)],
                    tr=_row_tile(shp[1], shp[2]), name="adamw_" + n)
        for kind, r in zip(("grad", "delta", "new_m", "new_v"), r4):
            outs[kind][n] = r.reshape(shp)

    return (red["loss"][0], grad_x[None], *[outs[k][n] for k in ("grad", "delta", "new_m", "new_v") for n in WEIGHTS])
```

```python
import functools
import math

import jax
import jax.numpy as jnp
from jax import lax
from jax.experimental import pallas as pl
from jax.experimental.pallas import tpu as pltpu

F32 = jnp.float32
BF = jnp.bfloat16
MESH_ID = pl.DeviceIdType.MESH

EPS = 1e-6
DEPTH = 2
D = 2048
DG = 512
HEADS = 4
HD = 128
CHUNK = 64
GDN_STEP = 4
LRU_C = 8.0
D_FF = 5632
N_IN = 5640
NP = 5760
OFF_Q, OFF_Z, OFF_LX, OFF_LG, OFF_UV, OFF_SB, OFF_SC, OFF_SH, OFF_BA = 0, 1536, 2048, 2560, 3072, 4096, 4608, 5120, 5632

ADAM_LR, ADAM_B1, ADAM_B2, ADAM_EPS, ADAM_WD, ADAM_STEP = 0.001, 0.9, 0.999, 1e-08, 0.01, 10

HALO = 8


def _mk_bdot(ca, cb):
    na, nb = 1 - ca, 1 - cb

    def dg(x, y, cx, cy):
        return lax.dot_general(x.astype(BF), y.astype(BF), (((cx,), (cy,)), ((), ())), preferred_element_type=F32)

    @jax.custom_vjp
    def f(a, b):
        return dg(a, b, ca, cb)

    def fwd(a, b):
        return dg(a, b, ca, cb), (a, b)

    def bwd(res, g):
        a, b = res
        da = dg(g, b, 1, nb) if ca == 1 else dg(b, g, nb, 1)
        db = dg(a, g, na, 0) if cb == 0 else dg(g, a, 0, na)
        return da.astype(a.dtype), db.astype(b.dtype)

    f.defvjp(fwd, bwd)
    return f


_bdot = _mk_bdot(1, 0)
_bdot_nt = _mk_bdot(1, 1)
_bdot_tn = _mk_bdot(0, 0)


def _sigmoid(x):
    return 1.0 / (1.0 + jnp.exp(-x))


def _silu(x):
    return x * _sigmoid(x)


def _gelu(x):
    return 0.5 * x * (1.0 + jnp.tanh(0.7978845608028654 * (x + 0.044715 * (x * x * x))))


def _log1p(z):
    u = 1.0 + z
    d = u - 1.0
    return jnp.where(d == 0.0, z, jnp.log(u) * (z / jnp.where(d == 0.0, 1.0, d)))


def _softplus(x):
    return jnp.maximum(x, 0.0) + _log1p(jnp.exp(-jnp.abs(x)))


def _neg_expm1(y):
    t = jnp.tanh(0.5 * y)
    return -2.0 * t / (1.0 - t)


def _rms(x, w):
    return x * lax.rsqrt(jnp.mean(x * x, axis=-1, keepdims=True) + EPS) * w


def _rows(x):
    return lax.broadcasted_iota(jnp.int32, x.shape, 0)


def _mk_shift():
    @functools.partial(jax.custom_vjp, nondiff_argnums=(1,))
    def shift(xx, s):
        n = xx.shape[0]
        return pltpu.roll(xx, s, axis=0)[HALO:n] if s else xx[HALO:n]

    def fwd(xx, s):
        return shift(xx, s), None

    def bwd(s, _, g):
        ext = jnp.concatenate([g, jnp.zeros((HALO, g.shape[1]), g.dtype)], axis=0)
        return (pltpu.roll(ext, HALO - s, axis=0),)

    shift.defvjp(fwd, bwd)
    return shift


_shift = _mk_shift()


def _mk_chunk_cumsum():
    def run(x, up):
        n = x.shape[0]
        pos = _rows(x) % CHUNK
        d = 1
        while d < CHUNK:
            if up:
                x = x + jnp.where(pos < CHUNK - d, pltpu.roll(x, n - d, axis=0), 0.0)
            else:
                x = x + jnp.where(pos >= d, pltpu.roll(x, d, axis=0), 0.0)
            d *= 2
        return x

    @jax.custom_vjp
    def cumsum(x):
        return run(x, False)

    cumsum.defvjp(lambda x: (run(x, False), None), lambda _, g: (run(g, True),))
    return cumsum


_chunk_cumsum = _mk_chunk_cumsum()


def _causal_conv(xx, w):
    K = w.shape[0]
    y = _shift(xx, K - 1) * w[0:1, :]
    for k in range(1, K):
        y = y + _shift(xx, K - 1 - k) * w[k:k + 1, :]
    return y


def _f_pre(x, w):
    return (_rms(x, w),)


def _f_post_pre(x, y, w_post, w_pre):
    x1 = x + _rms(y, w_post)
    return x1, _rms(x1, w_pre)


def _f_post(x, y, w_post):
    return (x + _rms(y, w_post),)


def _heads(fn, *xs):
    return jnp.concatenate([fn(*[x[:, h * HD:(h + 1) * HD] for x in xs]) for h in range(HEADS)], axis=1)


def _l2n(t):
    return t * lax.rsqrt(jnp.sum(t * t, axis=-1, keepdims=True) + EPS)


def _f_gdn_pre(qkv, ba, conv_w, alog, dtb):
    c = _silu(_causal_conv(qkv, conv_w))
    q = _heads(lambda t: _l2n(t) * (HD ** -0.5), c[:, 0:DG])
    k = _heads(_l2n, c[:, DG:2 * DG])
    v = c[:, 2 * DG:3 * DG]
    lane = lax.broadcasted_iota(jnp.int32, ba.shape, 1)
    beta = _sigmoid(ba)
    gcum = _chunk_cumsum(-jnp.exp(alog) * _softplus(ba + dtb))
    gc = jnp.where(lane < HEADS, beta, jnp.where(lane < 2 * HEADS, gcum, 0.0))
    return q, k, v, gc


def _f_gdn_chunk(q, k, v, gc, s0):
    C = q.shape[0]
    HC = HEADS * C
    sh = C.bit_length() - 1

    def stack(x):
        return jnp.concatenate([x[:, h * HD:(h + 1) * HD] for h in range(HEADS)], axis=0)

    def column(off):
        lane = lax.broadcasted_iota(jnp.int32, gc.shape, 1)
        return jnp.concatenate([jnp.sum(jnp.where(lane == off + h, gc, 0.0), axis=1, keepdims=True) for h in range(HEADS)], axis=0)

    def head(x, h):
        return x[h * C:(h + 1) * C]

    r = lax.broadcasted_iota(jnp.int32, (HC, HC), 0)
    c = lax.broadcasted_iota(jnp.int32, (HC, HC), 1)
    same = jnp.right_shift(r, sh) == jnp.right_shift(c, sh)
    causal = jnp.logical_and(same, r >= c)
    strict = jnp.logical_and(same, r > c)
    gcol, bcol = column(HEADS), column(0)
    grow = jnp.sum(jnp.where(r == c, gcol, 0.0), axis=0, keepdims=True)
    decay = jnp.where(causal, jnp.exp(jnp.where(causal, gcol - grow, 0.0)), 0.0)
    ks, qs, vs = stack(k), stack(q), stack(v)
    kb = ks * bcol
    kk = _bdot_nt(jnp.concatenate([kb, qs], axis=0), ks)
    m = jnp.where(strict, kk[0:HC] * decay, 0.0)
    attn = jnp.where(causal, kk[HC:2 * HC] * decay, 0.0)
    n = -m
    t = (r == c).astype(F32) + n
    p = n
    for _ in range(5):
        p = _bdot(p, p)
        t = t + _bdot(t, p)
    eg = jnp.exp(gcol)
    wu = _bdot(t, jnp.concatenate([kb * eg, vs * bcol], axis=1))
    w, u = wu[:, 0:HD], wu[:, HD:2 * HD]
    last = jnp.logical_and(same, jnp.bitwise_and(c, C - 1) == C - 1)
    glast = jnp.sum(jnp.where(last, grow, 0.0), axis=1, keepdims=True)
    k_g = ks * jnp.exp(glast - gcol)
    q_g = qs * eg
    ws = [_bdot(jnp.concatenate([head(w, h), head(q_g, h)], axis=0), s0[h * HD:(h + 1) * HD]) for h in range(HEADS)]
    v_new = u - jnp.concatenate([x[0:C] for x in ws], axis=0)
    o = jnp.concatenate([x[C:2 * C] for x in ws], axis=0) + _bdot(attn, v_new)
    s1 = [s0[h * HD:(h + 1) * HD] * jnp.exp(glast[h * C:h * C + 1]) + _bdot_tn(head(k_g, h), head(v_new, h)) for h in range(HEADS)]
    return jnp.concatenate([head(o, h) for h in range(HEADS)], axis=1), jnp.concatenate(s1, axis=0)


def _f_gdn_chunks(q, k, v, gc, s0):
    outs, s = [], s0
    for n in range(q.shape[0] // CHUNK):
        rs = slice(n * CHUNK, (n + 1) * CHUNK)
        o, s = _f_gdn_chunk(q[rs], k[rs], v[rs], gc[rs], s)
        outs.append(o)
    return jnp.concatenate(outs, axis=0), s


def _f_gdn_post(o, z, nw):
    return (_heads(lambda a, b: _rms(a, nw) * _silu(b), o, z),)


def _f_lru_ab(lx, conv_w, conv_b, wa, ba, wx, bx, lam):
    xc = _causal_conv(lx, conv_w) + conv_b
    r = _sigmoid(_bdot(xc, wa) + ba)
    i = _sigmoid(_bdot(xc, wx) + bx)
    log_a = -LRU_C * r * _softplus(-lam)
    a = jnp.exp(log_a)
    mult = jnp.sqrt(_neg_expm1(2.0 * log_a))
    return a, mult * (i * xc)


def _f_lru_post(hs, gate, gw):
    return (_rms(hs * _gelu(gate), gw),)


def _f_sgu(uv, ln_w, ln_b, ws, bst, gw):
    tm = uv.shape[0]
    uvf = _gelu(uv)
    u, v = uvf[:, 0:DG], uvf[:, DG:2 * DG]
    mu = jnp.mean(v, axis=-1, keepdims=True)
    vc = v - mu
    v = vc * lax.rsqrt(jnp.mean(vc * vc, axis=-1, keepdims=True) + EPS) * ln_w + ln_b
    lane = lax.broadcasted_iota(jnp.int32, bst.shape, 1)
    tril = lax.broadcasted_iota(jnp.int32, (128, 128), 0) >= lax.broadcasted_iota(jnp.int32, (128, 128), 1)
    wsm = [jnp.where(tril, ws[g * 128:(g + 1) * 128, :], 0.0) for g in range(4)]
    bias = [jnp.sum(jnp.where(lane == g, bst, 0.0), axis=1, keepdims=True) for g in range(4)]
    out = []
    for n in range(tm // 128):
        vn = v[n * 128:(n + 1) * 128, :]
        gs = [_bdot(wsm[g], vn[:, g * 128:(g + 1) * 128]) + bias[g] for g in range(4)]
        out.append(jnp.concatenate(gs, axis=1))
    vo = jnp.concatenate(out, axis=0) if len(out) > 1 else out[0]
    return (_rms(u * vo, gw),)


def _f_sconv(sb, sc, sh, conv_w, gw):
    return (_rms(sb * _causal_conv(sc * sh, conv_w), gw),)


def _f_ffn_act(ug, uv, wg, wv, bg, bv):
    return (_gelu(_causal_conv(ug, wg) + bg) * (_causal_conv(uv, wv) + bv),)


def _row_specs(rows, consts, tm, ncol, tile_of):
    specs, args = [], []
    for arr, cb, w, halo in rows:
        specs.append(pl.BlockSpec((tm, w), lambda j, i, cb=cb: (tile_of(i), cb + j)))
        args.append(arr)
        if halo:
            specs.append(pl.BlockSpec((HALO, w), lambda j, i, cb=cb: (jnp.maximum(tile_of(i) * (tm // HALO) - 1, 0), cb + j)))
            args.append(arr)
    for arr, tiled in consts:
        r, c = arr.shape
        specs.append(pl.BlockSpec((r, c // ncol), lambda j, i: (0, j)) if tiled else pl.BlockSpec((r, c), lambda j, i: (0, 0)))
        args.append(arr)
    return specs, args


def _load_rows(refs, rows, consts, tile):
    k, vals = 0, []
    for _arr, _cb, _w, halo in rows:
        t = refs[k][...].astype(F32)
        k += 1
        if halo:
            hl = jnp.where(tile > 0, refs[k][...].astype(F32), 0.0)
            k += 1
            t = jnp.concatenate([hl, t], axis=0)
        vals.append(t)
    for _ in consts:
        vals.append(refs[k][...])
        k += 1
    return vals, k


def _rowwise(fn, rows, consts, outs, *, tm, name, ncol=1, xch=None):
    n = rows[0][0].shape[0]
    nt = n // tm
    in_specs, args = _row_specs(rows, consts, tm, ncol, lambda i: i)

    def body(*refs):
        vals, k = _load_rows(refs, rows, consts, pl.program_id(1))
        for o_ref, r in zip(refs[k:], fn(*vals)):
            o_ref[...] = r.astype(o_ref.dtype)

    res, xres = _call(
        body, grid=(ncol, nt), in_specs=in_specs,
        out_specs=[pl.BlockSpec((tm, w), lambda j, i: (i, j)) for w, _ in outs],
        out_shape=[jax.ShapeDtypeStruct((n, w * ncol), dt) for w, dt in outs], scratch=[], name=name, args=args, xch=xch)
    return res if xch is None else (res, xres)


def _rowwise_vjp(fn, rows, consts, cots, *, tm, name, ncol=1):
    n = rows[0][0].shape[0]
    nt = n // tm
    rows4 = [r[:4] for r in rows]
    consts2 = [c[:2] for c in consts]
    in_specs, args = _row_specs(rows4, consts2, tm, ncol, lambda i: nt - 1 - i)
    for arr, cb, w in cots:
        in_specs.append(pl.BlockSpec((tm, w), lambda j, i, cb=cb: (nt - 1 - i, cb + j)))
        args.append(arr)
    out_specs, out_shape, scratch = [], [], []
    for arr, cb, w, halo, gdt in rows:
        if gdt is not None:
            out_specs.append(pl.BlockSpec((tm, w), lambda j, i: (nt - 1 - i, j)))
            out_shape.append(jax.ShapeDtypeStruct((n, w * ncol), gdt))
            if halo:
                scratch.append(pltpu.VMEM((HALO, w), F32))
    for arr, tiled, want in consts:
        if want:
            r, c = arr.shape
            out_specs.append(pl.BlockSpec((r, c // ncol), lambda j, i: (0, j)) if tiled else pl.BlockSpec((r, c), lambda j, i: (0, 0)))
            out_shape.append(jax.ShapeDtypeStruct((r, c), F32))
    n_in = len(in_specs)
    n_out = len(out_specs)

    def body(*refs):
        j, i = pl.program_id(0), pl.program_id(1)
        tile = nt - 1 - i
        vals, k = _load_rows(refs, rows4, consts2, tile)
        cvals = [refs[k + q][...].astype(F32) for q in range(len(cots))]
        outs = refs[n_in:n_in + n_out]
        carries = refs[n_in + n_out:]
        _, vjp = jax.vjp(fn, *vals)
        g = vjp(tuple(cvals))
        o, cidx = 0, 0
        for q, (arr, cb, w, halo, gdt) in enumerate(rows):
            if gdt is None:
                continue
            if halo:
                ge = g[q]
                main = ge[HALO:]
                carry = carries[cidx]
                cidx += 1
                tail = main[tm - HALO:] + jnp.where(i > 0, carry[...], 0.0)
                outs[o][0:tm - HALO, :] = main[0:tm - HALO].astype(gdt)
                outs[o][tm - HALO:tm, :] = tail.astype(gdt)
                carry[...] = ge[0:HALO]
            else:
                outs[o][...] = g[q].astype(gdt)
            o += 1
        for q, (arr, tiled, want) in enumerate(consts):
            if not want:
                continue
            first = (i == 0) if tiled else jnp.logical_and(i == 0, j == 0)
            acc = outs[o]
            gq = g[len(rows) + q].astype(F32)

            @pl.when(first)
            def _(acc=acc, gq=gq):
                acc[...] = gq

            @pl.when(jnp.logical_not(first))
            def _(acc=acc, gq=gq):
                acc[...] += gq

            o += 1

    res = pl.pallas_call(
        body, grid=(ncol, nt), in_specs=in_specs, out_specs=out_specs, out_shape=out_shape, scratch_shapes=scratch,
        name=name, compiler_params=pltpu.CompilerParams(dimension_semantics=("arbitrary", "arbitrary")),
    )(*args)
    nrow = sum(1 for r in rows if r[4] is not None)
    return list(res[:nrow]), list(res[nrow:])


_ANY = pl.BlockSpec(memory_space=pl.ANY)
_CHIP_REL = ((1, 0), (0, 1), (1, 1))
_DEV_REL = tuple((r >> 2 & 1, r >> 1 & 1, r & 1) for r in range(1, 8))


def _flip(v, f):
    return 1 - v if f else v


class _GatherChips:
    def __init__(self, shards):
        self.arrs = list(shards)
        n = len(self.arrs)
        self.out_shape = [jax.ShapeDtypeStruct((4,) + s.shape, s.dtype) for s in self.arrs]
        self.sems = [pltpu.SemaphoreType.DMA((3 * n,)), pltpu.SemaphoreType.DMA((3 * n,)), pltpu.SemaphoreType.DMA((n,))]

    def _copies(self, ins, outs, sems, arriving):
        send, recv, lsem = sems
        x, y, c = lax.axis_index("x"), lax.axis_index("y"), lax.axis_index("c")
        me = 2 * x + y
        if arriving:
            local = []
        else:
            local = [pltpu.make_async_copy(ins[a], outs[a].at[me], lsem.at[a]) for a in range(len(ins))]
        remote = []
        for a in range(len(ins)):
            for k, (fx, fy) in enumerate(_CHIP_REL):
                px, py = _flip(x, fx), _flip(y, fy)
                remote.append(pltpu.make_async_remote_copy(
                    src_ref=ins[a], dst_ref=outs[a].at[2 * px + py if arriving else me], send_sem=send.at[3 * a + k],
                    recv_sem=recv.at[3 * a + k], device_id=(px, py, c), device_id_type=MESH_ID))
        return local, remote

    def issue(self, ins, outs, sems):
        local, push = self._copies(ins, outs, sems, False)
        for cp in local + push:
            cp.start()

    def finish(self, ins, outs, sems):
        for cp in self._copies(ins, outs, sems, True)[1]:
            cp.wait_recv()
        local, push = self._copies(ins, outs, sems, False)
        for cp in push:
            cp.wait_send()
        for cp in local:
            cp.wait()


class _ScatterChips:
    def __init__(self, blocks):
        self.arrs = list(blocks)
        n = len(self.arrs)
        self.out_shape = [jax.ShapeDtypeStruct((3,) + b.shape[1:], b.dtype) for b in self.arrs]
        self.sems = [pltpu.SemaphoreType.DMA((3 * n,)), pltpu.SemaphoreType.DMA((3 * n,))]

    def _copies(self, ins, outs, sems):
        send, recv = sems
        x, y, c = lax.axis_index("x"), lax.axis_index("y"), lax.axis_index("c")
        copies = []
        for a in range(len(ins)):
            for k, (fx, fy) in enumerate(_CHIP_REL):
                px, py = _flip(x, fx), _flip(y, fy)
                copies.append(pltpu.make_async_remote_copy(
                    src_ref=ins[a].at[2 * px + py], dst_ref=outs[a].at[k], send_sem=send.at[3 * a + k], recv_sem=recv.at[3 * a + k],
                    device_id=(px, py, c), device_id_type=MESH_ID))
        return copies

    def issue(self, ins, outs, sems):
        for cp in self._copies(ins, outs, sems):
            cp.start()

    def finish(self, ins, outs, sems):
        copies = self._copies(ins, outs, sems)
        for cp in copies:
            cp.wait_recv()
        for cp in copies:
            cp.wait_send()


def _exchange(xch, *, name):
    ni, no = len(xch.arrs), len(xch.out_shape)

    def body(*refs):
        ins, outs, sems = refs[:ni], refs[ni:ni + no], refs[ni + no:]
        xch.issue(ins, outs, sems)
        xch.finish(ins, outs, sems)

    return pl.pallas_call(body, in_specs=[_ANY] * ni, out_specs=[_ANY] * no, out_shape=xch.out_shape, scratch_shapes=xch.sems, name=name)(*xch.arrs)


def _call(body, *, grid, in_specs, out_specs, out_shape, scratch, name, args, xch=None):
    params = pltpu.CompilerParams(dimension_semantics=("arbitrary",) * len(grid))
    if xch is None:
        res = pl.pallas_call(body, grid=grid, in_specs=in_specs, out_specs=out_specs, out_shape=out_shape, scratch_shapes=scratch,
                             name=name, compiler_params=params)(*args)
        return list(res), []
    n_in, n_out, n_sc = len(in_specs), len(out_specs), len(scratch)
    xi, xo = len(xch.arrs), len(xch.out_shape)

    def wrapped(*refs):
        ins, refs = refs[:n_in], refs[n_in:]
        xin, refs = refs[:xi], refs[xi:]
        outs, refs = refs[:n_out], refs[n_out:]
        xout, refs = refs[:xo], refs[xo:]
        sc, sems = refs[:n_sc], refs[n_sc:]
        first = functools.reduce(jnp.logical_and, [pl.program_id(d) == 0 for d in range(len(grid))])
        last = functools.reduce(jnp.logical_and, [pl.program_id(d) == grid[d] - 1 for d in range(len(grid))])

        @pl.when(first)
        def _():
            xch.issue(xin, xout, sems)

        body(*ins, *outs, *sc)

        @pl.when(last)
        def _():
            xch.finish(xin, xout, sems)

    res = pl.pallas_call(
        wrapped, grid=grid, in_specs=list(in_specs) + [_ANY] * xi, out_specs=list(out_specs) + [_ANY] * xo,
        out_shape=list(out_shape) + xch.out_shape, scratch_shapes=list(scratch) + xch.sems, name=name, compiler_params=params,
    )(*args, *xch.arrs)
    return list(res[:n_out]), list(res[n_out:])


def _matmul(a, b, mode, out_dtype, *, tm, tn, tk, name, b_blocks=False, out_blocks=False, xch=None):
    if b_blocks:
        _, br, bc4 = b.shape
        b2 = (br, 4 * bc4)
    else:
        b2 = b.shape

    def bspec(shape, index):
        if not b_blocks:
            return pl.BlockSpec(shape, index)
        per = bc4 // shape[1]

        def blocked(i, j, k):
            r, c = index(i, j, k)
            return (c // per, r, c % per)

        return pl.BlockSpec((None,) + shape, blocked)

    tm = min(tm, a.shape[1] if mode == "tn" else a.shape[0])
    tk = min(tk, a.shape[0] if mode == "tn" else a.shape[1])
    if mode == "tn":
        K, M = a.shape
        N = b2[1]
        a_spec = pl.BlockSpec((tk, tm), lambda i, j, k: (k, i))
        b_spec = bspec((tk, tn), lambda i, j, k: (k, j))
        dims = (((0,), (0,)), ((), ()))
    elif mode == "nt":
        M, K = a.shape
        N = b2[0]
        a_spec = pl.BlockSpec((tm, tk), lambda i, j, k: (i, k))
        b_spec = bspec((tn, tk), lambda i, j, k: (j, k))
        dims = (((1,), (1,)), ((), ()))
    else:
        M, K = a.shape
        N = b2[1]
        a_spec = pl.BlockSpec((tm, tk), lambda i, j, k: (i, k))
        b_spec = bspec((tk, tn), lambda i, j, k: (k, j))
        dims = (((1,), (0,)), ((), ()))
    assert M % tm == 0 and N % tn == 0 and K % tk == 0, (name, M, N, K, tm, tn, tk)
    nk = K // tk
    if out_blocks:
        per_o = (N // 4) // tn
        o_spec = pl.BlockSpec((None, tm, tn), lambda i, j, k: (j // per_o, i, j % per_o))
        o_shape = jax.ShapeDtypeStruct((4, M, N // 4), out_dtype)
    else:
        o_spec = pl.BlockSpec((tm, tn), lambda i, j, k: (i, j))
        o_shape = jax.ShapeDtypeStruct((M, N), out_dtype)

    def body(a_ref, b_ref, o_ref, acc_ref):
        k = pl.program_id(2)
        part = lax.dot_general(a_ref[...].astype(BF), b_ref[...].astype(BF), dims, preferred_element_type=F32)
        if nk == 1:
            o_ref[...] = part.astype(o_ref.dtype)
        else:
            @pl.when(k == 0)
            def _():
                acc_ref[...] = part

            @pl.when(k > 0)
            def _():
                acc_ref[...] += part

            @pl.when(k == nk - 1)
            def _():
                o_ref[...] = acc_ref[...].astype(o_ref.dtype)

    res, xres = _call(body, grid=(M // tm, N // tn, nk), in_specs=[a_spec, b_spec], out_specs=[o_spec], out_shape=[o_shape],
                      scratch=[pltpu.VMEM((tm, tn) if nk > 1 else (8, 128), F32)], name=name, args=(a, b), xch=xch)
    return res[0], xres


def _scan_fwd(a, b, *, tm, name):
    n, c = a.shape
    nt = n // tm

    def body(a_ref, b_ref, h_ref, carry):
        i = pl.program_id(0)
        av, bv = a_ref[...], b_ref[...]
        row = _rows(av)
        d = 1
        while d < tm:
            a_s = jnp.where(row >= d, pltpu.roll(av, d, axis=0), 1.0)
            b_s = jnp.where(row >= d, pltpu.roll(bv, d, axis=0), 0.0)
            bv = av * b_s + bv
            av = av * a_s
            d *= 2
        h = bv + av * jnp.where(i > 0, carry[HALO - 1:HALO, :], 0.0)
        h_ref[...] = h
        carry[...] = h[tm - HALO:tm]

    return pl.pallas_call(
        body, grid=(nt,), in_specs=[pl.BlockSpec((tm, c), lambda i: (i, 0))] * 2,
        out_specs=pl.BlockSpec((tm, c), lambda i: (i, 0)), out_shape=jax.ShapeDtypeStruct((n, c), F32),
        scratch_shapes=[pltpu.VMEM((HALO, c), F32)], name=name,
        compiler_params=pltpu.CompilerParams(dimension_semantics=("arbitrary",)),
    )(a, b)


def _scan_bwd(a, h, dh, *, tm, name):
    n, c = a.shape
    nt = n // tm
    tb = tm // HALO

    def body(a_ref, an_ref, h_ref, hp_ref, dh_ref, da_ref, db_ref, carry):
        i = pl.program_id(0)
        tile = nt - 1 - i
        av, hv, g = a_ref[...], h_ref[...], dh_ref[...]
        row = _rows(av)
        a_next = jnp.where(tile < nt - 1, an_ref[0:1, :], 0.0)
        au = jnp.where(row < tm - 1, pltpu.roll(av, tm - 1, axis=0), a_next)
        d = 1
        while d < tm:
            a_s = jnp.where(row < tm - d, pltpu.roll(au, tm - d, axis=0), 1.0)
            g_s = jnp.where(row < tm - d, pltpu.roll(g, tm - d, axis=0), 0.0)
            g = au * g_s + g
            au = au * a_s
            d *= 2
        g = g + au * jnp.where(i > 0, carry[0:1, :], 0.0)
        h_prev = jnp.where(row >= 1, pltpu.roll(hv, 1, axis=0), jnp.where(tile > 0, hp_ref[HALO - 1:HALO, :], 0.0))
        db_ref[...] = g
        da_ref[...] = g * h_prev
        carry[...] = g[0:HALO]

    cur = pl.BlockSpec((tm, c), lambda i: (nt - 1 - i, 0))
    nxt = pl.BlockSpec((HALO, c), lambda i: (jnp.minimum((nt - i) * tb, n // HALO - 1), 0))
    prv = pl.BlockSpec((HALO, c), lambda i: (jnp.maximum((nt - 1 - i) * tb - 1, 0), 0))
    return pl.pallas_call(
        body, grid=(nt,), in_specs=[cur, nxt, cur, prv, cur], out_specs=[cur, cur],
        out_shape=[jax.ShapeDtypeStruct((n, c), F32)] * 2, scratch_shapes=[pltpu.VMEM((HALO, c), F32)], name=name,
        compiler_params=pltpu.CompilerParams(dimension_semantics=("arbitrary",)),
    )(a, a, h, h, dh)


def _gdn_fwd(q, k, v, gb, *, name, xch=None):
    n = q.shape[0]
    rows = GDN_STEP * CHUNK
    ns = n // rows

    def body(q_ref, k_ref, v_ref, gb_ref, o_ref, s_ref, state):
        @pl.when(pl.program_id(0) == 0)
        def _():
            state[...] = jnp.zeros_like(state)

        s0 = state[...]
        s_ref[0] = s0
        o, s1 = _f_gdn_chunks(q_ref[...], k_ref[...], v_ref[...], gb_ref[...], s0)
        o_ref[...] = o
        state[...] = s1

    row = pl.BlockSpec((rows, DG), lambda i: (i, 0))
    return _call(
        body, grid=(ns,), in_specs=[row, row, row, pl.BlockSpec((rows, 128), lambda i: (i, 0))],
        out_specs=[row, pl.BlockSpec((1, DG, HD), lambda i: (i, 0, 0))],
        out_shape=[jax.ShapeDtypeStruct((n, DG), F32), jax.ShapeDtypeStruct((ns, DG, HD), F32)],
        scratch=[pltpu.VMEM((DG, HD), F32)], name=name, args=(q, k, v, gb), xch=xch)


def _gdn_bwd(q, k, v, gb, s_all, do, *, name, xch=None):
    n = q.shape[0]
    rows = GDN_STEP * CHUNK
    ns = n // rows

    def body(q_ref, k_ref, v_ref, gb_ref, s_ref, do_ref, dq_ref, dk_ref, dv_ref, dgb_ref, dstate):
        @pl.when(pl.program_id(0) == 0)
        def _():
            dstate[...] = jnp.zeros_like(dstate)

        _, vjp = jax.vjp(_f_gdn_chunks, q_ref[...], k_ref[...], v_ref[...], gb_ref[...], s_ref[0])
        dq_ref[...], dk_ref[...], dv_ref[...], dgb_ref[...], dstate[...] = vjp((do_ref[...], dstate[...]))

    row = pl.BlockSpec((rows, DG), lambda i: (ns - 1 - i, 0))
    gsp = pl.BlockSpec((rows, 128), lambda i: (ns - 1 - i, 0))
    return _call(
        body, grid=(ns,), in_specs=[row, row, row, gsp, pl.BlockSpec((1, DG, HD), lambda i: (ns - 1 - i, 0, 0)), row],
        out_specs=[row, row, row, gsp],
        out_shape=[jax.ShapeDtypeStruct((n, DG), F32)] * 3 + [jax.ShapeDtypeStruct((n, 128), F32)],
        scratch=[pltpu.VMEM((DG, HD), F32)], name=name, args=(q, k, v, gb, s_all, do), xch=xch)


def _loss_head(x1, y2, w, tgt, *, tm, name):
    n, c = x1.shape

    def body(x_ref, y_ref, w_ref, t_ref, l_ref, d_ref):
        err = x_ref[...] + _rms(y_ref[...], w_ref[...]) - t_ref[...]
        part = jnp.sum(jnp.sum(err * err, axis=1, keepdims=True), axis=0, keepdims=True) * (0.5 / c)

        @pl.when(pl.program_id(0) == 0)
        def _():
            l_ref[...] = jnp.zeros_like(l_ref)

        l_ref[...] += part
        d_ref[...] = err * (1.0 / c)

    row = pl.BlockSpec((tm, c), lambda i: (i, 0))
    return pl.pallas_call(
        body, grid=(n // tm,), in_specs=[row, row, pl.BlockSpec((1, c), lambda i: (0, 0)), row],
        out_specs=[pl.BlockSpec((8, 128), lambda i: (0, 0)), row],
        out_shape=[jax.ShapeDtypeStruct((8, 128), F32), jax.ShapeDtypeStruct((n, c), F32)], name=name,
        compiler_params=pltpu.CompilerParams(dimension_semantics=("arbitrary",)),
    )(x1, y2, w, tgt)


def _sum_parts(own, recv, *, tr, name):
    r, c = own.shape
    p = recv.shape[0]

    def body(o_ref, r_ref, s_ref):
        s = o_ref[...]
        for q in range(p):
            s = s + r_ref[q].astype(F32)
        s_ref[...] = s

    return pl.pallas_call(
        body, grid=(r // tr,), in_specs=[pl.BlockSpec((tr, c), lambda i: (i, 0)), pl.BlockSpec((p, tr, c), lambda i: (0, i, 0))],
        out_specs=pl.BlockSpec((tr, c), lambda i: (i, 0)), out_shape=jax.ShapeDtypeStruct((r, c), F32), name=name,
        compiler_params=pltpu.CompilerParams(dimension_semantics=("parallel",)),
    )(own, recv)


def _sum_slots(buf, *, name):
    p, r, c = buf.shape

    def body(b_ref, s_ref):
        s = b_ref[0]
        for q in range(1, p):
            s = s + b_ref[q]
        s_ref[...] = s

    return pl.pallas_call(body, out_shape=jax.ShapeDtypeStruct((r, c), F32), name=name)(buf)


def _adamw(w, m, v, gs, *, tr, name):
    r, c = w.shape
    ngrp = len(gs)
    nterm = len(gs[0])
    per = r // ngrp // tr
    c1 = 1.0 - ADAM_B1 ** ADAM_STEP
    c2 = 1.0 - ADAM_B2 ** ADAM_STEP

    def body(*refs):
        w_ref, m_ref, v_ref = refs[:3]
        g_refs = refs[3:3 + ngrp * nterm]
        g_ref, d_ref, nm_ref, nv_ref = refs[3 + ngrp * nterm:]
        grp = pl.program_id(0) // per
        g = None
        for q in range(ngrp):
            gq = g_refs[q * nterm][...]
            for t in range(1, nterm):
                gq = gq + g_refs[q * nterm + t][...]
            g = gq if g is None else jnp.where(grp == q, gq, g)
        nm = ADAM_B1 * m_ref[...] + (1.0 - ADAM_B1) * g
        nv = ADAM_B2 * v_ref[...] + (1.0 - ADAM_B2) * (g * g)
        g_ref[...] = g
        d_ref[...] = -ADAM_LR * ((nm / c1) / (jnp.sqrt(nv / c2) + ADAM_EPS) + ADAM_WD * w_ref[...])
        nm_ref[...] = nm
        nv_ref[...] = nv

    blk = pl.BlockSpec((tr, c), lambda i: (i, 0))
    g_specs = [pl.BlockSpec((tr, c), lambda i, q=q: (jnp.clip(i - q * per, 0, per - 1), 0)) for q in range(ngrp) for _ in range(nterm)]
    return pl.pallas_call(
        body, grid=(r // tr,), in_specs=[blk] * 3 + g_specs, out_specs=[blk] * 4,
        out_shape=[jax.ShapeDtypeStruct((r, c), F32)] * 4, name=name,
        compiler_params=pltpu.CompilerParams(dimension_semantics=("arbitrary",)),
    )(w, m, v, *[t for grp in gs for t in grp])


def _swap_cores(arrs, *, name):
    na = len(arrs)

    def body(*refs):
        ins, outs = refs[:na], refs[na:2 * na]
        send, recv = refs[2 * na:]
        sib = (lax.axis_index("x"), lax.axis_index("y"), 1 - lax.axis_index("c"))
        copies = []
        for a in range(na):
            cp = pltpu.make_async_remote_copy(src_ref=ins[a], dst_ref=outs[a], send_sem=send.at[a], recv_sem=recv.at[a],
                                              device_id=sib, device_id_type=MESH_ID)
            cp.start()
            copies.append(cp)
        for cp in copies:
            cp.wait_recv()
        for cp in copies:
            cp.wait_send()

    return pl.pallas_call(
        body, in_specs=[_ANY] * na, out_specs=[_ANY] * na,
        out_shape=[jax.ShapeDtypeStruct(a.shape, a.dtype) for a in arrs],
        scratch_shapes=[pltpu.SemaphoreType.DMA((na,)), pltpu.SemaphoreType.DMA((na,))], name=name,
    )(*arrs)


def _gather_devices(buf, *, name):
    def body(in_ref, out_ref, send, recv, lsem):
        x, y, c = lax.axis_index("x"), lax.axis_index("y"), lax.axis_index("c")
        me = 4 * x + 2 * y + c
        loc = pltpu.make_async_copy(in_ref, out_ref.at[me], lsem)
        loc.start()
        pairs = []
        for k, (fx, fy, fc) in enumerate(_DEV_REL):
            px, py, pc = _flip(x, fx), _flip(y, fy), _flip(c, fc)
            cp = pltpu.make_async_remote_copy(src_ref=in_ref, dst_ref=out_ref.at[me], send_sem=send.at[k], recv_sem=recv.at[k],
                                              device_id=(px, py, pc), device_id_type=MESH_ID)
            cp.start()
            arrive = pltpu.make_async_remote_copy(src_ref=in_ref, dst_ref=out_ref.at[4 * px + 2 * py + pc], send_sem=send.at[k],
                                                  recv_sem=recv.at[k], device_id=(px, py, pc), device_id_type=MESH_ID)
            pairs.append((cp, arrive))
        for cp, arrive in pairs:
            arrive.wait_recv()
        for cp, arrive in pairs:
            cp.wait_send()
        loc.wait()

    return pl.pallas_call(
        body, in_specs=[_ANY], out_specs=_ANY, out_shape=jax.ShapeDtypeStruct((8,) + buf.shape, buf.dtype),
        scratch_shapes=[pltpu.SemaphoreType.DMA((7,)), pltpu.SemaphoreType.DMA((7,)), pltpu.SemaphoreType.DMA(())], name=name,
    )(buf)


def _pad_cols(w):
    z = jnp.zeros(w.shape[:-1] + (NP - N_IN,), w.dtype)
    return jnp.concatenate([w[..., 0:2048], w[..., 2056:N_IN], w[..., 2048:2056], z], axis=-1)


def _unpad_cols(w):
    return jnp.concatenate([w[..., 0:2048], w[..., OFF_BA:OFF_BA + 8], w[..., 2048:OFF_BA]], axis=-1)


def _lanes(v, off):
    return jnp.pad(v.reshape(1, -1), ((0, 0), (off, 128 - off - v.size)))


def _block_diag(w):
    eye = jnp.eye(8, dtype=w.dtype)
    return (w[:, :, None, :] * eye[:, None, :, None]).reshape(DG, DG)


def _diag_blocks(w):
    return jnp.stack([w[h * 64:(h + 1) * 64, h * 64:(h + 1) * 64] for h in range(8)])


def _mixer_params(p):
    return dict(
        gdn_conv_w=p["gdn_conv_w"], alog=_lanes(p["gdn_a_log"], HEADS), dtb=_lanes(p["gdn_dt_bias"], HEADS),
        gdn_nw=p["gdn_norm_w"].reshape(1, HD),
        lru_conv_w=p["lru_conv_w"], lru_conv_b=p["lru_conv_b"].reshape(1, DG),
        wa=_block_diag(p["lru_wa"]), ba=p["lru_ba"].reshape(1, DG), wx=_block_diag(p["lru_wx"]), bx=p["lru_bx"].reshape(1, DG),
        lam=p["lru_lambda"].reshape(1, DG),
        ln_w=p["sgu_ln_w"].reshape(1, DG), ln_b=p["sgu_ln_b"].reshape(1, DG), ws=p["sgu_ws"].reshape(DG, 128),
        bst=jnp.pad(p["sgu_b"].T, ((0, 0), (0, 124))),
        sconv_w=p["sconv_w"], gw0=p["grp_norm_w"][0:1], gw1=p["grp_norm_w"][1:2], gw2=p["grp_norm_w"][2:3],
    )


def _mixer_param_grads(g):
    return dict(
        gdn_conv_w=g["gdn_conv_w"], gdn_a_log=g["alog"][0, HEADS:2 * HEADS], gdn_dt_bias=g["dtb"][0, HEADS:2 * HEADS],
        gdn_norm_w=g["gdn_nw"][0],
        lru_conv_w=g["lru_conv_w"], lru_conv_b=g["lru_conv_b"][0],
        lru_wa=_diag_blocks(g["wa"]), lru_ba=g["ba"].reshape(8, 64), lru_wx=_diag_blocks(g["wx"]), lru_bx=g["bx"].reshape(8, 64),
        lru_lambda=g["lam"][0],
        sgu_ln_w=g["ln_w"][0], sgu_ln_b=g["ln_b"][0], sgu_ws=g["ws"].reshape(4, 128, 128), sgu_b=g["bst"][:, 0:4].T,
        sconv_w=g["sconv_w"], grp_norm_w=jnp.concatenate([g["gw0"], g["gw1"], g["gw2"]], axis=0),
    )


TM_MIX = 256


def _mixers_fwd(p, mp, tag="", xch=None):
    c = lambda *names: [(mp[n], False) for n in names]
    q, k, v, gb = _rowwise(_f_gdn_pre, [(p, 0, 1536, True), (p, OFF_BA // 128, 128, False)], c("gdn_conv_w", "alog", "dtb"),
                           [(DG, F32)] * 3 + [(128, F32)], tm=TM_MIX, name="gdn_pre" + tag)
    (o, s_all), xres = _gdn_fwd(q, k, v, gb, name="gdn_chunks" + tag, xch=xch)
    y_a, = _rowwise(_f_gdn_post, [(o, 0, DG, False), (p, OFF_Z // DG, DG, False)], c("gdn_nw"), [(DG, BF)], tm=TM_MIX, name="gdn_post" + tag)
    a, b = _rowwise(_f_lru_ab, [(p, OFF_LX // DG, DG, True)], c("lru_conv_w", "lru_conv_b", "wa", "ba", "wx", "bx", "lam"),
                    [(DG, F32)] * 2, tm=TM_MIX, name="lru_ab" + tag)
    hs = _scan_fwd(a, b, tm=TM_MIX, name="lru_scan" + tag)
    y_b, = _rowwise(_f_lru_post, [(hs, 0, DG, False), (p, OFF_LG // DG, DG, False)], c("gw0"), [(DG, BF)], tm=TM_MIX, name="lru_post" + tag)
    y_c, = _rowwise(_f_sgu, [(p, OFF_UV // 1024, 1024, False)], c("ln_w", "ln_b", "ws", "bst", "gw1"), [(DG, BF)], tm=TM_MIX, name="sgu" + tag)
    y_d, = _rowwise(_f_sconv, [(p, OFF_SB // DG, DG, False), (p, OFF_SC // DG, DG, True), (p, OFF_SH // DG, DG, True)],
                    c("sconv_w", "gw2"), [(DG, BF)], tm=TM_MIX, name="sconv" + tag)
    return jnp.concatenate([y_a, y_b, y_c, y_d], axis=1), (q, k, v, gb, o, s_all, a, hs), xres


def _mixers_bwd(p, mp, saved, dy, tag="", xch=None):
    q, k, v, gb, o, s_all, a, hs = saved
    c = lambda *names: [(mp[n], False, True) for n in names]
    g = {}

    (do, dz), (g["gdn_nw"],) = _rowwise_vjp(
        _f_gdn_post, [(o, 0, DG, False, F32), (p, OFF_Z // DG, DG, False, BF)], c("gdn_nw"), [(dy, 0, DG)], tm=TM_MIX, name="gdn_post_b" + tag)
    (dq, dk, dv, dgb), xres = _gdn_bwd(q, k, v, gb, s_all, do, name="gdn_chunks_b" + tag, xch=xch)
    (dqkv, dba), (g["gdn_conv_w"], g["alog"], g["dtb"]) = _rowwise_vjp(
        _f_gdn_pre, [(p, 0, 1536, True, BF), (p, OFF_BA // 128, 128, False, BF)], c("gdn_conv_w", "alog", "dtb"),
        [(dq, 0, DG), (dk, 0, DG), (dv, 0, DG), (dgb, 0, 128)], tm=TM_MIX, name="gdn_pre_b" + tag)

    (dhs, dgate), (g["gw0"],) = _rowwise_vjp(
        _f_lru_post, [(hs, 0, DG, False, F32), (p, OFF_LG // DG, DG, False, BF)], c("gw0"), [(dy, 1, DG)], tm=TM_MIX, name="lru_post_b" + tag)
    da, db = _scan_bwd(a, hs, dhs, tm=TM_MIX, name="lru_scan_b" + tag)
    (dlx,), (g["lru_conv_w"], g["lru_conv_b"], g["wa"], g["ba"], g["wx"], g["bx"], g["lam"]) = _rowwise_vjp(
        _f_lru_ab, [(p, OFF_LX // DG, DG, True, BF)], c("lru_conv_w", "lru_conv_b", "wa", "ba", "wx", "bx", "lam"),
        [(da, 0, DG), (db, 0, DG)], tm=TM_MIX, name="lru_ab_b" + tag)

    (duv,), (g["ln_w"], g["ln_b"], g["ws"], g["bst"], g["gw1"]) = _rowwise_vjp(
        _f_sgu, [(p, OFF_UV // 1024, 1024, False, BF)], c("ln_w", "ln_b", "ws", "bst", "gw1"), [(dy, 2, DG)], tm=TM_MIX, name="sgu_b" + tag)

    (dsb, dsc, dsh), (g["sconv_w"], g["gw2"]) = _rowwise_vjp(
        _f_sconv, [(p, OFF_SB // DG, DG, False, BF), (p, OFF_SC // DG, DG, True, BF), (p, OFF_SH // DG, DG, True, BF)],
        c("sconv_w", "gw2"), [(dy, 3, DG)], tm=TM_MIX, name="sconv_b" + tag)

    dp = jnp.concatenate([dqkv, dz, dlx, dgate, duv, dsb, dsc, dsh, dba], axis=1)
    return dp, g, xres


WEIGHTS = ("pre_mix_norm", "w_in", "gdn_conv_w", "gdn_a_log", "gdn_dt_bias", "gdn_norm_w", "lru_conv_w", "lru_conv_b", "lru_wa",
           "lru_ba", "lru_wx", "lru_bx", "lru_lambda", "sgu_ln_w", "sgu_ln_b", "sgu_ws", "sgu_b", "sconv_w", "grp_norm_w", "w_out",
           "post_mix_norm", "pre_ffn_norm", "ffn_up", "ffn_conv_w", "ffn_conv_b", "ffn_down", "post_ffn_norm")
BIG = ("w_in", "ffn_up", "w_out", "ffn_down")
CHIP_SHARDED_SMALL = ("gdn_conv_w", "lru_conv_w", "sconv_w", "grp_norm_w", "ffn_conv_w")
MIXER_PARAMS = ("gdn_conv_w", "gdn_a_log", "gdn_dt_bias", "gdn_norm_w", "lru_conv_w", "lru_conv_b", "lru_wa", "lru_ba", "lru_wx",
                "lru_bx", "lru_lambda", "sgu_ln_w", "sgu_ln_b", "sgu_ws", "sgu_b", "sconv_w", "grp_norm_w")
TM_ROW = 256
N_FF_TILES = 11
FF_TILE = D_FF // N_FF_TILES
PACK_ROWS = 256


def _pack(arrs):
    flat = jnp.concatenate([a.reshape(-1) for a in arrs])
    n = flat.size
    rows = -(-n // (128 * PACK_ROWS)) * PACK_ROWS
    return jnp.pad(flat, (0, rows * 128 - n)).reshape(rows, 128)


def _unpack(buf, shapes):
    flat = buf.reshape(-1)
    out, off = [], 0
    for s in shapes:
        n = math.prod(s)
        out.append(flat[off:off + n].reshape(s))
        off += n
    return out


def _row_tile(rows, cols):
    return 256 if rows % 256 == 0 and cols <= 1024 else 128


def _w_in_full(got):
    return _pad_cols(got.transpose(1, 0, 2).reshape(D, N_IN))


def _w_in_blocks(dw):
    return _unpad_cols(dw).reshape(D, 4, N_IN // 4).transpose(1, 0, 2)


def _layer_fwd(l, xs, h, w_in_l, w_out_g, shard, sp, mp, nxt):
    t = str(l)
    half = D // 2
    p, (w_up_a,) = _matmul(h, w_in_l, "nn", F32, tm=1024, tn=1152, tk=D, name="mm_in" + t, xch=_GatherChips([shard["ffn_up"][l][:half]]))
    ycat, saved, (w_up_b,) = _mixers_fwd(p, mp, tag=t, xch=_GatherChips([shard["ffn_up"][l][half:]]))
    w_up = jnp.concatenate([w_up_a, w_up_b], axis=1)
    w_out_l = w_out_g.reshape(D, D)
    y, _ = _matmul(ycat, w_out_l, "nn", F32, tm=1024, tn=1024, tk=D, name="mm_out" + t)
    x1, h2 = _rowwise(_f_post_pre, [(xs, 0, D, False), (y, 0, D, False)], [(sp["post_mix_norm"][l], False), (sp["pre_ffn_norm"][l], False)],
                      [(D, F32), (D, BF)], tm=TM_ROW, name="post_mix" + t)
    u, (w_dn_g,) = _matmul(h2, w_up, "nn", F32, tm=1024, tn=1408, tk=D, name="mm_up" + t, b_blocks=True,
                           xch=_GatherChips([shard["ffn_down"][l]]))
    ffn_rows = [(u, 0, FF_TILE, True), (u, N_FF_TILES, FF_TILE, True)]
    ffn_consts = [(c, True) for c in sp["ffn_conv"][l]]
    if nxt is None:
        act, = _rowwise(_f_ffn_act, ffn_rows, ffn_consts, [(FF_TILE, BF)], tm=TM_ROW, ncol=N_FF_TILES, name="ffn_act" + t)
        n_out = None
    else:
        (act,), (n_out,) = _rowwise(_f_ffn_act, ffn_rows, ffn_consts, [(FF_TILE, BF)], tm=TM_ROW, ncol=N_FF_TILES, name="ffn_act" + t,
                                    xch=_GatherChips([nxt[1]]))
    w_dn_l = w_dn_g.reshape(D_FF, D)
    y2, n_in = _matmul(act, w_dn_l, "nn", F32, tm=1024, tn=1024, tk=1408, name="mm_down" + t,
                       xch=None if nxt is None else _GatherChips([nxt[0]]))
    keep = dict(xs=xs, h=h, p=p, saved=saved, ycat=ycat, y=y, x1=x1, h2=h2, u=u, act=act, y2=y2,
                w_in=w_in_l, w_out=w_out_l, w_up=w_up, w_dn=w_dn_l)
    return keep, (None if nxt is None else (n_in[0], n_out))


def _layer_bwd(l, a, dx1, dy2, sp, mp):
    t = str(l)
    g = {}
    dact, _ = _matmul(dy2, a["w_dn"], "nt", F32, tm=1024, tn=1408, tk=D, name="mm_down_dx" + t)
    dw_dn, _ = _matmul(a["act"], dy2, "tn", F32, tm=1408, tn=1024, tk=1024, name="mm_down_dw" + t)
    dw_dn = dw_dn.reshape(4, D_FF // 4, D)
    (dug, duv), gc = _rowwise_vjp(
        _f_ffn_act, [(a["u"], 0, FF_TILE, True, BF), (a["u"], N_FF_TILES, FF_TILE, True, BF)], [(c, True, True) for c in sp["ffn_conv"][l]],
        [(dact, 0, FF_TILE)], tm=TM_ROW, ncol=N_FF_TILES, name="ffn_act_b" + t)
    g["ffn_conv_w"] = jnp.concatenate([gc[0], gc[1]], axis=1)
    g["ffn_conv_b"] = jnp.concatenate([gc[2], gc[3]], axis=1)[0]
    du = jnp.concatenate([dug, duv], axis=1)
    dh2, (r_dn,) = _matmul(du, a["w_up"], "nt", F32, tm=1024, tn=1024, tk=2816, name="mm_up_dx" + t, b_blocks=True,
                           xch=_ScatterChips([dw_dn.astype(BF)]))
    dw_up, _ = _matmul(a["h2"], du, "tn", F32, tm=1024, tn=1408, tk=1024, name="mm_up_dw" + t, out_blocks=True)
    (dxs, dy), (gpm, gpf) = _rowwise_vjp(
        _f_post_pre, [(a["xs"], 0, D, False, F32), (a["y"], 0, D, False, BF)],
        [(sp["post_mix_norm"][l], False, True), (sp["pre_ffn_norm"][l], False, True)], [(dx1, 0, D), (dh2, 0, D)], tm=TM_ROW, name="post_mix_b" + t)
    g["post_mix_norm"], g["pre_ffn_norm"] = gpm[0], gpf[0]
    dycat, _ = _matmul(dy, a["w_out"], "nt", F32, tm=1024, tn=1024, tk=D, name="mm_out_dx" + t)
    dw_out, _ = _matmul(a["ycat"], dy, "tn", F32, tm=1024, tn=1024, tk=1024, name="mm_out_dw" + t)
    dw_out = dw_out.reshape(4, D // 4, D)
    dp, gm, (r_up,) = _mixers_bwd(a["p"], mp, a["saved"], dycat, tag=t, xch=_ScatterChips([dw_up.astype(BF)]))
    g.update(_mixer_param_grads(gm))
    dw_in, (r_out,) = _matmul(a["h"], dp, "tn", F32, tm=1024, tn=1152, tk=1024, name="mm_in_dw" + t, xch=_ScatterChips([dw_out.astype(BF)]))
    dw_in = _w_in_blocks(dw_in)
    dh, (r_in,) = _matmul(dp, a["w_in"], "nt", F32, tm=1024, tn=1024, tk=1920, name="mm_in_dx" + t, xch=_ScatterChips([dw_in.astype(BF)]))
    big = {"ffn_down": (dw_dn, r_dn), "ffn_up": (dw_up, r_up), "w_out": (dw_out, r_out), "w_in": (dw_in, r_in)}
    return dxs, dh, g, big


def kernel(x, pre_mix_norm, w_in, gdn_conv_w, gdn_a_log, gdn_dt_bias, gdn_norm_w, lru_conv_w, lru_conv_b, lru_wa, lru_ba, lru_wx, lru_bx, lru_lambda, sgu_ln_w, sgu_ln_b, sgu_ws, sgu_b, sconv_w, grp_norm_w, w_out, post_mix_norm, pre_ffn_norm, ffn_up, ffn_conv_w, ffn_conv_b, ffn_down, post_ffn_norm, loss_target, m_pre_mix_norm, m_w_in, m_gdn_conv_w, m_gdn_a_log, m_gdn_dt_bias, m_gdn_norm_w, m_lru_conv_w, m_lru_conv_b, m_lru_wa, m_lru_ba, m_lru_wx, m_lru_bx, m_lru_lambda, m_sgu_ln_w, m_sgu_ln_b, m_sgu_ws, m_sgu_b, m_sconv_w, m_grp_norm_w, m_w_out, m_post_mix_norm, m_pre_ffn_norm, m_ffn_up, m_ffn_conv_w, m_ffn_conv_b, m_ffn_down, m_post_ffn_norm, v_pre_mix_norm, v_w_in, v_gdn_conv_w, v_gdn_a_log, v_gdn_dt_bias, v_gdn_norm_w, v_lru_conv_w, v_lru_conv_b, v_lru_wa, v_lru_ba, v_lru_wx, v_lru_bx, v_lru_lambda, v_sgu_ln_w, v_sgu_ln_b, v_sgu_ws, v_sgu_b, v_sconv_w, v_grp_norm_w, v_w_out, v_post_mix_norm, v_pre_ffn_norm, v_ffn_up, v_ffn_conv_w, v_ffn_conv_b, v_ffn_down, v_post_ffn_norm):
    given = dict(locals())
    me = 2 * lax.axis_index("x") + lax.axis_index("y")
    xs0, tgt = x[0], loss_target[0]

    small_sh = [given[n] for n in CHIP_SHARDED_SMALL]
    shard = {n: [given[n][l].astype(BF) for l in range(DEPTH)] for n in BIG}
    got = _exchange(_GatherChips([shard["w_in"][0], shard["w_out"][0], _pack(small_sh)]), name="gather_first")
    full = {n: given[n] for n in WEIGHTS if n not in BIG and n not in CHIP_SHARDED_SMALL}
    per_chip = [_unpack(got[2][j], [s.shape for s in small_sh]) for j in range(4)]
    parts = [jnp.stack([per_chip[j][i] for j in range(4)]) for i in range(len(small_sh))]
    for n, pj in zip(CHIP_SHARDED_SMALL, parts):
        full[n] = pj.transpose(1, 2, 0, 3).reshape(pj.shape[1], pj.shape[2], 4 * pj.shape[3])
    sp = {n: [full[n][l:l + 1] for l in range(DEPTH)] for n in ("pre_mix_norm", "post_mix_norm", "pre_ffn_norm", "post_ffn_norm")}
    sp["ffn_conv"] = [[full["ffn_conv_w"][l][:, :D_FF], full["ffn_conv_w"][l][:, D_FF:], full["ffn_conv_b"][l:l + 1, :D_FF],
                       full["ffn_conv_b"][l:l + 1, D_FF:]] for l in range(DEPTH)]
    mps = [_mixer_params({n: full[n][l] for n in MIXER_PARAMS}) for l in range(DEPTH)]

    h, = _rowwise(_f_pre, [(xs0, 0, D, False)], [(sp["pre_mix_norm"][0], False)], [(D, BF)], tm=TM_ROW, name="pre_mix0")
    a0, (w_in1, w_out1) = _layer_fwd(0, xs0, h, _w_in_full(got[0]), got[1], shard, sp, mps[0], (shard["w_in"][1], shard["w_out"][1]))
    xs1, h1 = _rowwise(_f_post_pre, [(a0["x1"], 0, D, False), (a0["y2"], 0, D, False)],
                       [(sp["post_ffn_norm"][0], False), (sp["pre_mix_norm"][1], False)], [(D, F32), (D, BF)], tm=TM_ROW, name="post_ffn0")
    a1, _ = _layer_fwd(1, xs1, h1, _w_in_full(w_in1), w_out1, shard, sp, mps[1], None)
    lacc, dxo = _loss_head(a1["x1"], a1["y2"], sp["post_ffn_norm"][1], tgt, tm=TM_ROW, name="loss_head")

    gl = [None, None]
    (dx1, dy2), (gpf1,) = _rowwise_vjp(_f_post, [(a1["x1"], 0, D, False, F32), (a1["y2"], 0, D, False, BF)],
                                      [(sp["post_ffn_norm"][1], False, True)], [(dxo, 0, D)], tm=TM_ROW, name="post_ffn1_b")
    big = [None, None]
    dxs1, dh1, gl[1], big[1] = _layer_bwd(1, a1, dx1, dy2, sp, mps[1])
    gl[1]["post_ffn_norm"] = gpf1[0]
    (dx1, dy2), (gpf0, gpm1) = _rowwise_vjp(
        _f_post_pre, [(a0["x1"], 0, D, False, F32), (a0["y2"], 0, D, False, BF)],
        [(sp["post_ffn_norm"][0], False, True), (sp["pre_mix_norm"][1], False, True)], [(dxs1, 0, D), (dh1, 0, D)], tm=TM_ROW, name="post_ffn0_b")
    gl[1]["pre_mix_norm"] = gpm1[0]
    dxs0, dh0, gl[0], big[0] = _layer_bwd(0, a0, dx1, dy2, sp, mps[0])
    gl[0]["post_ffn_norm"] = gpf0[0]
    (grad_x,), (gpm0,) = _rowwise_vjp(lambda xv, w: (xv, _rms(xv, w)), [(xs0, 0, D, False, F32)], [(sp["pre_mix_norm"][0], False, True)],
                                     [(dxs0, 0, D), (dh0, 0, D)], tm=TM_ROW, name="pre_mix0_b")
    gl[0]["pre_mix_norm"] = gpm0[0]

    small = [n for n in WEIGHTS if n not in BIG]
    gfull = {n: jnp.stack([gl[0][n], gl[1][n]]) for n in small}
    tot = _sum_slots(_gather_devices(_pack([gfull[n] for n in small] + [lacc[0, 0:1]]), name="gather_small_grads"), name="sum_small_grads")
    red = dict(zip(small + ["loss"], _unpack(tot, [gfull[n].shape for n in small] + [(1,)])))
    for n in CHIP_SHARDED_SMALL:
        cb = given[n].shape[-1]
        red[n] = lax.dynamic_slice_in_dim(red[n], me * cb, cb, axis=red[n].ndim - 1)
    shapes = [given[n].shape for n in small]
    res = _adamw(_pack([given[n] for n in small]), _pack([given["m_" + n] for n in small]), _pack([given["v_" + n] for n in small]),
                 [[_pack([red[n] for n in small])]], tr=PACK_ROWS, name="adamw_small")
    outs = {kind: dict(zip(small, _unpack(r, shapes))) for kind, r in zip(("grad", "delta", "new_m", "new_v"), res)}

    keys = [(n, l) for n in BIG for l in range(DEPTH)]
    sums = []
    for n, l in keys:
        blocks, recv = big[l][n]
        own = lax.dynamic_index_in_dim(blocks, me, 0, keepdims=False)
        sums.append(_sum_parts(own, recv, tr=_row_tile(*own.shape), name="sum_grads_%s%d" % (n, l)))
    other = dict(zip(keys, _swap_cores(sums, name="swap_grad_sums")))
    sums = dict(zip(keys, sums))
    for n in BIG:
        shp = given[n].shape
        two = lambda z: z.reshape(shp[0] * shp[1], shp[2])
        r4 = _adamw(two(given[n]), two(given["m_" + n]), two(given["v_" + n]), [[sums[n, l], other[n, l]] for l in range(DEPTH)],
                    tr=_row_tile(shp[1], shp[2]), name="adamw_" + n)
        for kind, r in zip(("grad", "delta", "new_m", "new_v"), r4):
            outs[kind][n] = r.reshape(shp)

    return (red["loss"][0], grad_x[None], *[outs[k][n] for k in ("grad", "delta", "new_m", "new_v") for n in WEIGHTS])
```

```python
import functools
import math

import jax
import jax.numpy as jnp
from jax import lax
from jax.experimental import pallas as pl
from jax.experimental.pallas import tpu as pltpu

F32 = jnp.float32
BF = jnp.bfloat16
MESH_ID = pl.DeviceIdType.MESH

EPS = 1e-6
DEPTH = 2
D = 2048
DG = 512
HEADS = 4
HD = 128
CHUNK = 64
GDN_STEP = 4
LRU_C = 8.0
D_FF = 5632
N_IN = 5640
NP = 5760
OFF_Q, OFF_Z, OFF_LX, OFF_LG, OFF_UV, OFF_SB, OFF_SC, OFF_SH, OFF_BA = 0, 1536, 2048, 2560, 3072, 4096, 4608, 5120, 5632

ADAM_LR, ADAM_B1, ADAM_B2, ADAM_EPS, ADAM_WD, ADAM_STEP = 0.001, 0.9, 0.999, 1e-08, 0.01, 10

HALO = 8


def _mk_bdot(ca, cb):
    na, nb = 1 - ca, 1 - cb

    def dg(x, y, cx, cy):
        return lax.dot_general(x.astype(BF), y.astype(BF), (((cx,), (cy,)), ((), ())), preferred_element_type=F32)

    @jax.custom_vjp
    def f(a, b):
        return dg(a, b, ca, cb)

    def fwd(a, b):
        return dg(a, b, ca, cb), (a, b)

    def bwd(res, g):
        a, b = res
        da = dg(g, b, 1, nb) if ca == 1 else dg(b, g, nb, 1)
        db = dg(a, g, na, 0) if cb == 0 else dg(g, a, 0, na)
        return da.astype(a.dtype), db.astype(b.dtype)

    f.defvjp(fwd, bwd)
    return f


_bdot = _mk_bdot(1, 0)
_bdot_nt = _mk_bdot(1, 1)
_bdot_tn = _mk_bdot(0, 0)


def _sigmoid(x):
    return 1.0 / (1.0 + jnp.exp(-x))


def _silu(x):
    return x * _sigmoid(x)


def _gelu(x):
    return 0.5 * x * (1.0 + jnp.tanh(0.7978845608028654 * (x + 0.044715 * (x * x * x))))


def _log1p(z):
    u = 1.0 + z
    d = u - 1.0
    return jnp.where(d == 0.0, z, jnp.log(u) * (z / jnp.where(d == 0.0, 1.0, d)))


def _softplus(x):
    return jnp.maximum(x, 0.0) + _log1p(jnp.exp(-jnp.abs(x)))


def _neg_expm1(y):
    t = jnp.tanh(0.5 * y)
    return -2.0 * t / (1.0 - t)


def _rms(x, w):
    return x * lax.rsqrt(jnp.mean(x * x, axis=-1, keepdims=True) + EPS) * w


def _rows(x):
    return lax.broadcasted_iota(jnp.int32, x.shape, 0)


def _mk_shift():
    @functools.partial(jax.custom_vjp, nondiff_argnums=(1,))
    def shift(xx, s):
        n = xx.shape[0]
        return pltpu.roll(xx, s, axis=0)[HALO:n] if s else xx[HALO:n]

    def fwd(xx, s):
        return shift(xx, s), None

    def bwd(s, _, g):
        ext = jnp.concatenate([g, jnp.zeros((HALO, g.shape[1]), g.dtype)], axis=0)
        return (pltpu.roll(ext, HALO - s, axis=0),)

    shift.defvjp(fwd, bwd)
    return shift


_shift = _mk_shift()


def _mk_chunk_cumsum():
    def run(x, up):
        n = x.shape[0]
        pos = _rows(x) % CHUNK
        d = 1
        while d < CHUNK:
            if up:
                x = x + jnp.where(pos < CHUNK - d, pltpu.roll(x, n - d, axis=0), 0.0)
            else:
                x = x + jnp.where(pos >= d, pltpu.roll(x, d, axis=0), 0.0)
            d *= 2
        return x

    @jax.custom_vjp
    def cumsum(x):
        return run(x, False)

    cumsum.defvjp(lambda x: (run(x, False), None), lambda _, g: (run(g, True),))
    return cumsum


_chunk_cumsum = _mk_chunk_cumsum()


def _causal_conv(xx, w):
    K = w.shape[0]
    y = _shift(xx, K - 1) * w[0:1, :]
    for k in range(1, K):
        y = y + _shift(xx, K - 1 - k) * w[k:k + 1, :]
    return y


def _f_pre(x, w):
    return (_rms(x, w),)


def _f_post_pre(x, y, w_post, w_pre):
    x1 = x + _rms(y, w_post)
    return x1, _rms(x1, w_pre)


def _f_post(x, y, w_post):
    return (x + _rms(y, w_post),)


def _heads(fn, *xs):
    return jnp.concatenate([fn(*[x[:, h * HD:(h + 1) * HD] for x in xs]) for h in range(HEADS)], axis=1)


def _l2n(t):
    return t * lax.rsqrt(jnp.sum(t * t, axis=-1, keepdims=True) + EPS)


def _f_gdn_pre(qkv, ba, conv_w, alog, dtb):
    c = _silu(_causal_conv(qkv, conv_w))
    q = _heads(lambda t: _l2n(t) * (HD ** -0.5), c[:, 0:DG])
    k = _heads(_l2n, c[:, DG:2 * DG])
    v = c[:, 2 * DG:3 * DG]
    lane = lax.broadcasted_iota(jnp.int32, ba.shape, 1)
    beta = _sigmoid(ba)
    gcum = _chunk_cumsum(-jnp.exp(alog) * _softplus(ba + dtb))
    gc = jnp.where(lane < HEADS, beta, jnp.where(lane < 2 * HEADS, gcum, 0.0))
    return q, k, v, gc


def _f_gdn_chunk(q, k, v, gc, s0):
    C = q.shape[0]
    HC = HEADS * C
    sh = C.bit_length() - 1

    def stack(x):
        return jnp.concatenate([x[:, h * HD:(h + 1) * HD] for h in range(HEADS)], axis=0)

    def column(off):
        lane = lax.broadcasted_iota(jnp.int32, gc.shape, 1)
        return jnp.concatenate([jnp.sum(jnp.where(lane == off + h, gc, 0.0), axis=1, keepdims=True) for h in range(HEADS)], axis=0)

    def head(x, h):
        return x[h * C:(h + 1) * C]

    r = lax.broadcasted_iota(jnp.int32, (HC, HC), 0)
    c = lax.broadcasted_iota(jnp.int32, (HC, HC), 1)
    same = jnp.right_shift(r, sh) == jnp.right_shift(c, sh)
    causal = jnp.logical_and(same, r >= c)
    strict = jnp.logical_and(same, r > c)
    gcol, bcol = column(HEADS), column(0)
    grow = jnp.sum(jnp.where(r == c, gcol, 0.0), axis=0, keepdims=True)
    decay = jnp.where(causal, jnp.exp(jnp.where(causal, gcol - grow, 0.0)), 0.0)
    ks, qs, vs = stack(k), stack(q), stack(v)
    kb = ks * bcol
    kk = _bdot_nt(jnp.concatenate([kb, qs], axis=0), ks)
    m = jnp.where(strict, kk[0:HC] * decay, 0.0)
    attn = jnp.where(causal, kk[HC:2 * HC] * decay, 0.0)
    n = -m
    t = (r == c).astype(F32) + n
    p = n
    for _ in range(5):
        p = _bdot(p, p)
        t = t + _bdot(t, p)
    eg = jnp.exp(gcol)
    wu = _bdot(t, jnp.concatenate([kb * eg, vs * bcol], axis=1))
    w, u = wu[:, 0:HD], wu[:, HD:2 * HD]
    last = jnp.logical_and(same, jnp.bitwise_and(c, C - 1) == C - 1)
    glast = jnp.sum(jnp.where(last, grow, 0.0), axis=1, keepdims=True)
    k_g = ks * jnp.exp(glast - gcol)
    q_g = qs * eg
    ws = [_bdot(jnp.concatenate([head(w, h), head(q_g, h)], axis=0), s0[h * HD:(h + 1) * HD]) for h in range(HEADS)]
    v_new = u - jnp.concatenate([x[0:C] for x in ws], axis=0)
    o = jnp.concatenate([x[C:2 * C] for x in ws], axis=0) + _bdot(attn, v_new)
    s1 = [s0[h * HD:(h + 1) * HD] * jnp.exp(glast[h * C:h * C + 1]) + _bdot_tn(head(k_g, h), head(v_new, h)) for h in range(HEADS)]
    return jnp.concatenate([head(o, h) for h in range(HEADS)], axis=1), jnp.concatenate(s1, axis=0)


def _f_gdn_chunks(q, k, v, gc, s0):
    outs, s = [], s0
    for n in range(q.shape[0] // CHUNK):
        rs = slice(n * CHUNK, (n + 1) * CHUNK)
        o, s = _f_gdn_chunk(q[rs], k[rs], v[rs], gc[rs], s)
        outs.append(o)
    return jnp.concatenate(outs, axis=0), s


def _f_gdn_post(o, z, nw):
    return (_heads(lambda a, b: _rms(a, nw) * _silu(b), o, z),)


def _f_lru_ab(lx, conv_w, conv_b, wa, ba, wx, bx, lam):
    xc = _causal_conv(lx, conv_w) + conv_b
    r = _sigmoid(_bdot(xc, wa) + ba)
    i = _sigmoid(_bdot(xc, wx) + bx)
    log_a = -LRU_C * r * _softplus(-lam)
    a = jnp.exp(log_a)
    mult = jnp.sqrt(_neg_expm1(2.0 * log_a))
    return a, mult * (i * xc)


def _f_lru_post(hs, gate, gw):
    return (_rms(hs * _gelu(gate), gw),)


def _f_sgu(uv, ln_w, ln_b, ws, bst, gw):
    tm = uv.shape[0]
    uvf = _gelu(uv)
    u, v = uvf[:, 0:DG], uvf[:, DG:2 * DG]
    mu = jnp.mean(v, axis=-1, keepdims=True)
    vc = v - mu
    v = vc * lax.rsqrt(jnp.mean(vc * vc, axis=-1, keepdims=True) + EPS) * ln_w + ln_b
    lane = lax.broadcasted_iota(jnp.int32, bst.shape, 1)
    tril = lax.broadcasted_iota(jnp.int32, (128, 128), 0) >= lax.broadcasted_iota(jnp.int32, (128, 128), 1)
    wsm = [jnp.where(tril, ws[g * 128:(g + 1) * 128, :], 0.0) for g in range(4)]
    bias = [jnp.sum(jnp.where(lane == g, bst, 0.0), axis=1, keepdims=True) for g in range(4)]
    out = []
    for n in range(tm // 128):
        vn = v[n * 128:(n + 1) * 128, :]
        gs = [_bdot(wsm[g], vn[:, g * 128:(g + 1) * 128]) + bias[g] for g in range(4)]
        out.append(jnp.concatenate(gs, axis=1))
    vo = jnp.concatenate(out, axis=0) if len(out) > 1 else out[0]
    return (_rms(u * vo, gw),)


def _f_sconv(sb, sc, sh, conv_w, gw):
    return (_rms(sb * _causal_conv(sc * sh, conv_w), gw),)


def _f_ffn_act(ug, uv, wg, wv, bg, bv):
    return (_gelu(_causal_conv(ug, wg) + bg) * (_causal_conv(uv, wv) + bv),)


def _row_specs(rows, consts, tm, ncol, tile_of):
    specs, args = [], []
    for arr, cb, w, halo in rows:
        specs.append(pl.BlockSpec((tm, w), lambda j, i, cb=cb: (tile_of(i), cb + j)))
        args.append(arr)
        if halo:
            specs.append(pl.BlockSpec((HALO, w), lambda j, i, cb=cb: (jnp.maximum(tile_of(i) * (tm // HALO) - 1, 0), cb + j)))
            args.append(arr)
    for arr, tiled in consts:
        r, c = arr.shape
        specs.append(pl.BlockSpec((r, c // ncol), lambda j, i: (0, j)) if tiled else pl.BlockSpec((r, c), lambda j, i: (0, 0)))
        args.append(arr)
    return specs, args


def _load_rows(refs, rows, consts, tile):
    k, vals = 0, []
    for _arr, _cb, _w, halo in rows:
        t = refs[k][...].astype(F32)
        k += 1
        if halo:
            hl = jnp.where(tile > 0, refs[k][...].astype(F32), 0.0)
            k += 1
            t = jnp.concatenate([hl, t], axis=0)
        vals.append(t)
    for _ in consts:
        vals.append(refs[k][...])
        k += 1
    return vals, k


def _rowwise(fn, rows, consts, outs, *, tm, name, ncol=1, xch=None):
    n = rows[0][0].shape[0]
    nt = n // tm
    in_specs, args = _row_specs(rows, consts, tm, ncol, lambda i: i)

    def body(*refs):
        vals, k = _load_rows(refs, rows, consts, pl.program_id(1))
        for o_ref, r in zip(refs[k:], fn(*vals)):
            o_ref[...] = r.astype(o_ref.dtype)

    res, xres = _call(
        body, grid=(ncol, nt), in_specs=in_specs,
        out_specs=[pl.BlockSpec((tm, w), lambda j, i: (i, j)) for w, _ in outs],
        out_shape=[jax.ShapeDtypeStruct((n, w * ncol), dt) for w, dt in outs], scratch=[], name=name, args=args, xch=xch)
    return res if xch is None else (res, xres)


def _rowwise_vjp(fn, rows, consts, cots, *, tm, name, ncol=1):
    n = rows[0][0].shape[0]
    nt = n // tm
    rows4 = [r[:4] for r in rows]
    consts2 = [c[:2] for c in consts]
    in_specs, args = _row_specs(rows4, consts2, tm, ncol, lambda i: nt - 1 - i)
    for arr, cb, w in cots:
        in_specs.append(pl.BlockSpec((tm, w), lambda j, i, cb=cb: (nt - 1 - i, cb + j)))
        args.append(arr)
    out_specs, out_shape, scratch = [], [], []
    for arr, cb, w, halo, gdt in rows:
        if gdt is not None:
            out_specs.append(pl.BlockSpec((tm, w), lambda j, i: (nt - 1 - i, j)))
            out_shape.append(jax.ShapeDtypeStruct((n, w * ncol), gdt))
            if halo:
                scratch.append(pltpu.VMEM((HALO, w), F32))
    for arr, tiled, want in consts:
        if want:
            r, c = arr.shape
            out_specs.append(pl.BlockSpec((r, c // ncol), lambda j, i: (0, j)) if tiled else pl.BlockSpec((r, c), lambda j, i: (0, 0)))
            out_shape.append(jax.ShapeDtypeStruct((r, c), F32))
    n_in = len(in_specs)
    n_out = len(out_specs)

    def body(*refs):
        j, i = pl.program_id(0), pl.program_id(1)
        tile = nt - 1 - i
        vals, k = _load_rows(refs, rows4, consts2, tile)
        cvals = [refs[k + q][...].astype(F32) for q in range(len(cots))]
        outs = refs[n_in:n_in + n_out]
        carries = refs[n_in + n_out:]
        _, vjp = jax.vjp(fn, *vals)
        g = vjp(tuple(cvals))
        o, cidx = 0, 0
        for q, (arr, cb, w, halo, gdt) in enumerate(rows):
            if gdt is None:
                continue
            if halo:
                ge = g[q]
                main = ge[HALO:]
                carry = carries[cidx]
                cidx += 1
                tail = main[tm - HALO:] + jnp.where(i > 0, carry[...], 0.0)
                outs[o][0:tm - HALO, :] = main[0:tm - HALO].astype(gdt)
                outs[o][tm - HALO:tm, :] = tail.astype(gdt)
                carry[...] = ge[0:HALO]
            else:
                outs[o][...] = g[q].astype(gdt)
            o += 1
        for q, (arr, tiled, want) in enumerate(consts):
            if not want:
                continue
            first = (i == 0) if tiled else jnp.logical_and(i == 0, j == 0)
            acc = outs[o]
            gq = g[len(rows) + q].astype(F32)

            @pl.when(first)
            def _(acc=acc, gq=gq):
                acc[...] = gq

            @pl.when(jnp.logical_not(first))
            def _(acc=acc, gq=gq):
                acc[...] += gq

            o += 1

    res = pl.pallas_call(
        body, grid=(ncol, nt), in_specs=in_specs, out_specs=out_specs, out_shape=out_shape, scratch_shapes=scratch,
        name=name, compiler_params=pltpu.CompilerParams(dimension_semantics=("arbitrary", "arbitrary")),
    )(*args)
    nrow = sum(1 for r in rows if r[4] is not None)
    return list(res[:nrow]), list(res[nrow:])


_ANY = pl.BlockSpec(memory_space=pl.ANY)
_CHIP_REL = ((1, 0), (0, 1), (1, 1))
_DEV_REL = tuple((r >> 2 & 1, r >> 1 & 1, r & 1) for r in range(1, 8))


def _flip(v, f):
    return 1 - v if f else v


class _GatherChips:
    def __init__(self, shards):
        self.arrs = list(shards)
        n = len(self.arrs)
        self.out_shape = [jax.ShapeDtypeStruct((4,) + s.shape, s.dtype) for s in self.arrs]
        self.sems = [pltpu.SemaphoreType.DMA((3 * n,)), pltpu.SemaphoreType.DMA((3 * n,)), pltpu.SemaphoreType.DMA((n,))]

    def _copies(self, ins, outs, sems, arriving):
        send, recv, lsem = sems
        x, y, c = lax.axis_index("x"), lax.axis_index("y"), lax.axis_index("c")
        me = 2 * x + y
        if arriving:
            local = []
        else:
            local = [pltpu.make_async_copy(ins[a], outs[a].at[me], lsem.at[a]) for a in range(len(ins))]
        remote = []
        for a in range(len(ins)):
            for k, (fx, fy) in enumerate(_CHIP_REL):
                px, py = _flip(x, fx), _flip(y, fy)
                remote.append(pltpu.make_async_remote_copy(
                    src_ref=ins[a], dst_ref=outs[a].at[2 * px + py if arriving else me], send_sem=send.at[3 * a + k],
                    recv_sem=recv.at[3 * a + k], device_id=(px, py, c), device_id_type=MESH_ID))
        return local, remote

    def issue(self, ins, outs, sems):
        local, push = self._copies(ins, outs, sems, False)
        for cp in local + push:
            cp.start()

    def finish(self, ins, outs, sems):
        for cp in self._copies(ins, outs, sems, True)[1]:
            cp.wait_recv()
        local, push = self._copies(ins, outs, sems, False)
        for cp in push:
            cp.wait_send()
        for cp in local:
            cp.wait()


class _ScatterChips:
    def __init__(self, blocks):
        self.arrs = list(blocks)
        n = len(self.arrs)
        self.out_shape = [jax.ShapeDtypeStruct((3,) + b.shape[1:], b.dtype) for b in self.arrs]
        self.sems = [pltpu.SemaphoreType.DMA((3 * n,)), pltpu.SemaphoreType.DMA((3 * n,))]

    def _copies(self, ins, outs, sems):
        send, recv = sems
        x, y, c = lax.axis_index("x"), lax.axis_index("y"), lax.axis_index("c")
        copies = []
        for a in range(len(ins)):
            for k, (fx, fy) in enumerate(_CHIP_REL):
                px, py = _flip(x, fx), _flip(y, fy)
                copies.append(pltpu.make_async_remote_copy(
                    src_ref=ins[a].at[2 * px + py], dst_ref=outs[a].at[k], send_sem=send.at[3 * a + k], recv_sem=recv.at[3 * a + k],
                    device_id=(px, py, c), device_id_type=MESH_ID))
        return copies

    def issue(self, ins, outs, sems):
        for cp in self._copies(ins, outs, sems):
            cp.start()

    def finish(self, ins, outs, sems):
        copies = self._copies(ins, outs, sems)
        for cp in copies:
            cp.wait_recv()
        for cp in copies:
            cp.wait_send()


def _exchange(xch, *, name):
    ni, no = len(xch.arrs), len(xch.out_shape)

    def body(*refs):
        ins, outs, sems = refs[:ni], refs[ni:ni + no], refs[ni + no:]
        xch.issue(ins, outs, sems)
        xch.finish(ins, outs, sems)

    return pl.pallas_call(body, in_specs=[_ANY] * ni, out_specs=[_ANY] * no, out_shape=xch.out_shape, scratch_shapes=xch.sems, name=name)(*xch.arrs)


def _call(body, *, grid, in_specs, out_specs, out_shape, scratch, name, args, xch=None):
    params = pltpu.CompilerParams(dimension_semantics=("arbitrary",) * len(grid))
    if xch is None:
        res = pl.pallas_call(body, grid=grid, in_specs=in_specs, out_specs=out_specs, out_shape=out_shape, scratch_shapes=scratch,
                             name=name, compiler_params=params)(*args)
        return list(res), []
    n_in, n_out, n_sc = len(in_specs), len(out_specs), len(scratch)
    xi, xo = len(xch.arrs), len(xch.out_shape)

    def wrapped(*refs):
        ins, refs = refs[:n_in], refs[n_in:]
        xin, refs = refs[:xi], refs[xi:]
        outs, refs = refs[:n_out], refs[n_out:]
        xout, refs = refs[:xo], refs[xo:]
        sc, sems = refs[:n_sc], refs[n_sc:]
        first = functools.reduce(jnp.logical_and, [pl.program_id(d) == 0 for d in range(len(grid))])
        last = functools.reduce(jnp.logical_and, [pl.program_id(d) == grid[d] - 1 for d in range(len(grid))])

        @pl.when(first)
        def _():
            xch.issue(xin, xout, sems)

        body(*ins, *outs, *sc)

        @pl.when(last)
        def _():
            xch.finish(xin, xout, sems)

    res = pl.pallas_call(
        wrapped, grid=grid, in_specs=list(in_specs) + [_ANY] * xi, out_specs=list(out_specs) + [_ANY] * xo,
        out_shape=list(out_shape) + xch.out_shape, scratch_shapes=list(scratch) + xch.sems, name=name, compiler_params=params,
    )(*args, *xch.arrs)
    return list(res[:n_out]), list(res[n_out:])


def _matmul(a, b, mode, out_dtype, *, tm, tn, tk, name, b_blocks=False, out_blocks=False, xch=None):
    if b_blocks:
        _, br, bc4 = b.shape
        b2 = (br, 4 * bc4)
    else:
        b2 = b.shape

    def bspec(shape, index):
        if not b_blocks:
            return pl.BlockSpec(shape, index)
        per = bc4 // shape[1]

        def blocked(i, j, k):
            r, c = index(i, j, k)
            return (c // per, r, c % per)

        return pl.BlockSpec((None,) + shape, blocked)

    tm = min(tm, a.shape[1] if mode == "tn" else a.shape[0])
    tk = min(tk, a.shape[0] if mode == "tn" else a.shape[1])
    if mode == "tn":
        K, M = a.shape
        N = b2[1]
        a_spec = pl.BlockSpec((tk, tm), lambda i, j, k: (k, i))
        b_spec = bspec((tk, tn), lambda i, j, k: (k, j))
        dims = (((0,), (0,)), ((), ()))
    elif mode == "nt":
        M, K = a.shape
        N = b2[0]
        a_spec = pl.BlockSpec((tm, tk), lambda i, j, k: (i, k))
        b_spec = bspec((tn, tk), lambda i, j, k: (j, k))
        dims = (((1,), (1,)), ((), ()))
    else:
        M, K = a.shape
        N = b2[1]
        a_spec = pl.BlockSpec((tm, tk), lambda i, j, k: (i, k))
        b_spec = bspec((tk, tn), lambda i, j, k: (k, j))
        dims = (((1,), (0,)), ((), ()))
    assert M % tm == 0 and N % tn == 0 and K % tk == 0, (name, M, N, K, tm, tn, tk)
    nk = K // tk
    if out_blocks:
        per_o = (N // 4) // tn
        o_spec = pl.BlockSpec((None, tm, tn), lambda i, j, k: (j // per_o, i, j % per_o))
        o_shape = jax.ShapeDtypeStruct((4, M, N // 4), out_dtype)
    else:
        o_spec = pl.BlockSpec((tm, tn), lambda i, j, k: (i, j))
        o_shape = jax.ShapeDtypeStruct((M, N), out_dtype)

    def body(a_ref, b_ref, o_ref, acc_ref):
        k = pl.program_id(2)
        part = lax.dot_general(a_ref[...].astype(BF), b_ref[...].astype(BF), dims, preferred_element_type=F32)
        if nk == 1:
            o_ref[...] = part.astype(o_ref.dtype)
        else:
            @pl.when(k == 0)
            def _():
                acc_ref[...] = part

            @pl.when(k > 0)
            def _():
                acc_ref[...] += part

            @pl.when(k == nk - 1)
            def _():
                o_ref[...] = acc_ref[...].astype(o_ref.dtype)

    res, xres = _call(body, grid=(M // tm, N // tn, nk), in_specs=[a_spec, b_spec], out_specs=[o_spec], out_shape=[o_shape],
                      scratch=[pltpu.VMEM((tm, tn) if nk > 1 else (8, 128), F32)], name=name, args=(a, b), xch=xch)
    return res[0], xres


def _scan_fwd(a, b, *, tm, name):
    n, c = a.shape
    nt = n // tm

    def body(a_ref, b_ref, h_ref, carry):
        i = pl.program_id(0)
        av, bv = a_ref[...], b_ref[...]
        row = _rows(av)
        d = 1
        while d < tm:
            a_s = jnp.where(row >= d, pltpu.roll(av, d, axis=0), 1.0)
            b_s = jnp.where(row >= d, pltpu.roll(bv, d, axis=0), 0.0)
            bv = av * b_s + bv
            av = av * a_s
            d *= 2
        h = bv + av * jnp.where(i > 0, carry[HALO - 1:HALO, :], 0.0)
        h_ref[...] = h
        carry[...] = h[tm - HALO:tm]

    return pl.pallas_call(
        body, grid=(nt,), in_specs=[pl.BlockSpec((tm, c), lambda i: (i, 0))] * 2,
        out_specs=pl.BlockSpec((tm, c), lambda i: (i, 0)), out_shape=jax.ShapeDtypeStruct((n, c), F32),
        scratch_shapes=[pltpu.VMEM((HALO, c), F32)], name=name,
        compiler_params=pltpu.CompilerParams(dimension_semantics=("arbitrary",)),
    )(a, b)


def _scan_bwd(a, h, dh, *, tm, name):
    n, c = a.shape
    nt = n // tm
    tb = tm // HALO

    def body(a_ref, an_ref, h_ref, hp_ref, dh_ref, da_ref, db_ref, carry):
        i = pl.program_id(0)
        tile = nt - 1 - i
        av, hv, g = a_ref[...], h_ref[...], dh_ref[...]
        row = _rows(av)
        a_next = jnp.where(tile < nt - 1, an_ref[0:1, :], 0.0)
        au = jnp.where(row < tm - 1, pltpu.roll(av, tm - 1, axis=0), a_next)
        d = 1
        while d < tm:
            a_s = jnp.where(row < tm - d, pltpu.roll(au, tm - d, axis=0), 1.0)
            g_s = jnp.where(row < tm - d, pltpu.roll(g, tm - d, axis=0), 0.0)
            g = au * g_s + g
            au = au * a_s
            d *= 2
        g = g + au * jnp.where(i > 0, carry[0:1, :], 0.0)
        h_prev = jnp.where(row >= 1, pltpu.roll(hv, 1, axis=0), jnp.where(tile > 0, hp_ref[HALO - 1:HALO, :], 0.0))
        db_ref[...] = g
        da_ref[...] = g * h_prev
        carry[...] = g[0:HALO]

    cur = pl.BlockSpec((tm, c), lambda i: (nt - 1 - i, 0))
    nxt = pl.BlockSpec((HALO, c), lambda i: (jnp.minimum((nt - i) * tb, n // HALO - 1), 0))
    prv = pl.BlockSpec((HALO, c), lambda i: (jnp.maximum((nt - 1 - i) * tb - 1, 0), 0))
    return pl.pallas_call(
        body, grid=(nt,), in_specs=[cur, nxt, cur, prv, cur], out_specs=[cur, cur],
        out_shape=[jax.ShapeDtypeStruct((n, c), F32)] * 2, scratch_shapes=[pltpu.VMEM((HALO, c), F32)], name=name,
        compiler_params=pltpu.CompilerParams(dimension_semantics=("arbitrary",)),
    )(a, a, h, h, dh)


def _gdn_fwd(q, k, v, gb, *, name, xch=None):
    n = q.shape[0]
    rows = GDN_STEP * CHUNK
    ns = n // rows

    def body(q_ref, k_ref, v_ref, gb_ref, o_ref, s_ref, state):
        @pl.when(pl.program_id(0) == 0)
        def _():
            state[...] = jnp.zeros_like(state)

        s0 = state[...]
        s_ref[0] = s0
        o, s1 = _f_gdn_chunks(q_ref[...], k_ref[...], v_ref[...], gb_ref[...], s0)
        o_ref[...] = o
        state[...] = s1

    row = pl.BlockSpec((rows, DG), lambda i: (i, 0))
    return _call(
        body, grid=(ns,), in_specs=[row, row, row, pl.BlockSpec((rows, 128), lambda i: (i, 0))],
        out_specs=[row, pl.BlockSpec((1, DG, HD), lambda i: (i, 0, 0))],
        out_shape=[jax.ShapeDtypeStruct((n, DG), F32), jax.ShapeDtypeStruct((ns, DG, HD), F32)],
        scratch=[pltpu.VMEM((DG, HD), F32)], name=name, args=(q, k, v, gb), xch=xch)


def _gdn_bwd(q, k, v, gb, s_all, do, *, name, xch=None):
    n = q.shape[0]
    rows = GDN_STEP * CHUNK
    ns = n // rows

    def body(q_ref, k_ref, v_ref, gb_ref, s_ref, do_ref, dq_ref, dk_ref, dv_ref, dgb_ref, dstate):
        @pl.when(pl.program_id(0) == 0)
        def _():
            dstate[...] = jnp.zeros_like(dstate)

        _, vjp = jax.vjp(_f_gdn_chunks, q_ref[...], k_ref[...], v_ref[...], gb_ref[...], s_ref[0])
        dq_ref[...], dk_ref[...], dv_ref[...], dgb_ref[...], dstate[...] = vjp((do_ref[...], dstate[...]))

    row = pl.BlockSpec((rows, DG), lambda i: (ns - 1 - i, 0))
    gsp = pl.BlockSpec((rows, 128), lambda i: (ns - 1 - i, 0))
    return _call(
        body, grid=(ns,), in_specs=[row, row, row, gsp, pl.BlockSpec((1, DG, HD), lambda i: (ns - 1 - i, 0, 0)), row],
        out_specs=[row, row, row, gsp],
        out_shape=[jax.ShapeDtypeStruct((n, DG), F32)] * 3 + [jax.ShapeDtypeStruct((n, 128), F32)],
        scratch=[pltpu.VMEM((DG, HD), F32)], name=name, args=(q, k, v, gb, s_all, do), xch=xch)


def _loss_head(x1, y2, w, tgt, *, tm, name):
    n, c = x1.shape

    def body(x_ref, y_ref, w_ref, t_ref, l_ref, d_ref):
        err = x_ref[...] + _rms(y_ref[...], w_ref[...]) - t_ref[...]
        part = jnp.sum(jnp.sum(err * err, axis=1, keepdims=True), axis=0, keepdims=True) * (0.5 / c)

        @pl.when(pl.program_id(0) == 0)
        def _():
            l_ref[...] = jnp.zeros_like(l_ref)

        l_ref[...] += part
        d_ref[...] = err * (1.0 / c)

    row = pl.BlockSpec((tm, c), lambda i: (i, 0))
    return pl.pallas_call(
        body, grid=(n // tm,), in_specs=[row, row, pl.BlockSpec((1, c), lambda i: (0, 0)), row],
        out_specs=[pl.BlockSpec((8, 128), lambda i: (0, 0)), row],
        out_shape=[jax.ShapeDtypeStruct((8, 128), F32), jax.ShapeDtypeStruct((n, c), F32)], name=name,
        compiler_params=pltpu.CompilerParams(dimension_semantics=("arbitrary",)),
    )(x1, y2, w, tgt)


def _sum_parts(own, recv, *, tr, name):
    r, c = own.shape
    p = recv.shape[0]

    def body(o_ref, r_ref, s_ref):
        s = o_ref[...]
        for q in range(p):
            s = s + r_ref[q].astype(F32)
        s_ref[...] = s

    return pl.pallas_call(
        body, grid=(r // tr,), in_specs=[pl.BlockSpec((tr, c), lambda i: (i, 0)), pl.BlockSpec((p, tr, c), lambda i: (0, i, 0))],
        out_specs=pl.BlockSpec((tr, c), lambda i: (i, 0)), out_shape=jax.ShapeDtypeStruct((r, c), F32), name=name,
        compiler_params=pltpu.CompilerParams(dimension_semantics=("parallel",)),
    )(own, recv)


def _sum_slots(buf, *, name):
    p, r, c = buf.shape

    def body(b_ref, s_ref):
        s = b_ref[0]
        for q in range(1, p):
            s = s + b_ref[q]
        s_ref[...] = s

    return pl.pallas_call(body, out_shape=jax.ShapeDtypeStruct((r, c), F32), name=name)(buf)


def _adamw(w, m, v, gs, *, tr, name, xch=None):
    ngrp, r, c = w.shape
    nterm = len(gs[0])
    per = r // tr
    c1 = 1.0 - ADAM_B1 ** ADAM_STEP
    c2 = 1.0 - ADAM_B2 ** ADAM_STEP

    def body(*refs):
        w_ref, m_ref, v_ref = refs[:3]
        g_refs = refs[3:3 + ngrp * nterm]
        g_ref, d_ref, nm_ref, nv_ref = refs[3 + ngrp * nterm:]
        grp = pl.program_id(0) // per
        g = None
        for q in range(ngrp):
            gq = g_refs[q * nterm][...]
            for t in range(1, nterm):
                gq = gq + g_refs[q * nterm + t][...]
            g = gq if g is None else jnp.where(grp == q, gq, g)
        nm = ADAM_B1 * m_ref[...] + (1.0 - ADAM_B1) * g
        nv = ADAM_B2 * v_ref[...] + (1.0 - ADAM_B2) * (g * g)
        g_ref[...] = g
        d_ref[...] = -ADAM_LR * ((nm / c1) / (jnp.sqrt(nv / c2) + ADAM_EPS) + ADAM_WD * w_ref[...])
        nm_ref[...] = nm
        nv_ref[...] = nv

    blk = pl.BlockSpec((None, tr, c), lambda i: (i // per, i % per, 0))
    g_specs = [pl.BlockSpec((tr, c), lambda i, q=q: (jnp.clip(i - q * per, 0, per - 1), 0)) for q in range(ngrp) for _ in range(nterm)]
    return _call(body, grid=(ngrp * per,), in_specs=[blk] * 3 + g_specs, out_specs=[blk] * 4,
                 out_shape=[jax.ShapeDtypeStruct((ngrp, r, c), F32)] * 4, scratch=[], name=name,
                 args=(w, m, v, *[t for grp in gs for t in grp]), xch=xch)


class _SwapCores:
    def __init__(self, arrs):
        self.arrs = list(arrs)
        n = len(self.arrs)
        self.out_shape = [jax.ShapeDtypeStruct(a.shape, a.dtype) for a in self.arrs]
        self.sems = [pltpu.SemaphoreType.DMA((n,)), pltpu.SemaphoreType.DMA((n,))]

    def _copies(self, ins, outs, sems):
        send, recv = sems
        sib = (lax.axis_index("x"), lax.axis_index("y"), 1 - lax.axis_index("c"))
        return [pltpu.make_async_remote_copy(src_ref=ins[a], dst_ref=outs[a], send_sem=send.at[a], recv_sem=recv.at[a],
                                             device_id=sib, device_id_type=MESH_ID) for a in range(len(ins))]

    def issue(self, ins, outs, sems):
        for cp in self._copies(ins, outs, sems):
            cp.start()

    def finish(self, ins, outs, sems):
        copies = self._copies(ins, outs, sems)
        for cp in copies:
            cp.wait_recv()
        for cp in copies:
            cp.wait_send()


class _GatherDevices:
    def __init__(self, buf):
        self.arrs = [buf]
        self.out_shape = [jax.ShapeDtypeStruct((8,) + buf.shape, buf.dtype)]
        self.sems = [pltpu.SemaphoreType.DMA((7,)), pltpu.SemaphoreType.DMA((7,)), pltpu.SemaphoreType.DMA((1,))]

    def _copies(self, ins, outs, sems, arriving):
        send, recv, lsem = sems
        x, y, c = lax.axis_index("x"), lax.axis_index("y"), lax.axis_index("c")
        me = 4 * x + 2 * y + c
        local = [] if arriving else [pltpu.make_async_copy(ins[0], outs[0].at[me], lsem.at[0])]
        remote = []
        for k, (fx, fy, fc) in enumerate(_DEV_REL):
            px, py, pc = _flip(x, fx), _flip(y, fy), _flip(c, fc)
            remote.append(pltpu.make_async_remote_copy(
                src_ref=ins[0], dst_ref=outs[0].at[4 * px + 2 * py + pc if arriving else me], send_sem=send.at[k], recv_sem=recv.at[k],
                device_id=(px, py, pc), device_id_type=MESH_ID))
        return local, remote

    def issue(self, ins, outs, sems):
        local, push = self._copies(ins, outs, sems, False)
        for cp in local + push:
            cp.start()

    def finish(self, ins, outs, sems):
        for cp in self._copies(ins, outs, sems, True)[1]:
            cp.wait_recv()
        local, push = self._copies(ins, outs, sems, False)
        for cp in push:
            cp.wait_send()
        for cp in local:
            cp.wait()


class _Together:
    def __init__(self, parts):
        self.parts = list(parts)
        self.arrs = [a for p in self.parts for a in p.arrs]
        self.out_shape = [s for p in self.parts for s in p.out_shape]
        self.sems = [s for p in self.parts for s in p.sems]

    def _split(self, ins, outs, sems):
        i = o = s = 0
        for p in self.parts:
            ni, no, ns = len(p.arrs), len(p.out_shape), len(p.sems)
            yield p, ins[i:i + ni], outs[o:o + no], sems[s:s + ns]
            i, o, s = i + ni, o + no, s + ns

    def issue(self, ins, outs, sems):
        for p, a, b, c in self._split(ins, outs, sems):
            p.issue(a, b, c)

    def finish(self, ins, outs, sems):
        for p, a, b, c in self._split(ins, outs, sems):
            p.finish(a, b, c)


def _pad_cols(w):
    z = jnp.zeros(w.shape[:-1] + (NP - N_IN,), w.dtype)
    return jnp.concatenate([w[..., 0:2048], w[..., 2056:N_IN], w[..., 2048:2056], z], axis=-1)


def _unpad_cols(w):
    return jnp.concatenate([w[..., 0:2048], w[..., OFF_BA:OFF_BA + 8], w[..., 2048:OFF_BA]], axis=-1)


def _lanes(v, off):
    return jnp.pad(v.reshape(1, -1), ((0, 0), (off, 128 - off - v.size)))


def _block_diag(w):
    eye = jnp.eye(8, dtype=w.dtype)
    return (w[:, :, None, :] * eye[:, None, :, None]).reshape(DG, DG)


def _diag_blocks(w):
    return jnp.stack([w[h * 64:(h + 1) * 64, h * 64:(h + 1) * 64] for h in range(8)])


def _mixer_params(p):
    return dict(
        gdn_conv_w=p["gdn_conv_w"], alog=_lanes(p["gdn_a_log"], HEADS), dtb=_lanes(p["gdn_dt_bias"], HEADS),
        gdn_nw=p["gdn_norm_w"].reshape(1, HD),
        lru_conv_w=p["lru_conv_w"], lru_conv_b=p["lru_conv_b"].reshape(1, DG),
        wa=_block_diag(p["lru_wa"]), ba=p["lru_ba"].reshape(1, DG), wx=_block_diag(p["lru_wx"]), bx=p["lru_bx"].reshape(1, DG),
        lam=p["lru_lambda"].reshape(1, DG),
        ln_w=p["sgu_ln_w"].reshape(1, DG), ln_b=p["sgu_ln_b"].reshape(1, DG), ws=p["sgu_ws"].reshape(DG, 128),
        bst=jnp.pad(p["sgu_b"].T, ((0, 0), (0, 124))),
        sconv_w=p["sconv_w"], gw0=p["grp_norm_w"][0:1], gw1=p["grp_norm_w"][1:2], gw2=p["grp_norm_w"][2:3],
    )


def _mixer_param_grads(g):
    return dict(
        gdn_conv_w=g["gdn_conv_w"], gdn_a_log=g["alog"][0, HEADS:2 * HEADS], gdn_dt_bias=g["dtb"][0, HEADS:2 * HEADS],
        gdn_norm_w=g["gdn_nw"][0],
        lru_conv_w=g["lru_conv_w"], lru_conv_b=g["lru_conv_b"][0],
        lru_wa=_diag_blocks(g["wa"]), lru_ba=g["ba"].reshape(8, 64), lru_wx=_diag_blocks(g["wx"]), lru_bx=g["bx"].reshape(8, 64),
        lru_lambda=g["lam"][0],
        sgu_ln_w=g["ln_w"][0], sgu_ln_b=g["ln_b"][0], sgu_ws=g["ws"].reshape(4, 128, 128), sgu_b=g["bst"][:, 0:4].T,
        sconv_w=g["sconv_w"], grp_norm_w=jnp.concatenate([g["gw0"], g["gw1"], g["gw2"]], axis=0),
    )


TM_MIX = 256


def _mixers_fwd(p, mp, tag="", xch=None):
    c = lambda *names: [(mp[n], False) for n in names]
    q, k, v, gb = _rowwise(_f_gdn_pre, [(p, 0, 1536, True), (p, OFF_BA // 128, 128, False)], c("gdn_conv_w", "alog", "dtb"),
                           [(DG, F32)] * 3 + [(128, F32)], tm=TM_MIX, name="gdn_pre" + tag)
    (o, s_all), xres = _gdn_fwd(q, k, v, gb, name="gdn_chunks" + tag, xch=xch)
    y_a, = _rowwise(_f_gdn_post, [(o, 0, DG, False), (p, OFF_Z // DG, DG, False)], c("gdn_nw"), [(DG, BF)], tm=TM_MIX, name="gdn_post" + tag)
    a, b = _rowwise(_f_lru_ab, [(p, OFF_LX // DG, DG, True)], c("lru_conv_w", "lru_conv_b", "wa", "ba", "wx", "bx", "lam"),
                    [(DG, F32)] * 2, tm=TM_MIX, name="lru_ab" + tag)
    hs = _scan_fwd(a, b, tm=TM_MIX, name="lru_scan" + tag)
    y_b, = _rowwise(_f_lru_post, [(hs, 0, DG, False), (p, OFF_LG // DG, DG, False)], c("gw0"), [(DG, BF)], tm=TM_MIX, name="lru_post" + tag)
    y_c, = _rowwise(_f_sgu, [(p, OFF_UV // 1024, 1024, False)], c("ln_w", "ln_b", "ws", "bst", "gw1"), [(DG, BF)], tm=TM_MIX, name="sgu" + tag)
    y_d, = _rowwise(_f_sconv, [(p, OFF_SB // DG, DG, False), (p, OFF_SC // DG, DG, True), (p, OFF_SH // DG, DG, True)],
                    c("sconv_w", "gw2"), [(DG, BF)], tm=TM_MIX, name="sconv" + tag)
    return jnp.concatenate([y_a, y_b, y_c, y_d], axis=1), (q, k, v, gb, o, s_all, a, hs), xres


def _mixers_bwd(p, mp, saved, dy, tag="", xch=None):
    q, k, v, gb, o, s_all, a, hs = saved
    c = lambda *names: [(mp[n], False, True) for n in names]
    g = {}

    (do, dz), (g["gdn_nw"],) = _rowwise_vjp(
        _f_gdn_post, [(o, 0, DG, False, F32), (p, OFF_Z // DG, DG, False, BF)], c("gdn_nw"), [(dy, 0, DG)], tm=TM_MIX, name="gdn_post_b" + tag)
    (dq, dk, dv, dgb), xres = _gdn_bwd(q, k, v, gb, s_all, do, name="gdn_chunks_b" + tag, xch=xch)
    (dqkv, dba), (g["gdn_conv_w"], g["alog"], g["dtb"]) = _rowwise_vjp(
        _f_gdn_pre, [(p, 0, 1536, True, BF), (p, OFF_BA // 128, 128, False, BF)], c("gdn_conv_w", "alog", "dtb"),
        [(dq, 0, DG), (dk, 0, DG), (dv, 0, DG), (dgb, 0, 128)], tm=TM_MIX, name="gdn_pre_b" + tag)

    (dhs, dgate), (g["gw0"],) = _rowwise_vjp(
        _f_lru_post, [(hs, 0, DG, False, F32), (p, OFF_LG // DG, DG, False, BF)], c("gw0"), [(dy, 1, DG)], tm=TM_MIX, name="lru_post_b" + tag)
    da, db = _scan_bwd(a, hs, dhs, tm=TM_MIX, name="lru_scan_b" + tag)
    (dlx,), (g["lru_conv_w"], g["lru_conv_b"], g["wa"], g["ba"], g["wx"], g["bx"], g["lam"]) = _rowwise_vjp(
        _f_lru_ab, [(p, OFF_LX // DG, DG, True, BF)], c("lru_conv_w", "lru_conv_b", "wa", "ba", "wx", "bx", "lam"),
        [(da, 0, DG), (db, 0, DG)], tm=TM_MIX, name="lru_ab_b" + tag)

    (duv,), (g["ln_w"], g["ln_b"], g["ws"], g["bst"], g["gw1"]) = _rowwise_vjp(
        _f_sgu, [(p, OFF_UV // 1024, 1024, False, BF)], c("ln_w", "ln_b", "ws", "bst", "gw1"), [(dy, 2, DG)], tm=TM_MIX, name="sgu_b" + tag)

    (dsb, dsc, dsh), (g["sconv_w"], g["gw2"]) = _rowwise_vjp(
        _f_sconv, [(p, OFF_SB // DG, DG, False, BF), (p, OFF_SC // DG, DG, True, BF), (p, OFF_SH // DG, DG, True, BF)],
        c("sconv_w", "gw2"), [(dy, 3, DG)], tm=TM_MIX, name="sconv_b" + tag)

    dp = jnp.concatenate([dqkv, dz, dlx, dgate, duv, dsb, dsc, dsh, dba], axis=1)
    return dp, g, xres


WEIGHTS = ("pre_mix_norm", "w_in", "gdn_conv_w", "gdn_a_log", "gdn_dt_bias", "gdn_norm_w", "lru_conv_w", "lru_conv_b", "lru_wa",
           "lru_ba", "lru_wx", "lru_bx", "lru_lambda", "sgu_ln_w", "sgu_ln_b", "sgu_ws", "sgu_b", "sconv_w", "grp_norm_w", "w_out",
           "post_mix_norm", "pre_ffn_norm", "ffn_up", "ffn_conv_w", "ffn_conv_b", "ffn_down", "post_ffn_norm")
BIG = ("w_in", "ffn_up", "w_out", "ffn_down")
CHIP_SHARDED_SMALL = ("gdn_conv_w", "lru_conv_w", "sconv_w", "grp_norm_w", "ffn_conv_w")
MIXER_PARAMS = ("gdn_conv_w", "gdn_a_log", "gdn_dt_bias", "gdn_norm_w", "lru_conv_w", "lru_conv_b", "lru_wa", "lru_ba", "lru_wx",
                "lru_bx", "lru_lambda", "sgu_ln_w", "sgu_ln_b", "sgu_ws", "sgu_b", "sconv_w", "grp_norm_w")
TM_ROW = 256
N_FF_TILES = 11
FF_TILE = D_FF // N_FF_TILES
PACK_ROWS = 256
FFN_UP_CUTS = (688, 1712)


def _pack(arrs):
    parts = []
    for a in arrs:
        n = a.size
        parts.append(jnp.pad(a.reshape(-1), (0, -n % 1024)).reshape(-1, 128))
    rows = sum(p.shape[0] for p in parts)
    parts.append(jnp.zeros((-rows % PACK_ROWS, 128), F32))
    return jnp.concatenate(parts, axis=0)


def _unpack(buf, shapes):
    out, row = [], 0
    for s in shapes:
        n = math.prod(s)
        rows = -(-n // 1024) * 8
        out.append(buf[row:row + rows].reshape(-1)[:n].reshape(s))
        row += rows
    return out


def _row_tile(rows, cols):
    return 256 if rows % 256 == 0 and cols <= 1024 else 128


def _w_in_full(got):
    return _pad_cols(got.transpose(1, 0, 2).reshape(D, N_IN))


def _w_in_blocks(dw):
    return _unpad_cols(dw).reshape(D, 4, N_IN // 4).transpose(1, 0, 2)


def _layer_fwd(l, xs, h, w_in_l, w_out_g, shard, sp, mp, nxt):
    t = str(l)
    up = shard["ffn_up"][l]
    first = [up[:FFN_UP_CUTS[0]]] + ([shard["w_out"][l]] if w_out_g is None else [])
    p, got = _matmul(h, w_in_l, "nn", F32, tm=1024, tn=1152, tk=D, name="mm_in" + t, xch=_GatherChips(first))
    w_out_l = (got[1] if w_out_g is None else w_out_g).reshape(D, D)
    ycat, saved, (w_up_b,) = _mixers_fwd(p, mp, tag=t, xch=_GatherChips([up[FFN_UP_CUTS[0]:FFN_UP_CUTS[1]]]))
    y, (w_up_c,) = _matmul(ycat, w_out_l, "nn", F32, tm=1024, tn=1024, tk=D, name="mm_out" + t, xch=_GatherChips([up[FFN_UP_CUTS[1]:]]))
    w_up = jnp.concatenate([got[0], w_up_b, w_up_c], axis=1)
    x1, h2 = _rowwise(_f_post_pre, [(xs, 0, D, False), (y, 0, D, False)], [(sp["post_mix_norm"][l], False), (sp["pre_ffn_norm"][l], False)],
                      [(D, F32), (D, BF)], tm=TM_ROW, name="post_mix" + t)
    u, (w_dn_g,) = _matmul(h2, w_up, "nn", F32, tm=1024, tn=1408, tk=D, name="mm_up" + t, b_blocks=True,
                           xch=_GatherChips([shard["ffn_down"][l]]))
    ffn_rows = [(u, 0, FF_TILE, True), (u, N_FF_TILES, FF_TILE, True)]
    ffn_consts = [(c, True) for c in sp["ffn_conv"][l]]
    if nxt is None:
        act, = _rowwise(_f_ffn_act, ffn_rows, ffn_consts, [(FF_TILE, BF)], tm=TM_ROW, ncol=N_FF_TILES, name="ffn_act" + t)
        n_out = None
    else:
        (act,), (n_out,) = _rowwise(_f_ffn_act, ffn_rows, ffn_consts, [(FF_TILE, BF)], tm=TM_ROW, ncol=N_FF_TILES, name="ffn_act" + t,
                                    xch=_GatherChips([nxt[1]]))
    w_dn_l = w_dn_g.reshape(D_FF, D)
    y2, n_in = _matmul(act, w_dn_l, "nn", F32, tm=1024, tn=1024, tk=1408, name="mm_down" + t,
                       xch=None if nxt is None else _GatherChips([nxt[0]]))
    keep = dict(xs=xs, h=h, p=p, saved=saved, ycat=ycat, y=y, x1=x1, h2=h2, u=u, act=act, y2=y2,
                w_in=w_in_l, w_out=w_out_l, w_up=w_up, w_dn=w_dn_l)
    return keep, (None if nxt is None else (n_in[0], n_out))


def _layer_bwd(l, a, dx1, dy2, sp, mp):
    t = str(l)
    g = {}
    dact, _ = _matmul(dy2, a["w_dn"], "nt", F32, tm=1024, tn=1408, tk=D, name="mm_down_dx" + t)
    dw_dn, _ = _matmul(a["act"], dy2, "tn", F32, tm=1408, tn=1024, tk=1024, name="mm_down_dw" + t)
    dw_dn = dw_dn.reshape(4, D_FF // 4, D)
    (dug, duv), gc = _rowwise_vjp(
        _f_ffn_act, [(a["u"], 0, FF_TILE, True, BF), (a["u"], N_FF_TILES, FF_TILE, True, BF)], [(c, True, True) for c in sp["ffn_conv"][l]],
        [(dact, 0, FF_TILE)], tm=TM_ROW, ncol=N_FF_TILES, name="ffn_act_b" + t)
    g["ffn_conv_w"] = jnp.concatenate([gc[0], gc[1]], axis=1)
    g["ffn_conv_b"] = jnp.concatenate([gc[2], gc[3]], axis=1)[0]
    du = jnp.concatenate([dug, duv], axis=1)
    dh2, (r_dn,) = _matmul(du, a["w_up"], "nt", F32, tm=1024, tn=1024, tk=2816, name="mm_up_dx" + t, b_blocks=True,
                           xch=_ScatterChips([dw_dn.astype(BF)]))
    dw_up, _ = _matmul(a["h2"], du, "tn", F32, tm=1024, tn=1408, tk=1024, name="mm_up_dw" + t, out_blocks=True)
    (dxs, dy), (gpm, gpf) = _rowwise_vjp(
        _f_post_pre, [(a["xs"], 0, D, False, F32), (a["y"], 0, D, False, BF)],
        [(sp["post_mix_norm"][l], False, True), (sp["pre_ffn_norm"][l], False, True)], [(dx1, 0, D), (dh2, 0, D)], tm=TM_ROW, name="post_mix_b" + t)
    g["post_mix_norm"], g["pre_ffn_norm"] = gpm[0], gpf[0]
    dycat, _ = _matmul(dy, a["w_out"], "nt", F32, tm=1024, tn=1024, tk=D, name="mm_out_dx" + t)
    dw_out, _ = _matmul(a["ycat"], dy, "tn", F32, tm=1024, tn=1024, tk=1024, name="mm_out_dw" + t)
    dw_out = dw_out.reshape(4, D // 4, D)
    dp, gm, (r_up,) = _mixers_bwd(a["p"], mp, a["saved"], dycat, tag=t, xch=_ScatterChips([dw_up.astype(BF)]))
    g.update(_mixer_param_grads(gm))
    dw_in, (r_out,) = _matmul(a["h"], dp, "tn", F32, tm=1024, tn=1152, tk=1024, name="mm_in_dw" + t, xch=_ScatterChips([dw_out.astype(BF)]))
    dw_in = _w_in_blocks(dw_in)
    dh, (r_in,) = _matmul(dp, a["w_in"], "nt", F32, tm=1024, tn=1024, tk=1920, name="mm_in_dx" + t, xch=_ScatterChips([dw_in.astype(BF)]))
    big = {"ffn_down": (dw_dn, r_dn), "ffn_up": (dw_up, r_up), "w_out": (dw_out, r_out), "w_in": (dw_in, r_in)}
    return dxs, dh, g, big


def kernel(x, pre_mix_norm, w_in, gdn_conv_w, gdn_a_log, gdn_dt_bias, gdn_norm_w, lru_conv_w, lru_conv_b, lru_wa, lru_ba, lru_wx, lru_bx, lru_lambda, sgu_ln_w, sgu_ln_b, sgu_ws, sgu_b, sconv_w, grp_norm_w, w_out, post_mix_norm, pre_ffn_norm, ffn_up, ffn_conv_w, ffn_conv_b, ffn_down, post_ffn_norm, loss_target, m_pre_mix_norm, m_w_in, m_gdn_conv_w, m_gdn_a_log, m_gdn_dt_bias, m_gdn_norm_w, m_lru_conv_w, m_lru_conv_b, m_lru_wa, m_lru_ba, m_lru_wx, m_lru_bx, m_lru_lambda, m_sgu_ln_w, m_sgu_ln_b, m_sgu_ws, m_sgu_b, m_sconv_w, m_grp_norm_w, m_w_out, m_post_mix_norm, m_pre_ffn_norm, m_ffn_up, m_ffn_conv_w, m_ffn_conv_b, m_ffn_down, m_post_ffn_norm, v_pre_mix_norm, v_w_in, v_gdn_conv_w, v_gdn_a_log, v_gdn_dt_bias, v_gdn_norm_w, v_lru_conv_w, v_lru_conv_b, v_lru_wa, v_lru_ba, v_lru_wx, v_lru_bx, v_lru_lambda, v_sgu_ln_w, v_sgu_ln_b, v_sgu_ws, v_sgu_b, v_sconv_w, v_grp_norm_w, v_w_out, v_post_mix_norm, v_pre_ffn_norm, v_ffn_up, v_ffn_conv_w, v_ffn_conv_b, v_ffn_down, v_post_ffn_norm):
    given = dict(locals())
    me = 2 * lax.axis_index("x") + lax.axis_index("y")
    xs0, tgt = x[0], loss_target[0]

    small_sh = [given[n] for n in CHIP_SHARDED_SMALL]
    shard = {n: [given[n][l].astype(BF) for l in range(DEPTH)] for n in BIG}
    got = _exchange(_GatherChips([shard["w_in"][0], _pack(small_sh)]), name="gather_first")
    full = {n: given[n] for n in WEIGHTS if n not in BIG and n not in CHIP_SHARDED_SMALL}
    per_chip = [_unpack(got[1][j], [s.shape for s in small_sh]) for j in range(4)]
    parts = [jnp.stack([per_chip[j][i] for j in range(4)]) for i in range(len(small_sh))]
    for n, pj in zip(CHIP_SHARDED_SMALL, parts):
        full[n] = pj.transpose(1, 2, 0, 3).reshape(pj.shape[1], pj.shape[2], 4 * pj.shape[3])
    sp = {n: [full[n][l:l + 1] for l in range(DEPTH)] for n in ("pre_mix_norm", "post_mix_norm", "pre_ffn_norm", "post_ffn_norm")}
    sp["ffn_conv"] = [[full["ffn_conv_w"][l][:, :D_FF], full["ffn_conv_w"][l][:, D_FF:], full["ffn_conv_b"][l:l + 1, :D_FF],
                       full["ffn_conv_b"][l:l + 1, D_FF:]] for l in range(DEPTH)]
    mps = [_mixer_params({n: full[n][l] for n in MIXER_PARAMS}) for l in range(DEPTH)]

    h, = _rowwise(_f_pre, [(xs0, 0, D, False)], [(sp["pre_mix_norm"][0], False)], [(D, BF)], tm=TM_ROW, name="pre_mix0")
    a0, (w_in1, w_out1) = _layer_fwd(0, xs0, h, _w_in_full(got[0]), None, shard, sp, mps[0], (shard["w_in"][1], shard["w_out"][1]))
    xs1, h1 = _rowwise(_f_post_pre, [(a0["x1"], 0, D, False), (a0["y2"], 0, D, False)],
                       [(sp["post_ffn_norm"][0], False), (sp["pre_mix_norm"][1], False)], [(D, F32), (D, BF)], tm=TM_ROW, name="post_ffn0")
    a1, _ = _layer_fwd(1, xs1, h1, _w_in_full(w_in1), w_out1, shard, sp, mps[1], None)
    lacc, dxo = _loss_head(a1["x1"], a1["y2"], sp["post_ffn_norm"][1], tgt, tm=TM_ROW, name="loss_head")

    gl = [None, None]
    (dx1, dy2), (gpf1,) = _rowwise_vjp(_f_post, [(a1["x1"], 0, D, False, F32), (a1["y2"], 0, D, False, BF)],
                                      [(sp["post_ffn_norm"][1], False, True)], [(dxo, 0, D)], tm=TM_ROW, name="post_ffn1_b")
    big = [None, None]
    dxs1, dh1, gl[1], big[1] = _layer_bwd(1, a1, dx1, dy2, sp, mps[1])
    gl[1]["post_ffn_norm"] = gpf1[0]
    (dx1, dy2), (gpf0, gpm1) = _rowwise_vjp(
        _f_post_pre, [(a0["x1"], 0, D, False, F32), (a0["y2"], 0, D, False, BF)],
        [(sp["post_ffn_norm"][0], False, True), (sp["pre_mix_norm"][1], False, True)], [(dxs1, 0, D), (dh1, 0, D)], tm=TM_ROW, name="post_ffn0_b")
    gl[1]["pre_mix_norm"] = gpm1[0]
    dxs0, dh0, gl[0], big[0] = _layer_bwd(0, a0, dx1, dy2, sp, mps[0])
    gl[0]["post_ffn_norm"] = gpf0[0]
    (grad_x,), (gpm0,) = _rowwise_vjp(lambda xv, w: (xv, _rms(xv, w)), [(xs0, 0, D, False, F32)], [(sp["pre_mix_norm"][0], False, True)],
                                     [(dxs0, 0, D), (dh0, 0, D)], tm=TM_ROW, name="pre_mix0_b")
    gl[0]["pre_mix_norm"] = gpm0[0]

    small = [n for n in WEIGHTS if n not in BIG]
    gfull = {n: jnp.stack([gl[0][n], gl[1][n]]) for n in small}
    small_grads = _pack([gfull[n] for n in small] + [lacc[0, 0:1]])
    sums = {}
    for n in BIG:
        for l in range(DEPTH):
            blocks, recv = big[l][n]
            own = lax.dynamic_index_in_dim(blocks, me, 0, keepdims=False)
            sums[n] = sums.get(n, []) + [_sum_parts(own, recv, tr=_row_tile(*own.shape), name="sum_grads_%s%d" % (n, l))]
    kinds = ("grad", "delta", "new_m", "new_v")
    outs = {kind: {} for kind in kinds}
    order = ("w_out", "ffn_down", "w_in", "ffn_up")
    other = _exchange(_SwapCores(sums[order[0]]), name="swap_first")
    gathered = None
    for i, n in enumerate(order):
        riders = []
        if i + 1 < len(order):
            riders.append(_SwapCores(sums[order[i + 1]]))
        if n == "ffn_down":
            riders.append(_GatherDevices(small_grads))
        shp = given[n].shape
        r4, got = _adamw(given[n], given["m_" + n], given["v_" + n], [[s, o] for s, o in zip(sums[n], other)],
                         tr=_row_tile(shp[1], shp[2]), name="adamw_" + n, xch=_Together(riders) if riders else None)
        outs_n = dict(zip(kinds, r4))
        for kind in kinds:
            outs[kind][n] = outs_n[kind]
        if i + 1 < len(order):
            other = got[:DEPTH]
        if n == "ffn_down":
            gathered = got[DEPTH]

    tot = _sum_slots(gathered, name="sum_small_grads")
    red = dict(zip(small + ["loss"], _unpack(tot, [gfull[n].shape for n in small] + [(1,)])))
    for n in CHIP_SHARDED_SMALL:
        cb = given[n].shape[-1]
        red[n] = lax.dynamic_slice_in_dim(red[n], me * cb, cb, axis=red[n].ndim - 1)
    shapes = [given[n].shape for n in small]
    res, _ = _adamw(_pack([given[n] for n in small])[None], _pack([given["m_" + n] for n in small])[None],
                    _pack([given["v_" + n] for n in small])[None], [[_pack([red[n] for n in small])]], tr=PACK_ROWS, name="adamw_small")
    for kind, r in zip(kinds, res):
        outs[kind].update(zip(small, _unpack(r[0], shapes)))

    return (red["loss"][0], grad_x[None], *[outs[k][n] for k in ("grad", "delta", "new_m", "new_v") for n in WEIGHTS])
```

```python
import functools
import math

import jax
import jax.numpy as jnp
from jax import lax
from jax.experimental import pallas as pl
from jax.experimental.pallas import tpu as pltpu

F32 = jnp.float32
BF = jnp.bfloat16
MESH_ID = pl.DeviceIdType.MESH

EPS = 1e-6
DEPTH = 2
D = 2048
DG = 512
HEADS = 4
HD = 128
CHUNK = 64
GDN_STEP = 4
LRU_C = 8.0
D_FF = 5632
N_IN = 5640
NP = 5760
OFF_Q, OFF_Z, OFF_LX, OFF_LG, OFF_UV, OFF_SB, OFF_SC, OFF_SH, OFF_BA = 0, 1536, 2048, 2560, 3072, 4096, 4608, 5120, 5632

ADAM_LR, ADAM_B1, ADAM_B2, ADAM_EPS, ADAM_WD, ADAM_STEP = 0.001, 0.9, 0.999, 1e-08, 0.01, 10

HALO = 8


def _mk_bdot(ca, cb):
    na, nb = 1 - ca, 1 - cb

    def dg(x, y, cx, cy):
        return lax.dot_general(x.astype(BF), y.astype(BF), (((cx,), (cy,)), ((), ())), preferred_element_type=F32)

    @jax.custom_vjp
    def f(a, b):
        return dg(a, b, ca, cb)

    def fwd(a, b):
        return dg(a, b, ca, cb), (a, b)

    def bwd(res, g):
        a, b = res
        da = dg(g, b, 1, nb) if ca == 1 else dg(b, g, nb, 1)
        db = dg(a, g, na, 0) if cb == 0 else dg(g, a, 0, na)
        return da.astype(a.dtype), db.astype(b.dtype)

    f.defvjp(fwd, bwd)
    return f


_bdot = _mk_bdot(1, 0)
_bdot_nt = _mk_bdot(1, 1)
_bdot_tn = _mk_bdot(0, 0)


def _sigmoid(x):
    return 1.0 / (1.0 + jnp.exp(-x))


def _silu(x):
    return x * _sigmoid(x)


def _gelu(x):
    return 0.5 * x * (1.0 + jnp.tanh(0.7978845608028654 * (x + 0.044715 * (x * x * x))))


def _log1p(z):
    u = 1.0 + z
    d = u - 1.0
    return jnp.where(d == 0.0, z, jnp.log(u) * (z / jnp.where(d == 0.0, 1.0, d)))


def _softplus(x):
    return jnp.maximum(x, 0.0) + _log1p(jnp.exp(-jnp.abs(x)))


def _neg_expm1(y):
    t = jnp.tanh(0.5 * y)
    return -2.0 * t / (1.0 - t)


def _rms(x, w):
    return x * lax.rsqrt(jnp.mean(x * x, axis=-1, keepdims=True) + EPS) * w


def _rows(x):
    return lax.broadcasted_iota(jnp.int32, x.shape, 0)


def _mk_shift():
    @functools.partial(jax.custom_vjp, nondiff_argnums=(1,))
    def shift(xx, s):
        n = xx.shape[0]
        return pltpu.roll(xx, s, axis=0)[HALO:n] if s else xx[HALO:n]

    def fwd(xx, s):
        return shift(xx, s), None

    def bwd(s, _, g):
        ext = jnp.concatenate([g, jnp.zeros((HALO, g.shape[1]), g.dtype)], axis=0)
        return (pltpu.roll(ext, HALO - s, axis=0),)

    shift.defvjp(fwd, bwd)
    return shift


_shift = _mk_shift()


def _mk_chunk_cumsum():
    def run(x, up):
        n = x.shape[0]
        pos = _rows(x) % CHUNK
        d = 1
        while d < CHUNK:
            if up:
                x = x + jnp.where(pos < CHUNK - d, pltpu.roll(x, n - d, axis=0), 0.0)
            else:
                x = x + jnp.where(pos >= d, pltpu.roll(x, d, axis=0), 0.0)
            d *= 2
        return x

    @jax.custom_vjp
    def cumsum(x):
        return run(x, False)

    cumsum.defvjp(lambda x: (run(x, False), None), lambda _, g: (run(g, True),))
    return cumsum


_chunk_cumsum = _mk_chunk_cumsum()


def _causal_conv(xx, w):
    K = w.shape[0]
    y = _shift(xx, K - 1) * w[0:1, :]
    for k in range(1, K):
        y = y + _shift(xx, K - 1 - k) * w[k:k + 1, :]
    return y


def _f_pre(x, w):
    return (_rms(x, w),)


def _f_post_pre(x, y, w_post, w_pre):
    x1 = x + _rms(y, w_post)
    return x1, _rms(x1, w_pre)


def _f_post(x, y, w_post):
    return (x + _rms(y, w_post),)


def _heads(fn, *xs):
    return jnp.concatenate([fn(*[x[:, h * HD:(h + 1) * HD] for x in xs]) for h in range(HEADS)], axis=1)


def _l2n(t):
    return t * lax.rsqrt(jnp.sum(t * t, axis=-1, keepdims=True) + EPS)


def _f_gdn_pre(qkv, ba, conv_w, alog, dtb):
    c = _silu(_causal_conv(qkv, conv_w))
    q = _heads(lambda t: _l2n(t) * (HD ** -0.5), c[:, 0:DG])
    k = _heads(_l2n, c[:, DG:2 * DG])
    v = c[:, 2 * DG:3 * DG]
    lane = lax.broadcasted_iota(jnp.int32, ba.shape, 1)
    beta = _sigmoid(ba)
    gcum = _chunk_cumsum(-jnp.exp(alog) * _softplus(ba + dtb))
    gc = jnp.where(lane < HEADS, beta, jnp.where(lane < 2 * HEADS, gcum, 0.0))
    return q, k, v, gc


def _f_gdn_chunk(q, k, v, gc, s0):
    C = q.shape[0]
    HC = HEADS * C
    sh = C.bit_length() - 1

    def stack(x):
        return jnp.concatenate([x[:, h * HD:(h + 1) * HD] for h in range(HEADS)], axis=0)

    def column(off):
        lane = lax.broadcasted_iota(jnp.int32, gc.shape, 1)
        return jnp.concatenate([jnp.sum(jnp.where(lane == off + h, gc, 0.0), axis=1, keepdims=True) for h in range(HEADS)], axis=0)

    def head(x, h):
        return x[h * C:(h + 1) * C]

    r = lax.broadcasted_iota(jnp.int32, (HC, HC), 0)
    c = lax.broadcasted_iota(jnp.int32, (HC, HC), 1)
    same = jnp.right_shift(r, sh) == jnp.right_shift(c, sh)
    causal = jnp.logical_and(same, r >= c)
    strict = jnp.logical_and(same, r > c)
    gcol, bcol = column(HEADS), column(0)
    grow = jnp.sum(jnp.where(r == c, gcol, 0.0), axis=0, keepdims=True)
    decay = jnp.where(causal, jnp.exp(jnp.where(causal, gcol - grow, 0.0)), 0.0)
    ks, qs, vs = stack(k), stack(q), stack(v)
    kb = ks * bcol
    kk = _bdot_nt(jnp.concatenate([kb, qs], axis=0), ks)
    m = jnp.where(strict, kk[0:HC] * decay, 0.0)
    attn = jnp.where(causal, kk[HC:2 * HC] * decay, 0.0)
    n = -m
    t = (r == c).astype(F32) + n
    p = n
    for _ in range(5):
        p = _bdot(p, p)
        t = t + _bdot(t, p)
    eg = jnp.exp(gcol)
    wu = _bdot(t, jnp.concatenate([kb * eg, vs * bcol], axis=1))
    w, u = wu[:, 0:HD], wu[:, HD:2 * HD]
    last = jnp.logical_and(same, jnp.bitwise_and(c, C - 1) == C - 1)
    glast = jnp.sum(jnp.where(last, grow, 0.0), axis=1, keepdims=True)
    k_g = ks * jnp.exp(glast - gcol)
    q_g = qs * eg
    ws = [_bdot(jnp.concatenate([head(w, h), head(q_g, h)], axis=0), s0[h * HD:(h + 1) * HD]) for h in range(HEADS)]
    v_new = u - jnp.concatenate([x[0:C] for x in ws], axis=0)
    o = jnp.concatenate([x[C:2 * C] for x in ws], axis=0) + _bdot(attn, v_new)
    s1 = [s0[h * HD:(h + 1) * HD] * jnp.exp(glast[h * C:h * C + 1]) + _bdot_tn(head(k_g, h), head(v_new, h)) for h in range(HEADS)]
    return jnp.concatenate([head(o, h) for h in range(HEADS)], axis=1), jnp.concatenate(s1, axis=0)


def _f_gdn_chunks(q, k, v, gc, s0):
    outs, s = [], s0
    for n in range(q.shape[0] // CHUNK):
        rs = slice(n * CHUNK, (n + 1) * CHUNK)
        o, s = _f_gdn_chunk(q[rs], k[rs], v[rs], gc[rs], s)
        outs.append(o)
    return jnp.concatenate(outs, axis=0), s


def _f_gdn_post(o, z, nw):
    return (_heads(lambda a, b: _rms(a, nw) * _silu(b), o, z),)


def _f_lru_ab(lx, conv_w, conv_b, wa, ba, wx, bx, lam):
    xc = _causal_conv(lx, conv_w) + conv_b
    r = _sigmoid(_bdot(xc, wa) + ba)
    i = _sigmoid(_bdot(xc, wx) + bx)
    log_a = -LRU_C * r * _softplus(-lam)
    a = jnp.exp(log_a)
    mult = jnp.sqrt(_neg_expm1(2.0 * log_a))
    return a, mult * (i * xc)


def _f_lru_post(hs, gate, gw):
    return (_rms(hs * _gelu(gate), gw),)


def _f_sgu(uv, ln_w, ln_b, ws, bst, gw):
    tm = uv.shape[0]
    uvf = _gelu(uv)
    u, v = uvf[:, 0:DG], uvf[:, DG:2 * DG]
    mu = jnp.mean(v, axis=-1, keepdims=True)
    vc = v - mu
    v = vc * lax.rsqrt(jnp.mean(vc * vc, axis=-1, keepdims=True) + EPS) * ln_w + ln_b
    lane = lax.broadcasted_iota(jnp.int32, bst.shape, 1)
    tril = lax.broadcasted_iota(jnp.int32, (128, 128), 0) >= lax.broadcasted_iota(jnp.int32, (128, 128), 1)
    wsm = [jnp.where(tril, ws[g * 128:(g + 1) * 128, :], 0.0) for g in range(4)]
    bias = [jnp.sum(jnp.where(lane == g, bst, 0.0), axis=1, keepdims=True) for g in range(4)]
    out = []
    for n in range(tm // 128):
        vn = v[n * 128:(n + 1) * 128, :]
        gs = [_bdot(wsm[g], vn[:, g * 128:(g + 1) * 128]) + bias[g] for g in range(4)]
        out.append(jnp.concatenate(gs, axis=1))
    vo = jnp.concatenate(out, axis=0) if len(out) > 1 else out[0]
    return (_rms(u * vo, gw),)


def _f_sconv(sb, sc, sh, conv_w, gw):
    return (_rms(sb * _causal_conv(sc * sh, conv_w), gw),)


def _f_ffn_act(ug, uv, wg, wv, bg, bv):
    return (_gelu(_causal_conv(ug, wg) + bg) * (_causal_conv(uv, wv) + bv),)


def _row_specs(rows, consts, tm, ncol, tile_of):
    specs, args = [], []
    for arr, cb, w, halo in rows:
        specs.append(pl.BlockSpec((tm, w), lambda j, i, cb=cb: (tile_of(i), cb + j)))
        args.append(arr)
        if halo:
            specs.append(pl.BlockSpec((HALO, w), lambda j, i, cb=cb: (jnp.maximum(tile_of(i) * (tm // HALO) - 1, 0), cb + j)))
            args.append(arr)
    for arr, tiled in consts:
        r, c = arr.shape
        specs.append(pl.BlockSpec((r, c // ncol), lambda j, i: (0, j)) if tiled else pl.BlockSpec((r, c), lambda j, i: (0, 0)))
        args.append(arr)
    return specs, args


def _load_rows(refs, rows, consts, tile):
    k, vals = 0, []
    for _arr, _cb, _w, halo in rows:
        t = refs[k][...].astype(F32)
        k += 1
        if halo:
            hl = jnp.where(tile > 0, refs[k][...].astype(F32), 0.0)
            k += 1
            t = jnp.concatenate([hl, t], axis=0)
        vals.append(t)
    for _ in consts:
        vals.append(refs[k][...])
        k += 1
    return vals, k


def _rowwise(fn, rows, consts, outs, *, tm, name, ncol=1, xch=None):
    n = rows[0][0].shape[0]
    nt = n // tm
    in_specs, args = _row_specs(rows, consts, tm, ncol, lambda i: i)

    def body(*refs):
        vals, k = _load_rows(refs, rows, consts, pl.program_id(1))
        for o_ref, r in zip(refs[k:], fn(*vals)):
            o_ref[...] = r.astype(o_ref.dtype)

    res, xres = _call(
        body, grid=(ncol, nt), in_specs=in_specs,
        out_specs=[pl.BlockSpec((tm, w), lambda j, i: (i, j)) for w, _ in outs],
        out_shape=[jax.ShapeDtypeStruct((n, w * ncol), dt) for w, dt in outs], scratch=[], name=name, args=args, xch=xch)
    return res if xch is None else (res, xres)


def _rowwise_vjp(fn, rows, consts, cots, *, tm, name, ncol=1):
    n = rows[0][0].shape[0]
    nt = n // tm
    rows4 = [r[:4] for r in rows]
    consts2 = [c[:2] for c in consts]
    in_specs, args = _row_specs(rows4, consts2, tm, ncol, lambda i: nt - 1 - i)
    for arr, cb, w in cots:
        in_specs.append(pl.BlockSpec((tm, w), lambda j, i, cb=cb: (nt - 1 - i, cb + j)))
        args.append(arr)
    out_specs, out_shape, scratch = [], [], []
    for arr, cb, w, halo, gdt in rows:
        if gdt is not None:
            out_specs.append(pl.BlockSpec((tm, w), lambda j, i: (nt - 1 - i, j)))
            out_shape.append(jax.ShapeDtypeStruct((n, w * ncol), gdt))
            if halo:
                scratch.append(pltpu.VMEM((HALO, w), F32))
    for arr, tiled, want in consts:
        if want:
            r, c = arr.shape
            out_specs.append(pl.BlockSpec((r, c // ncol), lambda j, i: (0, j)) if tiled else pl.BlockSpec((r, c), lambda j, i: (0, 0)))
            out_shape.append(jax.ShapeDtypeStruct((r, c), F32))
    n_in = len(in_specs)
    n_out = len(out_specs)

    def body(*refs):
        j, i = pl.program_id(0), pl.program_id(1)
        tile = nt - 1 - i
        vals, k = _load_rows(refs, rows4, consts2, tile)
        cvals = [refs[k + q][...].astype(F32) for q in range(len(cots))]
        outs = refs[n_in:n_in + n_out]
        carries = refs[n_in + n_out:]
        _, vjp = jax.vjp(fn, *vals)
        g = vjp(tuple(cvals))
        o, cidx = 0, 0
        for q, (arr, cb, w, halo, gdt) in enumerate(rows):
            if gdt is None:
                continue
            if halo:
                ge = g[q]
                main = ge[HALO:]
                carry = carries[cidx]
                cidx += 1
                tail = main[tm - HALO:] + jnp.where(i > 0, carry[...], 0.0)
                outs[o][0:tm - HALO, :] = main[0:tm - HALO].astype(gdt)
                outs[o][tm - HALO:tm, :] = tail.astype(gdt)
                carry[...] = ge[0:HALO]
            else:
                outs[o][...] = g[q].astype(gdt)
            o += 1
        for q, (arr, tiled, want) in enumerate(consts):
            if not want:
                continue
            first = (i == 0) if tiled else jnp.logical_and(i == 0, j == 0)
            acc = outs[o]
            gq = g[len(rows) + q].astype(F32)

            @pl.when(first)
            def _(acc=acc, gq=gq):
                acc[...] = gq

            @pl.when(jnp.logical_not(first))
            def _(acc=acc, gq=gq):
                acc[...] += gq

            o += 1

    res = pl.pallas_call(
        body, grid=(ncol, nt), in_specs=in_specs, out_specs=out_specs, out_shape=out_shape, scratch_shapes=scratch,
        name=name, compiler_params=pltpu.CompilerParams(dimension_semantics=("arbitrary", "arbitrary")),
    )(*args)
    nrow = sum(1 for r in rows if r[4] is not None)
    return list(res[:nrow]), list(res[nrow:])


_ANY = pl.BlockSpec(memory_space=pl.ANY)
_CHIP_REL = ((1, 0), (0, 1), (1, 1))
_DEV_REL = tuple((r >> 2 & 1, r >> 1 & 1, r & 1) for r in range(1, 8))


def _flip(v, f):
    return 1 - v if f else v


class _GatherChips:
    def __init__(self, shards):
        self.arrs = list(shards)
        n = len(self.arrs)
        self.out_shape = [jax.ShapeDtypeStruct((4,) + s.shape, s.dtype) for s in self.arrs]
        self.sems = [pltpu.SemaphoreType.DMA((3 * n,)), pltpu.SemaphoreType.DMA((3 * n,)), pltpu.SemaphoreType.DMA((n,))]

    def _copies(self, ins, outs, sems, arriving):
        send, recv, lsem = sems
        x, y, c = lax.axis_index("x"), lax.axis_index("y"), lax.axis_index("c")
        me = 2 * x + y
        if arriving:
            local = []
        else:
            local = [pltpu.make_async_copy(ins[a], outs[a].at[me], lsem.at[a]) for a in range(len(ins))]
        remote = []
        for a in range(len(ins)):
            for k, (fx, fy) in enumerate(_CHIP_REL):
                px, py = _flip(x, fx), _flip(y, fy)
                remote.append(pltpu.make_async_remote_copy(
                    src_ref=ins[a], dst_ref=outs[a].at[2 * px + py if arriving else me], send_sem=send.at[3 * a + k],
                    recv_sem=recv.at[3 * a + k], device_id=(px, py, c), device_id_type=MESH_ID))
        return local, remote

    def issue(self, ins, outs, sems):
        local, push = self._copies(ins, outs, sems, False)
        for cp in local + push:
            cp.start()

    def finish(self, ins, outs, sems):
        for cp in self._copies(ins, outs, sems, True)[1]:
            cp.wait_recv()
        local, push = self._copies(ins, outs, sems, False)
        for cp in push:
            cp.wait_send()
        for cp in local:
            cp.wait()


class _ScatterChips:
    def __init__(self, blocks):
        self.arrs = list(blocks)
        n = len(self.arrs)
        self.out_shape = [jax.ShapeDtypeStruct((3,) + b.shape[1:], b.dtype) for b in self.arrs]
        self.sems = [pltpu.SemaphoreType.DMA((3 * n,)), pltpu.SemaphoreType.DMA((3 * n,))]

    def _copies(self, ins, outs, sems):
        send, recv = sems
        x, y, c = lax.axis_index("x"), lax.axis_index("y"), lax.axis_index("c")
        copies = []
        for a in range(len(ins)):
            for k, (fx, fy) in enumerate(_CHIP_REL):
                px, py = _flip(x, fx), _flip(y, fy)
                copies.append(pltpu.make_async_remote_copy(
                    src_ref=ins[a].at[2 * px + py], dst_ref=outs[a].at[k], send_sem=send.at[3 * a + k], recv_sem=recv.at[3 * a + k],
                    device_id=(px, py, c), device_id_type=MESH_ID))
        return copies

    def issue(self, ins, outs, sems):
        for cp in self._copies(ins, outs, sems):
            cp.start()

    def finish(self, ins, outs, sems):
        copies = self._copies(ins, outs, sems)
        for cp in copies:
            cp.wait_recv()
        for cp in copies:
            cp.wait_send()


def _exchange(xch, *, name):
    ni, no = len(xch.arrs), len(xch.out_shape)

    def body(*refs):
        ins, outs, sems = refs[:ni], refs[ni:ni + no], refs[ni + no:]
        xch.issue(ins, outs, sems)
        xch.finish(ins, outs, sems)

    return pl.pallas_call(body, in_specs=[_ANY] * ni, out_specs=[_ANY] * no, out_shape=xch.out_shape, scratch_shapes=xch.sems, name=name)(*xch.arrs)


def _call(body, *, grid, in_specs, out_specs, out_shape, scratch, name, args, xch=None):
    params = pltpu.CompilerParams(dimension_semantics=("arbitrary",) * len(grid))
    if xch is None:
        res = pl.pallas_call(body, grid=grid, in_specs=in_specs, out_specs=out_specs, out_shape=out_shape, scratch_shapes=scratch,
                             name=name, compiler_params=params)(*args)
        return list(res), []
    n_in, n_out, n_sc = len(in_specs), len(out_specs), len(scratch)
    xi, xo = len(xch.arrs), len(xch.out_shape)

    def wrapped(*refs):
        ins, refs = refs[:n_in], refs[n_in:]
        xin, refs = refs[:xi], refs[xi:]
        outs, refs = refs[:n_out], refs[n_out:]
        xout, refs = refs[:xo], refs[xo:]
        sc, sems = refs[:n_sc], refs[n_sc:]
        first = functools.reduce(jnp.logical_and, [pl.program_id(d) == 0 for d in range(len(grid))])
        last = functools.reduce(jnp.logical_and, [pl.program_id(d) == grid[d] - 1 for d in range(len(grid))])

        @pl.when(first)
        def _():
            xch.issue(xin, xout, sems)

        body(*ins, *outs, *sc)

        @pl.when(last)
        def _():
            xch.finish(xin, xout, sems)

    res = pl.pallas_call(
        wrapped, grid=grid, in_specs=list(in_specs) + [_ANY] * xi, out_specs=list(out_specs) + [_ANY] * xo,
        out_shape=list(out_shape) + xch.out_shape, scratch_shapes=list(scratch) + xch.sems, name=name, compiler_params=params,
    )(*args, *xch.arrs)
    return list(res[:n_out]), list(res[n_out:])


def _matmul(a, b, mode, out_dtype, *, tm, tn, tk, name, b_blocks=False, out_blocks=False, xch=None):
    if b_blocks:
        _, br, bc4 = b.shape
        b2 = (br, 4 * bc4)
    else:
        b2 = b.shape

    def bspec(shape, index):
        if not b_blocks:
            return pl.BlockSpec(shape, index)
        per = bc4 // shape[1]

        def blocked(i, j, k):
            r, c = index(i, j, k)
            return (c // per, r, c % per)

        return pl.BlockSpec((None,) + shape, blocked)

    tm = min(tm, a.shape[1] if mode == "tn" else a.shape[0])
    tk = min(tk, a.shape[0] if mode == "tn" else a.shape[1])
    if mode == "tn":
        K, M = a.shape
        N = b2[1]
        a_spec = pl.BlockSpec((tk, tm), lambda i, j, k: (k, i))
        b_spec = bspec((tk, tn), lambda i, j, k: (k, j))
        dims = (((0,), (0,)), ((), ()))
    elif mode == "nt":
        M, K = a.shape
        N = b2[0]
        a_spec = pl.BlockSpec((tm, tk), lambda i, j, k: (i, k))
        b_spec = bspec((tn, tk), lambda i, j, k: (j, k))
        dims = (((1,), (1,)), ((), ()))
    else:
        M, K = a.shape
        N = b2[1]
        a_spec = pl.BlockSpec((tm, tk), lambda i, j, k: (i, k))
        b_spec = bspec((tk, tn), lambda i, j, k: (k, j))
        dims = (((1,), (0,)), ((), ()))
    assert M % tm == 0 and N % tn == 0 and K % tk == 0, (name, M, N, K, tm, tn, tk)
    nk = K // tk
    if out_blocks:
        per_o = (N // 4) // tn
        o_spec = pl.BlockSpec((None, tm, tn), lambda i, j, k: (j // per_o, i, j % per_o))
        o_shape = jax.ShapeDtypeStruct((4, M, N // 4), out_dtype)
    else:
        o_spec = pl.BlockSpec((tm, tn), lambda i, j, k: (i, j))
        o_shape = jax.ShapeDtypeStruct((M, N), out_dtype)

    def body(a_ref, b_ref, o_ref, acc_ref):
        k = pl.program_id(2)
        part = lax.dot_general(a_ref[...].astype(BF), b_ref[...].astype(BF), dims, preferred_element_type=F32)
        if nk == 1:
            o_ref[...] = part.astype(o_ref.dtype)
        else:
            @pl.when(k == 0)
            def _():
                acc_ref[...] = part

            @pl.when(k > 0)
            def _():
                acc_ref[...] += part

            @pl.when(k == nk - 1)
            def _():
                o_ref[...] = acc_ref[...].astype(o_ref.dtype)

    res, xres = _call(body, grid=(M // tm, N // tn, nk), in_specs=[a_spec, b_spec], out_specs=[o_spec], out_shape=[o_shape],
                      scratch=[pltpu.VMEM((tm, tn) if nk > 1 else (8, 128), F32)], name=name, args=(a, b), xch=xch)
    return res[0], xres


def _scan_fwd(a, b, *, tm, name):
    n, c = a.shape
    nt = n // tm

    def body(a_ref, b_ref, h_ref, carry):
        i = pl.program_id(0)
        av, bv = a_ref[...], b_ref[...]
        row = _rows(av)
        d = 1
        while d < tm:
            a_s = jnp.where(row >= d, pltpu.roll(av, d, axis=0), 1.0)
            b_s = jnp.where(row >= d, pltpu.roll(bv, d, axis=0), 0.0)
            bv = av * b_s + bv
            av = av * a_s
            d *= 2
        h = bv + av * jnp.where(i > 0, carry[HALO - 1:HALO, :], 0.0)
        h_ref[...] = h
        carry[...] = h[tm - HALO:tm]

    return pl.pallas_call(
        body, grid=(nt,), in_specs=[pl.BlockSpec((tm, c), lambda i: (i, 0))] * 2,
        out_specs=pl.BlockSpec((tm, c), lambda i: (i, 0)), out_shape=jax.ShapeDtypeStruct((n, c), F32),
        scratch_shapes=[pltpu.VMEM((HALO, c), F32)], name=name,
        compiler_params=pltpu.CompilerParams(dimension_semantics=("arbitrary",)),
    )(a, b)


def _scan_bwd(a, h, dh, *, tm, name):
    n, c = a.shape
    nt = n // tm
    tb = tm // HALO

    def body(a_ref, an_ref, h_ref, hp_ref, dh_ref, da_ref, db_ref, carry):
        i = pl.program_id(0)
        tile = nt - 1 - i
        av, hv, g = a_ref[...], h_ref[...], dh_ref[...]
        row = _rows(av)
        a_next = jnp.where(tile < nt - 1, an_ref[0:1, :], 0.0)
        au = jnp.where(row < tm - 1, pltpu.roll(av, tm - 1, axis=0), a_next)
        d = 1
        while d < tm:
            a_s = jnp.where(row < tm - d, pltpu.roll(au, tm - d, axis=0), 1.0)
            g_s = jnp.where(row < tm - d, pltpu.roll(g, tm - d, axis=0), 0.0)
            g = au * g_s + g
            au = au * a_s
            d *= 2
        g = g + au * jnp.where(i > 0, carry[0:1, :], 0.0)
        h_prev = jnp.where(row >= 1, pltpu.roll(hv, 1, axis=0), jnp.where(tile > 0, hp_ref[HALO - 1:HALO, :], 0.0))
        db_ref[...] = g
        da_ref[...] = g * h_prev
        carry[...] = g[0:HALO]

    cur = pl.BlockSpec((tm, c), lambda i: (nt - 1 - i, 0))
    nxt = pl.BlockSpec((HALO, c), lambda i: (jnp.minimum((nt - i) * tb, n // HALO - 1), 0))
    prv = pl.BlockSpec((HALO, c), lambda i: (jnp.maximum((nt - 1 - i) * tb - 1, 0), 0))
    return pl.pallas_call(
        body, grid=(nt,), in_specs=[cur, nxt, cur, prv, cur], out_specs=[cur, cur],
        out_shape=[jax.ShapeDtypeStruct((n, c), F32)] * 2, scratch_shapes=[pltpu.VMEM((HALO, c), F32)], name=name,
        compiler_params=pltpu.CompilerParams(dimension_semantics=("arbitrary",)),
    )(a, a, h, h, dh)


def _gdn_fwd(q, k, v, gb, *, name, xch=None):
    n = q.shape[0]
    rows = GDN_STEP * CHUNK
    ns = n // rows

    def body(q_ref, k_ref, v_ref, gb_ref, o_ref, s_ref, state):
        @pl.when(pl.program_id(0) == 0)
        def _():
            state[...] = jnp.zeros_like(state)

        s0 = state[...]
        s_ref[0] = s0
        o, s1 = _f_gdn_chunks(q_ref[...], k_ref[...], v_ref[...], gb_ref[...], s0)
        o_ref[...] = o
        state[...] = s1

    row = pl.BlockSpec((rows, DG), lambda i: (i, 0))
    return _call(
        body, grid=(ns,), in_specs=[row, row, row, pl.BlockSpec((rows, 128), lambda i: (i, 0))],
        out_specs=[row, pl.BlockSpec((1, DG, HD), lambda i: (i, 0, 0))],
        out_shape=[jax.ShapeDtypeStruct((n, DG), F32), jax.ShapeDtypeStruct((ns, DG, HD), F32)],
        scratch=[pltpu.VMEM((DG, HD), F32)], name=name, args=(q, k, v, gb), xch=xch)


def _gdn_bwd(q, k, v, gb, s_all, do, *, name, xch=None):
    n = q.shape[0]
    rows = GDN_STEP * CHUNK
    ns = n // rows

    def body(q_ref, k_ref, v_ref, gb_ref, s_ref, do_ref, dq_ref, dk_ref, dv_ref, dgb_ref, dstate):
        @pl.when(pl.program_id(0) == 0)
        def _():
            dstate[...] = jnp.zeros_like(dstate)

        _, vjp = jax.vjp(_f_gdn_chunks, q_ref[...], k_ref[...], v_ref[...], gb_ref[...], s_ref[0])
        dq_ref[...], dk_ref[...], dv_ref[...], dgb_ref[...], dstate[...] = vjp((do_ref[...], dstate[...]))

    row = pl.BlockSpec((rows, DG), lambda i: (ns - 1 - i, 0))
    gsp = pl.BlockSpec((rows, 128), lambda i: (ns - 1 - i, 0))
    return _call(
        body, grid=(ns,), in_specs=[row, row, row, gsp, pl.BlockSpec((1, DG, HD), lambda i: (ns - 1 - i, 0, 0)), row],
        out_specs=[row, row, row, gsp],
        out_shape=[jax.ShapeDtypeStruct((n, DG), F32)] * 3 + [jax.ShapeDtypeStruct((n, 128), F32)],
        scratch=[pltpu.VMEM((DG, HD), F32)], name=name, args=(q, k, v, gb, s_all, do), xch=xch)


def _loss_head(x1, y2, w, tgt, *, tm, name):
    n, c = x1.shape

    def body(x_ref, y_ref, w_ref, t_ref, l_ref, d_ref):
        err = x_ref[...] + _rms(y_ref[...], w_ref[...]) - t_ref[...]
        part = jnp.sum(jnp.sum(err * err, axis=1, keepdims=True), axis=0, keepdims=True) * (0.5 / c)

        @pl.when(pl.program_id(0) == 0)
        def _():
            l_ref[...] = jnp.zeros_like(l_ref)

        l_ref[...] += part
        d_ref[...] = err * (1.0 / c)

    row = pl.BlockSpec((tm, c), lambda i: (i, 0))
    return pl.pallas_call(
        body, grid=(n // tm,), in_specs=[row, row, pl.BlockSpec((1, c), lambda i: (0, 0)), row],
        out_specs=[pl.BlockSpec((8, 128), lambda i: (0, 0)), row],
        out_shape=[jax.ShapeDtypeStruct((8, 128), F32), jax.ShapeDtypeStruct((n, c), F32)], name=name,
        compiler_params=pltpu.CompilerParams(dimension_semantics=("arbitrary",)),
    )(x1, y2, w, tgt)


def _sum_parts(own, recv, *, tr, name):
    r, c = own.shape
    p = recv.shape[0]

    def body(o_ref, r_ref, s_ref):
        s = o_ref[...]
        for q in range(p):
            s = s + r_ref[q].astype(F32)
        s_ref[...] = s

    return pl.pallas_call(
        body, grid=(r // tr,), in_specs=[pl.BlockSpec((tr, c), lambda i: (i, 0)), pl.BlockSpec((p, tr, c), lambda i: (0, i, 0))],
        out_specs=pl.BlockSpec((tr, c), lambda i: (i, 0)), out_shape=jax.ShapeDtypeStruct((r, c), F32), name=name,
        compiler_params=pltpu.CompilerParams(dimension_semantics=("parallel",)),
    )(own, recv)


def _sum_slots(buf, *, name):
    p, r, c = buf.shape

    def body(b_ref, s_ref):
        s = b_ref[0]
        for q in range(1, p):
            s = s + b_ref[q]
        s_ref[...] = s

    return pl.pallas_call(body, out_shape=jax.ShapeDtypeStruct((r, c), F32), name=name)(buf)


def _adamw(w, m, v, gs, *, tr, name, xch=None):
    ngrp, r, c = w.shape
    nterm = len(gs[0])
    per = r // tr
    c1 = 1.0 - ADAM_B1 ** ADAM_STEP
    c2 = 1.0 - ADAM_B2 ** ADAM_STEP

    def body(*refs):
        w_ref, m_ref, v_ref = refs[:3]
        g_refs = refs[3:3 + ngrp * nterm]
        g_ref, d_ref, nm_ref, nv_ref = refs[3 + ngrp * nterm:]
        grp = pl.program_id(0) // per
        g = None
        for q in range(ngrp):
            gq = g_refs[q * nterm][...]
            for t in range(1, nterm):
                gq = gq + g_refs[q * nterm + t][...]
            g = gq if g is None else jnp.where(grp == q, gq, g)
        nm = ADAM_B1 * m_ref[...] + (1.0 - ADAM_B1) * g
        nv = ADAM_B2 * v_ref[...] + (1.0 - ADAM_B2) * (g * g)
        g_ref[...] = g
        d_ref[...] = -ADAM_LR * ((nm / c1) / (jnp.sqrt(nv / c2) + ADAM_EPS) + ADAM_WD * w_ref[...])
        nm_ref[...] = nm
        nv_ref[...] = nv

    blk = pl.BlockSpec((None, tr, c), lambda i: (i // per, i % per, 0))
    g_specs = [pl.BlockSpec((tr, c), lambda i, q=q: (jnp.clip(i - q * per, 0, per - 1), 0)) for q in range(ngrp) for _ in range(nterm)]
    return _call(body, grid=(ngrp * per,), in_specs=[blk] * 3 + g_specs, out_specs=[blk] * 4,
                 out_shape=[jax.ShapeDtypeStruct((ngrp, r, c), F32)] * 4, scratch=[], name=name,
                 args=(w, m, v, *[t for grp in gs for t in grp]), xch=xch)


class _SwapCores:
    def __init__(self, arrs):
        self.arrs = list(arrs)
        n = len(self.arrs)
        self.out_shape = [jax.ShapeDtypeStruct(a.shape, a.dtype) for a in self.arrs]
        self.sems = [pltpu.SemaphoreType.DMA((n,)), pltpu.SemaphoreType.DMA((n,))]

    def _copies(self, ins, outs, sems):
        send, recv = sems
        sib = (lax.axis_index("x"), lax.axis_index("y"), 1 - lax.axis_index("c"))
        return [pltpu.make_async_remote_copy(src_ref=ins[a], dst_ref=outs[a], send_sem=send.at[a], recv_sem=recv.at[a],
                                             device_id=sib, device_id_type=MESH_ID) for a in range(len(ins))]

    def issue(self, ins, outs, sems):
        for cp in self._copies(ins, outs, sems):
            cp.start()

    def finish(self, ins, outs, sems):
        copies = self._copies(ins, outs, sems)
        for cp in copies:
            cp.wait_recv()
        for cp in copies:
            cp.wait_send()


class _GatherDevices:
    def __init__(self, buf):
        self.arrs = [buf]
        self.out_shape = [jax.ShapeDtypeStruct((8,) + buf.shape, buf.dtype)]
        self.sems = [pltpu.SemaphoreType.DMA((7,)), pltpu.SemaphoreType.DMA((7,)), pltpu.SemaphoreType.DMA((1,))]

    def _copies(self, ins, outs, sems, arriving):
        send, recv, lsem = sems
        x, y, c = lax.axis_index("x"), lax.axis_index("y"), lax.axis_index("c")
        me = 4 * x + 2 * y + c
        local = [] if arriving else [pltpu.make_async_copy(ins[0], outs[0].at[me], lsem.at[0])]
        remote = []
        for k, (fx, fy, fc) in enumerate(_DEV_REL):
            px, py, pc = _flip(x, fx), _flip(y, fy), _flip(c, fc)
            remote.append(pltpu.make_async_remote_copy(
                src_ref=ins[0], dst_ref=outs[0].at[4 * px + 2 * py + pc if arriving else me], send_sem=send.at[k], recv_sem=recv.at[k],
                device_id=(px, py, pc), device_id_type=MESH_ID))
        return local, remote

    def issue(self, ins, outs, sems):
        local, push = self._copies(ins, outs, sems, False)
        for cp in local + push:
            cp.start()

    def finish(self, ins, outs, sems):
        for cp in self._copies(ins, outs, sems, True)[1]:
            cp.wait_recv()
        local, push = self._copies(ins, outs, sems, False)
        for cp in push:
            cp.wait_send()
        for cp in local:
            cp.wait()


class _Together:
    def __init__(self, parts):
        self.parts = list(parts)
        self.arrs = [a for p in self.parts for a in p.arrs]
        self.out_shape = [s for p in self.parts for s in p.out_shape]
        self.sems = [s for p in self.parts for s in p.sems]

    def _split(self, ins, outs, sems):
        i = o = s = 0
        for p in self.parts:
            ni, no, ns = len(p.arrs), len(p.out_shape), len(p.sems)
            yield p, ins[i:i + ni], outs[o:o + no], sems[s:s + ns]
            i, o, s = i + ni, o + no, s + ns

    def issue(self, ins, outs, sems):
        for p, a, b, c in self._split(ins, outs, sems):
            p.issue(a, b, c)

    def finish(self, ins, outs, sems):
        for p, a, b, c in self._split(ins, outs, sems):
            p.finish(a, b, c)


def _pad_cols(w):
    z = jnp.zeros(w.shape[:-1] + (NP - N_IN,), w.dtype)
    return jnp.concatenate([w[..., 0:2048], w[..., 2056:N_IN], w[..., 2048:2056], z], axis=-1)


def _unpad_cols(w):
    return jnp.concatenate([w[..., 0:2048], w[..., OFF_BA:OFF_BA + 8], w[..., 2048:OFF_BA]], axis=-1)


def _lanes(v, off):
    return jnp.pad(v.reshape(1, -1), ((0, 0), (off, 128 - off - v.size)))


def _block_diag(w):
    eye = jnp.eye(8, dtype=w.dtype)
    return (w[:, :, None, :] * eye[:, None, :, None]).reshape(DG, DG)


def _diag_blocks(w):
    return jnp.stack([w[h * 64:(h + 1) * 64, h * 64:(h + 1) * 64] for h in range(8)])


def _mixer_params(p):
    return dict(
        gdn_conv_w=p["gdn_conv_w"], alog=_lanes(p["gdn_a_log"], HEADS), dtb=_lanes(p["gdn_dt_bias"], HEADS),
        gdn_nw=p["gdn_norm_w"].reshape(1, HD),
        lru_conv_w=p["lru_conv_w"], lru_conv_b=p["lru_conv_b"].reshape(1, DG),
        wa=_block_diag(p["lru_wa"]), ba=p["lru_ba"].reshape(1, DG), wx=_block_diag(p["lru_wx"]), bx=p["lru_bx"].reshape(1, DG),
        lam=p["lru_lambda"].reshape(1, DG),
        ln_w=p["sgu_ln_w"].reshape(1, DG), ln_b=p["sgu_ln_b"].reshape(1, DG), ws=p["sgu_ws"].reshape(DG, 128),
        bst=jnp.pad(p["sgu_b"].T, ((0, 0), (0, 124))),
        sconv_w=p["sconv_w"], gw0=p["grp_norm_w"][0:1], gw1=p["grp_norm_w"][1:2], gw2=p["grp_norm_w"][2:3],
    )


def _mixer_param_grads(g):
    return dict(
        gdn_conv_w=g["gdn_conv_w"], gdn_a_log=g["alog"][0, HEADS:2 * HEADS], gdn_dt_bias=g["dtb"][0, HEADS:2 * HEADS],
        gdn_norm_w=g["gdn_nw"][0],
        lru_conv_w=g["lru_conv_w"], lru_conv_b=g["lru_conv_b"][0],
        lru_wa=_diag_blocks(g["wa"]), lru_ba=g["ba"].reshape(8, 64), lru_wx=_diag_blocks(g["wx"]), lru_bx=g["bx"].reshape(8, 64),
        lru_lambda=g["lam"][0],
        sgu_ln_w=g["ln_w"][0], sgu_ln_b=g["ln_b"][0], sgu_ws=g["ws"].reshape(4, 128, 128), sgu_b=g["bst"][:, 0:4].T,
        sconv_w=g["sconv_w"], grp_norm_w=jnp.concatenate([g["gw0"], g["gw1"], g["gw2"]], axis=0),
    )


TM_MIX = 256


def _mixers_fwd(p, mp, tag="", xch=None):
    c = lambda *names: [(mp[n], False) for n in names]
    q, k, v, gb = _rowwise(_f_gdn_pre, [(p, 0, 1536, True), (p, OFF_BA // 128, 128, False)], c("gdn_conv_w", "alog", "dtb"),
                           [(DG, F32)] * 3 + [(128, F32)], tm=TM_MIX, name="gdn_pre" + tag)
    (o, s_all), xres = _gdn_fwd(q, k, v, gb, name="gdn_chunks" + tag, xch=xch)
    y_a, = _rowwise(_f_gdn_post, [(o, 0, DG, False), (p, OFF_Z // DG, DG, False)], c("gdn_nw"), [(DG, BF)], tm=TM_MIX, name="gdn_post" + tag)
    a, b = _rowwise(_f_lru_ab, [(p, OFF_LX // DG, DG, True)], c("lru_conv_w", "lru_conv_b", "wa", "ba", "wx", "bx", "lam"),
                    [(DG, F32)] * 2, tm=TM_MIX, name="lru_ab" + tag)
    hs = _scan_fwd(a, b, tm=TM_MIX, name="lru_scan" + tag)
    y_b, = _rowwise(_f_lru_post, [(hs, 0, DG, False), (p, OFF_LG // DG, DG, False)], c("gw0"), [(DG, BF)], tm=TM_MIX, name="lru_post" + tag)
    y_c, = _rowwise(_f_sgu, [(p, OFF_UV // 1024, 1024, False)], c("ln_w", "ln_b", "ws", "bst", "gw1"), [(DG, BF)], tm=TM_MIX, name="sgu" + tag)
    y_d, = _rowwise(_f_sconv, [(p, OFF_SB // DG, DG, False), (p, OFF_SC // DG, DG, True), (p, OFF_SH // DG, DG, True)],
                    c("sconv_w", "gw2"), [(DG, BF)], tm=TM_MIX, name="sconv" + tag)
    return jnp.concatenate([y_a, y_b, y_c, y_d], axis=1), (q, k, v, gb, o, s_all, a, hs), xres


def _mixers_bwd(p, mp, saved, dy, tag="", xch=None):
    q, k, v, gb, o, s_all, a, hs = saved
    c = lambda *names: [(mp[n], False, True) for n in names]
    g = {}

    (do, dz), (g["gdn_nw"],) = _rowwise_vjp(
        _f_gdn_post, [(o, 0, DG, False, F32), (p, OFF_Z // DG, DG, False, BF)], c("gdn_nw"), [(dy, 0, DG)], tm=TM_MIX, name="gdn_post_b" + tag)
    (dq, dk, dv, dgb), xres = _gdn_bwd(q, k, v, gb, s_all, do, name="gdn_chunks_b" + tag, xch=xch)
    (dqkv, dba), (g["gdn_conv_w"], g["alog"], g["dtb"]) = _rowwise_vjp(
        _f_gdn_pre, [(p, 0, 1536, True, BF), (p, OFF_BA // 128, 128, False, BF)], c("gdn_conv_w", "alog", "dtb"),
        [(dq, 0, DG), (dk, 0, DG), (dv, 0, DG), (dgb, 0, 128)], tm=TM_MIX, name="gdn_pre_b" + tag)

    (dhs, dgate), (g["gw0"],) = _rowwise_vjp(
        _f_lru_post, [(hs, 0, DG, False, F32), (p, OFF_LG // DG, DG, False, BF)], c("gw0"), [(dy, 1, DG)], tm=TM_MIX, name="lru_post_b" + tag)
    da, db = _scan_bwd(a, hs, dhs, tm=TM_MIX, name="lru_scan_b" + tag)
    (dlx,), (g["lru_conv_w"], g["lru_conv_b"], g["wa"], g["ba"], g["wx"], g["bx"], g["lam"]) = _rowwise_vjp(
        _f_lru_ab, [(p, OFF_LX // DG, DG, True, BF)], c("lru_conv_w", "lru_conv_b", "wa", "ba", "wx", "bx", "lam"),
        [(da, 0, DG), (db, 0, DG)], tm=TM_MIX, name="lru_ab_b" + tag)

    (duv,), (g["ln_w"], g["ln_b"], g["ws"], g["bst"], g["gw1"]) = _rowwise_vjp(
        _f_sgu, [(p, OFF_UV // 1024, 1024, False, BF)], c("ln_w", "ln_b", "ws", "bst", "gw1"), [(dy, 2, DG)], tm=TM_MIX, name="sgu_b" + tag)

    (dsb, dsc, dsh), (g["sconv_w"], g["gw2"]) = _rowwise_vjp(
        _f_sconv, [(p, OFF_SB // DG, DG, False, BF), (p, OFF_SC // DG, DG, True, BF), (p, OFF_SH // DG, DG, True, BF)],
        c("sconv_w", "gw2"), [(dy, 3, DG)], tm=TM_MIX, name="sconv_b" + tag)

    dp = jnp.concatenate([dqkv, dz, dlx, dgate, duv, dsb, dsc, dsh, dba], axis=1)
    return dp, g, xres


WEIGHTS = ("pre_mix_norm", "w_in", "gdn_conv_w", "gdn_a_log", "gdn_dt_bias", "gdn_norm_w", "lru_conv_w", "lru_conv_b", "lru_wa",
           "lru_ba", "lru_wx", "lru_bx", "lru_lambda", "sgu_ln_w", "sgu_ln_b", "sgu_ws", "sgu_b", "sconv_w", "grp_norm_w", "w_out",
           "post_mix_norm", "pre_ffn_norm", "ffn_up", "ffn_conv_w", "ffn_conv_b", "ffn_down", "post_ffn_norm")
BIG = ("w_in", "ffn_up", "w_out", "ffn_down")
CHIP_SHARDED_SMALL = ("gdn_conv_w", "lru_conv_w", "sconv_w", "grp_norm_w", "ffn_conv_w")
MIXER_PARAMS = ("gdn_conv_w", "gdn_a_log", "gdn_dt_bias", "gdn_norm_w", "lru_conv_w", "lru_conv_b", "lru_wa", "lru_ba", "lru_wx",
                "lru_bx", "lru_lambda", "sgu_ln_w", "sgu_ln_b", "sgu_ws", "sgu_b", "sconv_w", "grp_norm_w")
TM_ROW = 256
N_FF_TILES = 11
FF_TILE = D_FF // N_FF_TILES
PACK_ROWS = 256
FFN_UP_CUTS = (688, 1712)
TK_DW = 4096


def _pack(arrs):
    parts = []
    for a in arrs:
        n = a.size
        parts.append(jnp.pad(a.reshape(-1), (0, -n % 1024)).reshape(-1, 128))
    rows = sum(p.shape[0] for p in parts)
    parts.append(jnp.zeros((-rows % PACK_ROWS, 128), F32))
    return jnp.concatenate(parts, axis=0)


def _unpack(buf, shapes):
    out, row = [], 0
    for s in shapes:
        n = math.prod(s)
        rows = -(-n // 1024) * 8
        out.append(buf[row:row + rows].reshape(-1)[:n].reshape(s))
        row += rows
    return out


def _row_tile(rows, cols):
    return 256 if rows % 256 == 0 and cols <= 1024 else 128


def _w_in_full(got):
    c = N_IN // 4
    z = jnp.zeros((D, NP - N_IN), got.dtype)
    return jnp.concatenate([got[0], got[1][:, :2048 - c], got[1][:, 2056 - c:], got[2], got[3], got[1][:, 2048 - c:2056 - c], z], axis=1)


def _w_in_blocks(dw):
    c = N_IN // 4
    cut = 2048 + 2 * c - 2056
    b1 = jnp.concatenate([dw[:, c:2048], dw[:, OFF_BA:OFF_BA + 8], dw[:, 2048:cut]], axis=1)
    return jnp.stack([dw[:, 0:c], b1, dw[:, cut:cut + c], dw[:, cut + c:OFF_BA]])


def _layer_fwd(l, xs, h, w_in_l, w_out_g, shard, sp, mp, nxt):
    t = str(l)
    up = shard["ffn_up"][l]
    first = [up[:FFN_UP_CUTS[0]]] + ([shard["w_out"][l]] if w_out_g is None else [])
    p, got = _matmul(h, w_in_l, "nn", F32, tm=1024, tn=1920, tk=D, name="mm_in" + t, xch=_GatherChips(first))
    w_out_l = (got[1] if w_out_g is None else w_out_g).reshape(D, D)
    ycat, saved, (w_up_b,) = _mixers_fwd(p, mp, tag=t, xch=_GatherChips([up[FFN_UP_CUTS[0]:FFN_UP_CUTS[1]]]))
    y, (w_up_c,) = _matmul(ycat, w_out_l, "nn", F32, tm=1024, tn=1024, tk=D, name="mm_out" + t, xch=_GatherChips([up[FFN_UP_CUTS[1]:]]))
    w_up = jnp.concatenate([got[0], w_up_b, w_up_c], axis=1)
    x1, h2 = _rowwise(_f_post_pre, [(xs, 0, D, False), (y, 0, D, False)], [(sp["post_mix_norm"][l], False), (sp["pre_ffn_norm"][l], False)],
                      [(D, F32), (D, BF)], tm=TM_ROW, name="post_mix" + t)
    u, (w_dn_g,) = _matmul(h2, w_up, "nn", F32, tm=1024, tn=1408, tk=D, name="mm_up" + t, b_blocks=True,
                           xch=_GatherChips([shard["ffn_down"][l]]))
    ffn_rows = [(u, 0, FF_TILE, True), (u, N_FF_TILES, FF_TILE, True)]
    ffn_consts = [(c, True) for c in sp["ffn_conv"][l]]
    if nxt is None:
        act, = _rowwise(_f_ffn_act, ffn_rows, ffn_consts, [(FF_TILE, BF)], tm=TM_ROW, ncol=N_FF_TILES, name="ffn_act" + t)
        n_out = None
    else:
        (act,), (n_out,) = _rowwise(_f_ffn_act, ffn_rows, ffn_consts, [(FF_TILE, BF)], tm=TM_ROW, ncol=N_FF_TILES, name="ffn_act" + t,
                                    xch=_GatherChips([nxt[1]]))
    w_dn_l = w_dn_g.reshape(D_FF, D)
    y2, n_in = _matmul(act, w_dn_l, "nn", F32, tm=1024, tn=1024, tk=2816, name="mm_down" + t,
                       xch=None if nxt is None else _GatherChips([nxt[0]]))
    keep = dict(xs=xs, h=h, p=p, saved=saved, ycat=ycat, y=y, x1=x1, h2=h2, u=u, act=act, y2=y2,
                w_in=w_in_l, w_out=w_out_l, w_up=w_up, w_dn=w_dn_l)
    return keep, (None if nxt is None else (n_in[0], n_out))


def _layer_bwd(l, a, dx1, dy2, sp, mp):
    t = str(l)
    g = {}
    dact, _ = _matmul(dy2, a["w_dn"], "nt", F32, tm=1024, tn=1408, tk=D, name="mm_down_dx" + t)
    dw_dn, _ = _matmul(a["act"], dy2, "tn", F32, tm=512, tn=1024, tk=TK_DW, name="mm_down_dw" + t)
    dw_dn = dw_dn.reshape(4, D_FF // 4, D)
    (dug, duv), gc = _rowwise_vjp(
        _f_ffn_act, [(a["u"], 0, FF_TILE, True, BF), (a["u"], N_FF_TILES, FF_TILE, True, BF)], [(c, True, True) for c in sp["ffn_conv"][l]],
        [(dact, 0, FF_TILE)], tm=TM_ROW, ncol=N_FF_TILES, name="ffn_act_b" + t)
    g["ffn_conv_w"] = jnp.concatenate([gc[0], gc[1]], axis=1)
    g["ffn_conv_b"] = jnp.concatenate([gc[2], gc[3]], axis=1)[0]
    du = jnp.concatenate([dug, duv], axis=1)
    dh2, (r_dn,) = _matmul(du, a["w_up"], "nt", F32, tm=1024, tn=1024, tk=2816, name="mm_up_dx" + t, b_blocks=True,
                           xch=_ScatterChips([dw_dn.astype(BF)]))
    dw_up, _ = _matmul(a["h2"], du, "tn", F32, tm=512, tn=1408, tk=TK_DW, name="mm_up_dw" + t, out_blocks=True)
    (dxs, dy), (gpm, gpf) = _rowwise_vjp(
        _f_post_pre, [(a["xs"], 0, D, False, F32), (a["y"], 0, D, False, BF)],
        [(sp["post_mix_norm"][l], False, True), (sp["pre_ffn_norm"][l], False, True)], [(dx1, 0, D), (dh2, 0, D)], tm=TM_ROW, name="post_mix_b" + t)
    g["post_mix_norm"], g["pre_ffn_norm"] = gpm[0], gpf[0]
    dycat, _ = _matmul(dy, a["w_out"], "nt", F32, tm=1024, tn=1024, tk=D, name="mm_out_dx" + t)
    dw_out, _ = _matmul(a["ycat"], dy, "tn", F32, tm=512, tn=1024, tk=TK_DW, name="mm_out_dw" + t)
    dw_out = dw_out.reshape(4, D // 4, D)
    dp, gm, (r_up,) = _mixers_bwd(a["p"], mp, a["saved"], dycat, tag=t, xch=_ScatterChips([dw_up.astype(BF)]))
    g.update(_mixer_param_grads(gm))
    dw_in, (r_out,) = _matmul(a["h"], dp, "tn", F32, tm=512, tn=1920, tk=TK_DW, name="mm_in_dw" + t, xch=_ScatterChips([dw_out.astype(BF)]))
    dw_in = _w_in_blocks(dw_in)
    dh, (r_in,) = _matmul(dp, a["w_in"], "nt", F32, tm=512, tn=1024, tk=NP, name="mm_in_dx" + t, xch=_ScatterChips([dw_in.astype(BF)]))
    big = {"ffn_down": (dw_dn, r_dn), "ffn_up": (dw_up, r_up), "w_out": (dw_out, r_out), "w_in": (dw_in, r_in)}
    return dxs, dh, g, big


def kernel(x, pre_mix_norm, w_in, gdn_conv_w, gdn_a_log, gdn_dt_bias, gdn_norm_w, lru_conv_w, lru_conv_b, lru_wa, lru_ba, lru_wx, lru_bx, lru_lambda, sgu_ln_w, sgu_ln_b, sgu_ws, sgu_b, sconv_w, grp_norm_w, w_out, post_mix_norm, pre_ffn_norm, ffn_up, ffn_conv_w, ffn_conv_b, ffn_down, post_ffn_norm, loss_target, m_pre_mix_norm, m_w_in, m_gdn_conv_w, m_gdn_a_log, m_gdn_dt_bias, m_gdn_norm_w, m_lru_conv_w, m_lru_conv_b, m_lru_wa, m_lru_ba, m_lru_wx, m_lru_bx, m_lru_lambda, m_sgu_ln_w, m_sgu_ln_b, m_sgu_ws, m_sgu_b, m_sconv_w, m_grp_norm_w, m_w_out, m_post_mix_norm, m_pre_ffn_norm, m_ffn_up, m_ffn_conv_w, m_ffn_conv_b, m_ffn_down, m_post_ffn_norm, v_pre_mix_norm, v_w_in, v_gdn_conv_w, v_gdn_a_log, v_gdn_dt_bias, v_gdn_norm_w, v_lru_conv_w, v_lru_conv_b, v_lru_wa, v_lru_ba, v_lru_wx, v_lru_bx, v_lru_lambda, v_sgu_ln_w, v_sgu_ln_b, v_sgu_ws, v_sgu_b, v_sconv_w, v_grp_norm_w, v_w_out, v_post_mix_norm, v_pre_ffn_norm, v_ffn_up, v_ffn_conv_w, v_ffn_conv_b, v_ffn_down, v_post_ffn_norm):
    given = dict(locals())
    me = 2 * lax.axis_index("x") + lax.axis_index("y")
    xs0, tgt = x[0], loss_target[0]

    small_sh = [given[n] for n in CHIP_SHARDED_SMALL]
    shard = {n: [given[n][l].astype(BF) for l in range(DEPTH)] for n in BIG}
    got = _exchange(_GatherChips([shard["w_in"][0], _pack(small_sh)]), name="gather_first")
    full = {n: given[n] for n in WEIGHTS if n not in BIG and n not in CHIP_SHARDED_SMALL}
    per_chip = [_unpack(got[1][j], [s.shape for s in small_sh]) for j in range(4)]
    parts = [jnp.stack([per_chip[j][i] for j in range(4)]) for i in range(len(small_sh))]
    for n, pj in zip(CHIP_SHARDED_SMALL, parts):
        full[n] = pj.transpose(1, 2, 0, 3).reshape(pj.shape[1], pj.shape[2], 4 * pj.shape[3])
    sp = {n: [full[n][l:l + 1] for l in range(DEPTH)] for n in ("pre_mix_norm", "post_mix_norm", "pre_ffn_norm", "post_ffn_norm")}
    sp["ffn_conv"] = [[full["ffn_conv_w"][l][:, :D_FF], full["ffn_conv_w"][l][:, D_FF:], full["ffn_conv_b"][l:l + 1, :D_FF],
                       full["ffn_conv_b"][l:l + 1, D_FF:]] for l in range(DEPTH)]
    mps = [_mixer_params({n: full[n][l] for n in MIXER_PARAMS}) for l in range(DEPTH)]

    h, = _rowwise(_f_pre, [(xs0, 0, D, False)], [(sp["pre_mix_norm"][0], False)], [(D, BF)], tm=TM_ROW, name="pre_mix0")
    a0, (w_in1, w_out1) = _layer_fwd(0, xs0, h, _w_in_full(got[0]), None, shard, sp, mps[0], (shard["w_in"][1], shard["w_out"][1]))
    xs1, h1 = _rowwise(_f_post_pre, [(a0["x1"], 0, D, False), (a0["y2"], 0, D, False)],
                       [(sp["post_ffn_norm"][0], False), (sp["pre_mix_norm"][1], False)], [(D, F32), (D, BF)], tm=TM_ROW, name="post_ffn0")
    a1, _ = _layer_fwd(1, xs1, h1, _w_in_full(w_in1), w_out1, shard, sp, mps[1], None)
    lacc, dxo = _loss_head(a1["x1"], a1["y2"], sp["post_ffn_norm"][1], tgt, tm=TM_ROW, name="loss_head")

    gl = [None, None]
    (dx1, dy2), (gpf1,) = _rowwise_vjp(_f_post, [(a1["x1"], 0, D, False, F32), (a1["y2"], 0, D, False, BF)],
                                      [(sp["post_ffn_norm"][1], False, True)], [(dxo, 0, D)], tm=TM_ROW, name="post_ffn1_b")
    big = [None, None]
    dxs1, dh1, gl[1], big[1] = _layer_bwd(1, a1, dx1, dy2, sp, mps[1])
    gl[1]["post_ffn_norm"] = gpf1[0]
    (dx1, dy2), (gpf0, gpm1) = _rowwise_vjp(
        _f_post_pre, [(a0["x1"], 0, D, False, F32), (a0["y2"], 0, D, False, BF)],
        [(sp["post_ffn_norm"][0], False, True), (sp["pre_mix_norm"][1], False, True)], [(dxs1, 0, D), (dh1, 0, D)], tm=TM_ROW, name="post_ffn0_b")
    gl[1]["pre_mix_norm"] = gpm1[0]
    dxs0, dh0, gl[0], big[0] = _layer_bwd(0, a0, dx1, dy2, sp, mps[0])
    gl[0]["post_ffn_norm"] = gpf0[0]
    (grad_x,), (gpm0,) = _rowwise_vjp(lambda xv, w: (xv, _rms(xv, w)), [(xs0, 0, D, False, F32)], [(sp["pre_mix_norm"][0], False, True)],
                                     [(dxs0, 0, D), (dh0, 0, D)], tm=TM_ROW, name="pre_mix0_b")
    gl[0]["pre_mix_norm"] = gpm0[0]

    small = [n for n in WEIGHTS if n not in BIG]
    gfull = {n: jnp.stack([gl[0][n], gl[1][n]]) for n in small}
    small_grads = _pack([gfull[n] for n in small] + [lacc[0, 0:1]])
    sums = {}
    for n in BIG:
        for l in range(DEPTH):
            blocks, recv = big[l][n]
            own = lax.dynamic_index_in_dim(blocks, me, 0, keepdims=False)
            sums[n] = sums.get(n, []) + [_sum_parts(own, recv, tr=_row_tile(*own.shape), name="sum_grads_%s%d" % (n, l))]
    kinds = ("grad", "delta", "new_m", "new_v")
    outs = {kind: {} for kind in kinds}
    order = ("w_out", "ffn_down", "w_in", "ffn_up")
    other = _exchange(_SwapCores(sums[order[0]]), name="swap_first")
    gathered = None
    for i, n in enumerate(order):
        riders = []
        if i + 1 < len(order):
            riders.append(_SwapCores(sums[order[i + 1]]))
        if n == "ffn_down":
            riders.append(_GatherDevices(small_grads))
        shp = given[n].shape
        r4, got = _adamw(given[n], given["m_" + n], given["v_" + n], [[s, o] for s, o in zip(sums[n], other)],
                         tr=_row_tile(shp[1], shp[2]), name="adamw_" + n, xch=_Together(riders) if riders else None)
        outs_n = dict(zip(kinds, r4))
        for kind in kinds:
            outs[kind][n] = outs_n[kind]
        if i + 1 < len(order):
            other = got[:DEPTH]
        if n == "ffn_down":
            gathered = got[DEPTH]

    tot = _sum_slots(gathered, name="sum_small_grads")
    red = dict(zip(small + ["loss"], _unpack(tot, [gfull[n].shape for n in small] + [(1,)])))
    for n in CHIP_SHARDED_SMALL:
        cb = given[n].shape[-1]
        red[n] = lax.dynamic_slice_in_dim(red[n], me * cb, cb, axis=red[n].ndim - 1)
    shapes = [given[n].shape for n in small]
    res, _ = _adamw(_pack([given[n] for n in small])[None], _pack([given["m_" + n] for n in small])[None],
                    _pack([given["v_" + n] for n in small])[None], [[_pack([red[n] for n in small])]], tr=PACK_ROWS, name="adamw_small")
    for kind, r in zip(kinds, res):
        outs[kind].update(zip(small, _unpack(r[0], shapes)))

    return (red["loss"][0], grad_x[None], *[outs[k][n] for k in ("grad", "delta", "new_m", "new_v") for n in WEIGHTS])
```

```python
import functools
import math

import jax
import jax.numpy as jnp
from jax import lax
from jax.experimental import pallas as pl
from jax.experimental.pallas import tpu as pltpu

F32 = jnp.float32
BF = jnp.bfloat16
MESH_ID = pl.DeviceIdType.MESH

EPS = 1e-6
DEPTH = 2
D = 2048
DG = 512
HEADS = 4
HD = 128
CHUNK = 64
GDN_STEP = 4
LRU_C = 8.0
D_FF = 5632
N_IN = 5640
NP = 5760
OFF_Q, OFF_Z, OFF_LX, OFF_LG, OFF_UV, OFF_SB, OFF_SC, OFF_SH, OFF_BA = 0, 1536, 2048, 2560, 3072, 4096, 4608, 5120, 5632

ADAM_LR, ADAM_B1, ADAM_B2, ADAM_EPS, ADAM_WD, ADAM_STEP = 0.001, 0.9, 0.999, 1e-08, 0.01, 10

HALO = 8


def _mk_bdot(ca, cb):
    na, nb = 1 - ca, 1 - cb

    def dg(x, y, cx, cy):
        return lax.dot_general(x.astype(BF), y.astype(BF), (((cx,), (cy,)), ((), ())), preferred_element_type=F32)

    @jax.custom_vjp
    def f(a, b):
        return dg(a, b, ca, cb)

    def fwd(a, b):
        return dg(a, b, ca, cb), (a, b)

    def bwd(res, g):
        a, b = res
        da = dg(g, b, 1, nb) if ca == 1 else dg(b, g, nb, 1)
        db = dg(a, g, na, 0) if cb == 0 else dg(g, a, 0, na)
        return da.astype(a.dtype), db.astype(b.dtype)

    f.defvjp(fwd, bwd)
    return f


_bdot = _mk_bdot(1, 0)
_bdot_nt = _mk_bdot(1, 1)
_bdot_tn = _mk_bdot(0, 0)


def _sigmoid(x):
    return 1.0 / (1.0 + jnp.exp(-x))


def _silu(x):
    return x * _sigmoid(x)


def _gelu(x):
    return 0.5 * x * (1.0 + jnp.tanh(0.7978845608028654 * (x + 0.044715 * (x * x * x))))


def _log1p(z):
    u = 1.0 + z
    d = u - 1.0
    return jnp.where(d == 0.0, z, jnp.log(u) * (z / jnp.where(d == 0.0, 1.0, d)))


def _softplus(x):
    return jnp.maximum(x, 0.0) + _log1p(jnp.exp(-jnp.abs(x)))


def _neg_expm1(y):
    t = jnp.tanh(0.5 * y)
    return -2.0 * t / (1.0 - t)


def _rms(x, w):
    return x * lax.rsqrt(jnp.mean(x * x, axis=-1, keepdims=True) + EPS) * w


def _rows(x):
    return lax.broadcasted_iota(jnp.int32, x.shape, 0)


def _mk_shift():
    @functools.partial(jax.custom_vjp, nondiff_argnums=(1,))
    def shift(xx, s):
        n = xx.shape[0]
        return pltpu.roll(xx, s, axis=0)[HALO:n] if s else xx[HALO:n]

    def fwd(xx, s):
        return shift(xx, s), None

    def bwd(s, _, g):
        ext = jnp.concatenate([g, jnp.zeros((HALO, g.shape[1]), g.dtype)], axis=0)
        return (pltpu.roll(ext, HALO - s, axis=0),)

    shift.defvjp(fwd, bwd)
    return shift


_shift = _mk_shift()


def _mk_chunk_cumsum():
    def run(x, up):
        n = x.shape[0]
        pos = _rows(x) % CHUNK
        d = 1
        while d < CHUNK:
            if up:
                x = x + jnp.where(pos < CHUNK - d, pltpu.roll(x, n - d, axis=0), 0.0)
            else:
                x = x + jnp.where(pos >= d, pltpu.roll(x, d, axis=0), 0.0)
            d *= 2
        return x

    @jax.custom_vjp
    def cumsum(x):
        return run(x, False)

    cumsum.defvjp(lambda x: (run(x, False), None), lambda _, g: (run(g, True),))
    return cumsum


_chunk_cumsum = _mk_chunk_cumsum()


def _causal_conv(xx, w):
    K = w.shape[0]
    y = _shift(xx, K - 1) * w[0:1, :]
    for k in range(1, K):
        y = y + _shift(xx, K - 1 - k) * w[k:k + 1, :]
    return y


def _f_pre(x, w):
    return (_rms(x, w),)


def _f_post_pre(x, y, w_post, w_pre):
    x1 = x + _rms(y, w_post)
    return x1, _rms(x1, w_pre)


def _f_post(x, y, w_post):
    return (x + _rms(y, w_post),)


def _heads(fn, *xs):
    return jnp.concatenate([fn(*[x[:, h * HD:(h + 1) * HD] for x in xs]) for h in range(HEADS)], axis=1)


def _l2n(t):
    return t * lax.rsqrt(jnp.sum(t * t, axis=-1, keepdims=True) + EPS)


def _f_gdn_pre(qkv, ba, conv_w, alog, dtb):
    c = _silu(_causal_conv(qkv, conv_w))
    q = _heads(lambda t: _l2n(t) * (HD ** -0.5), c[:, 0:DG])
    k = _heads(_l2n, c[:, DG:2 * DG])
    v = c[:, 2 * DG:3 * DG]
    lane = lax.broadcasted_iota(jnp.int32, ba.shape, 1)
    beta = _sigmoid(ba)
    gcum = _chunk_cumsum(-jnp.exp(alog) * _softplus(ba + dtb))
    gc = jnp.where(lane < HEADS, beta, jnp.where(lane < 2 * HEADS, gcum, 0.0))
    return q, k, v, gc


def _f_gdn_chunk(q, k, v, gc, s0):
    C = q.shape[0]
    HC = HEADS * C
    sh = C.bit_length() - 1

    def stack(x):
        return jnp.concatenate([x[:, h * HD:(h + 1) * HD] for h in range(HEADS)], axis=0)

    def column(off):
        lane = lax.broadcasted_iota(jnp.int32, gc.shape, 1)
        return jnp.concatenate([jnp.sum(jnp.where(lane == off + h, gc, 0.0), axis=1, keepdims=True) for h in range(HEADS)], axis=0)

    def head(x, h):
        return x[h * C:(h + 1) * C]

    r = lax.broadcasted_iota(jnp.int32, (HC, HC), 0)
    c = lax.broadcasted_iota(jnp.int32, (HC, HC), 1)
    same = jnp.right_shift(r, sh) == jnp.right_shift(c, sh)
    causal = jnp.logical_and(same, r >= c)
    strict = jnp.logical_and(same, r > c)
    gcol, bcol = column(HEADS), column(0)
    grow = jnp.sum(jnp.where(r == c, gcol, 0.0), axis=0, keepdims=True)
    decay = jnp.where(causal, jnp.exp(jnp.where(causal, gcol - grow, 0.0)), 0.0)
    ks, qs, vs = stack(k), stack(q), stack(v)
    kb = ks * bcol
    kk = _bdot_nt(jnp.concatenate([kb, qs], axis=0), ks)
    m = jnp.where(strict, kk[0:HC] * decay, 0.0)
    attn = jnp.where(causal, kk[HC:2 * HC] * decay, 0.0)
    n = -m
    t = (r == c).astype(F32) + n
    p = n
    for _ in range(5):
        p = _bdot(p, p)
        t = t + _bdot(t, p)
    eg = jnp.exp(gcol)
    wu = _bdot(t, jnp.concatenate([kb * eg, vs * bcol], axis=1))
    w, u = wu[:, 0:HD], wu[:, HD:2 * HD]
    last = jnp.logical_and(same, jnp.bitwise_and(c, C - 1) == C - 1)
    glast = jnp.sum(jnp.where(last, grow, 0.0), axis=1, keepdims=True)
    k_g = ks * jnp.exp(glast - gcol)
    q_g = qs * eg
    ws = [_bdot(jnp.concatenate([head(w, h), head(q_g, h)], axis=0), s0[h * HD:(h + 1) * HD]) for h in range(HEADS)]
    v_new = u - jnp.concatenate([x[0:C] for x in ws], axis=0)
    o = jnp.concatenate([x[C:2 * C] for x in ws], axis=0) + _bdot(attn, v_new)
    s1 = [s0[h * HD:(h + 1) * HD] * jnp.exp(glast[h * C:h * C + 1]) + _bdot_tn(head(k_g, h), head(v_new, h)) for h in range(HEADS)]
    return jnp.concatenate([head(o, h) for h in range(HEADS)], axis=1), jnp.concatenate(s1, axis=0)


def _f_gdn_chunks(q, k, v, gc, s0):
    outs, s = [], s0
    for n in range(q.shape[0] // CHUNK):
        rs = slice(n * CHUNK, (n + 1) * CHUNK)
        o, s = _f_gdn_chunk(q[rs], k[rs], v[rs], gc[rs], s)
        outs.append(o)
    return jnp.concatenate(outs, axis=0), s


def _f_gdn_post(o, z, nw):
    return (_heads(lambda a, b: _rms(a, nw) * _silu(b), o, z),)


def _f_lru_ab(lx, conv_w, conv_b, wa, ba, wx, bx, lam):
    xc = _causal_conv(lx, conv_w) + conv_b
    r = _sigmoid(_bdot(xc, wa) + ba)
    i = _sigmoid(_bdot(xc, wx) + bx)
    log_a = -LRU_C * r * _softplus(-lam)
    a = jnp.exp(log_a)
    mult = jnp.sqrt(_neg_expm1(2.0 * log_a))
    return a, mult * (i * xc)


def _f_lru_post(hs, gate, gw):
    return (_rms(hs * _gelu(gate), gw),)


def _f_sgu(uv, ln_w, ln_b, ws, bst, gw):
    tm = uv.shape[0]
    uvf = _gelu(uv)
    u, v = uvf[:, 0:DG], uvf[:, DG:2 * DG]
    mu = jnp.mean(v, axis=-1, keepdims=True)
    vc = v - mu
    v = vc * lax.rsqrt(jnp.mean(vc * vc, axis=-1, keepdims=True) + EPS) * ln_w + ln_b
    lane = lax.broadcasted_iota(jnp.int32, bst.shape, 1)
    tril = lax.broadcasted_iota(jnp.int32, (128, 128), 0) >= lax.broadcasted_iota(jnp.int32, (128, 128), 1)
    wsm = [jnp.where(tril, ws[g * 128:(g + 1) * 128, :], 0.0) for g in range(4)]
    bias = [jnp.sum(jnp.where(lane == g, bst, 0.0), axis=1, keepdims=True) for g in range(4)]
    out = []
    for n in range(tm // 128):
        vn = v[n * 128:(n + 1) * 128, :]
        gs = [_bdot(wsm[g], vn[:, g * 128:(g + 1) * 128]) + bias[g] for g in range(4)]
        out.append(jnp.concatenate(gs, axis=1))
    vo = jnp.concatenate(out, axis=0) if len(out) > 1 else out[0]
    return (_rms(u * vo, gw),)


def _f_sconv(sb, sc, sh, conv_w, gw):
    return (_rms(sb * _causal_conv(sc * sh, conv_w), gw),)


def _f_ffn_act(ug, uv, wg, wv, bg, bv):
    return (_gelu(_causal_conv(ug, wg) + bg) * (_causal_conv(uv, wv) + bv),)


def _row_specs(rows, consts, tm, ncol, tile_of):
    specs, args = [], []
    for arr, cb, w, halo in rows:
        specs.append(pl.BlockSpec((tm, w), lambda j, i, cb=cb: (tile_of(i), cb + j)))
        args.append(arr)
        if halo:
            specs.append(pl.BlockSpec((HALO, w), lambda j, i, cb=cb: (jnp.maximum(tile_of(i) * (tm // HALO) - 1, 0), cb + j)))
            args.append(arr)
    for arr, tiled in consts:
        r, c = arr.shape
        specs.append(pl.BlockSpec((r, c // ncol), lambda j, i: (0, j)) if tiled else pl.BlockSpec((r, c), lambda j, i: (0, 0)))
        args.append(arr)
    return specs, args


def _load_rows(refs, rows, consts, tile):
    k, vals = 0, []
    for _arr, _cb, _w, halo in rows:
        t = refs[k][...].astype(F32)
        k += 1
        if halo:
            hl = jnp.where(tile > 0, refs[k][...].astype(F32), 0.0)
            k += 1
            t = jnp.concatenate([hl, t], axis=0)
        vals.append(t)
    for _ in consts:
        vals.append(refs[k][...])
        k += 1
    return vals, k


def _rowwise(fn, rows, consts, outs, *, tm, name, ncol=1, xch=None):
    n = rows[0][0].shape[0]
    nt = n // tm
    in_specs, args = _row_specs(rows, consts, tm, ncol, lambda i: i)

    def body(*refs):
        vals, k = _load_rows(refs, rows, consts, pl.program_id(1))
        for o_ref, r in zip(refs[k:], fn(*vals)):
            o_ref[...] = r.astype(o_ref.dtype)

    res, xres = _call(
        body, grid=(ncol, nt), in_specs=in_specs,
        out_specs=[pl.BlockSpec((tm, w), lambda j, i: (i, j)) for w, _ in outs],
        out_shape=[jax.ShapeDtypeStruct((n, w * ncol), dt) for w, dt in outs], scratch=[], name=name, args=args, xch=xch)
    return res if xch is None else (res, xres)


def _rowwise_vjp(fn, rows, consts, cots, *, tm, name, ncol=1):
    n = rows[0][0].shape[0]
    nt = n // tm
    rows4 = [r[:4] for r in rows]
    consts2 = [c[:2] for c in consts]
    in_specs, args = _row_specs(rows4, consts2, tm, ncol, lambda i: nt - 1 - i)
    for arr, cb, w in cots:
        in_specs.append(pl.BlockSpec((tm, w), lambda j, i, cb=cb: (nt - 1 - i, cb + j)))
        args.append(arr)
    out_specs, out_shape, scratch = [], [], []
    for arr, cb, w, halo, gdt in rows:
        if gdt is not None:
            out_specs.append(pl.BlockSpec((tm, w), lambda j, i: (nt - 1 - i, j)))
            out_shape.append(jax.ShapeDtypeStruct((n, w * ncol), gdt))
            if halo:
                scratch.append(pltpu.VMEM((HALO, w), F32))
    for arr, tiled, want in consts:
        if want:
            r, c = arr.shape
            out_specs.append(pl.BlockSpec((r, c // ncol), lambda j, i: (0, j)) if tiled else pl.BlockSpec((r, c), lambda j, i: (0, 0)))
            out_shape.append(jax.ShapeDtypeStruct((r, c), F32))
    n_in = len(in_specs)
    n_out = len(out_specs)

    def body(*refs):
        j, i = pl.program_id(0), pl.program_id(1)
        tile = nt - 1 - i
        vals, k = _load_rows(refs, rows4, consts2, tile)
        cvals = [refs[k + q][...].astype(F32) for q in range(len(cots))]
        outs = refs[n_in:n_in + n_out]
        carries = refs[n_in + n_out:]
        _, vjp = jax.vjp(fn, *vals)
        g = vjp(tuple(cvals))
        o, cidx = 0, 0
        for q, (arr, cb, w, halo, gdt) in enumerate(rows):
            if gdt is None:
                continue
            if halo:
                ge = g[q]
                main = ge[HALO:]
                carry = carries[cidx]
                cidx += 1
                tail = main[tm - HALO:] + jnp.where(i > 0, carry[...], 0.0)
                outs[o][0:tm - HALO, :] = main[0:tm - HALO].astype(gdt)
                outs[o][tm - HALO:tm, :] = tail.astype(gdt)
                carry[...] = ge[0:HALO]
            else:
                outs[o][...] = g[q].astype(gdt)
            o += 1
        for q, (arr, tiled, want) in enumerate(consts):
            if not want:
                continue
            first = (i == 0) if tiled else jnp.logical_and(i == 0, j == 0)
            acc = outs[o]
            gq = g[len(rows) + q].astype(F32)

            @pl.when(first)
            def _(acc=acc, gq=gq):
                acc[...] = gq

            @pl.when(jnp.logical_not(first))
            def _(acc=acc, gq=gq):
                acc[...] += gq

            o += 1

    res = pl.pallas_call(
        body, grid=(ncol, nt), in_specs=in_specs, out_specs=out_specs, out_shape=out_shape, scratch_shapes=scratch,
        name=name, compiler_params=pltpu.CompilerParams(dimension_semantics=("arbitrary", "arbitrary")),
    )(*args)
    nrow = sum(1 for r in rows if r[4] is not None)
    return list(res[:nrow]), list(res[nrow:])


_ANY = pl.BlockSpec(memory_space=pl.ANY)
_CHIP_REL = ((1, 0), (0, 1), (1, 1))
_DEV_REL = tuple((r >> 2 & 1, r >> 1 & 1, r & 1) for r in range(1, 8))


def _flip(v, f):
    return 1 - v if f else v


class _GatherChips:
    def __init__(self, shards):
        self.arrs = list(shards)
        n = len(self.arrs)
        self.out_shape = [jax.ShapeDtypeStruct((4,) + s.shape, s.dtype) for s in self.arrs]
        self.sems = [pltpu.SemaphoreType.DMA((3 * n,)), pltpu.SemaphoreType.DMA((3 * n,)), pltpu.SemaphoreType.DMA((n,))]

    def _copies(self, ins, outs, sems, arriving):
        send, recv, lsem = sems
        x, y, c = lax.axis_index("x"), lax.axis_index("y"), lax.axis_index("c")
        me = 2 * x + y
        if arriving:
            local = []
        else:
            local = [pltpu.make_async_copy(ins[a], outs[a].at[me], lsem.at[a]) for a in range(len(ins))]
        remote = []
        for a in range(len(ins)):
            for k, (fx, fy) in enumerate(_CHIP_REL):
                px, py = _flip(x, fx), _flip(y, fy)
                remote.append(pltpu.make_async_remote_copy(
                    src_ref=ins[a], dst_ref=outs[a].at[2 * px + py if arriving else me], send_sem=send.at[3 * a + k],
                    recv_sem=recv.at[3 * a + k], device_id=(px, py, c), device_id_type=MESH_ID))
        return local, remote

    def issue(self, ins, outs, sems):
        local, push = self._copies(ins, outs, sems, False)
        for cp in local + push:
            cp.start()

    def finish(self, ins, outs, sems):
        for cp in self._copies(ins, outs, sems, True)[1]:
            cp.wait_recv()
        local, push = self._copies(ins, outs, sems, False)
        for cp in push:
            cp.wait_send()
        for cp in local:
            cp.wait()


class _GatherChipsTwoLevel:
    def __init__(self, shards):
        self.arrs = list(shards)
        n = len(self.arrs)
        self.out_shape = [jax.ShapeDtypeStruct((4,) + s.shape, s.dtype) for s in self.arrs]
        self.sems = [pltpu.SemaphoreType.DMA((3 * n,)) for _ in range(4)] + [pltpu.SemaphoreType.DMA((n,))]

    def _far(self, ins, outs, sems, arriving):
        send, recv = sems[0], sems[1]
        x, y, c = lax.axis_index("x"), lax.axis_index("y"), lax.axis_index("c")
        me = 2 * x + y
        copies = []
        for a in range(len(ins)):
            half = ins[a].shape[0] // 2
            rows = pl.ds(c * half, half)
            for k, (fx, fy) in enumerate(_CHIP_REL):
                px, py = _flip(x, fx), _flip(y, fy)
                copies.append(pltpu.make_async_remote_copy(
                    src_ref=ins[a].at[rows], dst_ref=outs[a].at[2 * px + py if arriving else me, rows], send_sem=send.at[3 * a + k],
                    recv_sem=recv.at[3 * a + k], device_id=(px, py, c), device_id_type=MESH_ID))
        return copies

    def _near(self, ins, outs, sems, arriving):
        send, recv = sems[2], sems[3]
        x, y, c = lax.axis_index("x"), lax.axis_index("y"), lax.axis_index("c")
        copies = []
        for a in range(len(ins)):
            half = ins[a].shape[0] // 2
            rows = pl.ds((1 - c if arriving else c) * half, half)
            for k, (fx, fy) in enumerate(_CHIP_REL):
                block = outs[a].at[2 * _flip(x, fx) + _flip(y, fy), rows]
                copies.append(pltpu.make_async_remote_copy(
                    src_ref=block, dst_ref=block, send_sem=send.at[3 * a + k], recv_sem=recv.at[3 * a + k],
                    device_id=(x, y, 1 - c), device_id_type=MESH_ID))
        return copies

    def _local(self, ins, outs, sems):
        me = 2 * lax.axis_index("x") + lax.axis_index("y")
        return [pltpu.make_async_copy(ins[a], outs[a].at[me], sems[4].at[a]) for a in range(len(ins))]

    def issue(self, ins, outs, sems):
        for cp in self._local(ins, outs, sems) + self._far(ins, outs, sems, False):
            cp.start()

    def finish(self, ins, outs, sems):
        landed = self._far(ins, outs, sems, True)
        onward = self._near(ins, outs, sems, False)
        for cp, fwd in zip(landed, onward):
            cp.wait_recv()
            fwd.start()
        for cp in self._near(ins, outs, sems, True):
            cp.wait_recv()
        for cp in self._far(ins, outs, sems, False) + onward:
            cp.wait_send()
        for cp in self._local(ins, outs, sems):
            cp.wait()


class _ScatterChips:
    def __init__(self, blocks):
        self.arrs = list(blocks)
        n = len(self.arrs)
        self.out_shape = [jax.ShapeDtypeStruct((3,) + b.shape[1:], b.dtype) for b in self.arrs]
        self.sems = [pltpu.SemaphoreType.DMA((3 * n,)), pltpu.SemaphoreType.DMA((3 * n,))]

    def _copies(self, ins, outs, sems):
        send, recv = sems
        x, y, c = lax.axis_index("x"), lax.axis_index("y"), lax.axis_index("c")
        copies = []
        for a in range(len(ins)):
            for k, (fx, fy) in enumerate(_CHIP_REL):
                px, py = _flip(x, fx), _flip(y, fy)
                copies.append(pltpu.make_async_remote_copy(
                    src_ref=ins[a].at[2 * px + py], dst_ref=outs[a].at[k], send_sem=send.at[3 * a + k], recv_sem=recv.at[3 * a + k],
                    device_id=(px, py, c), device_id_type=MESH_ID))
        return copies

    def issue(self, ins, outs, sems):
        for cp in self._copies(ins, outs, sems):
            cp.start()

    def finish(self, ins, outs, sems):
        copies = self._copies(ins, outs, sems)
        for cp in copies:
            cp.wait_recv()
        for cp in copies:
            cp.wait_send()


def _exchange(xch, *, name):
    ni, no = len(xch.arrs), len(xch.out_shape)

    def body(*refs):
        ins, outs, sems = refs[:ni], refs[ni:ni + no], refs[ni + no:]
        xch.issue(ins, outs, sems)
        xch.finish(ins, outs, sems)

    return pl.pallas_call(body, in_specs=[_ANY] * ni, out_specs=[_ANY] * no, out_shape=xch.out_shape, scratch_shapes=xch.sems, name=name)(*xch.arrs)


def _call(body, *, grid, in_specs, out_specs, out_shape, scratch, name, args, xch=None):
    params = pltpu.CompilerParams(dimension_semantics=("arbitrary",) * len(grid))
    if xch is None:
        res = pl.pallas_call(body, grid=grid, in_specs=in_specs, out_specs=out_specs, out_shape=out_shape, scratch_shapes=scratch,
                             name=name, compiler_params=params)(*args)
        return list(res), []
    n_in, n_out, n_sc = len(in_specs), len(out_specs), len(scratch)
    xi, xo = len(xch.arrs), len(xch.out_shape)

    def wrapped(*refs):
        ins, refs = refs[:n_in], refs[n_in:]
        xin, refs = refs[:xi], refs[xi:]
        outs, refs = refs[:n_out], refs[n_out:]
        xout, refs = refs[:xo], refs[xo:]
        sc, sems = refs[:n_sc], refs[n_sc:]
        first = functools.reduce(jnp.logical_and, [pl.program_id(d) == 0 for d in range(len(grid))])
        last = functools.reduce(jnp.logical_and, [pl.program_id(d) == grid[d] - 1 for d in range(len(grid))])

        @pl.when(first)
        def _():
            xch.issue(xin, xout, sems)

        body(*ins, *outs, *sc)

        @pl.when(last)
        def _():
            xch.finish(xin, xout, sems)

    res = pl.pallas_call(
        wrapped, grid=grid, in_specs=list(in_specs) + [_ANY] * xi, out_specs=list(out_specs) + [_ANY] * xo,
        out_shape=list(out_shape) + xch.out_shape, scratch_shapes=list(scratch) + xch.sems, name=name, compiler_params=params,
    )(*args, *xch.arrs)
    return list(res[:n_out]), list(res[n_out:])


def _matmul(a, b, mode, out_dtype, *, tm, tn, tk, name, b_blocks=False, out_blocks=False, xch=None):
    if b_blocks:
        _, br, bc4 = b.shape
        b2 = (br, 4 * bc4)
    else:
        b2 = b.shape

    def bspec(shape, index):
        if not b_blocks:
            return pl.BlockSpec(shape, index)
        per = bc4 // shape[1]

        def blocked(i, j, k):
            r, c = index(i, j, k)
            return (c // per, r, c % per)

        return pl.BlockSpec((None,) + shape, blocked)

    tm = min(tm, a.shape[1] if mode == "tn" else a.shape[0])
    tk = min(tk, a.shape[0] if mode == "tn" else a.shape[1])
    if mode == "tn":
        K, M = a.shape
        N = b2[1]
        a_spec = pl.BlockSpec((tk, tm), lambda i, j, k: (k, i))
        b_spec = bspec((tk, tn), lambda i, j, k: (k, j))
        dims = (((0,), (0,)), ((), ()))
    elif mode == "nt":
        M, K = a.shape
        N = b2[0]
        a_spec = pl.BlockSpec((tm, tk), lambda i, j, k: (i, k))
        b_spec = bspec((tn, tk), lambda i, j, k: (j, k))
        dims = (((1,), (1,)), ((), ()))
    else:
        M, K = a.shape
        N = b2[1]
        a_spec = pl.BlockSpec((tm, tk), lambda i, j, k: (i, k))
        b_spec = bspec((tk, tn), lambda i, j, k: (k, j))
        dims = (((1,), (0,)), ((), ()))
    assert M % tm == 0 and N % tn == 0 and K % tk == 0, (name, M, N, K, tm, tn, tk)
    nk = K // tk
    if out_blocks:
        per_o = (N // 4) // tn
        o_spec = pl.BlockSpec((None, tm, tn), lambda i, j, k: (j // per_o, i, j % per_o))
        o_shape = jax.ShapeDtypeStruct((4, M, N // 4), out_dtype)
    else:
        o_spec = pl.BlockSpec((tm, tn), lambda i, j, k: (i, j))
        o_shape = jax.ShapeDtypeStruct((M, N), out_dtype)

    def body(a_ref, b_ref, o_ref, acc_ref):
        k = pl.program_id(2)
        part = lax.dot_general(a_ref[...].astype(BF), b_ref[...].astype(BF), dims, preferred_element_type=F32)
        if nk == 1:
            o_ref[...] = part.astype(o_ref.dtype)
        else:
            @pl.when(k == 0)
            def _():
                acc_ref[...] = part

            @pl.when(k > 0)
            def _():
                acc_ref[...] += part

            @pl.when(k == nk - 1)
            def _():
                o_ref[...] = acc_ref[...].astype(o_ref.dtype)

    res, xres = _call(body, grid=(M // tm, N // tn, nk), in_specs=[a_spec, b_spec], out_specs=[o_spec], out_shape=[o_shape],
                      scratch=[pltpu.VMEM((tm, tn) if nk > 1 else (8, 128), F32)], name=name, args=(a, b), xch=xch)
    return res[0], xres


def _scan_fwd(a, b, *, tm, name):
    n, c = a.shape
    nt = n // tm

    def body(a_ref, b_ref, h_ref, carry):
        i = pl.program_id(0)
        av, bv = a_ref[...], b_ref[...]
        row = _rows(av)
        d = 1
        while d < tm:
            a_s = jnp.where(row >= d, pltpu.roll(av, d, axis=0), 1.0)
            b_s = jnp.where(row >= d, pltpu.roll(bv, d, axis=0), 0.0)
            bv = av * b_s + bv
            av = av * a_s
            d *= 2
        h = bv + av * jnp.where(i > 0, carry[HALO - 1:HALO, :], 0.0)
        h_ref[...] = h
        carry[...] = h[tm - HALO:tm]

    return pl.pallas_call(
        body, grid=(nt,), in_specs=[pl.BlockSpec((tm, c), lambda i: (i, 0))] * 2,
        out_specs=pl.BlockSpec((tm, c), lambda i: (i, 0)), out_shape=jax.ShapeDtypeStruct((n, c), F32),
        scratch_shapes=[pltpu.VMEM((HALO, c), F32)], name=name,
        compiler_params=pltpu.CompilerParams(dimension_semantics=("arbitrary",)),
    )(a, b)


def _scan_bwd(a, h, dh, *, tm, name):
    n, c = a.shape
    nt = n // tm
    tb = tm // HALO

    def body(a_ref, an_ref, h_ref, hp_ref, dh_ref, da_ref, db_ref, carry):
        i = pl.program_id(0)
        tile = nt - 1 - i
        av, hv, g = a_ref[...], h_ref[...], dh_ref[...]
        row = _rows(av)
        a_next = jnp.where(tile < nt - 1, an_ref[0:1, :], 0.0)
        au = jnp.where(row < tm - 1, pltpu.roll(av, tm - 1, axis=0), a_next)
        d = 1
        while d < tm:
            a_s = jnp.where(row < tm - d, pltpu.roll(au, tm - d, axis=0), 1.0)
            g_s = jnp.where(row < tm - d, pltpu.roll(g, tm - d, axis=0), 0.0)
            g = au * g_s + g
            au = au * a_s
            d *= 2
        g = g + au * jnp.where(i > 0, carry[0:1, :], 0.0)
        h_prev = jnp.where(row >= 1, pltpu.roll(hv, 1, axis=0), jnp.where(tile > 0, hp_ref[HALO - 1:HALO, :], 0.0))
        db_ref[...] = g
        da_ref[...] = g * h_prev
        carry[...] = g[0:HALO]

    cur = pl.BlockSpec((tm, c), lambda i: (nt - 1 - i, 0))
    nxt = pl.BlockSpec((HALO, c), lambda i: (jnp.minimum((nt - i) * tb, n // HALO - 1), 0))
    prv = pl.BlockSpec((HALO, c), lambda i: (jnp.maximum((nt - 1 - i) * tb - 1, 0), 0))
    return pl.pallas_call(
        body, grid=(nt,), in_specs=[cur, nxt, cur, prv, cur], out_specs=[cur, cur],
        out_shape=[jax.ShapeDtypeStruct((n, c), F32)] * 2, scratch_shapes=[pltpu.VMEM((HALO, c), F32)], name=name,
        compiler_params=pltpu.CompilerParams(dimension_semantics=("arbitrary",)),
    )(a, a, h, h, dh)


def _gdn_fwd(q, k, v, gb, *, name, xch=None):
    n = q.shape[0]
    rows = GDN_STEP * CHUNK
    ns = n // rows

    def body(q_ref, k_ref, v_ref, gb_ref, o_ref, s_ref, state):
        @pl.when(pl.program_id(0) == 0)
        def _():
            state[...] = jnp.zeros_like(state)

        s0 = state[...]
        s_ref[0] = s0
        o, s1 = _f_gdn_chunks(q_ref[...], k_ref[...], v_ref[...], gb_ref[...], s0)
        o_ref[...] = o
        state[...] = s1

    row = pl.BlockSpec((rows, DG), lambda i: (i, 0))
    return _call(
        body, grid=(ns,), in_specs=[row, row, row, pl.BlockSpec((rows, 128), lambda i: (i, 0))],
        out_specs=[row, pl.BlockSpec((1, DG, HD), lambda i: (i, 0, 0))],
        out_shape=[jax.ShapeDtypeStruct((n, DG), F32), jax.ShapeDtypeStruct((ns, DG, HD), F32)],
        scratch=[pltpu.VMEM((DG, HD), F32)], name=name, args=(q, k, v, gb), xch=xch)


def _gdn_bwd(q, k, v, gb, s_all, do, *, name, xch=None):
    n = q.shape[0]
    rows = GDN_STEP * CHUNK
    ns = n // rows

    def body(q_ref, k_ref, v_ref, gb_ref, s_ref, do_ref, dq_ref, dk_ref, dv_ref, dgb_ref, dstate):
        @pl.when(pl.program_id(0) == 0)
        def _():
            dstate[...] = jnp.zeros_like(dstate)

        _, vjp = jax.vjp(_f_gdn_chunks, q_ref[...], k_ref[...], v_ref[...], gb_ref[...], s_ref[0])
        dq_ref[...], dk_ref[...], dv_ref[...], dgb_ref[...], dstate[...] = vjp((do_ref[...], dstate[...]))

    row = pl.BlockSpec((rows, DG), lambda i: (ns - 1 - i, 0))
    gsp = pl.BlockSpec((rows, 128), lambda i: (ns - 1 - i, 0))
    return _call(
        body, grid=(ns,), in_specs=[row, row, row, gsp, pl.BlockSpec((1, DG, HD), lambda i: (ns - 1 - i, 0, 0)), row],
        out_specs=[row, row, row, gsp],
        out_shape=[jax.ShapeDtypeStruct((n, DG), F32)] * 3 + [jax.ShapeDtypeStruct((n, 128), F32)],
        scratch=[pltpu.VMEM((DG, HD), F32)], name=name, args=(q, k, v, gb, s_all, do), xch=xch)


def _loss_head(x1, y2, w, tgt, *, tm, name):
    n, c = x1.shape

    def body(x_ref, y_ref, w_ref, t_ref, l_ref, d_ref):
        err = x_ref[...] + _rms(y_ref[...], w_ref[...]) - t_ref[...]
        part = jnp.sum(jnp.sum(err * err, axis=1, keepdims=True), axis=0, keepdims=True) * (0.5 / c)

        @pl.when(pl.program_id(0) == 0)
        def _():
            l_ref[...] = jnp.zeros_like(l_ref)

        l_ref[...] += part
        d_ref[...] = err * (1.0 / c)

    row = pl.BlockSpec((tm, c), lambda i: (i, 0))
    return pl.pallas_call(
        body, grid=(n // tm,), in_specs=[row, row, pl.BlockSpec((1, c), lambda i: (0, 0)), row],
        out_specs=[pl.BlockSpec((8, 128), lambda i: (0, 0)), row],
        out_shape=[jax.ShapeDtypeStruct((8, 128), F32), jax.ShapeDtypeStruct((n, c), F32)], name=name,
        compiler_params=pltpu.CompilerParams(dimension_semantics=("arbitrary",)),
    )(x1, y2, w, tgt)


def _sum_parts(own, recv, *, tr, name):
    r, c = own.shape
    p = recv.shape[0]

    def body(o_ref, r_ref, s_ref):
        s = o_ref[...]
        for q in range(p):
            s = s + r_ref[q].astype(F32)
        s_ref[...] = s

    return pl.pallas_call(
        body, grid=(r // tr,), in_specs=[pl.BlockSpec((tr, c), lambda i: (i, 0)), pl.BlockSpec((p, tr, c), lambda i: (0, i, 0))],
        out_specs=pl.BlockSpec((tr, c), lambda i: (i, 0)), out_shape=jax.ShapeDtypeStruct((r, c), F32), name=name,
        compiler_params=pltpu.CompilerParams(dimension_semantics=("parallel",)),
    )(own, recv)


def _sum_slots(buf, *, name):
    p, r, c = buf.shape

    def body(b_ref, s_ref):
        s = b_ref[0]
        for q in range(1, p):
            s = s + b_ref[q]
        s_ref[...] = s

    return pl.pallas_call(body, out_shape=jax.ShapeDtypeStruct((r, c), F32), name=name)(buf)


def _adamw(w, m, v, gs, *, tr, name, xch=None):
    ngrp, r, c = w.shape
    nterm = len(gs[0])
    per = r // tr
    c1 = 1.0 - ADAM_B1 ** ADAM_STEP
    c2 = 1.0 - ADAM_B2 ** ADAM_STEP

    def body(*refs):
        w_ref, m_ref, v_ref = refs[:3]
        g_refs = refs[3:3 + ngrp * nterm]
        g_ref, d_ref, nm_ref, nv_ref = refs[3 + ngrp * nterm:]
        grp = pl.program_id(0) // per
        g = None
        for q in range(ngrp):
            gq = g_refs[q * nterm][...]
            for t in range(1, nterm):
                gq = gq + g_refs[q * nterm + t][...]
            g = gq if g is None else jnp.where(grp == q, gq, g)
        nm = ADAM_B1 * m_ref[...] + (1.0 - ADAM_B1) * g
        nv = ADAM_B2 * v_ref[...] + (1.0 - ADAM_B2) * (g * g)
        g_ref[...] = g
        d_ref[...] = -ADAM_LR * ((nm / c1) / (jnp.sqrt(nv / c2) + ADAM_EPS) + ADAM_WD * w_ref[...])
        nm_ref[...] = nm
        nv_ref[...] = nv

    blk = pl.BlockSpec((None, tr, c), lambda i: (i // per, i % per, 0))
    g_specs = [pl.BlockSpec((tr, c), lambda i, q=q: (jnp.clip(i - q * per, 0, per - 1), 0)) for q in range(ngrp) for _ in range(nterm)]
    return _call(body, grid=(ngrp * per,), in_specs=[blk] * 3 + g_specs, out_specs=[blk] * 4,
                 out_shape=[jax.ShapeDtypeStruct((ngrp, r, c), F32)] * 4, scratch=[], name=name,
                 args=(w, m, v, *[t for grp in gs for t in grp]), xch=xch)


class _SwapCores:
    def __init__(self, arrs):
        self.arrs = list(arrs)
        n = len(self.arrs)
        self.out_shape = [jax.ShapeDtypeStruct(a.shape, a.dtype) for a in self.arrs]
        self.sems = [pltpu.SemaphoreType.DMA((n,)), pltpu.SemaphoreType.DMA((n,))]

    def _copies(self, ins, outs, sems):
        send, recv = sems
        sib = (lax.axis_index("x"), lax.axis_index("y"), 1 - lax.axis_index("c"))
        return [pltpu.make_async_remote_copy(src_ref=ins[a], dst_ref=outs[a], send_sem=send.at[a], recv_sem=recv.at[a],
                                             device_id=sib, device_id_type=MESH_ID) for a in range(len(ins))]

    def issue(self, ins, outs, sems):
        for cp in self._copies(ins, outs, sems):
            cp.start()

    def finish(self, ins, outs, sems):
        copies = self._copies(ins, outs, sems)
        for cp in copies:
            cp.wait_recv()
        for cp in copies:
            cp.wait_send()


class _GatherDevices:
    def __init__(self, buf):
        self.arrs = [buf]
        self.out_shape = [jax.ShapeDtypeStruct((8,) + buf.shape, buf.dtype)]
        self.sems = [pltpu.SemaphoreType.DMA((7,)), pltpu.SemaphoreType.DMA((7,)), pltpu.SemaphoreType.DMA((1,))]

    def _copies(self, ins, outs, sems, arriving):
        send, recv, lsem = sems
        x, y, c = lax.axis_index("x"), lax.axis_index("y"), lax.axis_index("c")
        me = 4 * x + 2 * y + c
        local = [] if arriving else [pltpu.make_async_copy(ins[0], outs[0].at[me], lsem.at[0])]
        remote = []
        for k, (fx, fy, fc) in enumerate(_DEV_REL):
            px, py, pc = _flip(x, fx), _flip(y, fy), _flip(c, fc)
            remote.append(pltpu.make_async_remote_copy(
                src_ref=ins[0], dst_ref=outs[0].at[4 * px + 2 * py + pc if arriving else me], send_sem=send.at[k], recv_sem=recv.at[k],
                device_id=(px, py, pc), device_id_type=MESH_ID))
        return local, remote

    def issue(self, ins, outs, sems):
        local, push = self._copies(ins, outs, sems, False)
        for cp in local + push:
            cp.start()

    def finish(self, ins, outs, sems):
        for cp in self._copies(ins, outs, sems, True)[1]:
            cp.wait_recv()
        local, push = self._copies(ins, outs, sems, False)
        for cp in push:
            cp.wait_send()
        for cp in local:
            cp.wait()


class _Together:
    def __init__(self, parts):
        self.parts = list(parts)
        self.arrs = [a for p in self.parts for a in p.arrs]
        self.out_shape = [s for p in self.parts for s in p.out_shape]
        self.sems = [s for p in self.parts for s in p.sems]

    def _split(self, ins, outs, sems):
        i = o = s = 0
        for p in self.parts:
            ni, no, ns = len(p.arrs), len(p.out_shape), len(p.sems)
            yield p, ins[i:i + ni], outs[o:o + no], sems[s:s + ns]
            i, o, s = i + ni, o + no, s + ns

    def issue(self, ins, outs, sems):
        for p, a, b, c in self._split(ins, outs, sems):
            p.issue(a, b, c)

    def finish(self, ins, outs, sems):
        for p, a, b, c in self._split(ins, outs, sems):
            p.finish(a, b, c)


def _lanes(v, off):
    return jnp.pad(v.reshape(1, -1), ((0, 0), (off, 128 - off - v.size)))


def _block_diag(w):
    eye = jnp.eye(8, dtype=w.dtype)
    return (w[:, :, None, :] * eye[:, None, :, None]).reshape(DG, DG)


def _diag_blocks(w):
    return jnp.stack([w[h * 64:(h + 1) * 64, h * 64:(h + 1) * 64] for h in range(8)])


def _mixer_params(p):
    return dict(
        gdn_conv_w=p["gdn_conv_w"], alog=_lanes(p["gdn_a_log"], HEADS), dtb=_lanes(p["gdn_dt_bias"], HEADS),
        gdn_nw=p["gdn_norm_w"].reshape(1, HD),
        lru_conv_w=p["lru_conv_w"], lru_conv_b=p["lru_conv_b"].reshape(1, DG),
        wa=_block_diag(p["lru_wa"]), ba=p["lru_ba"].reshape(1, DG), wx=_block_diag(p["lru_wx"]), bx=p["lru_bx"].reshape(1, DG),
        lam=p["lru_lambda"].reshape(1, DG),
        ln_w=p["sgu_ln_w"].reshape(1, DG), ln_b=p["sgu_ln_b"].reshape(1, DG), ws=p["sgu_ws"].reshape(DG, 128),
        bst=jnp.pad(p["sgu_b"].T, ((0, 0), (0, 124))),
        sconv_w=p["sconv_w"], gw0=p["grp_norm_w"][0:1], gw1=p["grp_norm_w"][1:2], gw2=p["grp_norm_w"][2:3],
    )


def _mixer_param_grads(g):
    return dict(
        gdn_conv_w=g["gdn_conv_w"], gdn_a_log=g["alog"][0, HEADS:2 * HEADS], gdn_dt_bias=g["dtb"][0, HEADS:2 * HEADS],
        gdn_norm_w=g["gdn_nw"][0],
        lru_conv_w=g["lru_conv_w"], lru_conv_b=g["lru_conv_b"][0],
        lru_wa=_diag_blocks(g["wa"]), lru_ba=g["ba"].reshape(8, 64), lru_wx=_diag_blocks(g["wx"]), lru_bx=g["bx"].reshape(8, 64),
        lru_lambda=g["lam"][0],
        sgu_ln_w=g["ln_w"][0], sgu_ln_b=g["ln_b"][0], sgu_ws=g["ws"].reshape(4, 128, 128), sgu_b=g["bst"][:, 0:4].T,
        sconv_w=g["sconv_w"], grp_norm_w=jnp.concatenate([g["gw0"], g["gw1"], g["gw2"]], axis=0),
    )


TM_MIX = 256


def _mixers_fwd(p, mp, tag="", xch=None):
    c = lambda *names: [(mp[n], False) for n in names]
    q, k, v, gb = _rowwise(_f_gdn_pre, [(p, 0, 1536, True), (p, OFF_BA // 128, 128, False)], c("gdn_conv_w", "alog", "dtb"),
                           [(DG, F32)] * 3 + [(128, F32)], tm=TM_MIX, name="gdn_pre" + tag)
    (o, s_all), xres = _gdn_fwd(q, k, v, gb, name="gdn_chunks" + tag, xch=xch)
    y_a, = _rowwise(_f_gdn_post, [(o, 0, DG, False), (p, OFF_Z // DG, DG, False)], c("gdn_nw"), [(DG, BF)], tm=TM_MIX, name="gdn_post" + tag)
    a, b = _rowwise(_f_lru_ab, [(p, OFF_LX // DG, DG, True)], c("lru_conv_w", "lru_conv_b", "wa", "ba", "wx", "bx", "lam"),
                    [(DG, F32)] * 2, tm=TM_MIX, name="lru_ab" + tag)
    hs = _scan_fwd(a, b, tm=TM_MIX, name="lru_scan" + tag)
    y_b, = _rowwise(_f_lru_post, [(hs, 0, DG, False), (p, OFF_LG // DG, DG, False)], c("gw0"), [(DG, BF)], tm=TM_MIX, name="lru_post" + tag)
    y_c, = _rowwise(_f_sgu, [(p, OFF_UV // 1024, 1024, False)], c("ln_w", "ln_b", "ws", "bst", "gw1"), [(DG, BF)], tm=TM_MIX, name="sgu" + tag)
    y_d, = _rowwise(_f_sconv, [(p, OFF_SB // DG, DG, False), (p, OFF_SC // DG, DG, True), (p, OFF_SH // DG, DG, True)],
                    c("sconv_w", "gw2"), [(DG, BF)], tm=TM_MIX, name="sconv" + tag)
    return jnp.concatenate([y_a, y_b, y_c, y_d], axis=1), (q, k, v, gb, o, s_all, a, hs), xres


def _mixers_bwd(p, mp, saved, dy, tag="", xch=None):
    q, k, v, gb, o, s_all, a, hs = saved
    c = lambda *names: [(mp[n], False, True) for n in names]
    g = {}

    (do, dz), (g["gdn_nw"],) = _rowwise_vjp(
        _f_gdn_post, [(o, 0, DG, False, F32), (p, OFF_Z // DG, DG, False, BF)], c("gdn_nw"), [(dy, 0, DG)], tm=TM_MIX, name="gdn_post_b" + tag)
    (dq, dk, dv, dgb), xres = _gdn_bwd(q, k, v, gb, s_all, do, name="gdn_chunks_b" + tag, xch=xch)
    (dqkv, dba), (g["gdn_conv_w"], g["alog"], g["dtb"]) = _rowwise_vjp(
        _f_gdn_pre, [(p, 0, 1536, True, BF), (p, OFF_BA // 128, 128, False, BF)], c("gdn_conv_w", "alog", "dtb"),
        [(dq, 0, DG), (dk, 0, DG), (dv, 0, DG), (dgb, 0, 128)], tm=TM_MIX, name="gdn_pre_b" + tag)

    (dhs, dgate), (g["gw0"],) = _rowwise_vjp(
        _f_lru_post, [(hs, 0, DG, False, F32), (p, OFF_LG // DG, DG, False, BF)], c("gw0"), [(dy, 1, DG)], tm=TM_MIX, name="lru_post_b" + tag)
    da, db = _scan_bwd(a, hs, dhs, tm=TM_MIX, name="lru_scan_b" + tag)
    (dlx,), (g["lru_conv_w"], g["lru_conv_b"], g["wa"], g["ba"], g["wx"], g["bx"], g["lam"]) = _rowwise_vjp(
        _f_lru_ab, [(p, OFF_LX // DG, DG, True, BF)], c("lru_conv_w", "lru_conv_b", "wa", "ba", "wx", "bx", "lam"),
        [(da, 0, DG), (db, 0, DG)], tm=TM_MIX, name="lru_ab_b" + tag)

    (duv,), (g["ln_w"], g["ln_b"], g["ws"], g["bst"], g["gw1"]) = _rowwise_vjp(
        _f_sgu, [(p, OFF_UV // 1024, 1024, False, BF)], c("ln_w", "ln_b", "ws", "bst", "gw1"), [(dy, 2, DG)], tm=TM_MIX, name="sgu_b" + tag)

    (dsb, dsc, dsh), (g["sconv_w"], g["gw2"]) = _rowwise_vjp(
        _f_sconv, [(p, OFF_SB // DG, DG, False, BF), (p, OFF_SC // DG, DG, True, BF), (p, OFF_SH // DG, DG, True, BF)],
        c("sconv_w", "gw2"), [(dy, 3, DG)], tm=TM_MIX, name="sconv_b" + tag)

    dp = jnp.concatenate([dqkv, dz, dlx, dgate, duv, dsb, dsc, dsh, dba], axis=1)
    return dp, g, xres


WEIGHTS = ("pre_mix_norm", "w_in", "gdn_conv_w", "gdn_a_log", "gdn_dt_bias", "gdn_norm_w", "lru_conv_w", "lru_conv_b", "lru_wa",
           "lru_ba", "lru_wx", "lru_bx", "lru_lambda", "sgu_ln_w", "sgu_ln_b", "sgu_ws", "sgu_b", "sconv_w", "grp_norm_w", "w_out",
           "post_mix_norm", "pre_ffn_norm", "ffn_up", "ffn_conv_w", "ffn_conv_b", "ffn_down", "post_ffn_norm")
BIG = ("w_in", "ffn_up", "w_out", "ffn_down")
CHIP_SHARDED_SMALL = ("gdn_conv_w", "lru_conv_w", "sconv_w", "grp_norm_w", "ffn_conv_w")
MIXER_PARAMS = ("gdn_conv_w", "gdn_a_log", "gdn_dt_bias", "gdn_norm_w", "lru_conv_w", "lru_conv_b", "lru_wa", "lru_ba", "lru_wx",
                "lru_bx", "lru_lambda", "sgu_ln_w", "sgu_ln_b", "sgu_ws", "sgu_b", "sconv_w", "grp_norm_w")
TM_ROW = 256
N_FF_TILES = 11
FF_TILE = D_FF // N_FF_TILES
PACK_ROWS = 256
FFN_UP_CUTS = (512, 1536, 1872)
TK_DW = 4096


def _pack(arrs):
    parts = []
    for a in arrs:
        n = a.size
        parts.append(jnp.pad(a.reshape(-1), (0, -n % 1024)).reshape(-1, 128))
    rows = sum(p.shape[0] for p in parts)
    parts.append(jnp.zeros((-rows % PACK_ROWS, 128), F32))
    return jnp.concatenate(parts, axis=0)


def _unpack(buf, shapes):
    out, row = [], 0
    for s in shapes:
        n = math.prod(s)
        rows = -(-n // 1024) * 8
        out.append(buf[row:row + rows].reshape(-1)[:n].reshape(s))
        row += rows
    return out


def _row_tile(rows, cols):
    return 256 if rows % 256 == 0 and cols <= 1024 else 128


def _w_in_full(got):
    c = N_IN // 4
    z = jnp.zeros((D, NP - N_IN), got.dtype)
    return jnp.concatenate([got[0], got[1][:, :2048 - c], got[1][:, 2056 - c:], got[2], got[3], got[1][:, 2048 - c:2056 - c], z], axis=1)


def _w_in_blocks(dw):
    c = N_IN // 4
    cut = 2048 + 2 * c - 2056
    b1 = jnp.concatenate([dw[:, c:2048], dw[:, OFF_BA:OFF_BA + 8], dw[:, 2048:cut]], axis=1)
    return jnp.stack([dw[:, 0:c], b1, dw[:, cut:cut + c], dw[:, cut + c:OFF_BA]])


def _layer_fwd(l, xs, h, w_in_l, w_out_g, shard, sp, mp, nxt):
    t = str(l)
    up = shard["ffn_up"][l]
    c0, c1, c2 = FFN_UP_CUTS
    first = [up[:c0]] + ([shard["w_out"][l]] if w_out_g is None else [])
    p, got = _matmul(h, w_in_l, "nn", F32, tm=1024, tn=1920, tk=D, name="mm_in" + t, xch=_GatherChips(first))
    w_out_l = (got[1] if w_out_g is None else w_out_g).reshape(D, D)
    ycat, saved, (w_up_b,) = _mixers_fwd(p, mp, tag=t, xch=_GatherChips([up[c0:c1]]))
    y, (w_up_c,) = _matmul(ycat, w_out_l, "nn", F32, tm=1024, tn=1024, tk=D, name="mm_out" + t, xch=_GatherChips([up[c1:c2]]))
    (x1, h2), (w_up_d,) = _rowwise(
        _f_post_pre, [(xs, 0, D, False), (y, 0, D, False)], [(sp["post_mix_norm"][l], False), (sp["pre_ffn_norm"][l], False)],
        [(D, F32), (D, BF)], tm=TM_ROW, name="post_mix" + t, xch=_GatherChips([up[c2:]]))
    w_up = jnp.concatenate([got[0], w_up_b, w_up_c, w_up_d], axis=1)
    u, (w_dn_g,) = _matmul(h2, w_up, "nn", F32, tm=1024, tn=1408, tk=D, name="mm_up" + t, b_blocks=True,
                           xch=_GatherChips([shard["ffn_down"][l]]))
    ffn_rows = [(u, 0, FF_TILE, True), (u, N_FF_TILES, FF_TILE, True)]
    ffn_consts = [(c, True) for c in sp["ffn_conv"][l]]
    half = D // 2
    if nxt is None:
        act, = _rowwise(_f_ffn_act, ffn_rows, ffn_consts, [(FF_TILE, BF)], tm=TM_ROW, ncol=N_FF_TILES, name="ffn_act" + t)
    else:
        (act,), (n_out, n_in_a) = _rowwise(_f_ffn_act, ffn_rows, ffn_consts, [(FF_TILE, BF)], tm=TM_ROW, ncol=N_FF_TILES, name="ffn_act" + t,
                                           xch=_GatherChips([nxt[1], nxt[0][:half]]))
    w_dn_l = w_dn_g.reshape(D_FF, D)
    y2, n_in_b = _matmul(act, w_dn_l, "nn", F32, tm=1024, tn=1024, tk=2816, name="mm_down" + t,
                         xch=None if nxt is None else _GatherChips([nxt[0][half:]]))
    keep = dict(xs=xs, h=h, p=p, saved=saved, ycat=ycat, y=y, x1=x1, h2=h2, u=u, act=act, y2=y2,
                w_in=w_in_l, w_out=w_out_l, w_up=w_up, w_dn=w_dn_l)
    return keep, (None if nxt is None else (jnp.concatenate([n_in_a, n_in_b[0]], axis=1), n_out))


def _layer_bwd(l, a, dx1, dy2, sp, mp):
    t = str(l)
    g = {}
    dact, _ = _matmul(dy2, a["w_dn"], "nt", F32, tm=1024, tn=1408, tk=D, name="mm_down_dx" + t)
    dw_dn, _ = _matmul(a["act"], dy2, "tn", F32, tm=512, tn=1024, tk=TK_DW, name="mm_down_dw" + t)
    dw_dn = dw_dn.reshape(4, D_FF // 4, D)
    (dug, duv), gc = _rowwise_vjp(
        _f_ffn_act, [(a["u"], 0, FF_TILE, True, BF), (a["u"], N_FF_TILES, FF_TILE, True, BF)], [(c, True, True) for c in sp["ffn_conv"][l]],
        [(dact, 0, FF_TILE)], tm=TM_ROW, ncol=N_FF_TILES, name="ffn_act_b" + t)
    g["ffn_conv_w"] = jnp.concatenate([gc[0], gc[1]], axis=1)
    g["ffn_conv_b"] = jnp.concatenate([gc[2], gc[3]], axis=1)[0]
    du = jnp.concatenate([dug, duv], axis=1)
    dh2, (r_dn,) = _matmul(du, a["w_up"], "nt", F32, tm=1024, tn=1024, tk=2816, name="mm_up_dx" + t, b_blocks=True,
                           xch=_ScatterChips([dw_dn.astype(BF)]))
    dw_up, _ = _matmul(a["h2"], du, "tn", F32, tm=512, tn=1408, tk=TK_DW, name="mm_up_dw" + t, out_blocks=True)
    (dxs, dy), (gpm, gpf) = _rowwise_vjp(
        _f_post_pre, [(a["xs"], 0, D, False, F32), (a["y"], 0, D, False, BF)],
        [(sp["post_mix_norm"][l], False, True), (sp["pre_ffn_norm"][l], False, True)], [(dx1, 0, D), (dh2, 0, D)], tm=TM_ROW, name="post_mix_b" + t)
    g["post_mix_norm"], g["pre_ffn_norm"] = gpm[0], gpf[0]
    dycat, _ = _matmul(dy, a["w_out"], "nt", F32, tm=1024, tn=1024, tk=D, name="mm_out_dx" + t)
    dw_out, _ = _matmul(a["ycat"], dy, "tn", F32, tm=512, tn=1024, tk=TK_DW, name="mm_out_dw" + t)
    dw_out = dw_out.reshape(4, D // 4, D)
    dp, gm, (r_up,) = _mixers_bwd(a["p"], mp, a["saved"], dycat, tag=t, xch=_ScatterChips([dw_up.astype(BF)]))
    g.update(_mixer_param_grads(gm))
    dw_in, (r_out,) = _matmul(a["h"], dp, "tn", F32, tm=512, tn=1920, tk=TK_DW, name="mm_in_dw" + t, xch=_ScatterChips([dw_out.astype(BF)]))
    dw_in = _w_in_blocks(dw_in)
    dh, (r_in,) = _matmul(dp, a["w_in"], "nt", F32, tm=512, tn=1024, tk=NP, name="mm_in_dx" + t, xch=_ScatterChips([dw_in.astype(BF)]))
    big = {"ffn_down": (dw_dn, r_dn), "ffn_up": (dw_up, r_up), "w_out": (dw_out, r_out), "w_in": (dw_in, r_in)}
    return dxs, dh, g, big


def kernel(x, pre_mix_norm, w_in, gdn_conv_w, gdn_a_log, gdn_dt_bias, gdn_norm_w, lru_conv_w, lru_conv_b, lru_wa, lru_ba, lru_wx, lru_bx, lru_lambda, sgu_ln_w, sgu_ln_b, sgu_ws, sgu_b, sconv_w, grp_norm_w, w_out, post_mix_norm, pre_ffn_norm, ffn_up, ffn_conv_w, ffn_conv_b, ffn_down, post_ffn_norm, loss_target, m_pre_mix_norm, m_w_in, m_gdn_conv_w, m_gdn_a_log, m_gdn_dt_bias, m_gdn_norm_w, m_lru_conv_w, m_lru_conv_b, m_lru_wa, m_lru_ba, m_lru_wx, m_lru_bx, m_lru_lambda, m_sgu_ln_w, m_sgu_ln_b, m_sgu_ws, m_sgu_b, m_sconv_w, m_grp_norm_w, m_w_out, m_post_mix_norm, m_pre_ffn_norm, m_ffn_up, m_ffn_conv_w, m_ffn_conv_b, m_ffn_down, m_post_ffn_norm, v_pre_mix_norm, v_w_in, v_gdn_conv_w, v_gdn_a_log, v_gdn_dt_bias, v_gdn_norm_w, v_lru_conv_w, v_lru_conv_b, v_lru_wa, v_lru_ba, v_lru_wx, v_lru_bx, v_lru_lambda, v_sgu_ln_w, v_sgu_ln_b, v_sgu_ws, v_sgu_b, v_sconv_w, v_grp_norm_w, v_w_out, v_post_mix_norm, v_pre_ffn_norm, v_ffn_up, v_ffn_conv_w, v_ffn_conv_b, v_ffn_down, v_post_ffn_norm):
    given = dict(locals())
    me = 2 * lax.axis_index("x") + lax.axis_index("y")
    xs0, tgt = x[0], loss_target[0]

    small_sh = [given[n] for n in CHIP_SHARDED_SMALL]
    shard = {n: [given[n][l].astype(BF) for l in range(DEPTH)] for n in BIG}
    got = _exchange(_GatherChipsTwoLevel([shard["w_in"][0], _pack(small_sh)]), name="gather_first")
    full = {n: given[n] for n in WEIGHTS if n not in BIG and n not in CHIP_SHARDED_SMALL}
    per_chip = [_unpack(got[1][j], [s.shape for s in small_sh]) for j in range(4)]
    parts = [jnp.stack([per_chip[j][i] for j in range(4)]) for i in range(len(small_sh))]
    for n, pj in zip(CHIP_SHARDED_SMALL, parts):
        full[n] = pj.transpose(1, 2, 0, 3).reshape(pj.shape[1], pj.shape[2], 4 * pj.shape[3])
    sp = {n: [full[n][l:l + 1] for l in range(DEPTH)] for n in ("pre_mix_norm", "post_mix_norm", "pre_ffn_norm", "post_ffn_norm")}
    sp["ffn_conv"] = [[full["ffn_conv_w"][l][:, :D_FF], full["ffn_conv_w"][l][:, D_FF:], full["ffn_conv_b"][l:l + 1, :D_FF],
                       full["ffn_conv_b"][l:l + 1, D_FF:]] for l in range(DEPTH)]
    mps = [_mixer_params({n: full[n][l] for n in MIXER_PARAMS}) for l in range(DEPTH)]

    h, = _rowwise(_f_pre, [(xs0, 0, D, False)], [(sp["pre_mix_norm"][0], False)], [(D, BF)], tm=TM_ROW, name="pre_mix0")
    a0, (w_in1, w_out1) = _layer_fwd(0, xs0, h, _w_in_full(got[0]), None, shard, sp, mps[0], (shard["w_in"][1], shard["w_out"][1]))
    xs1, h1 = _rowwise(_f_post_pre, [(a0["x1"], 0, D, False), (a0["y2"], 0, D, False)],
                       [(sp["post_ffn_norm"][0], False), (sp["pre_mix_norm"][1], False)], [(D, F32), (D, BF)], tm=TM_ROW, name="post_ffn0")
    a1, _ = _layer_fwd(1, xs1, h1, _w_in_full(w_in1), w_out1, shard, sp, mps[1], None)
    lacc, dxo = _loss_head(a1["x1"], a1["y2"], sp["post_ffn_norm"][1], tgt, tm=TM_ROW, name="loss_head")

    gl = [None, None]
    (dx1, dy2), (gpf1,) = _rowwise_vjp(_f_post, [(a1["x1"], 0, D, False, F32), (a1["y2"], 0, D, False, BF)],
                                      [(sp["post_ffn_norm"][1], False, True)], [(dxo, 0, D)], tm=TM_ROW, name="post_ffn1_b")
    big = [None, None]
    dxs1, dh1, gl[1], big[1] = _layer_bwd(1, a1, dx1, dy2, sp, mps[1])
    gl[1]["post_ffn_norm"] = gpf1[0]
    (dx1, dy2), (gpf0, gpm1) = _rowwise_vjp(
        _f_post_pre, [(a0["x1"], 0, D, False, F32), (a0["y2"], 0, D, False, BF)],
        [(sp["post_ffn_norm"][0], False, True), (sp["pre_mix_norm"][1], False, True)], [(dxs1, 0, D), (dh1, 0, D)], tm=TM_ROW, name="post_ffn0_b")
    gl[1]["pre_mix_norm"] = gpm1[0]
    dxs0, dh0, gl[0], big[0] = _layer_bwd(0, a0, dx1, dy2, sp, mps[0])
    gl[0]["post_ffn_norm"] = gpf0[0]
    (grad_x,), (gpm0,) = _rowwise_vjp(lambda xv, w: (xv, _rms(xv, w)), [(xs0, 0, D, False, F32)], [(sp["pre_mix_norm"][0], False, True)],
                                     [(dxs0, 0, D), (dh0, 0, D)], tm=TM_ROW, name="pre_mix0_b")
    gl[0]["pre_mix_norm"] = gpm0[0]

    small = [n for n in WEIGHTS if n not in BIG]
    gfull = {n: jnp.stack([gl[0][n], gl[1][n]]) for n in small}
    small_grads = _pack([gfull[n] for n in small] + [lacc[0, 0:1]])
    sums = {}
    for n in BIG:
        for l in range(DEPTH):
            blocks, recv = big[l][n]
            own = lax.dynamic_index_in_dim(blocks, me, 0, keepdims=False)
            sums[n] = sums.get(n, []) + [_sum_parts(own, recv, tr=_row_tile(*own.shape), name="sum_grads_%s%d" % (n, l))]
    kinds = ("grad", "delta", "new_m", "new_v")
    outs = {kind: {} for kind in kinds}
    order = ("w_out", "ffn_down", "w_in", "ffn_up")
    other = _exchange(_SwapCores(sums[order[0]]), name="swap_first")
    gathered = None
    for i, n in enumerate(order):
        riders = []
        if i + 1 < len(order):
            riders.append(_SwapCores(sums[order[i + 1]]))
        if n == "ffn_down":
            riders.append(_GatherDevices(small_grads))
        shp = given[n].shape
        r4, got = _adamw(given[n], given["m_" + n], given["v_" + n], [[s, o] for s, o in zip(sums[n], other)],
                         tr=_row_tile(shp[1], shp[2]), name="adamw_" + n, xch=_Together(riders) if riders else None)
        outs_n = dict(zip(kinds, r4))
        for kind in kinds:
            outs[kind][n] = outs_n[kind]
        if i + 1 < len(order):
            other = got[:DEPTH]
        if n == "ffn_down":
            gathered = got[DEPTH]

    tot = _sum_slots(gathered, name="sum_small_grads")
    red = dict(zip(small + ["loss"], _unpack(tot, [gfull[n].shape for n in small] + [(1,)])))
    for n in CHIP_SHARDED_SMALL:
        cb = given[n].shape[-1]
        red[n] = lax.dynamic_slice_in_dim(red[n], me * cb, cb, axis=red[n].ndim - 1)
    shapes = [given[n].shape for n in small]
    res, _ = _adamw(_pack([given[n] for n in small])[None], _pack([given["m_" + n] for n in small])[None],
                    _pack([given["v_" + n] for n in small])[None], [[_pack([red[n] for n in small])]], tr=PACK_ROWS, name="adamw_small")
    for kind, r in zip(kinds, res):
        outs[kind].update(zip(small, _unpack(r[0], shapes)))

    return (red["loss"][0], grad_x[None], *[outs[k][n] for k in ("grad", "delta", "new_m", "new_v") for n in WEIGHTS])
```

```python
import functools
import math

import jax
import jax.numpy as jnp
from jax import lax
from jax.experimental import pallas as pl
from jax.experimental.pallas import tpu as pltpu

F32 = jnp.float32
BF = jnp.bfloat16
MESH_ID = pl.DeviceIdType.MESH

EPS = 1e-6
DEPTH = 2
D = 2048
DG = 512
HEADS = 4
HD = 128
CHUNK = 64
GDN_STEP = 4
LRU_C = 8.0
D_FF = 5632
N_IN = 5640
NP = 5760
OFF_Q, OFF_Z, OFF_LX, OFF_LG, OFF_UV, OFF_SB, OFF_SC, OFF_SH, OFF_BA = 0, 1536, 2048, 2560, 3072, 4096, 4608, 5120, 5632

ADAM_LR, ADAM_B1, ADAM_B2, ADAM_EPS, ADAM_WD, ADAM_STEP = 0.001, 0.9, 0.999, 1e-08, 0.01, 10

HALO = 8


def _mk_bdot(ca, cb):
    na, nb = 1 - ca, 1 - cb

    def dg(x, y, cx, cy):
        return lax.dot_general(x.astype(BF), y.astype(BF), (((cx,), (cy,)), ((), ())), preferred_element_type=F32)

    @jax.custom_vjp
    def f(a, b):
        return dg(a, b, ca, cb)

    def fwd(a, b):
        return dg(a, b, ca, cb), (a, b)

    def bwd(res, g):
        a, b = res
        da = dg(g, b, 1, nb) if ca == 1 else dg(b, g, nb, 1)
        db = dg(a, g, na, 0) if cb == 0 else dg(g, a, 0, na)
        return da.astype(a.dtype), db.astype(b.dtype)

    f.defvjp(fwd, bwd)
    return f


_bdot = _mk_bdot(1, 0)
_bdot_nt = _mk_bdot(1, 1)
_bdot_tn = _mk_bdot(0, 0)


def _sigmoid(x):
    return 1.0 / (1.0 + jnp.exp(-x))


def _silu(x):
    return x * _sigmoid(x)


def _gelu(x):
    return 0.5 * x * (1.0 + jnp.tanh(0.7978845608028654 * (x + 0.044715 * (x * x * x))))


def _log1p(z):
    u = 1.0 + z
    d = u - 1.0
    return jnp.where(d == 0.0, z, jnp.log(u) * (z / jnp.where(d == 0.0, 1.0, d)))


def _softplus(x):
    return jnp.maximum(x, 0.0) + _log1p(jnp.exp(-jnp.abs(x)))


def _neg_expm1(y):
    t = jnp.tanh(0.5 * y)
    return -2.0 * t / (1.0 - t)


def _rms(x, w):
    return x * lax.rsqrt(jnp.mean(x * x, axis=-1, keepdims=True) + EPS) * w


def _rows(x):
    return lax.broadcasted_iota(jnp.int32, x.shape, 0)


def _mk_shift():
    @functools.partial(jax.custom_vjp, nondiff_argnums=(1,))
    def shift(xx, s):
        n = xx.shape[0]
        return pltpu.roll(xx, s, axis=0)[HALO:n] if s else xx[HALO:n]

    def fwd(xx, s):
        return shift(xx, s), None

    def bwd(s, _, g):
        ext = jnp.concatenate([g, jnp.zeros((HALO, g.shape[1]), g.dtype)], axis=0)
        return (pltpu.roll(ext, HALO - s, axis=0),)

    shift.defvjp(fwd, bwd)
    return shift


_shift = _mk_shift()


def _mk_chunk_cumsum():
    def run(x, up):
        n = x.shape[0]
        pos = _rows(x) % CHUNK
        d = 1
        while d < CHUNK:
            if up:
                x = x + jnp.where(pos < CHUNK - d, pltpu.roll(x, n - d, axis=0), 0.0)
            else:
                x = x + jnp.where(pos >= d, pltpu.roll(x, d, axis=0), 0.0)
            d *= 2
        return x

    @jax.custom_vjp
    def cumsum(x):
        return run(x, False)

    cumsum.defvjp(lambda x: (run(x, False), None), lambda _, g: (run(g, True),))
    return cumsum


_chunk_cumsum = _mk_chunk_cumsum()


def _causal_conv(xx, w):
    K = w.shape[0]
    y = _shift(xx, K - 1) * w[0:1, :]
    for k in range(1, K):
        y = y + _shift(xx, K - 1 - k) * w[k:k + 1, :]
    return y


def _f_pre(x, w):
    return (_rms(x, w),)


def _f_post_pre(x, y, w_post, w_pre):
    x1 = x + _rms(y, w_post)
    return x1, _rms(x1, w_pre)


def _f_post(x, y, w_post):
    return (x + _rms(y, w_post),)


def _heads(fn, *xs):
    return jnp.concatenate([fn(*[x[:, h * HD:(h + 1) * HD] for x in xs]) for h in range(HEADS)], axis=1)


def _l2n(t):
    return t * lax.rsqrt(jnp.sum(t * t, axis=-1, keepdims=True) + EPS)


def _f_gdn_pre(qkv, ba, conv_w, alog, dtb):
    c = _silu(_causal_conv(qkv, conv_w))
    q = _heads(lambda t: _l2n(t) * (HD ** -0.5), c[:, 0:DG])
    k = _heads(_l2n, c[:, DG:2 * DG])
    v = c[:, 2 * DG:3 * DG]
    lane = lax.broadcasted_iota(jnp.int32, ba.shape, 1)
    beta = _sigmoid(ba)
    gcum = _chunk_cumsum(-jnp.exp(alog) * _softplus(ba + dtb))
    gc = jnp.where(lane < HEADS, beta, jnp.where(lane < 2 * HEADS, gcum, 0.0))
    return q, k, v, gc


def _f_gdn_chunk(q, k, v, gc, s0):
    C = q.shape[0]
    HC = HEADS * C
    sh = C.bit_length() - 1

    def stack(x):
        return jnp.concatenate([x[:, h * HD:(h + 1) * HD] for h in range(HEADS)], axis=0)

    def column(off):
        lane = lax.broadcasted_iota(jnp.int32, gc.shape, 1)
        return jnp.concatenate([jnp.sum(jnp.where(lane == off + h, gc, 0.0), axis=1, keepdims=True) for h in range(HEADS)], axis=0)

    def head(x, h):
        return x[h * C:(h + 1) * C]

    r = lax.broadcasted_iota(jnp.int32, (HC, HC), 0)
    c = lax.broadcasted_iota(jnp.int32, (HC, HC), 1)
    same = jnp.right_shift(r, sh) == jnp.right_shift(c, sh)
    causal = jnp.logical_and(same, r >= c)
    strict = jnp.logical_and(same, r > c)
    gcol, bcol = column(HEADS), column(0)
    grow = jnp.sum(jnp.where(r == c, gcol, 0.0), axis=0, keepdims=True)
    decay = jnp.where(causal, jnp.exp(jnp.where(causal, gcol - grow, 0.0)), 0.0)
    ks, qs, vs = stack(k), stack(q), stack(v)
    kb = ks * bcol
    kk = _bdot_nt(jnp.concatenate([kb, qs], axis=0), ks)
    m = jnp.where(strict, kk[0:HC] * decay, 0.0)
    attn = jnp.where(causal, kk[HC:2 * HC] * decay, 0.0)
    n = -m
    t = (r == c).astype(F32) + n
    p = n
    for _ in range(5):
        p = _bdot(p, p)
        t = t + _bdot(t, p)
    eg = jnp.exp(gcol)
    wu = _bdot(t, jnp.concatenate([kb * eg, vs * bcol], axis=1))
    w, u = wu[:, 0:HD], wu[:, HD:2 * HD]
    last = jnp.logical_and(same, jnp.bitwise_and(c, C - 1) == C - 1)
    glast = jnp.sum(jnp.where(last, grow, 0.0), axis=1, keepdims=True)
    k_g = ks * jnp.exp(glast - gcol)
    q_g = qs * eg
    ws = [_bdot(jnp.concatenate([head(w, h), head(q_g, h)], axis=0), s0[h * HD:(h + 1) * HD]) for h in range(HEADS)]
    v_new = u - jnp.concatenate([x[0:C] for x in ws], axis=0)
    o = jnp.concatenate([x[C:2 * C] for x in ws], axis=0) + _bdot(attn, v_new)
    s1 = [s0[h * HD:(h + 1) * HD] * jnp.exp(glast[h * C:h * C + 1]) + _bdot_tn(head(k_g, h), head(v_new, h)) for h in range(HEADS)]
    return jnp.concatenate([head(o, h) for h in range(HEADS)], axis=1), jnp.concatenate(s1, axis=0)


def _f_gdn_chunks(q, k, v, gc, s0):
    outs, s = [], s0
    for n in range(q.shape[0] // CHUNK):
        rs = slice(n * CHUNK, (n + 1) * CHUNK)
        o, s = _f_gdn_chunk(q[rs], k[rs], v[rs], gc[rs], s)
        outs.append(o)
    return jnp.concatenate(outs, axis=0), s


def _f_gdn_post(o, z, nw):
    return (_heads(lambda a, b: _rms(a, nw) * _silu(b), o, z),)


def _f_lru_ab(lx, conv_w, conv_b, wa, ba, wx, bx, lam):
    xc = _causal_conv(lx, conv_w) + conv_b
    r = _sigmoid(_bdot(xc, wa) + ba)
    i = _sigmoid(_bdot(xc, wx) + bx)
    log_a = -LRU_C * r * _softplus(-lam)
    a = jnp.exp(log_a)
    mult = jnp.sqrt(_neg_expm1(2.0 * log_a))
    return a, mult * (i * xc)


def _f_lru_post(hs, gate, gw):
    return (_rms(hs * _gelu(gate), gw),)


def _f_sgu(uv, ln_w, ln_b, ws, bst, gw):
    tm = uv.shape[0]
    uvf = _gelu(uv)
    u, v = uvf[:, 0:DG], uvf[:, DG:2 * DG]
    mu = jnp.mean(v, axis=-1, keepdims=True)
    vc = v - mu
    v = vc * lax.rsqrt(jnp.mean(vc * vc, axis=-1, keepdims=True) + EPS) * ln_w + ln_b
    lane = lax.broadcasted_iota(jnp.int32, bst.shape, 1)
    tril = lax.broadcasted_iota(jnp.int32, (128, 128), 0) >= lax.broadcasted_iota(jnp.int32, (128, 128), 1)
    wsm = [jnp.where(tril, ws[g * 128:(g + 1) * 128, :], 0.0) for g in range(4)]
    bias = [jnp.sum(jnp.where(lane == g, bst, 0.0), axis=1, keepdims=True) for g in range(4)]
    out = []
    for n in range(tm // 128):
        vn = v[n * 128:(n + 1) * 128, :]
        gs = [_bdot(wsm[g], vn[:, g * 128:(g + 1) * 128]) + bias[g] for g in range(4)]
        out.append(jnp.concatenate(gs, axis=1))
    vo = jnp.concatenate(out, axis=0) if len(out) > 1 else out[0]
    return (_rms(u * vo, gw),)


def _f_sconv(sb, sc, sh, conv_w, gw):
    return (_rms(sb * _causal_conv(sc * sh, conv_w), gw),)


def _f_ffn_act(ug, uv, wg, wv, bg, bv):
    return (_gelu(_causal_conv(ug, wg) + bg) * (_causal_conv(uv, wv) + bv),)


def _row_specs(rows, consts, tm, ncol, tile_of):
    specs, args = [], []
    for arr, cb, w, halo in rows:
        specs.append(pl.BlockSpec((tm, w), lambda j, i, cb=cb: (tile_of(i), cb + j)))
        args.append(arr)
        if halo:
            specs.append(pl.BlockSpec((HALO, w), lambda j, i, cb=cb: (jnp.maximum(tile_of(i) * (tm // HALO) - 1, 0), cb + j)))
            args.append(arr)
    for arr, tiled in consts:
        r, c = arr.shape
        specs.append(pl.BlockSpec((r, c // ncol), lambda j, i: (0, j)) if tiled else pl.BlockSpec((r, c), lambda j, i: (0, 0)))
        args.append(arr)
    return specs, args


def _load_rows(refs, rows, consts, tile):
    k, vals = 0, []
    for _arr, _cb, _w, halo in rows:
        t = refs[k][...].astype(F32)
        k += 1
        if halo:
            hl = jnp.where(tile > 0, refs[k][...].astype(F32), 0.0)
            k += 1
            t = jnp.concatenate([hl, t], axis=0)
        vals.append(t)
    for _ in consts:
        vals.append(refs[k][...])
        k += 1
    return vals, k


def _rowwise(fn, rows, consts, outs, *, tm, name, ncol=1, xch=None):
    n = rows[0][0].shape[0]
    nt = n // tm
    in_specs, args = _row_specs(rows, consts, tm, ncol, lambda i: i)

    def body(*refs):
        vals, k = _load_rows(refs, rows, consts, pl.program_id(1))
        for o_ref, r in zip(refs[k:], fn(*vals)):
            o_ref[...] = r.astype(o_ref.dtype)

    res, xres = _call(
        body, grid=(ncol, nt), in_specs=in_specs,
        out_specs=[pl.BlockSpec((tm, w), lambda j, i: (i, j)) for w, _ in outs],
        out_shape=[jax.ShapeDtypeStruct((n, w * ncol), dt) for w, dt in outs], scratch=[], name=name, args=args, xch=xch)
    return res if xch is None else (res, xres)


def _rowwise_vjp(fn, rows, consts, cots, *, tm, name, ncol=1):
    n = rows[0][0].shape[0]
    nt = n // tm
    rows4 = [r[:4] for r in rows]
    consts2 = [c[:2] for c in consts]
    in_specs, args = _row_specs(rows4, consts2, tm, ncol, lambda i: nt - 1 - i)
    for arr, cb, w in cots:
        in_specs.append(pl.BlockSpec((tm, w), lambda j, i, cb=cb: (nt - 1 - i, cb + j)))
        args.append(arr)
    out_specs, out_shape, scratch = [], [], []
    for arr, cb, w, halo, gdt in rows:
        if gdt is not None:
            out_specs.append(pl.BlockSpec((tm, w), lambda j, i: (nt - 1 - i, j)))
            out_shape.append(jax.ShapeDtypeStruct((n, w * ncol), gdt))
            if halo:
                scratch.append(pltpu.VMEM((HALO, w), F32))
    for arr, tiled, want in consts:
        if want:
            r, c = arr.shape
            out_specs.append(pl.BlockSpec((r, c // ncol), lambda j, i: (0, j)) if tiled else pl.BlockSpec((r, c), lambda j, i: (0, 0)))
            out_shape.append(jax.ShapeDtypeStruct((r, c), F32))
    n_in = len(in_specs)
    n_out = len(out_specs)

    def body(*refs):
        j, i = pl.program_id(0), pl.program_id(1)
        tile = nt - 1 - i
        vals, k = _load_rows(refs, rows4, consts2, tile)
        cvals = [refs[k + q][...].astype(F32) for q in range(len(cots))]
        outs = refs[n_in:n_in + n_out]
        carries = refs[n_in + n_out:]
        _, vjp = jax.vjp(fn, *vals)
        g = vjp(tuple(cvals))
        o, cidx = 0, 0
        for q, (arr, cb, w, halo, gdt) in enumerate(rows):
            if gdt is None:
                continue
            if halo:
                ge = g[q]
                main = ge[HALO:]
                carry = carries[cidx]
                cidx += 1
                tail = main[tm - HALO:] + jnp.where(i > 0, carry[...], 0.0)
                outs[o][0:tm - HALO, :] = main[0:tm - HALO].astype(gdt)
                outs[o][tm - HALO:tm, :] = tail.astype(gdt)
                carry[...] = ge[0:HALO]
            else:
                outs[o][...] = g[q].astype(gdt)
            o += 1
        for q, (arr, tiled, want) in enumerate(consts):
            if not want:
                continue
            first = (i == 0) if tiled else jnp.logical_and(i == 0, j == 0)
            acc = outs[o]
            gq = g[len(rows) + q].astype(F32)

            @pl.when(first)
            def _(acc=acc, gq=gq):
                acc[...] = gq

            @pl.when(jnp.logical_not(first))
            def _(acc=acc, gq=gq):
                acc[...] += gq

            o += 1

    res = pl.pallas_call(
        body, grid=(ncol, nt), in_specs=in_specs, out_specs=out_specs, out_shape=out_shape, scratch_shapes=scratch,
        name=name, compiler_params=pltpu.CompilerParams(dimension_semantics=("arbitrary", "arbitrary")),
    )(*args)
    nrow = sum(1 for r in rows if r[4] is not None)
    return list(res[:nrow]), list(res[nrow:])


_ANY = pl.BlockSpec(memory_space=pl.ANY)
_CHIP_REL = ((1, 0), (0, 1), (1, 1))
_DEV_REL = tuple((r >> 2 & 1, r >> 1 & 1, r & 1) for r in range(1, 8))


def _flip(v, f):
    return 1 - v if f else v


class _GatherChips:
    def __init__(self, shards):
        self.arrs = list(shards)
        n = len(self.arrs)
        self.out_shape = [jax.ShapeDtypeStruct((4,) + s.shape, s.dtype) for s in self.arrs]
        self.sems = [pltpu.SemaphoreType.DMA((3 * n,)), pltpu.SemaphoreType.DMA((3 * n,)), pltpu.SemaphoreType.DMA((n,))]

    def _copies(self, ins, outs, sems, arriving):
        send, recv, lsem = sems
        x, y, c = lax.axis_index("x"), lax.axis_index("y"), lax.axis_index("c")
        me = 2 * x + y
        if arriving:
            local = []
        else:
            local = [pltpu.make_async_copy(ins[a], outs[a].at[me], lsem.at[a]) for a in range(len(ins))]
        remote = []
        for a in range(len(ins)):
            for k, (fx, fy) in enumerate(_CHIP_REL):
                px, py = _flip(x, fx), _flip(y, fy)
                remote.append(pltpu.make_async_remote_copy(
                    src_ref=ins[a], dst_ref=outs[a].at[2 * px + py if arriving else me], send_sem=send.at[3 * a + k],
                    recv_sem=recv.at[3 * a + k], device_id=(px, py, c), device_id_type=MESH_ID))
        return local, remote

    def issue(self, ins, outs, sems):
        local, push = self._copies(ins, outs, sems, False)
        for cp in local + push:
            cp.start()

    def finish(self, ins, outs, sems):
        for cp in self._copies(ins, outs, sems, True)[1]:
            cp.wait_recv()
        local, push = self._copies(ins, outs, sems, False)
        for cp in push:
            cp.wait_send()
        for cp in local:
            cp.wait()


class _GatherChipsTwoLevel:
    def __init__(self, shards):
        self.arrs = list(shards)
        n = len(self.arrs)
        self.out_shape = [jax.ShapeDtypeStruct((4,) + s.shape, s.dtype) for s in self.arrs]
        self.sems = [pltpu.SemaphoreType.DMA((3 * n,)) for _ in range(4)] + [pltpu.SemaphoreType.DMA((n,))]

    def _far(self, ins, outs, sems, arriving):
        send, recv = sems[0], sems[1]
        x, y, c = lax.axis_index("x"), lax.axis_index("y"), lax.axis_index("c")
        me = 2 * x + y
        copies = []
        for a in range(len(ins)):
            half = ins[a].shape[0] // 2
            rows = pl.ds(c * half, half)
            for k, (fx, fy) in enumerate(_CHIP_REL):
                px, py = _flip(x, fx), _flip(y, fy)
                copies.append(pltpu.make_async_remote_copy(
                    src_ref=ins[a].at[rows], dst_ref=outs[a].at[2 * px + py if arriving else me, rows], send_sem=send.at[3 * a + k],
                    recv_sem=recv.at[3 * a + k], device_id=(px, py, c), device_id_type=MESH_ID))
        return copies

    def _near(self, ins, outs, sems, arriving):
        send, recv = sems[2], sems[3]
        x, y, c = lax.axis_index("x"), lax.axis_index("y"), lax.axis_index("c")
        copies = []
        for a in range(len(ins)):
            half = ins[a].shape[0] // 2
            rows = pl.ds((1 - c if arriving else c) * half, half)
            for k, (fx, fy) in enumerate(_CHIP_REL):
                block = outs[a].at[2 * _flip(x, fx) + _flip(y, fy), rows]
                copies.append(pltpu.make_async_remote_copy(
                    src_ref=block, dst_ref=block, send_sem=send.at[3 * a + k], recv_sem=recv.at[3 * a + k],
                    device_id=(x, y, 1 - c), device_id_type=MESH_ID))
        return copies

    def _local(self, ins, outs, sems):
        me = 2 * lax.axis_index("x") + lax.axis_index("y")
        return [pltpu.make_async_copy(ins[a], outs[a].at[me], sems[4].at[a]) for a in range(len(ins))]

    def issue(self, ins, outs, sems):
        for cp in self._local(ins, outs, sems) + self._far(ins, outs, sems, False):
            cp.start()

    def finish(self, ins, outs, sems):
        landed = self._far(ins, outs, sems, True)
        onward = self._near(ins, outs, sems, False)
        for cp, fwd in zip(landed, onward):
            cp.wait_recv()
            fwd.start()
        for cp in self._near(ins, outs, sems, True):
            cp.wait_recv()
        for cp in self._far(ins, outs, sems, False) + onward:
            cp.wait_send()
        for cp in self._local(ins, outs, sems):
            cp.wait()


class _ScatterChips:
    def __init__(self, blocks):
        self.arrs = list(blocks)
        n = len(self.arrs)
        self.out_shape = [jax.ShapeDtypeStruct((3,) + b.shape[1:], b.dtype) for b in self.arrs]
        self.sems = [pltpu.SemaphoreType.DMA((3 * n,)), pltpu.SemaphoreType.DMA((3 * n,))]

    def _copies(self, ins, outs, sems):
        send, recv = sems
        x, y, c = lax.axis_index("x"), lax.axis_index("y"), lax.axis_index("c")
        copies = []
        for a in range(len(ins)):
            for k, (fx, fy) in enumerate(_CHIP_REL):
                px, py = _flip(x, fx), _flip(y, fy)
                copies.append(pltpu.make_async_remote_copy(
                    src_ref=ins[a].at[2 * px + py], dst_ref=outs[a].at[k], send_sem=send.at[3 * a + k], recv_sem=recv.at[3 * a + k],
                    device_id=(px, py, c), device_id_type=MESH_ID))
        return copies

    def issue(self, ins, outs, sems):
        for cp in self._copies(ins, outs, sems):
            cp.start()

    def finish(self, ins, outs, sems):
        copies = self._copies(ins, outs, sems)
        for cp in copies:
            cp.wait_recv()
        for cp in copies:
            cp.wait_send()


def _exchange(xch, *, name):
    ni, no = len(xch.arrs), len(xch.out_shape)

    def body(*refs):
        ins, outs, sems = refs[:ni], refs[ni:ni + no], refs[ni + no:]
        xch.issue(ins, outs, sems)
        xch.finish(ins, outs, sems)

    return pl.pallas_call(body, in_specs=[_ANY] * ni, out_specs=[_ANY] * no, out_shape=xch.out_shape, scratch_shapes=xch.sems, name=name)(*xch.arrs)


def _call(body, *, grid, in_specs, out_specs, out_shape, scratch, name, args, xch=None):
    params = pltpu.CompilerParams(dimension_semantics=("arbitrary",) * len(grid))
    if xch is None:
        res = pl.pallas_call(body, grid=grid, in_specs=in_specs, out_specs=out_specs, out_shape=out_shape, scratch_shapes=scratch,
                             name=name, compiler_params=params)(*args)
        return list(res), []
    n_in, n_out, n_sc = len(in_specs), len(out_specs), len(scratch)
    xi, xo = len(xch.arrs), len(xch.out_shape)

    def wrapped(*refs):
        ins, refs = refs[:n_in], refs[n_in:]
        xin, refs = refs[:xi], refs[xi:]
        outs, refs = refs[:n_out], refs[n_out:]
        xout, refs = refs[:xo], refs[xo:]
        sc, sems = refs[:n_sc], refs[n_sc:]
        first = functools.reduce(jnp.logical_and, [pl.program_id(d) == 0 for d in range(len(grid))])
        last = functools.reduce(jnp.logical_and, [pl.program_id(d) == grid[d] - 1 for d in range(len(grid))])

        @pl.when(first)
        def _():
            xch.issue(xin, xout, sems)

        body(*ins, *outs, *sc)

        @pl.when(last)
        def _():
            xch.finish(xin, xout, sems)

    res = pl.pallas_call(
        wrapped, grid=grid, in_specs=list(in_specs) + [_ANY] * xi, out_specs=list(out_specs) + [_ANY] * xo,
        out_shape=list(out_shape) + xch.out_shape, scratch_shapes=list(scratch) + xch.sems, name=name, compiler_params=params,
    )(*args, *xch.arrs)
    return list(res[:n_out]), list(res[n_out:])


def _matmul(a, b, mode, out_dtype, *, tm, tn, tk, name, b_blocks=False, out_blocks=False, xch=None):
    pair = None
    if isinstance(a, tuple):
        assert mode == "nt"
        pair, a = a, jax.ShapeDtypeStruct((a[0].shape[0], 2 * a[0].shape[1]), a[0].dtype)
    if isinstance(b, tuple):
        assert mode == "tn" and not b_blocks
        pair, b = b, jax.ShapeDtypeStruct((b[0].shape[0], 2 * b[0].shape[1]), b[0].dtype)
    if b_blocks:
        _, br, bc4 = b.shape
        b2 = (br, 4 * bc4)
    else:
        b2 = b.shape

    def bspec(shape, index):
        if not b_blocks:
            return pl.BlockSpec(shape, index)
        per = bc4 // shape[1]

        def blocked(i, j, k):
            r, c = index(i, j, k)
            return (c // per, r, c % per)

        return pl.BlockSpec((None,) + shape, blocked)

    tm = min(tm, a.shape[1] if mode == "tn" else a.shape[0])
    tk = min(tk, a.shape[0] if mode == "tn" else a.shape[1])
    if mode == "tn":
        K, M = a.shape
        N = b2[1]
        a_spec = pl.BlockSpec((tk, tm), lambda i, j, k: (k, i))
        b_spec = bspec((tk, tn), lambda i, j, k: (k, j))
        dims = (((0,), (0,)), ((), ()))
    elif mode == "nt":
        M, K = a.shape
        N = b2[0]
        a_spec = pl.BlockSpec((tm, tk), lambda i, j, k: (i, k))
        b_spec = bspec((tn, tk), lambda i, j, k: (j, k))
        dims = (((1,), (1,)), ((), ()))
    else:
        M, K = a.shape
        N = b2[1]
        a_spec = pl.BlockSpec((tm, tk), lambda i, j, k: (i, k))
        b_spec = bspec((tk, tn), lambda i, j, k: (k, j))
        dims = (((1,), (0,)), ((), ()))
    assert M % tm == 0 and N % tn == 0 and K % tk == 0, (name, M, N, K, tm, tn, tk)
    nk = K // tk
    if out_blocks:
        per_o = (N // 4) // tn
        o_spec = pl.BlockSpec((None, tm, tn), lambda i, j, k: (j // per_o, i, j % per_o))
        o_shape = jax.ShapeDtypeStruct((4, M, N // 4), out_dtype)
    else:
        o_spec = pl.BlockSpec((tm, tn), lambda i, j, k: (i, j))
        o_shape = jax.ShapeDtypeStruct((M, N), out_dtype)

    def step(a_ref, b_ref, o_ref, acc_ref):
        k = pl.program_id(2)
        part = lax.dot_general(a_ref[...].astype(BF), b_ref[...].astype(BF), dims, preferred_element_type=F32)
        if nk == 1:
            o_ref[...] = part.astype(o_ref.dtype)
        else:
            @pl.when(k == 0)
            def _():
                acc_ref[...] = part

            @pl.when(k > 0)
            def _():
                acc_ref[...] += part

            @pl.when(k == nk - 1)
            def _():
                o_ref[...] = acc_ref[...].astype(o_ref.dtype)

    scratch = [pltpu.VMEM((tm, tn) if nk > 1 else (8, 128), F32)]
    grid = (M // tm, N // tn, nk)
    if pair is None:
        res, xres = _call(step, grid=grid, in_specs=[a_spec, b_spec], out_specs=[o_spec], out_shape=[o_shape], scratch=scratch,
                          name=name, args=(a, b), xch=xch)
        return res[0], xres

    axis = 2 if mode == "nt" else 1
    half = grid[axis] // 2
    assert grid[axis] % 2 == 0, (name, grid)
    which = a_spec if mode == "nt" else b_spec

    def halves(first):
        def index(i, j, k):
            pos = (i, j, k)[axis]
            pos = jnp.minimum(pos, half - 1) if first else jnp.maximum(pos - half, 0)
            return which.index_map(*[pos if d == axis else v for d, v in enumerate((i, j, k))])
        return pl.BlockSpec(which.block_shape, index)

    def body(x1_ref, x2_ref, y_ref, o_ref, acc_ref):
        first = pl.program_id(axis) < half
        for x_ref, cond in ((x1_ref, first), (x2_ref, jnp.logical_not(first))):
            @pl.when(cond)
            def _(x_ref=x_ref):
                step(*((x_ref, y_ref) if mode == "nt" else (y_ref, x_ref)), o_ref, acc_ref)

    other = b if mode == "nt" else a
    res, xres = _call(body, grid=grid, in_specs=[halves(True), halves(False), b_spec if mode == "nt" else a_spec], out_specs=[o_spec],
                      out_shape=[o_shape], scratch=scratch, name=name, args=(pair[0], pair[1], other), xch=xch)
    return res[0], xres


def _scan_fwd(a, b, *, tm, name):
    n, c = a.shape
    nt = n // tm

    def body(a_ref, b_ref, h_ref, carry):
        i = pl.program_id(0)
        av, bv = a_ref[...], b_ref[...]
        row = _rows(av)
        d = 1
        while d < tm:
            a_s = jnp.where(row >= d, pltpu.roll(av, d, axis=0), 1.0)
            b_s = jnp.where(row >= d, pltpu.roll(bv, d, axis=0), 0.0)
            bv = av * b_s + bv
            av = av * a_s
            d *= 2
        h = bv + av * jnp.where(i > 0, carry[HALO - 1:HALO, :], 0.0)
        h_ref[...] = h
        carry[...] = h[tm - HALO:tm]

    return pl.pallas_call(
        body, grid=(nt,), in_specs=[pl.BlockSpec((tm, c), lambda i: (i, 0))] * 2,
        out_specs=pl.BlockSpec((tm, c), lambda i: (i, 0)), out_shape=jax.ShapeDtypeStruct((n, c), F32),
        scratch_shapes=[pltpu.VMEM((HALO, c), F32)], name=name,
        compiler_params=pltpu.CompilerParams(dimension_semantics=("arbitrary",)),
    )(a, b)


def _scan_bwd(a, h, dh, *, tm, name):
    n, c = a.shape
    nt = n // tm
    tb = tm // HALO

    def body(a_ref, an_ref, h_ref, hp_ref, dh_ref, da_ref, db_ref, carry):
        i = pl.program_id(0)
        tile = nt - 1 - i
        av, hv, g = a_ref[...], h_ref[...], dh_ref[...]
        row = _rows(av)
        a_next = jnp.where(tile < nt - 1, an_ref[0:1, :], 0.0)
        au = jnp.where(row < tm - 1, pltpu.roll(av, tm - 1, axis=0), a_next)
        d = 1
        while d < tm:
            a_s = jnp.where(row < tm - d, pltpu.roll(au, tm - d, axis=0), 1.0)
            g_s = jnp.where(row < tm - d, pltpu.roll(g, tm - d, axis=0), 0.0)
            g = au * g_s + g
            au = au * a_s
            d *= 2
        g = g + au * jnp.where(i > 0, carry[0:1, :], 0.0)
        h_prev = jnp.where(row >= 1, pltpu.roll(hv, 1, axis=0), jnp.where(tile > 0, hp_ref[HALO - 1:HALO, :], 0.0))
        db_ref[...] = g
        da_ref[...] = g * h_prev
        carry[...] = g[0:HALO]

    cur = pl.BlockSpec((tm, c), lambda i: (nt - 1 - i, 0))
    nxt = pl.BlockSpec((HALO, c), lambda i: (jnp.minimum((nt - i) * tb, n // HALO - 1), 0))
    prv = pl.BlockSpec((HALO, c), lambda i: (jnp.maximum((nt - 1 - i) * tb - 1, 0), 0))
    return pl.pallas_call(
        body, grid=(nt,), in_specs=[cur, nxt, cur, prv, cur], out_specs=[cur, cur],
        out_shape=[jax.ShapeDtypeStruct((n, c), F32)] * 2, scratch_shapes=[pltpu.VMEM((HALO, c), F32)], name=name,
        compiler_params=pltpu.CompilerParams(dimension_semantics=("arbitrary",)),
    )(a, a, h, h, dh)


def _gdn_fwd(q, k, v, gb, *, name, xch=None):
    n = q.shape[0]
    rows = GDN_STEP * CHUNK
    ns = n // rows

    def body(q_ref, k_ref, v_ref, gb_ref, o_ref, s_ref, state):
        @pl.when(pl.program_id(0) == 0)
        def _():
            state[...] = jnp.zeros_like(state)

        s0 = state[...]
        s_ref[0] = s0
        o, s1 = _f_gdn_chunks(q_ref[...], k_ref[...], v_ref[...], gb_ref[...], s0)
        o_ref[...] = o
        state[...] = s1

    row = pl.BlockSpec((rows, DG), lambda i: (i, 0))
    return _call(
        body, grid=(ns,), in_specs=[row, row, row, pl.BlockSpec((rows, 128), lambda i: (i, 0))],
        out_specs=[row, pl.BlockSpec((1, DG, HD), lambda i: (i, 0, 0))],
        out_shape=[jax.ShapeDtypeStruct((n, DG), F32), jax.ShapeDtypeStruct((ns, DG, HD), F32)],
        scratch=[pltpu.VMEM((DG, HD), F32)], name=name, args=(q, k, v, gb), xch=xch)


def _gdn_bwd(q, k, v, gb, s_all, do, *, name, xch=None):
    n = q.shape[0]
    rows = GDN_STEP * CHUNK
    ns = n // rows

    def body(q_ref, k_ref, v_ref, gb_ref, s_ref, do_ref, dq_ref, dk_ref, dv_ref, dgb_ref, dstate):
        @pl.when(pl.program_id(0) == 0)
        def _():
            dstate[...] = jnp.zeros_like(dstate)

        _, vjp = jax.vjp(_f_gdn_chunks, q_ref[...], k_ref[...], v_ref[...], gb_ref[...], s_ref[0])
        dq_ref[...], dk_ref[...], dv_ref[...], dgb_ref[...], dstate[...] = vjp((do_ref[...], dstate[...]))

    row = pl.BlockSpec((rows, DG), lambda i: (ns - 1 - i, 0))
    gsp = pl.BlockSpec((rows, 128), lambda i: (ns - 1 - i, 0))
    return _call(
        body, grid=(ns,), in_specs=[row, row, row, gsp, pl.BlockSpec((1, DG, HD), lambda i: (ns - 1 - i, 0, 0)), row],
        out_specs=[row, row, row, gsp],
        out_shape=[jax.ShapeDtypeStruct((n, DG), F32)] * 3 + [jax.ShapeDtypeStruct((n, 128), F32)],
        scratch=[pltpu.VMEM((DG, HD), F32)], name=name, args=(q, k, v, gb, s_all, do), xch=xch)


def _loss_head(x1, y2, w, tgt, *, tm, name):
    n, c = x1.shape

    def body(x_ref, y_ref, w_ref, t_ref, l_ref, d_ref):
        err = x_ref[...] + _rms(y_ref[...], w_ref[...]) - t_ref[...]
        part = jnp.sum(jnp.sum(err * err, axis=1, keepdims=True), axis=0, keepdims=True) * (0.5 / c)

        @pl.when(pl.program_id(0) == 0)
        def _():
            l_ref[...] = jnp.zeros_like(l_ref)

        l_ref[...] += part
        d_ref[...] = err * (1.0 / c)

    row = pl.BlockSpec((tm, c), lambda i: (i, 0))
    return pl.pallas_call(
        body, grid=(n // tm,), in_specs=[row, row, pl.BlockSpec((1, c), lambda i: (0, 0)), row],
        out_specs=[pl.BlockSpec((8, 128), lambda i: (0, 0)), row],
        out_shape=[jax.ShapeDtypeStruct((8, 128), F32), jax.ShapeDtypeStruct((n, c), F32)], name=name,
        compiler_params=pltpu.CompilerParams(dimension_semantics=("arbitrary",)),
    )(x1, y2, w, tgt)


def _sum_parts(own, recv, *, tr, name):
    r, c = own.shape
    p = recv.shape[0]

    def body(o_ref, r_ref, s_ref):
        s = o_ref[...]
        for q in range(p):
            s = s + r_ref[q].astype(F32)
        s_ref[...] = s

    return pl.pallas_call(
        body, grid=(r // tr,), in_specs=[pl.BlockSpec((tr, c), lambda i: (i, 0)), pl.BlockSpec((p, tr, c), lambda i: (0, i, 0))],
        out_specs=pl.BlockSpec((tr, c), lambda i: (i, 0)), out_shape=jax.ShapeDtypeStruct((r, c), F32), name=name,
        compiler_params=pltpu.CompilerParams(dimension_semantics=("parallel",)),
    )(own, recv)


def _sum_slots(buf, *, name):
    p, r, c = buf.shape

    def body(b_ref, s_ref):
        s = b_ref[0]
        for q in range(1, p):
            s = s + b_ref[q]
        s_ref[...] = s

    return pl.pallas_call(body, out_shape=jax.ShapeDtypeStruct((r, c), F32), name=name)(buf)


def _adamw(w, m, v, gs, *, tr, name, xch=None):
    ngrp, r, c = w.shape
    nterm = len(gs[0])
    per = r // tr
    c1 = 1.0 - ADAM_B1 ** ADAM_STEP
    c2 = 1.0 - ADAM_B2 ** ADAM_STEP

    def body(*refs):
        w_ref, m_ref, v_ref = refs[:3]
        g_refs = refs[3:3 + ngrp * nterm]
        g_ref, d_ref, nm_ref, nv_ref = refs[3 + ngrp * nterm:]
        grp = pl.program_id(0) // per
        g = None
        for q in range(ngrp):
            gq = g_refs[q * nterm][...]
            for t in range(1, nterm):
                gq = gq + g_refs[q * nterm + t][...]
            g = gq if g is None else jnp.where(grp == q, gq, g)
        nm = ADAM_B1 * m_ref[...] + (1.0 - ADAM_B1) * g
        nv = ADAM_B2 * v_ref[...] + (1.0 - ADAM_B2) * (g * g)
        g_ref[...] = g
        d_ref[...] = -ADAM_LR * ((nm / c1) / (jnp.sqrt(nv / c2) + ADAM_EPS) + ADAM_WD * w_ref[...])
        nm_ref[...] = nm
        nv_ref[...] = nv

    blk = pl.BlockSpec((None, tr, c), lambda i: (i // per, i % per, 0))
    g_specs = [pl.BlockSpec((tr, c), lambda i, q=q: (jnp.clip(i - q * per, 0, per - 1), 0)) for q in range(ngrp) for _ in range(nterm)]
    return _call(body, grid=(ngrp * per,), in_specs=[blk] * 3 + g_specs, out_specs=[blk] * 4,
                 out_shape=[jax.ShapeDtypeStruct((ngrp, r, c), F32)] * 4, scratch=[], name=name,
                 args=(w, m, v, *[t for grp in gs for t in grp]), xch=xch)


class _SwapCores:
    def __init__(self, arrs):
        self.arrs = list(arrs)
        n = len(self.arrs)
        self.out_shape = [jax.ShapeDtypeStruct(a.shape, a.dtype) for a in self.arrs]
        self.sems = [pltpu.SemaphoreType.DMA((n,)), pltpu.SemaphoreType.DMA((n,))]

    def _copies(self, ins, outs, sems):
        send, recv = sems
        sib = (lax.axis_index("x"), lax.axis_index("y"), 1 - lax.axis_index("c"))
        return [pltpu.make_async_remote_copy(src_ref=ins[a], dst_ref=outs[a], send_sem=send.at[a], recv_sem=recv.at[a],
                                             device_id=sib, device_id_type=MESH_ID) for a in range(len(ins))]

    def issue(self, ins, outs, sems):
        for cp in self._copies(ins, outs, sems):
            cp.start()

    def finish(self, ins, outs, sems):
        copies = self._copies(ins, outs, sems)
        for cp in copies:
            cp.wait_recv()
        for cp in copies:
            cp.wait_send()


class _GatherDevices:
    def __init__(self, buf):
        self.arrs = [buf]
        self.out_shape = [jax.ShapeDtypeStruct((8,) + buf.shape, buf.dtype)]
        self.sems = [pltpu.SemaphoreType.DMA((7,)), pltpu.SemaphoreType.DMA((7,)), pltpu.SemaphoreType.DMA((1,))]

    def _copies(self, ins, outs, sems, arriving):
        send, recv, lsem = sems
        x, y, c = lax.axis_index("x"), lax.axis_index("y"), lax.axis_index("c")
        me = 4 * x + 2 * y + c
        local = [] if arriving else [pltpu.make_async_copy(ins[0], outs[0].at[me], lsem.at[0])]
        remote = []
        for k, (fx, fy, fc) in enumerate(_DEV_REL):
            px, py, pc = _flip(x, fx), _flip(y, fy), _flip(c, fc)
            remote.append(pltpu.make_async_remote_copy(
                src_ref=ins[0], dst_ref=outs[0].at[4 * px + 2 * py + pc if arriving else me], send_sem=send.at[k], recv_sem=recv.at[k],
                device_id=(px, py, pc), device_id_type=MESH_ID))
        return local, remote

    def issue(self, ins, outs, sems):
        local, push = self._copies(ins, outs, sems, False)
        for cp in local + push:
            cp.start()

    def finish(self, ins, outs, sems):
        for cp in self._copies(ins, outs, sems, True)[1]:
            cp.wait_recv()
        local, push = self._copies(ins, outs, sems, False)
        for cp in push:
            cp.wait_send()
        for cp in local:
            cp.wait()


class _Together:
    def __init__(self, parts):
        self.parts = list(parts)
        self.arrs = [a for p in self.parts for a in p.arrs]
        self.out_shape = [s for p in self.parts for s in p.out_shape]
        self.sems = [s for p in self.parts for s in p.sems]

    def _split(self, ins, outs, sems):
        i = o = s = 0
        for p in self.parts:
            ni, no, ns = len(p.arrs), len(p.out_shape), len(p.sems)
            yield p, ins[i:i + ni], outs[o:o + no], sems[s:s + ns]
            i, o, s = i + ni, o + no, s + ns

    def issue(self, ins, outs, sems):
        for p, a, b, c in self._split(ins, outs, sems):
            p.issue(a, b, c)

    def finish(self, ins, outs, sems):
        for p, a, b, c in self._split(ins, outs, sems):
            p.finish(a, b, c)


def _lanes(v, off):
    return jnp.pad(v.reshape(1, -1), ((0, 0), (off, 128 - off - v.size)))


def _block_diag(w):
    eye = jnp.eye(8, dtype=w.dtype)
    return (w[:, :, None, :] * eye[:, None, :, None]).reshape(DG, DG)


def _diag_blocks(w):
    return jnp.stack([w[h * 64:(h + 1) * 64, h * 64:(h + 1) * 64] for h in range(8)])


def _mixer_params(p):
    return dict(
        gdn_conv_w=p["gdn_conv_w"], alog=_lanes(p["gdn_a_log"], HEADS), dtb=_lanes(p["gdn_dt_bias"], HEADS),
        gdn_nw=p["gdn_norm_w"].reshape(1, HD),
        lru_conv_w=p["lru_conv_w"], lru_conv_b=p["lru_conv_b"].reshape(1, DG),
        wa=_block_diag(p["lru_wa"]), ba=p["lru_ba"].reshape(1, DG), wx=_block_diag(p["lru_wx"]), bx=p["lru_bx"].reshape(1, DG),
        lam=p["lru_lambda"].reshape(1, DG),
        ln_w=p["sgu_ln_w"].reshape(1, DG), ln_b=p["sgu_ln_b"].reshape(1, DG), ws=p["sgu_ws"].reshape(DG, 128),
        bst=jnp.pad(p["sgu_b"].T, ((0, 0), (0, 124))),
        sconv_w=p["sconv_w"], gw0=p["grp_norm_w"][0:1], gw1=p["grp_norm_w"][1:2], gw2=p["grp_norm_w"][2:3],
    )


def _mixer_param_grads(g):
    return dict(
        gdn_conv_w=g["gdn_conv_w"], gdn_a_log=g["alog"][0, HEADS:2 * HEADS], gdn_dt_bias=g["dtb"][0, HEADS:2 * HEADS],
        gdn_norm_w=g["gdn_nw"][0],
        lru_conv_w=g["lru_conv_w"], lru_conv_b=g["lru_conv_b"][0],
        lru_wa=_diag_blocks(g["wa"]), lru_ba=g["ba"].reshape(8, 64), lru_wx=_diag_blocks(g["wx"]), lru_bx=g["bx"].reshape(8, 64),
        lru_lambda=g["lam"][0],
        sgu_ln_w=g["ln_w"][0], sgu_ln_b=g["ln_b"][0], sgu_ws=g["ws"].reshape(4, 128, 128), sgu_b=g["bst"][:, 0:4].T,
        sconv_w=g["sconv_w"], grp_norm_w=jnp.concatenate([g["gw0"], g["gw1"], g["gw2"]], axis=0),
    )


TM_MIX = 256


def _mixers_fwd(p, mp, tag="", xch=None):
    c = lambda *names: [(mp[n], False) for n in names]
    q, k, v, gb = _rowwise(_f_gdn_pre, [(p, 0, 1536, True), (p, OFF_BA // 128, 128, False)], c("gdn_conv_w", "alog", "dtb"),
                           [(DG, F32)] * 3 + [(128, F32)], tm=TM_MIX, name="gdn_pre" + tag)
    (o, s_all), xres = _gdn_fwd(q, k, v, gb, name="gdn_chunks" + tag, xch=xch)
    y_a, = _rowwise(_f_gdn_post, [(o, 0, DG, False), (p, OFF_Z // DG, DG, False)], c("gdn_nw"), [(DG, BF)], tm=TM_MIX, name="gdn_post" + tag)
    a, b = _rowwise(_f_lru_ab, [(p, OFF_LX // DG, DG, True)], c("lru_conv_w", "lru_conv_b", "wa", "ba", "wx", "bx", "lam"),
                    [(DG, F32)] * 2, tm=TM_MIX, name="lru_ab" + tag)
    hs = _scan_fwd(a, b, tm=TM_MIX, name="lru_scan" + tag)
    y_b, = _rowwise(_f_lru_post, [(hs, 0, DG, False), (p, OFF_LG // DG, DG, False)], c("gw0"), [(DG, BF)], tm=TM_MIX, name="lru_post" + tag)
    y_c, = _rowwise(_f_sgu, [(p, OFF_UV // 1024, 1024, False)], c("ln_w", "ln_b", "ws", "bst", "gw1"), [(DG, BF)], tm=TM_MIX, name="sgu" + tag)
    y_d, = _rowwise(_f_sconv, [(p, OFF_SB // DG, DG, False), (p, OFF_SC // DG, DG, True), (p, OFF_SH // DG, DG, True)],
                    c("sconv_w", "gw2"), [(DG, BF)], tm=TM_MIX, name="sconv" + tag)
    return jnp.concatenate([y_a, y_b, y_c, y_d], axis=1), (q, k, v, gb, o, s_all, a, hs), xres


def _mixers_bwd(p, mp, saved, dy, tag="", xch=None):
    q, k, v, gb, o, s_all, a, hs = saved
    c = lambda *names: [(mp[n], False, True) for n in names]
    g = {}

    (do, dz), (g["gdn_nw"],) = _rowwise_vjp(
        _f_gdn_post, [(o, 0, DG, False, F32), (p, OFF_Z // DG, DG, False, BF)], c("gdn_nw"), [(dy, 0, DG)], tm=TM_MIX, name="gdn_post_b" + tag)
    (dq, dk, dv, dgb), xres = _gdn_bwd(q, k, v, gb, s_all, do, name="gdn_chunks_b" + tag, xch=xch)
    (dqkv, dba), (g["gdn_conv_w"], g["alog"], g["dtb"]) = _rowwise_vjp(
        _f_gdn_pre, [(p, 0, 1536, True, BF), (p, OFF_BA // 128, 128, False, BF)], c("gdn_conv_w", "alog", "dtb"),
        [(dq, 0, DG), (dk, 0, DG), (dv, 0, DG), (dgb, 0, 128)], tm=TM_MIX, name="gdn_pre_b" + tag)

    (dhs, dgate), (g["gw0"],) = _rowwise_vjp(
        _f_lru_post, [(hs, 0, DG, False, F32), (p, OFF_LG // DG, DG, False, BF)], c("gw0"), [(dy, 1, DG)], tm=TM_MIX, name="lru_post_b" + tag)
    da, db = _scan_bwd(a, hs, dhs, tm=TM_MIX, name="lru_scan_b" + tag)
    (dlx,), (g["lru_conv_w"], g["lru_conv_b"], g["wa"], g["ba"], g["wx"], g["bx"], g["lam"]) = _rowwise_vjp(
        _f_lru_ab, [(p, OFF_LX // DG, DG, True, BF)], c("lru_conv_w", "lru_conv_b", "wa", "ba", "wx", "bx", "lam"),
        [(da, 0, DG), (db, 0, DG)], tm=TM_MIX, name="lru_ab_b" + tag)

    (duv,), (g["ln_w"], g["ln_b"], g["ws"], g["bst"], g["gw1"]) = _rowwise_vjp(
        _f_sgu, [(p, OFF_UV // 1024, 1024, False, BF)], c("ln_w", "ln_b", "ws", "bst", "gw1"), [(dy, 2, DG)], tm=TM_MIX, name="sgu_b" + tag)

    (dsb, dsc, dsh), (g["sconv_w"], g["gw2"]) = _rowwise_vjp(
        _f_sconv, [(p, OFF_SB // DG, DG, False, BF), (p, OFF_SC // DG, DG, True, BF), (p, OFF_SH // DG, DG, True, BF)],
        c("sconv_w", "gw2"), [(dy, 3, DG)], tm=TM_MIX, name="sconv_b" + tag)

    dp = jnp.concatenate([dqkv, dz, dlx, dgate, duv, dsb, dsc, dsh, dba], axis=1)
    return dp, g, xres


WEIGHTS = ("pre_mix_norm", "w_in", "gdn_conv_w", "gdn_a_log", "gdn_dt_bias", "gdn_norm_w", "lru_conv_w", "lru_conv_b", "lru_wa",
           "lru_ba", "lru_wx", "lru_bx", "lru_lambda", "sgu_ln_w", "sgu_ln_b", "sgu_ws", "sgu_b", "sconv_w", "grp_norm_w", "w_out",
           "post_mix_norm", "pre_ffn_norm", "ffn_up", "ffn_conv_w", "ffn_conv_b", "ffn_down", "post_ffn_norm")
BIG = ("w_in", "ffn_up", "w_out", "ffn_down")
CHIP_SHARDED_SMALL = ("gdn_conv_w", "lru_conv_w", "sconv_w", "grp_norm_w", "ffn_conv_w")
MIXER_PARAMS = ("gdn_conv_w", "gdn_a_log", "gdn_dt_bias", "gdn_norm_w", "lru_conv_w", "lru_conv_b", "lru_wa", "lru_ba", "lru_wx",
                "lru_bx", "lru_lambda", "sgu_ln_w", "sgu_ln_b", "sgu_ws", "sgu_b", "sconv_w", "grp_norm_w")
TM_ROW = 256
FF_TILE = 256
FF_ROWS = 2048
N_FF_TILES = D_FF // FF_TILE
PACK_ROWS = 256
FFN_UP_CUTS = (512, 1536, 1872)
TK_DW = 4096


def _pack(arrs):
    parts = []
    for a in arrs:
        n = a.size
        parts.append(jnp.pad(a.reshape(-1), (0, -n % 1024)).reshape(-1, 128))
    rows = sum(p.shape[0] for p in parts)
    parts.append(jnp.zeros((-rows % PACK_ROWS, 128), F32))
    return jnp.concatenate(parts, axis=0)


def _unpack(buf, shapes):
    out, row = [], 0
    for s in shapes:
        n = math.prod(s)
        rows = -(-n // 1024) * 8
        out.append(buf[row:row + rows].reshape(-1)[:n].reshape(s))
        row += rows
    return out


def _row_tile(rows, cols):
    return 256 if rows % 256 == 0 and cols <= 1024 else 128


def _w_in_full(got):
    c = N_IN // 4
    z = jnp.zeros((D, NP - N_IN), got.dtype)
    return jnp.concatenate([got[0], got[1][:, :2048 - c], got[1][:, 2056 - c:], got[2], got[3], got[1][:, 2048 - c:2056 - c], z], axis=1)


def _w_in_blocks(dw):
    c = N_IN // 4
    cut = 2048 + 2 * c - 2056
    b1 = jnp.concatenate([dw[:, c:2048], dw[:, OFF_BA:OFF_BA + 8], dw[:, 2048:cut]], axis=1)
    return jnp.stack([dw[:, 0:c], b1, dw[:, cut:cut + c], dw[:, cut + c:OFF_BA]])


def _layer_fwd(l, xs, h, w_in_l, w_out_g, shard, sp, mp, nxt):
    t = str(l)
    up = shard["ffn_up"][l]
    c0, c1, c2 = FFN_UP_CUTS
    first = [up[:c0]] + ([shard["w_out"][l]] if w_out_g is None else [])
    p, got = _matmul(h, w_in_l, "nn", F32, tm=1024, tn=1920, tk=D, name="mm_in" + t, xch=_GatherChips(first))
    w_out_l = (got[1] if w_out_g is None else w_out_g).reshape(D, D)
    ycat, saved, (w_up_b,) = _mixers_fwd(p, mp, tag=t, xch=_GatherChips([up[c0:c1]]))
    y, (w_up_c,) = _matmul(ycat, w_out_l, "nn", F32, tm=1024, tn=1024, tk=D, name="mm_out" + t, xch=_GatherChips([up[c1:c2]]))
    (x1, h2), (w_up_d,) = _rowwise(
        _f_post_pre, [(xs, 0, D, False), (y, 0, D, False)], [(sp["post_mix_norm"][l], False), (sp["pre_ffn_norm"][l], False)],
        [(D, F32), (D, BF)], tm=TM_ROW, name="post_mix" + t, xch=_GatherChips([up[c2:]]))
    w_up = jnp.concatenate([got[0], w_up_b, w_up_c, w_up_d], axis=1)
    u, (w_dn_g,) = _matmul(h2, w_up, "nn", F32, tm=1024, tn=1408, tk=D, name="mm_up" + t, b_blocks=True,
                           xch=_GatherChips([shard["ffn_down"][l]]))
    ffn_rows = [(u, 0, FF_TILE, True), (u, N_FF_TILES, FF_TILE, True)]
    ffn_consts = [(c, True) for c in sp["ffn_conv"][l]]
    ffn_tm = min(FF_ROWS, u.shape[0])
    half = D // 2
    if nxt is None:
        act, = _rowwise(_f_ffn_act, ffn_rows, ffn_consts, [(FF_TILE, BF)], tm=ffn_tm, ncol=N_FF_TILES, name="ffn_act" + t)
    else:
        (act,), (n_out, n_in_a) = _rowwise(_f_ffn_act, ffn_rows, ffn_consts, [(FF_TILE, BF)], tm=ffn_tm, ncol=N_FF_TILES, name="ffn_act" + t,
                                           xch=_GatherChips([nxt[1], nxt[0][:half]]))
    w_dn_l = w_dn_g.reshape(D_FF, D)
    y2, n_in_b = _matmul(act, w_dn_l, "nn", F32, tm=1024, tn=1024, tk=2816, name="mm_down" + t,
                         xch=None if nxt is None else _GatherChips([nxt[0][half:]]))
    keep = dict(xs=xs, h=h, p=p, saved=saved, ycat=ycat, y=y, x1=x1, h2=h2, u=u, act=act, y2=y2,
                w_in=w_in_l, w_out=w_out_l, w_up=w_up, w_dn=w_dn_l)
    return keep, (None if nxt is None else (jnp.concatenate([n_in_a, n_in_b[0]], axis=1), n_out))


def _layer_bwd(l, a, dx1, dy2, sp, mp):
    t = str(l)
    g = {}
    dact, _ = _matmul(dy2, a["w_dn"], "nt", F32, tm=1024, tn=1408, tk=D, name="mm_down_dx" + t)
    dw_dn, _ = _matmul(a["act"], dy2, "tn", F32, tm=512, tn=1024, tk=TK_DW, name="mm_down_dw" + t)
    dw_dn = dw_dn.reshape(4, D_FF // 4, D)
    (dug, duv), gc = _rowwise_vjp(
        _f_ffn_act, [(a["u"], 0, FF_TILE, True, BF), (a["u"], N_FF_TILES, FF_TILE, True, BF)], [(c, True, True) for c in sp["ffn_conv"][l]],
        [(dact, 0, FF_TILE)], tm=min(FF_ROWS, dact.shape[0]), ncol=N_FF_TILES, name="ffn_act_b" + t)
    g["ffn_conv_w"] = jnp.concatenate([gc[0], gc[1]], axis=1)
    g["ffn_conv_b"] = jnp.concatenate([gc[2], gc[3]], axis=1)[0]
    dh2, (r_dn,) = _matmul((dug, duv), a["w_up"], "nt", F32, tm=1024, tn=1024, tk=2816, name="mm_up_dx" + t, b_blocks=True,
                           xch=_ScatterChips([dw_dn.astype(BF)]))
    dw_up, _ = _matmul(a["h2"], (dug, duv), "tn", F32, tm=512, tn=1408, tk=TK_DW // 2, name="mm_up_dw" + t, out_blocks=True)
    (dxs, dy), (gpm, gpf) = _rowwise_vjp(
        _f_post_pre, [(a["xs"], 0, D, False, F32), (a["y"], 0, D, False, BF)],
        [(sp["post_mix_norm"][l], False, True), (sp["pre_ffn_norm"][l], False, True)], [(dx1, 0, D), (dh2, 0, D)], tm=TM_ROW, name="post_mix_b" + t)
    g["post_mix_norm"], g["pre_ffn_norm"] = gpm[0], gpf[0]
    dycat, _ = _matmul(dy, a["w_out"], "nt", F32, tm=1024, tn=1024, tk=D, name="mm_out_dx" + t)
    dw_out, _ = _matmul(a["ycat"], dy, "tn", F32, tm=512, tn=1024, tk=TK_DW, name="mm_out_dw" + t)
    dw_out = dw_out.reshape(4, D // 4, D)
    dp, gm, (r_up,) = _mixers_bwd(a["p"], mp, a["saved"], dycat, tag=t, xch=_ScatterChips([dw_up.astype(BF)]))
    g.update(_mixer_param_grads(gm))
    dw_in, (r_out,) = _matmul(a["h"], dp, "tn", F32, tm=512, tn=1920, tk=TK_DW, name="mm_in_dw" + t, xch=_ScatterChips([dw_out.astype(BF)]))
    dw_in = _w_in_blocks(dw_in)
    dh, (r_in,) = _matmul(dp, a["w_in"], "nt", F32, tm=512, tn=1024, tk=NP, name="mm_in_dx" + t, xch=_ScatterChips([dw_in.astype(BF)]))
    big = {"ffn_down": (dw_dn, r_dn), "ffn_up": (dw_up, r_up), "w_out": (dw_out, r_out), "w_in": (dw_in, r_in)}
    return dxs, dh, g, big


def kernel(x, pre_mix_norm, w_in, gdn_conv_w, gdn_a_log, gdn_dt_bias, gdn_norm_w, lru_conv_w, lru_conv_b, lru_wa, lru_ba, lru_wx, lru_bx, lru_lambda, sgu_ln_w, sgu_ln_b, sgu_ws, sgu_b, sconv_w, grp_norm_w, w_out, post_mix_norm, pre_ffn_norm, ffn_up, ffn_conv_w, ffn_conv_b, ffn_down, post_ffn_norm, loss_target, m_pre_mix_norm, m_w_in, m_gdn_conv_w, m_gdn_a_log, m_gdn_dt_bias, m_gdn_norm_w, m_lru_conv_w, m_lru_conv_b, m_lru_wa, m_lru_ba, m_lru_wx, m_lru_bx, m_lru_lambda, m_sgu_ln_w, m_sgu_ln_b, m_sgu_ws, m_sgu_b, m_sconv_w, m_grp_norm_w, m_w_out, m_post_mix_norm, m_pre_ffn_norm, m_ffn_up, m_ffn_conv_w, m_ffn_conv_b, m_ffn_down, m_post_ffn_norm, v_pre_mix_norm, v_w_in, v_gdn_conv_w, v_gdn_a_log, v_gdn_dt_bias, v_gdn_norm_w, v_lru_conv_w, v_lru_conv_b, v_lru_wa, v_lru_ba, v_lru_wx, v_lru_bx, v_lru_lambda, v_sgu_ln_w, v_sgu_ln_b, v_sgu_ws, v_sgu_b, v_sconv_w, v_grp_norm_w, v_w_out, v_post_mix_norm, v_pre_ffn_norm, v_ffn_up, v_ffn_conv_w, v_ffn_conv_b, v_ffn_down, v_post_ffn_norm):
    given = dict(locals())
    me = 2 * lax.axis_index("x") + lax.axis_index("y")
    xs0, tgt = x[0], loss_target[0]

    small_sh = [given[n] for n in CHIP_SHARDED_SMALL]
    shard = {n: [given[n][l].astype(BF) for l in range(DEPTH)] for n in BIG}
    got = _exchange(_GatherChipsTwoLevel([shard["w_in"][0], _pack(small_sh)]), name="gather_first")
    full = {n: given[n] for n in WEIGHTS if n not in BIG and n not in CHIP_SHARDED_SMALL}
    per_chip = [_unpack(got[1][j], [s.shape for s in small_sh]) for j in range(4)]
    parts = [jnp.stack([per_chip[j][i] for j in range(4)]) for i in range(len(small_sh))]
    for n, pj in zip(CHIP_SHARDED_SMALL, parts):
        full[n] = pj.transpose(1, 2, 0, 3).reshape(pj.shape[1], pj.shape[2], 4 * pj.shape[3])
    sp = {n: [full[n][l:l + 1] for l in range(DEPTH)] for n in ("pre_mix_norm", "post_mix_norm", "pre_ffn_norm", "post_ffn_norm")}
    sp["ffn_conv"] = [[full["ffn_conv_w"][l][:, :D_FF], full["ffn_conv_w"][l][:, D_FF:], full["ffn_conv_b"][l:l + 1, :D_FF],
                       full["ffn_conv_b"][l:l + 1, D_FF:]] for l in range(DEPTH)]
    mps = [_mixer_params({n: full[n][l] for n in MIXER_PARAMS}) for l in range(DEPTH)]

    h, = _rowwise(_f_pre, [(xs0, 0, D, False)], [(sp["pre_mix_norm"][0], False)], [(D, BF)], tm=TM_ROW, name="pre_mix0")
    a0, (w_in1, w_out1) = _layer_fwd(0, xs0, h, _w_in_full(got[0]), None, shard, sp, mps[0], (shard["w_in"][1], shard["w_out"][1]))
    xs1, h1 = _rowwise(_f_post_pre, [(a0["x1"], 0, D, False), (a0["y2"], 0, D, False)],
                       [(sp["post_ffn_norm"][0], False), (sp["pre_mix_norm"][1], False)], [(D, F32), (D, BF)], tm=TM_ROW, name="post_ffn0")
    a1, _ = _layer_fwd(1, xs1, h1, _w_in_full(w_in1), w_out1, shard, sp, mps[1], None)
    lacc, dxo = _loss_head(a1["x1"], a1["y2"], sp["post_ffn_norm"][1], tgt, tm=TM_ROW, name="loss_head")

    gl = [None, None]
    (dx1, dy2), (gpf1,) = _rowwise_vjp(_f_post, [(a1["x1"], 0, D, False, F32), (a1["y2"], 0, D, False, BF)],
                                      [(sp["post_ffn_norm"][1], False, True)], [(dxo, 0, D)], tm=TM_ROW, name="post_ffn1_b")
    big = [None, None]
    dxs1, dh1, gl[1], big[1] = _layer_bwd(1, a1, dx1, dy2, sp, mps[1])
    gl[1]["post_ffn_norm"] = gpf1[0]
    (dx1, dy2), (gpf0, gpm1) = _rowwise_vjp(
        _f_post_pre, [(a0["x1"], 0, D, False, F32), (a0["y2"], 0, D, False, BF)],
        [(sp["post_ffn_norm"][0], False, True), (sp["pre_mix_norm"][1], False, True)], [(dxs1, 0, D), (dh1, 0, D)], tm=TM_ROW, name="post_ffn0_b")
    gl[1]["pre_mix_norm"] = gpm1[0]
    dxs0, dh0, gl[0], big[0] = _layer_bwd(0, a0, dx1, dy2, sp, mps[0])
    gl[0]["post_ffn_norm"] = gpf0[0]
    (grad_x,), (gpm0,) = _rowwise_vjp(lambda xv, w: (xv, _rms(xv, w)), [(xs0, 0, D, False, F32)], [(sp["pre_mix_norm"][0], False, True)],
                                     [(dxs0, 0, D), (dh0, 0, D)], tm=TM_ROW, name="pre_mix0_b")
    gl[0]["pre_mix_norm"] = gpm0[0]

    small = [n for n in WEIGHTS if n not in BIG]
    gfull = {n: jnp.stack([gl[0][n], gl[1][n]]) for n in small}
    small_grads = _pack([gfull[n] for n in small] + [lacc[0, 0:1]])
    sums = {}
    for n in BIG:
        for l in range(DEPTH):
            blocks, recv = big[l][n]
            own = lax.dynamic_index_in_dim(blocks, me, 0, keepdims=False)
            sums[n] = sums.get(n, []) + [_sum_parts(own, recv, tr=_row_tile(*own.shape), name="sum_grads_%s%d" % (n, l))]
    kinds = ("grad", "delta", "new_m", "new_v")
    outs = {kind: {} for kind in kinds}
    order = ("w_out", "ffn_down", "w_in", "ffn_up")
    other = _exchange(_SwapCores(sums[order[0]]), name="swap_first")
    gathered = None
    for i, n in enumerate(order):
        riders = []
        if i + 1 < len(order):
            riders.append(_SwapCores(sums[order[i + 1]]))
        if n == "ffn_down":
            riders.append(_GatherDevices(small_grads))
        shp = given[n].shape
        r4, got = _adamw(given[n], given["m_" + n], given["v_" + n], [[s, o] for s, o in zip(sums[n], other)],
                         tr=_row_tile(shp[1], shp[2]), name="adamw_" + n, xch=_Together(riders) if riders else None)
        outs_n = dict(zip(kinds, r4))
        for kind in kinds:
            outs[kind][n] = outs_n[kind]
        if i + 1 < len(order):
            other = got[:DEPTH]
        if n == "ffn_down":
            gathered = got[DEPTH]

    tot = _sum_slots(gathered, name="sum_small_grads")
    red = dict(zip(small + ["loss"], _unpack(tot, [gfull[n].shape for n in small] + [(1,)])))
    for n in CHIP_SHARDED_SMALL:
        cb = given[n].shape[-1]
        red[n] = lax.dynamic_slice_in_dim(red[n], me * cb, cb, axis=red[n].ndim - 1)
    shapes = [given[n].shape for n in small]
    res, _ = _adamw(_pack([given[n] for n in small])[None], _pack([given["m_" + n] for n in small])[None],
                    _pack([given["v_" + n] for n in small])[None], [[_pack([red[n] for n in small])]], tr=PACK_ROWS, name="adamw_small")
    for kind, r in zip(kinds, res):
        outs[kind].update(zip(small, _unpack(r[0], shapes)))

    return (red["loss"][0], grad_x[None], *[outs[k][n] for k in ("grad", "delta", "new_m", "new_v") for n in WEIGHTS])
```

```python
import functools
import math

import jax
import jax.numpy as jnp
from jax import lax
from jax.experimental import pallas as pl
from jax.experimental.pallas import tpu as pltpu

F32 = jnp.float32
BF = jnp.bfloat16
MESH_ID = pl.DeviceIdType.MESH

EPS = 1e-6
DEPTH = 2
D = 2048
DG = 512
HEADS = 4
HD = 128
CHUNK = 64
GDN_STEP = 4
LRU_C = 8.0
D_FF = 5632
N_IN = 5640
NP = 5760
OFF_Q, OFF_Z, OFF_LX, OFF_LG, OFF_UV, OFF_SB, OFF_SC, OFF_SH, OFF_BA = 0, 1536, 2048, 2560, 3072, 4096, 4608, 5120, 5632

ADAM_LR, ADAM_B1, ADAM_B2, ADAM_EPS, ADAM_WD, ADAM_STEP = 0.001, 0.9, 0.999, 1e-08, 0.01, 10

HALO = 8


def _mk_bdot(ca, cb):
    na, nb = 1 - ca, 1 - cb

    def dg(x, y, cx, cy):
        return lax.dot_general(x.astype(BF), y.astype(BF), (((cx,), (cy,)), ((), ())), preferred_element_type=F32)

    @jax.custom_vjp
    def f(a, b):
        return dg(a, b, ca, cb)

    def fwd(a, b):
        return dg(a, b, ca, cb), (a, b)

    def bwd(res, g):
        a, b = res
        da = dg(g, b, 1, nb) if ca == 1 else dg(b, g, nb, 1)
        db = dg(a, g, na, 0) if cb == 0 else dg(g, a, 0, na)
        return da.astype(a.dtype), db.astype(b.dtype)

    f.defvjp(fwd, bwd)
    return f


_bdot = _mk_bdot(1, 0)
_bdot_nt = _mk_bdot(1, 1)
_bdot_tn = _mk_bdot(0, 0)


def _sigmoid(x):
    return 1.0 / (1.0 + jnp.exp(-x))


def _silu(x):
    return x * _sigmoid(x)


def _gelu(x):
    return 0.5 * x * (1.0 + jnp.tanh(0.7978845608028654 * (x + 0.044715 * (x * x * x))))


def _log1p(z):
    u = 1.0 + z
    d = u - 1.0
    return jnp.where(d == 0.0, z, jnp.log(u) * (z / jnp.where(d == 0.0, 1.0, d)))


def _softplus(x):
    return jnp.maximum(x, 0.0) + _log1p(jnp.exp(-jnp.abs(x)))


def _neg_expm1(y):
    t = jnp.tanh(0.5 * y)
    return -2.0 * t / (1.0 - t)


def _rms(x, w):
    return x * lax.rsqrt(jnp.mean(x * x, axis=-1, keepdims=True) + EPS) * w


def _rows(x):
    return lax.broadcasted_iota(jnp.int32, x.shape, 0)


def _mk_shift():
    @functools.partial(jax.custom_vjp, nondiff_argnums=(1,))
    def shift(xx, s):
        n = xx.shape[0]
        return pltpu.roll(xx, s, axis=0)[HALO:n] if s else xx[HALO:n]

    def fwd(xx, s):
        return shift(xx, s), None

    def bwd(s, _, g):
        ext = jnp.concatenate([g, jnp.zeros((HALO, g.shape[1]), g.dtype)], axis=0)
        return (pltpu.roll(ext, HALO - s, axis=0),)

    shift.defvjp(fwd, bwd)
    return shift


_shift = _mk_shift()


def _mk_chunk_cumsum():
    def run(x, up):
        n = x.shape[0]
        pos = _rows(x) % CHUNK
        d = 1
        while d < CHUNK:
            if up:
                x = x + jnp.where(pos < CHUNK - d, pltpu.roll(x, n - d, axis=0), 0.0)
            else:
                x = x + jnp.where(pos >= d, pltpu.roll(x, d, axis=0), 0.0)
            d *= 2
        return x

    @jax.custom_vjp
    def cumsum(x):
        return run(x, False)

    cumsum.defvjp(lambda x: (run(x, False), None), lambda _, g: (run(g, True),))
    return cumsum


_chunk_cumsum = _mk_chunk_cumsum()


def _causal_conv(xx, w):
    K = w.shape[0]
    y = _shift(xx, K - 1) * w[0:1, :]
    for k in range(1, K):
        y = y + _shift(xx, K - 1 - k) * w[k:k + 1, :]
    return y


def _f_pre(x, w):
    return (_rms(x, w),)


def _f_post_pre(x, y, w_post, w_pre):
    x1 = x + _rms(y, w_post)
    return x1, _rms(x1, w_pre)


def _f_post(x, y, w_post):
    return (x + _rms(y, w_post),)


def _heads(fn, *xs):
    return jnp.concatenate([fn(*[x[:, h * HD:(h + 1) * HD] for x in xs]) for h in range(HEADS)], axis=1)


def _l2n(t):
    return t * lax.rsqrt(jnp.sum(t * t, axis=-1, keepdims=True) + EPS)


def _f_gdn_pre(qkv, ba, conv_w, alog, dtb):
    c = _silu(_causal_conv(qkv, conv_w))
    q = _heads(lambda t: _l2n(t) * (HD ** -0.5), c[:, 0:DG])
    k = _heads(_l2n, c[:, DG:2 * DG])
    v = c[:, 2 * DG:3 * DG]
    lane = lax.broadcasted_iota(jnp.int32, ba.shape, 1)
    beta = _sigmoid(ba)
    gcum = _chunk_cumsum(-jnp.exp(alog) * _softplus(ba + dtb))
    gc = jnp.where(lane < HEADS, beta, jnp.where(lane < 2 * HEADS, gcum, 0.0))
    return q, k, v, gc


def _f_gdn_chunk(q, k, v, gc, s0):
    C = q.shape[0]
    HC = HEADS * C
    sh = C.bit_length() - 1

    def stack(x):
        return jnp.concatenate([x[:, h * HD:(h + 1) * HD] for h in range(HEADS)], axis=0)

    def column(off):
        lane = lax.broadcasted_iota(jnp.int32, gc.shape, 1)
        return jnp.concatenate([jnp.sum(jnp.where(lane == off + h, gc, 0.0), axis=1, keepdims=True) for h in range(HEADS)], axis=0)

    def head(x, h):
        return x[h * C:(h + 1) * C]

    r = lax.broadcasted_iota(jnp.int32, (HC, HC), 0)
    c = lax.broadcasted_iota(jnp.int32, (HC, HC), 1)
    same = jnp.right_shift(r, sh) == jnp.right_shift(c, sh)
    causal = jnp.logical_and(same, r >= c)
    strict = jnp.logical_and(same, r > c)
    gcol, bcol = column(HEADS), column(0)
    grow = jnp.sum(jnp.where(r == c, gcol, 0.0), axis=0, keepdims=True)
    decay = jnp.where(causal, jnp.exp(jnp.where(causal, gcol - grow, 0.0)), 0.0)
    ks, qs, vs = stack(k), stack(q), stack(v)
    kb = ks * bcol
    kk = _bdot_nt(jnp.concatenate([kb, qs], axis=0), ks)
    m = jnp.where(strict, kk[0:HC] * decay, 0.0)
    attn = jnp.where(causal, kk[HC:2 * HC] * decay, 0.0)
    n = -m
    t = (r == c).astype(F32) + n
    p = n
    for _ in range(5):
        p = _bdot(p, p)
        t = t + _bdot(t, p)
    eg = jnp.exp(gcol)
    wu = _bdot(t, jnp.concatenate([kb * eg, vs * bcol], axis=1))
    w, u = wu[:, 0:HD], wu[:, HD:2 * HD]
    last = jnp.logical_and(same, jnp.bitwise_and(c, C - 1) == C - 1)
    glast = jnp.sum(jnp.where(last, grow, 0.0), axis=1, keepdims=True)
    k_g = ks * jnp.exp(glast - gcol)
    q_g = qs * eg
    ws = [_bdot(jnp.concatenate([head(w, h), head(q_g, h)], axis=0), s0[h * HD:(h + 1) * HD]) for h in range(HEADS)]
    v_new = u - jnp.concatenate([x[0:C] for x in ws], axis=0)
    o = jnp.concatenate([x[C:2 * C] for x in ws], axis=0) + _bdot(attn, v_new)
    s1 = [s0[h * HD:(h + 1) * HD] * jnp.exp(glast[h * C:h * C + 1]) + _bdot_tn(head(k_g, h), head(v_new, h)) for h in range(HEADS)]
    return jnp.concatenate([head(o, h) for h in range(HEADS)], axis=1), jnp.concatenate(s1, axis=0)


def _f_gdn_chunks(q, k, v, gc, s0):
    outs, s = [], s0
    for n in range(q.shape[0] // CHUNK):
        rs = slice(n * CHUNK, (n + 1) * CHUNK)
        o, s = _f_gdn_chunk(q[rs], k[rs], v[rs], gc[rs], s)
        outs.append(o)
    return jnp.concatenate(outs, axis=0), s


def _f_gdn_post(o, z, nw):
    return (_heads(lambda a, b: _rms(a, nw) * _silu(b), o, z),)


def _f_lru_ab(lx, conv_w, conv_b, wa, ba, wx, bx, lam):
    xc = _causal_conv(lx, conv_w) + conv_b
    r = _sigmoid(_bdot(xc, wa) + ba)
    i = _sigmoid(_bdot(xc, wx) + bx)
    log_a = -LRU_C * r * _softplus(-lam)
    a = jnp.exp(log_a)
    mult = jnp.sqrt(_neg_expm1(2.0 * log_a))
    return a, mult * (i * xc)


def _f_lru_post(hs, gate, gw):
    return (_rms(hs * _gelu(gate), gw),)


def _f_sgu(uv, ln_w, ln_b, ws, bst, gw):
    tm = uv.shape[0]
    uvf = _gelu(uv)
    u, v = uvf[:, 0:DG], uvf[:, DG:2 * DG]
    mu = jnp.mean(v, axis=-1, keepdims=True)
    vc = v - mu
    v = vc * lax.rsqrt(jnp.mean(vc * vc, axis=-1, keepdims=True) + EPS) * ln_w + ln_b
    lane = lax.broadcasted_iota(jnp.int32, bst.shape, 1)
    tril = lax.broadcasted_iota(jnp.int32, (128, 128), 0) >= lax.broadcasted_iota(jnp.int32, (128, 128), 1)
    wsm = [jnp.where(tril, ws[g * 128:(g + 1) * 128, :], 0.0) for g in range(4)]
    bias = [jnp.sum(jnp.where(lane == g, bst, 0.0), axis=1, keepdims=True) for g in range(4)]
    out = []
    for n in range(tm // 128):
        vn = v[n * 128:(n + 1) * 128, :]
        gs = [_bdot(wsm[g], vn[:, g * 128:(g + 1) * 128]) + bias[g] for g in range(4)]
        out.append(jnp.concatenate(gs, axis=1))
    vo = jnp.concatenate(out, axis=0) if len(out) > 1 else out[0]
    return (_rms(u * vo, gw),)


def _f_sconv(sb, sc, sh, conv_w, gw):
    return (_rms(sb * _causal_conv(sc * sh, conv_w), gw),)


def _f_ffn_act(ug, uv, wg, wv, bg, bv):
    return (_gelu(_causal_conv(ug, wg) + bg) * (_causal_conv(uv, wv) + bv),)


def _row_specs(rows, consts, tm, ncol, tile_of):
    specs, args = [], []
    for arr, cb, w, halo in rows:
        specs.append(pl.BlockSpec((tm, w), lambda j, i, cb=cb: (tile_of(i), cb + j)))
        args.append(arr)
        if halo:
            specs.append(pl.BlockSpec((HALO, w), lambda j, i, cb=cb: (jnp.maximum(tile_of(i) * (tm // HALO) - 1, 0), cb + j)))
            args.append(arr)
    for arr, tiled in consts:
        r, c = arr.shape
        specs.append(pl.BlockSpec((r, c // ncol), lambda j, i: (0, j)) if tiled else pl.BlockSpec((r, c), lambda j, i: (0, 0)))
        args.append(arr)
    return specs, args


def _load_rows(refs, rows, consts, tile):
    k, vals = 0, []
    for _arr, _cb, _w, halo in rows:
        t = refs[k][...].astype(F32)
        k += 1
        if halo:
            hl = jnp.where(tile > 0, refs[k][...].astype(F32), 0.0)
            k += 1
            t = jnp.concatenate([hl, t], axis=0)
        vals.append(t)
    for _ in consts:
        vals.append(refs[k][...])
        k += 1
    return vals, k


def _rowwise(fn, rows, consts, outs, *, tm, name, ncol=1, xch=None):
    n = rows[0][0].shape[0]
    nt = n // tm
    in_specs, args = _row_specs(rows, consts, tm, ncol, lambda i: i)

    def body(*refs):
        vals, k = _load_rows(refs, rows, consts, pl.program_id(1))
        for o_ref, r in zip(refs[k:], fn(*vals)):
            o_ref[...] = r.astype(o_ref.dtype)

    res, xres = _call(
        body, grid=(ncol, nt), in_specs=in_specs,
        out_specs=[pl.BlockSpec((tm, w), lambda j, i: (i, j)) for w, _ in outs],
        out_shape=[jax.ShapeDtypeStruct((n, w * ncol), dt) for w, dt in outs], scratch=[], name=name, args=args, xch=xch)
    return res if xch is None else (res, xres)


def _rowwise_vjp(fn, rows, consts, cots, *, tm, name, ncol=1):
    n = rows[0][0].shape[0]
    nt = n // tm
    rows4 = [r[:4] for r in rows]
    consts2 = [c[:2] for c in consts]
    in_specs, args = _row_specs(rows4, consts2, tm, ncol, lambda i: nt - 1 - i)
    for arr, cb, w in cots:
        in_specs.append(pl.BlockSpec((tm, w), lambda j, i, cb=cb: (nt - 1 - i, cb + j)))
        args.append(arr)
    out_specs, out_shape, scratch = [], [], []
    for arr, cb, w, halo, gdt in rows:
        if gdt is not None:
            out_specs.append(pl.BlockSpec((tm, w), lambda j, i: (nt - 1 - i, j)))
            out_shape.append(jax.ShapeDtypeStruct((n, w * ncol), gdt))
            if halo:
                scratch.append(pltpu.VMEM((HALO, w), F32))
    for arr, tiled, want in consts:
        if want:
            r, c = arr.shape
            out_specs.append(pl.BlockSpec((r, c // ncol), lambda j, i: (0, j)) if tiled else pl.BlockSpec((r, c), lambda j, i: (0, 0)))
            out_shape.append(jax.ShapeDtypeStruct((r, c), F32))
    n_in = len(in_specs)
    n_out = len(out_specs)

    def body(*refs):
        j, i = pl.program_id(0), pl.program_id(1)
        tile = nt - 1 - i
        vals, k = _load_rows(refs, rows4, consts2, tile)
        cvals = [refs[k + q][...].astype(F32) for q in range(len(cots))]
        outs = refs[n_in:n_in + n_out]
        carries = refs[n_in + n_out:]
        _, vjp = jax.vjp(fn, *vals)
        g = vjp(tuple(cvals))
        o, cidx = 0, 0
        for q, (arr, cb, w, halo, gdt) in enumerate(rows):
            if gdt is None:
                continue
            if halo:
                ge = g[q]
                main = ge[HALO:]
                carry = carries[cidx]
                cidx += 1
                tail = main[tm - HALO:] + jnp.where(i > 0, carry[...], 0.0)
                outs[o][0:tm - HALO, :] = main[0:tm - HALO].astype(gdt)
                outs[o][tm - HALO:tm, :] = tail.astype(gdt)
                carry[...] = ge[0:HALO]
            else:
                outs[o][...] = g[q].astype(gdt)
            o += 1
        for q, (arr, tiled, want) in enumerate(consts):
            if not want:
                continue
            first = (i == 0) if tiled else jnp.logical_and(i == 0, j == 0)
            acc = outs[o]
            gq = g[len(rows) + q].astype(F32)

            @pl.when(first)
            def _(acc=acc, gq=gq):
                acc[...] = gq

            @pl.when(jnp.logical_not(first))
            def _(acc=acc, gq=gq):
                acc[...] += gq

            o += 1

    res = pl.pallas_call(
        body, grid=(ncol, nt), in_specs=in_specs, out_specs=out_specs, out_shape=out_shape, scratch_shapes=scratch,
        name=name, compiler_params=pltpu.CompilerParams(dimension_semantics=("arbitrary", "arbitrary")),
    )(*args)
    nrow = sum(1 for r in rows if r[4] is not None)
    return list(res[:nrow]), list(res[nrow:])


_ANY = pl.BlockSpec(memory_space=pl.ANY)
_CHIP_REL = ((1, 0), (0, 1), (1, 1))
_DEV_REL = tuple((r >> 2 & 1, r >> 1 & 1, r & 1) for r in range(1, 8))


def _flip(v, f):
    return 1 - v if f else v


class _GatherChips:
    def __init__(self, shards):
        self.arrs = list(shards)
        n = len(self.arrs)
        self.out_shape = [jax.ShapeDtypeStruct((4,) + s.shape, s.dtype) for s in self.arrs]
        self.sems = [pltpu.SemaphoreType.DMA((3 * n,)), pltpu.SemaphoreType.DMA((3 * n,)), pltpu.SemaphoreType.DMA((n,))]

    def _copies(self, ins, outs, sems, arriving):
        send, recv, lsem = sems
        x, y, c = lax.axis_index("x"), lax.axis_index("y"), lax.axis_index("c")
        me = 2 * x + y
        if arriving:
            local = []
        else:
            local = [pltpu.make_async_copy(ins[a], outs[a].at[me], lsem.at[a]) for a in range(len(ins))]
        remote = []
        for a in range(len(ins)):
            for k, (fx, fy) in enumerate(_CHIP_REL):
                px, py = _flip(x, fx), _flip(y, fy)
                remote.append(pltpu.make_async_remote_copy(
                    src_ref=ins[a], dst_ref=outs[a].at[2 * px + py if arriving else me], send_sem=send.at[3 * a + k],
                    recv_sem=recv.at[3 * a + k], device_id=(px, py, c), device_id_type=MESH_ID))
        return local, remote

    def issue(self, ins, outs, sems):
        local, push = self._copies(ins, outs, sems, False)
        for cp in local + push:
            cp.start()

    def finish(self, ins, outs, sems):
        for cp in self._copies(ins, outs, sems, True)[1]:
            cp.wait_recv()
        local, push = self._copies(ins, outs, sems, False)
        for cp in push:
            cp.wait_send()
        for cp in local:
            cp.wait()


class _GatherChipsTwoLevel:
    def __init__(self, shards):
        self.arrs = list(shards)
        n = len(self.arrs)
        self.out_shape = [jax.ShapeDtypeStruct((4,) + s.shape, s.dtype) for s in self.arrs]
        self.sems = [pltpu.SemaphoreType.DMA((3 * n,)) for _ in range(4)] + [pltpu.SemaphoreType.DMA((n,))]

    def _far(self, ins, outs, sems, arriving):
        send, recv = sems[0], sems[1]
        x, y, c = lax.axis_index("x"), lax.axis_index("y"), lax.axis_index("c")
        me = 2 * x + y
        copies = []
        for a in range(len(ins)):
            half = ins[a].shape[0] // 2
            rows = pl.ds(c * half, half)
            for k, (fx, fy) in enumerate(_CHIP_REL):
                px, py = _flip(x, fx), _flip(y, fy)
                copies.append(pltpu.make_async_remote_copy(
                    src_ref=ins[a].at[rows], dst_ref=outs[a].at[2 * px + py if arriving else me, rows], send_sem=send.at[3 * a + k],
                    recv_sem=recv.at[3 * a + k], device_id=(px, py, c), device_id_type=MESH_ID))
        return copies

    def _near(self, ins, outs, sems, arriving):
        send, recv = sems[2], sems[3]
        x, y, c = lax.axis_index("x"), lax.axis_index("y"), lax.axis_index("c")
        copies = []
        for a in range(len(ins)):
            half = ins[a].shape[0] // 2
            rows = pl.ds((1 - c if arriving else c) * half, half)
            for k, (fx, fy) in enumerate(_CHIP_REL):
                block = outs[a].at[2 * _flip(x, fx) + _flip(y, fy), rows]
                copies.append(pltpu.make_async_remote_copy(
                    src_ref=block, dst_ref=block, send_sem=send.at[3 * a + k], recv_sem=recv.at[3 * a + k],
                    device_id=(x, y, 1 - c), device_id_type=MESH_ID))
        return copies

    def _local(self, ins, outs, sems):
        me = 2 * lax.axis_index("x") + lax.axis_index("y")
        return [pltpu.make_async_copy(ins[a], outs[a].at[me], sems[4].at[a]) for a in range(len(ins))]

    def issue(self, ins, outs, sems):
        for cp in self._local(ins, outs, sems) + self._far(ins, outs, sems, False):
            cp.start()

    def finish(self, ins, outs, sems):
        landed = self._far(ins, outs, sems, True)
        onward = self._near(ins, outs, sems, False)
        for cp, fwd in zip(landed, onward):
            cp.wait_recv()
            fwd.start()
        for cp in self._near(ins, outs, sems, True):
            cp.wait_recv()
        for cp in self._far(ins, outs, sems, False) + onward:
            cp.wait_send()
        for cp in self._local(ins, outs, sems):
            cp.wait()


class _ScatterChips:
    def __init__(self, blocks):
        self.arrs = list(blocks)
        n = len(self.arrs)
        self.out_shape = [jax.ShapeDtypeStruct((3,) + b.shape[1:], b.dtype) for b in self.arrs]
        self.sems = [pltpu.SemaphoreType.DMA((3 * n,)), pltpu.SemaphoreType.DMA((3 * n,))]

    def _copies(self, ins, outs, sems):
        send, recv = sems
        x, y, c = lax.axis_index("x"), lax.axis_index("y"), lax.axis_index("c")
        copies = []
        for a in range(len(ins)):
            for k, (fx, fy) in enumerate(_CHIP_REL):
                px, py = _flip(x, fx), _flip(y, fy)
                copies.append(pltpu.make_async_remote_copy(
                    src_ref=ins[a].at[2 * px + py], dst_ref=outs[a].at[k], send_sem=send.at[3 * a + k], recv_sem=recv.at[3 * a + k],
                    device_id=(px, py, c), device_id_type=MESH_ID))
        return copies

    def issue(self, ins, outs, sems):
        for cp in self._copies(ins, outs, sems):
            cp.start()

    def finish(self, ins, outs, sems):
        copies = self._copies(ins, outs, sems)
        for cp in copies:
            cp.wait_recv()
        for cp in copies:
            cp.wait_send()


def _exchange(xch, *, name):
    ni, no = len(xch.arrs), len(xch.out_shape)

    def body(*refs):
        ins, outs, sems = refs[:ni], refs[ni:ni + no], refs[ni + no:]
        xch.issue(ins, outs, sems)
        xch.finish(ins, outs, sems)

    return pl.pallas_call(body, in_specs=[_ANY] * ni, out_specs=[_ANY] * no, out_shape=xch.out_shape, scratch_shapes=xch.sems, name=name)(*xch.arrs)


def _call(body, *, grid, in_specs, out_specs, out_shape, scratch, name, args, xch=None):
    params = pltpu.CompilerParams(dimension_semantics=("arbitrary",) * len(grid))
    if xch is None:
        res = pl.pallas_call(body, grid=grid, in_specs=in_specs, out_specs=out_specs, out_shape=out_shape, scratch_shapes=scratch,
                             name=name, compiler_params=params)(*args)
        return list(res), []
    n_in, n_out, n_sc = len(in_specs), len(out_specs), len(scratch)
    xi, xo = len(xch.arrs), len(xch.out_shape)

    def wrapped(*refs):
        ins, refs = refs[:n_in], refs[n_in:]
        xin, refs = refs[:xi], refs[xi:]
        outs, refs = refs[:n_out], refs[n_out:]
        xout, refs = refs[:xo], refs[xo:]
        sc, sems = refs[:n_sc], refs[n_sc:]
        first = functools.reduce(jnp.logical_and, [pl.program_id(d) == 0 for d in range(len(grid))])
        last = functools.reduce(jnp.logical_and, [pl.program_id(d) == grid[d] - 1 for d in range(len(grid))])

        @pl.when(first)
        def _():
            xch.issue(xin, xout, sems)

        body(*ins, *outs, *sc)

        @pl.when(last)
        def _():
            xch.finish(xin, xout, sems)

    res = pl.pallas_call(
        wrapped, grid=grid, in_specs=list(in_specs) + [_ANY] * xi, out_specs=list(out_specs) + [_ANY] * xo,
        out_shape=list(out_shape) + xch.out_shape, scratch_shapes=list(scratch) + xch.sems, name=name, compiler_params=params,
    )(*args, *xch.arrs)
    return list(res[:n_out]), list(res[n_out:])


def _matmul(a, b, mode, out_dtype, *, tm, tn, tk, name, b_blocks=False, out_blocks=False, xch=None):
    pair = None
    if isinstance(a, tuple):
        assert mode == "nt"
        pair, a = a, jax.ShapeDtypeStruct((a[0].shape[0], 2 * a[0].shape[1]), a[0].dtype)
    if isinstance(b, tuple):
        assert mode == "tn" and not b_blocks
        pair, b = b, jax.ShapeDtypeStruct((b[0].shape[0], 2 * b[0].shape[1]), b[0].dtype)
    if b_blocks:
        _, br, bc4 = b.shape
        b2 = (br, 4 * bc4)
    else:
        b2 = b.shape

    def bspec(shape, index):
        if not b_blocks:
            return pl.BlockSpec(shape, index)
        per = bc4 // shape[1]

        def blocked(i, j, k):
            r, c = index(i, j, k)
            return (c // per, r, c % per)

        return pl.BlockSpec((None,) + shape, blocked)

    tm = min(tm, a.shape[1] if mode == "tn" else a.shape[0])
    tk = min(tk, a.shape[0] if mode == "tn" else a.shape[1])
    if mode == "tn":
        K, M = a.shape
        N = b2[1]
        a_spec = pl.BlockSpec((tk, tm), lambda i, j, k: (k, i))
        b_spec = bspec((tk, tn), lambda i, j, k: (k, j))
        dims = (((0,), (0,)), ((), ()))
    elif mode == "nt":
        M, K = a.shape
        N = b2[0]
        a_spec = pl.BlockSpec((tm, tk), lambda i, j, k: (i, k))
        b_spec = bspec((tn, tk), lambda i, j, k: (j, k))
        dims = (((1,), (1,)), ((), ()))
    else:
        M, K = a.shape
        N = b2[1]
        a_spec = pl.BlockSpec((tm, tk), lambda i, j, k: (i, k))
        b_spec = bspec((tk, tn), lambda i, j, k: (k, j))
        dims = (((1,), (0,)), ((), ()))
    assert M % tm == 0 and N % tn == 0 and K % tk == 0, (name, M, N, K, tm, tn, tk)
    nk = K // tk
    if out_blocks:
        per_o = (N // 4) // tn
        o_spec = pl.BlockSpec((None, tm, tn), lambda i, j, k: (j // per_o, i, j % per_o))
        o_shape = jax.ShapeDtypeStruct((4, M, N // 4), out_dtype)
    else:
        o_spec = pl.BlockSpec((tm, tn), lambda i, j, k: (i, j))
        o_shape = jax.ShapeDtypeStruct((M, N), out_dtype)

    def step(a_ref, b_ref, o_ref, acc_ref):
        k = pl.program_id(2)
        part = lax.dot_general(a_ref[...].astype(BF), b_ref[...].astype(BF), dims, preferred_element_type=F32)
        if nk == 1:
            o_ref[...] = part.astype(o_ref.dtype)
        else:
            @pl.when(k == 0)
            def _():
                acc_ref[...] = part

            @pl.when(k > 0)
            def _():
                acc_ref[...] += part

            @pl.when(k == nk - 1)
            def _():
                o_ref[...] = acc_ref[...].astype(o_ref.dtype)

    scratch = [pltpu.VMEM((tm, tn) if nk > 1 else (8, 128), F32)]
    grid = (M // tm, N // tn, nk)
    if pair is None:
        res, xres = _call(step, grid=grid, in_specs=[a_spec, b_spec], out_specs=[o_spec], out_shape=[o_shape], scratch=scratch,
                          name=name, args=(a, b), xch=xch)
        return res[0], xres

    axis = 2 if mode == "nt" else 1
    half = grid[axis] // 2
    assert grid[axis] % 2 == 0, (name, grid)
    which = a_spec if mode == "nt" else b_spec

    def halves(first):
        def index(i, j, k):
            pos = (i, j, k)[axis]
            used = pos < half if first else pos >= half
            local = jnp.clip(pos if first else pos - half, 0, half - 1)
            at = [local if d == axis else (v if axis == 2 else jnp.where(used, v, 0)) for d, v in enumerate((i, j, k))]
            return which.index_map(*at)
        return pl.BlockSpec(which.block_shape, index)

    def body(x1_ref, x2_ref, y_ref, o_ref, acc_ref):
        first = pl.program_id(axis) < half
        for x_ref, cond in ((x1_ref, first), (x2_ref, jnp.logical_not(first))):
            @pl.when(cond)
            def _(x_ref=x_ref):
                step(*((x_ref, y_ref) if mode == "nt" else (y_ref, x_ref)), o_ref, acc_ref)

    other = b if mode == "nt" else a
    res, xres = _call(body, grid=grid, in_specs=[halves(True), halves(False), b_spec if mode == "nt" else a_spec], out_specs=[o_spec],
                      out_shape=[o_shape], scratch=scratch, name=name, args=(pair[0], pair[1], other), xch=xch)
    return res[0], xres


def _scan_fwd(a, b, *, tm, name):
    n, c = a.shape
    nt = n // tm

    def body(a_ref, b_ref, h_ref, carry):
        i = pl.program_id(0)
        av, bv = a_ref[...], b_ref[...]
        row = _rows(av)
        d = 1
        while d < tm:
            a_s = jnp.where(row >= d, pltpu.roll(av, d, axis=0), 1.0)
            b_s = jnp.where(row >= d, pltpu.roll(bv, d, axis=0), 0.0)
            bv = av * b_s + bv
            av = av * a_s
            d *= 2
        h = bv + av * jnp.where(i > 0, carry[HALO - 1:HALO, :], 0.0)
        h_ref[...] = h
        carry[...] = h[tm - HALO:tm]

    return pl.pallas_call(
        body, grid=(nt,), in_specs=[pl.BlockSpec((tm, c), lambda i: (i, 0))] * 2,
        out_specs=pl.BlockSpec((tm, c), lambda i: (i, 0)), out_shape=jax.ShapeDtypeStruct((n, c), F32),
        scratch_shapes=[pltpu.VMEM((HALO, c), F32)], name=name,
        compiler_params=pltpu.CompilerParams(dimension_semantics=("arbitrary",)),
    )(a, b)


def _scan_bwd(a, h, dh, *, tm, name):
    n, c = a.shape
    nt = n // tm
    tb = tm // HALO

    def body(a_ref, an_ref, h_ref, hp_ref, dh_ref, da_ref, db_ref, carry):
        i = pl.program_id(0)
        tile = nt - 1 - i
        av, hv, g = a_ref[...], h_ref[...], dh_ref[...]
        row = _rows(av)
        a_next = jnp.where(tile < nt - 1, an_ref[0:1, :], 0.0)
        au = jnp.where(row < tm - 1, pltpu.roll(av, tm - 1, axis=0), a_next)
        d = 1
        while d < tm:
            a_s = jnp.where(row < tm - d, pltpu.roll(au, tm - d, axis=0), 1.0)
            g_s = jnp.where(row < tm - d, pltpu.roll(g, tm - d, axis=0), 0.0)
            g = au * g_s + g
            au = au * a_s
            d *= 2
        g = g + au * jnp.where(i > 0, carry[0:1, :], 0.0)
        h_prev = jnp.where(row >= 1, pltpu.roll(hv, 1, axis=0), jnp.where(tile > 0, hp_ref[HALO - 1:HALO, :], 0.0))
        db_ref[...] = g
        da_ref[...] = g * h_prev
        carry[...] = g[0:HALO]

    cur = pl.BlockSpec((tm, c), lambda i: (nt - 1 - i, 0))
    nxt = pl.BlockSpec((HALO, c), lambda i: (jnp.minimum((nt - i) * tb, n // HALO - 1), 0))
    prv = pl.BlockSpec((HALO, c), lambda i: (jnp.maximum((nt - 1 - i) * tb - 1, 0), 0))
    return pl.pallas_call(
        body, grid=(nt,), in_specs=[cur, nxt, cur, prv, cur], out_specs=[cur, cur],
        out_shape=[jax.ShapeDtypeStruct((n, c), F32)] * 2, scratch_shapes=[pltpu.VMEM((HALO, c), F32)], name=name,
        compiler_params=pltpu.CompilerParams(dimension_semantics=("arbitrary",)),
    )(a, a, h, h, dh)


def _gdn_fwd(q, k, v, gb, *, name, xch=None):
    n = q.shape[0]
    rows = GDN_STEP * CHUNK
    ns = n // rows

    def body(q_ref, k_ref, v_ref, gb_ref, o_ref, s_ref, state):
        @pl.when(pl.program_id(0) == 0)
        def _():
            state[...] = jnp.zeros_like(state)

        s0 = state[...]
        s_ref[0] = s0
        o, s1 = _f_gdn_chunks(q_ref[...], k_ref[...], v_ref[...], gb_ref[...], s0)
        o_ref[...] = o
        state[...] = s1

    row = pl.BlockSpec((rows, DG), lambda i: (i, 0))
    return _call(
        body, grid=(ns,), in_specs=[row, row, row, pl.BlockSpec((rows, 128), lambda i: (i, 0))],
        out_specs=[row, pl.BlockSpec((1, DG, HD), lambda i: (i, 0, 0))],
        out_shape=[jax.ShapeDtypeStruct((n, DG), F32), jax.ShapeDtypeStruct((ns, DG, HD), F32)],
        scratch=[pltpu.VMEM((DG, HD), F32)], name=name, args=(q, k, v, gb), xch=xch)


def _gdn_bwd(q, k, v, gb, s_all, do, *, name, xch=None):
    n = q.shape[0]
    rows = GDN_STEP * CHUNK
    ns = n // rows

    def body(q_ref, k_ref, v_ref, gb_ref, s_ref, do_ref, dq_ref, dk_ref, dv_ref, dgb_ref, dstate):
        @pl.when(pl.program_id(0) == 0)
        def _():
            dstate[...] = jnp.zeros_like(dstate)

        _, vjp = jax.vjp(_f_gdn_chunks, q_ref[...], k_ref[...], v_ref[...], gb_ref[...], s_ref[0])
        dq_ref[...], dk_ref[...], dv_ref[...], dgb_ref[...], dstate[...] = vjp((do_ref[...], dstate[...]))

    row = pl.BlockSpec((rows, DG), lambda i: (ns - 1 - i, 0))
    gsp = pl.BlockSpec((rows, 128), lambda i: (ns - 1 - i, 0))
    return _call(
        body, grid=(ns,), in_specs=[row, row, row, gsp, pl.BlockSpec((1, DG, HD), lambda i: (ns - 1 - i, 0, 0)), row],
        out_specs=[row, row, row, gsp],
        out_shape=[jax.ShapeDtypeStruct((n, DG), F32)] * 3 + [jax.ShapeDtypeStruct((n, 128), F32)],
        scratch=[pltpu.VMEM((DG, HD), F32)], name=name, args=(q, k, v, gb, s_all, do), xch=xch)


def _loss_head(x1, y2, w, tgt, *, tm, name):
    n, c = x1.shape

    def body(x_ref, y_ref, w_ref, t_ref, l_ref, d_ref):
        err = x_ref[...] + _rms(y_ref[...], w_ref[...]) - t_ref[...]
        part = jnp.sum(jnp.sum(err * err, axis=1, keepdims=True), axis=0, keepdims=True) * (0.5 / c)

        @pl.when(pl.program_id(0) == 0)
        def _():
            l_ref[...] = jnp.zeros_like(l_ref)

        l_ref[...] += part
        d_ref[...] = err * (1.0 / c)

    row = pl.BlockSpec((tm, c), lambda i: (i, 0))
    return pl.pallas_call(
        body, grid=(n // tm,), in_specs=[row, row, pl.BlockSpec((1, c), lambda i: (0, 0)), row],
        out_specs=[pl.BlockSpec((8, 128), lambda i: (0, 0)), row],
        out_shape=[jax.ShapeDtypeStruct((8, 128), F32), jax.ShapeDtypeStruct((n, c), F32)], name=name,
        compiler_params=pltpu.CompilerParams(dimension_semantics=("arbitrary",)),
    )(x1, y2, w, tgt)


def _sum_parts(own, recv, *, tr, name):
    r, c = own.shape
    p = recv.shape[0]

    def body(o_ref, r_ref, s_ref):
        s = o_ref[...]
        for q in range(p):
            s = s + r_ref[q].astype(F32)
        s_ref[...] = s

    return pl.pallas_call(
        body, grid=(r // tr,), in_specs=[pl.BlockSpec((tr, c), lambda i: (i, 0)), pl.BlockSpec((p, tr, c), lambda i: (0, i, 0))],
        out_specs=pl.BlockSpec((tr, c), lambda i: (i, 0)), out_shape=jax.ShapeDtypeStruct((r, c), F32), name=name,
        compiler_params=pltpu.CompilerParams(dimension_semantics=("parallel",)),
    )(own, recv)


def _sum_slots(buf, *, name):
    p, r, c = buf.shape

    def body(b_ref, s_ref):
        s = b_ref[0]
        for q in range(1, p):
            s = s + b_ref[q]
        s_ref[...] = s

    return pl.pallas_call(body, out_shape=jax.ShapeDtypeStruct((r, c), F32), name=name)(buf)


def _adamw(w, m, v, gs, *, tr, name, xch=None):
    ngrp, r, c = w.shape
    nterm = len(gs[0])
    per = r // tr
    c1 = 1.0 - ADAM_B1 ** ADAM_STEP
    c2 = 1.0 - ADAM_B2 ** ADAM_STEP

    def body(*refs):
        w_ref, m_ref, v_ref = refs[:3]
        g_refs = refs[3:3 + ngrp * nterm]
        g_ref, d_ref, nm_ref, nv_ref = refs[3 + ngrp * nterm:]
        grp = pl.program_id(0) // per
        g = None
        for q in range(ngrp):
            gq = g_refs[q * nterm][...]
            for t in range(1, nterm):
                gq = gq + g_refs[q * nterm + t][...]
            g = gq if g is None else jnp.where(grp == q, gq, g)
        nm = ADAM_B1 * m_ref[...] + (1.0 - ADAM_B1) * g
        nv = ADAM_B2 * v_ref[...] + (1.0 - ADAM_B2) * (g * g)
        g_ref[...] = g
        d_ref[...] = -ADAM_LR * ((nm / c1) / (jnp.sqrt(nv / c2) + ADAM_EPS) + ADAM_WD * w_ref[...])
        nm_ref[...] = nm
        nv_ref[...] = nv

    blk = pl.BlockSpec((None, tr, c), lambda i: (i // per, i % per, 0))
    g_specs = [pl.BlockSpec((tr, c), lambda i, q=q: (jnp.clip(i - q * per, 0, per - 1), 0)) for q in range(ngrp) for _ in range(nterm)]
    return _call(body, grid=(ngrp * per,), in_specs=[blk] * 3 + g_specs, out_specs=[blk] * 4,
                 out_shape=[jax.ShapeDtypeStruct((ngrp, r, c), F32)] * 4, scratch=[], name=name,
                 args=(w, m, v, *[t for grp in gs for t in grp]), xch=xch)


class _SwapCores:
    def __init__(self, arrs):
        self.arrs = list(arrs)
        n = len(self.arrs)
        self.out_shape = [jax.ShapeDtypeStruct(a.shape, a.dtype) for a in self.arrs]
        self.sems = [pltpu.SemaphoreType.DMA((n,)), pltpu.SemaphoreType.DMA((n,))]

    def _copies(self, ins, outs, sems):
        send, recv = sems
        sib = (lax.axis_index("x"), lax.axis_index("y"), 1 - lax.axis_index("c"))
        return [pltpu.make_async_remote_copy(src_ref=ins[a], dst_ref=outs[a], send_sem=send.at[a], recv_sem=recv.at[a],
                                             device_id=sib, device_id_type=MESH_ID) for a in range(len(ins))]

    def issue(self, ins, outs, sems):
        for cp in self._copies(ins, outs, sems):
            cp.start()

    def finish(self, ins, outs, sems):
        copies = self._copies(ins, outs, sems)
        for cp in copies:
            cp.wait_recv()
        for cp in copies:
            cp.wait_send()


class _GatherDevices:
    def __init__(self, buf):
        self.arrs = [buf]
        self.out_shape = [jax.ShapeDtypeStruct((8,) + buf.shape, buf.dtype)]
        self.sems = [pltpu.SemaphoreType.DMA((7,)), pltpu.SemaphoreType.DMA((7,)), pltpu.SemaphoreType.DMA((1,))]

    def _copies(self, ins, outs, sems, arriving):
        send, recv, lsem = sems
        x, y, c = lax.axis_index("x"), lax.axis_index("y"), lax.axis_index("c")
        me = 4 * x + 2 * y + c
        local = [] if arriving else [pltpu.make_async_copy(ins[0], outs[0].at[me], lsem.at[0])]
        remote = []
        for k, (fx, fy, fc) in enumerate(_DEV_REL):
            px, py, pc = _flip(x, fx), _flip(y, fy), _flip(c, fc)
            remote.append(pltpu.make_async_remote_copy(
                src_ref=ins[0], dst_ref=outs[0].at[4 * px + 2 * py + pc if arriving else me], send_sem=send.at[k], recv_sem=recv.at[k],
                device_id=(px, py, pc), device_id_type=MESH_ID))
        return local, remote

    def issue(self, ins, outs, sems):
        local, push = self._copies(ins, outs, sems, False)
        for cp in local + push:
            cp.start()

    def finish(self, ins, outs, sems):
        for cp in self._copies(ins, outs, sems, True)[1]:
            cp.wait_recv()
        local, push = self._copies(ins, outs, sems, False)
        for cp in push:
            cp.wait_send()
        for cp in local:
            cp.wait()


class _Together:
    def __init__(self, parts):
        self.parts = list(parts)
        self.arrs = [a for p in self.parts for a in p.arrs]
        self.out_shape = [s for p in self.parts for s in p.out_shape]
        self.sems = [s for p in self.parts for s in p.sems]

    def _split(self, ins, outs, sems):
        i = o = s = 0
        for p in self.parts:
            ni, no, ns = len(p.arrs), len(p.out_shape), len(p.sems)
            yield p, ins[i:i + ni], outs[o:o + no], sems[s:s + ns]
            i, o, s = i + ni, o + no, s + ns

    def issue(self, ins, outs, sems):
        for p, a, b, c in self._split(ins, outs, sems):
            p.issue(a, b, c)

    def finish(self, ins, outs, sems):
        for p, a, b, c in self._split(ins, outs, sems):
            p.finish(a, b, c)


def _lanes(v, off):
    return jnp.pad(v.reshape(1, -1), ((0, 0), (off, 128 - off - v.size)))


def _block_diag(w):
    eye = jnp.eye(8, dtype=w.dtype)
    return (w[:, :, None, :] * eye[:, None, :, None]).reshape(DG, DG)


def _diag_blocks(w):
    return jnp.stack([w[h * 64:(h + 1) * 64, h * 64:(h + 1) * 64] for h in range(8)])


def _mixer_params(p):
    return dict(
        gdn_conv_w=p["gdn_conv_w"], alog=_lanes(p["gdn_a_log"], HEADS), dtb=_lanes(p["gdn_dt_bias"], HEADS),
        gdn_nw=p["gdn_norm_w"].reshape(1, HD),
        lru_conv_w=p["lru_conv_w"], lru_conv_b=p["lru_conv_b"].reshape(1, DG),
        wa=_block_diag(p["lru_wa"]), ba=p["lru_ba"].reshape(1, DG), wx=_block_diag(p["lru_wx"]), bx=p["lru_bx"].reshape(1, DG),
        lam=p["lru_lambda"].reshape(1, DG),
        ln_w=p["sgu_ln_w"].reshape(1, DG), ln_b=p["sgu_ln_b"].reshape(1, DG), ws=p["sgu_ws"].reshape(DG, 128),
        bst=jnp.pad(p["sgu_b"].T, ((0, 0), (0, 124))),
        sconv_w=p["sconv_w"], gw0=p["grp_norm_w"][0:1], gw1=p["grp_norm_w"][1:2], gw2=p["grp_norm_w"][2:3],
    )


def _mixer_param_grads(g):
    return dict(
        gdn_conv_w=g["gdn_conv_w"], gdn_a_log=g["alog"][0, HEADS:2 * HEADS], gdn_dt_bias=g["dtb"][0, HEADS:2 * HEADS],
        gdn_norm_w=g["gdn_nw"][0],
        lru_conv_w=g["lru_conv_w"], lru_conv_b=g["lru_conv_b"][0],
        lru_wa=_diag_blocks(g["wa"]), lru_ba=g["ba"].reshape(8, 64), lru_wx=_diag_blocks(g["wx"]), lru_bx=g["bx"].reshape(8, 64),
        lru_lambda=g["lam"][0],
        sgu_ln_w=g["ln_w"][0], sgu_ln_b=g["ln_b"][0], sgu_ws=g["ws"].reshape(4, 128, 128), sgu_b=g["bst"][:, 0:4].T,
        sconv_w=g["sconv_w"], grp_norm_w=jnp.concatenate([g["gw0"], g["gw1"], g["gw2"]], axis=0),
    )


TM_MIX = 256


def _mixers_fwd(p, mp, tag="", xch=None):
    c = lambda *names: [(mp[n], False) for n in names]
    q, k, v, gb = _rowwise(_f_gdn_pre, [(p, 0, 1536, True), (p, OFF_BA // 128, 128, False)], c("gdn_conv_w", "alog", "dtb"),
                           [(DG, F32)] * 3 + [(128, F32)], tm=TM_MIX, name="gdn_pre" + tag)
    (o, s_all), xres = _gdn_fwd(q, k, v, gb, name="gdn_chunks" + tag, xch=xch)
    y_a, = _rowwise(_f_gdn_post, [(o, 0, DG, False), (p, OFF_Z // DG, DG, False)], c("gdn_nw"), [(DG, BF)], tm=TM_MIX, name="gdn_post" + tag)
    a, b = _rowwise(_f_lru_ab, [(p, OFF_LX // DG, DG, True)], c("lru_conv_w", "lru_conv_b", "wa", "ba", "wx", "bx", "lam"),
                    [(DG, F32)] * 2, tm=TM_MIX, name="lru_ab" + tag)
    hs = _scan_fwd(a, b, tm=TM_MIX, name="lru_scan" + tag)
    y_b, = _rowwise(_f_lru_post, [(hs, 0, DG, False), (p, OFF_LG // DG, DG, False)], c("gw0"), [(DG, BF)], tm=TM_MIX, name="lru_post" + tag)
    y_c, = _rowwise(_f_sgu, [(p, OFF_UV // 1024, 1024, False)], c("ln_w", "ln_b", "ws", "bst", "gw1"), [(DG, BF)], tm=TM_MIX, name="sgu" + tag)
    y_d, = _rowwise(_f_sconv, [(p, OFF_SB // DG, DG, False), (p, OFF_SC // DG, DG, True), (p, OFF_SH // DG, DG, True)],
                    c("sconv_w", "gw2"), [(DG, BF)], tm=TM_MIX, name="sconv" + tag)
    return jnp.concatenate([y_a, y_b, y_c, y_d], axis=1), (q, k, v, gb, o, s_all, a, hs), xres


def _mixers_bwd(p, mp, saved, dy, tag="", xch=None):
    q, k, v, gb, o, s_all, a, hs = saved
    c = lambda *names: [(mp[n], False, True) for n in names]
    g = {}

    (do, dz), (g["gdn_nw"],) = _rowwise_vjp(
        _f_gdn_post, [(o, 0, DG, False, F32), (p, OFF_Z // DG, DG, False, BF)], c("gdn_nw"), [(dy, 0, DG)], tm=TM_MIX, name="gdn_post_b" + tag)
    (dq, dk, dv, dgb), xres = _gdn_bwd(q, k, v, gb, s_all, do, name="gdn_chunks_b" + tag, xch=xch)
    (dqkv, dba), (g["gdn_conv_w"], g["alog"], g["dtb"]) = _rowwise_vjp(
        _f_gdn_pre, [(p, 0, 1536, True, BF), (p, OFF_BA // 128, 128, False, BF)], c("gdn_conv_w", "alog", "dtb"),
        [(dq, 0, DG), (dk, 0, DG), (dv, 0, DG), (dgb, 0, 128)], tm=TM_MIX, name="gdn_pre_b" + tag)

    (dhs, dgate), (g["gw0"],) = _rowwise_vjp(
        _f_lru_post, [(hs, 0, DG, False, F32), (p, OFF_LG // DG, DG, False, BF)], c("gw0"), [(dy, 1, DG)], tm=TM_MIX, name="lru_post_b" + tag)
    da, db = _scan_bwd(a, hs, dhs, tm=TM_MIX, name="lru_scan_b" + tag)
    (dlx,), (g["lru_conv_w"], g["lru_conv_b"], g["wa"], g["ba"], g["wx"], g["bx"], g["lam"]) = _rowwise_vjp(
        _f_lru_ab, [(p, OFF_LX // DG, DG, True, BF)], c("lru_conv_w", "lru_conv_b", "wa", "ba", "wx", "bx", "lam"),
        [(da, 0, DG), (db, 0, DG)], tm=TM_MIX, name="lru_ab_b" + tag)

    (duv,), (g["ln_w"], g["ln_b"], g["ws"], g["bst"], g["gw1"]) = _rowwise_vjp(
        _f_sgu, [(p, OFF_UV // 1024, 1024, False, BF)], c("ln_w", "ln_b", "ws", "bst", "gw1"), [(dy, 2, DG)], tm=TM_MIX, name="sgu_b" + tag)

    (dsb, dsc, dsh), (g["sconv_w"], g["gw2"]) = _rowwise_vjp(
        _f_sconv, [(p, OFF_SB // DG, DG, False, BF), (p, OFF_SC // DG, DG, True, BF), (p, OFF_SH // DG, DG, True, BF)],
        c("sconv_w", "gw2"), [(dy, 3, DG)], tm=TM_MIX, name="sconv_b" + tag)

    dp = jnp.concatenate([dqkv, dz, dlx, dgate, duv, dsb, dsc, dsh, dba], axis=1)
    return dp, g, xres


WEIGHTS = ("pre_mix_norm", "w_in", "gdn_conv_w", "gdn_a_log", "gdn_dt_bias", "gdn_norm_w", "lru_conv_w", "lru_conv_b", "lru_wa",
           "lru_ba", "lru_wx", "lru_bx", "lru_lambda", "sgu_ln_w", "sgu_ln_b", "sgu_ws", "sgu_b", "sconv_w", "grp_norm_w", "w_out",
           "post_mix_norm", "pre_ffn_norm", "ffn_up", "ffn_conv_w", "ffn_conv_b", "ffn_down", "post_ffn_norm")
BIG = ("w_in", "ffn_up", "w_out", "ffn_down")
CHIP_SHARDED_SMALL = ("gdn_conv_w", "lru_conv_w", "sconv_w", "grp_norm_w", "ffn_conv_w")
MIXER_PARAMS = ("gdn_conv_w", "gdn_a_log", "gdn_dt_bias", "gdn_norm_w", "lru_conv_w", "lru_conv_b", "lru_wa", "lru_ba", "lru_wx",
                "lru_bx", "lru_lambda", "sgu_ln_w", "sgu_ln_b", "sgu_ws", "sgu_b", "sconv_w", "grp_norm_w")
TM_ROW = 256
FF_TILE = 256
FF_ROWS = 2048
N_FF_TILES = D_FF // FF_TILE
PACK_ROWS = 256
FFN_UP_CUTS = (512, 1536, 1872)
TK_DW = 4096


def _pack(arrs):
    parts = []
    for a in arrs:
        n = a.size
        parts.append(jnp.pad(a.reshape(-1), (0, -n % 1024)).reshape(-1, 128))
    rows = sum(p.shape[0] for p in parts)
    parts.append(jnp.zeros((-rows % PACK_ROWS, 128), F32))
    return jnp.concatenate(parts, axis=0)


def _unpack(buf, shapes):
    out, row = [], 0
    for s in shapes:
        n = math.prod(s)
        rows = -(-n // 1024) * 8
        out.append(buf[row:row + rows].reshape(-1)[:n].reshape(s))
        row += rows
    return out


def _row_tile(rows, cols):
    return 256 if rows % 256 == 0 and cols <= 1024 else 128


def _w_in_full(got):
    c = N_IN // 4
    z = jnp.zeros((D, NP - N_IN), got.dtype)
    return jnp.concatenate([got[0], got[1][:, :2048 - c], got[1][:, 2056 - c:], got[2], got[3], got[1][:, 2048 - c:2056 - c], z], axis=1)


def _w_in_blocks(dw):
    c = N_IN // 4
    cut = 2048 + 2 * c - 2056
    b1 = jnp.concatenate([dw[:, c:2048], dw[:, OFF_BA:OFF_BA + 8], dw[:, 2048:cut]], axis=1)
    return jnp.stack([dw[:, 0:c], b1, dw[:, cut:cut + c], dw[:, cut + c:OFF_BA]])


def _layer_fwd(l, xs, h, w_in_l, w_out_g, shard, sp, mp, nxt):
    t = str(l)
    up = shard["ffn_up"][l]
    c0, c1, c2 = FFN_UP_CUTS
    first = [up[:c0]] + ([shard["w_out"][l]] if w_out_g is None else [])
    p, got = _matmul(h, w_in_l, "nn", F32, tm=1024, tn=1920, tk=D, name="mm_in" + t, xch=_GatherChips(first))
    w_out_l = (got[1] if w_out_g is None else w_out_g).reshape(D, D)
    ycat, saved, (w_up_b,) = _mixers_fwd(p, mp, tag=t, xch=_GatherChips([up[c0:c1]]))
    y, (w_up_c,) = _matmul(ycat, w_out_l, "nn", F32, tm=1024, tn=1024, tk=D, name="mm_out" + t, xch=_GatherChips([up[c1:c2]]))
    (x1, h2), (w_up_d,) = _rowwise(
        _f_post_pre, [(xs, 0, D, False), (y, 0, D, False)], [(sp["post_mix_norm"][l], False), (sp["pre_ffn_norm"][l], False)],
        [(D, F32), (D, BF)], tm=TM_ROW, name="post_mix" + t, xch=_GatherChips([up[c2:]]))
    w_up = jnp.concatenate([got[0], w_up_b, w_up_c, w_up_d], axis=1)
    u, (w_dn_g,) = _matmul(h2, w_up, "nn", F32, tm=1024, tn=1408, tk=D, name="mm_up" + t, b_blocks=True,
                           xch=_GatherChips([shard["ffn_down"][l]]))
    ffn_rows = [(u, 0, FF_TILE, True), (u, N_FF_TILES, FF_TILE, True)]
    ffn_consts = [(c, True) for c in sp["ffn_conv"][l]]
    ffn_tm = min(FF_ROWS, u.shape[0])
    half = D // 2
    if nxt is None:
        act, = _rowwise(_f_ffn_act, ffn_rows, ffn_consts, [(FF_TILE, BF)], tm=ffn_tm, ncol=N_FF_TILES, name="ffn_act" + t)
    else:
        (act,), (n_out, n_in_a) = _rowwise(_f_ffn_act, ffn_rows, ffn_consts, [(FF_TILE, BF)], tm=ffn_tm, ncol=N_FF_TILES, name="ffn_act" + t,
                                           xch=_GatherChips([nxt[1], nxt[0][:half]]))
    w_dn_l = w_dn_g.reshape(D_FF, D)
    y2, n_in_b = _matmul(act, w_dn_l, "nn", F32, tm=1024, tn=1024, tk=2816, name="mm_down" + t,
                         xch=None if nxt is None else _GatherChips([nxt[0][half:]]))
    keep = dict(xs=xs, h=h, p=p, saved=saved, ycat=ycat, y=y, x1=x1, h2=h2, u=u, act=act, y2=y2,
                w_in=w_in_l, w_out=w_out_l, w_up=w_up, w_dn=w_dn_l)
    return keep, (None if nxt is None else (jnp.concatenate([n_in_a, n_in_b[0]], axis=1), n_out))


def _layer_bwd(l, a, dx1, dy2, sp, mp):
    t = str(l)
    g = {}
    dact, _ = _matmul(dy2, a["w_dn"], "nt", F32, tm=1024, tn=1408, tk=D, name="mm_down_dx" + t)
    dw_dn, _ = _matmul(a["act"], dy2, "tn", F32, tm=512, tn=1024, tk=TK_DW, name="mm_down_dw" + t)
    dw_dn = dw_dn.reshape(4, D_FF // 4, D)
    (dug, duv), gc = _rowwise_vjp(
        _f_ffn_act, [(a["u"], 0, FF_TILE, True, BF), (a["u"], N_FF_TILES, FF_TILE, True, BF)], [(c, True, True) for c in sp["ffn_conv"][l]],
        [(dact, 0, FF_TILE)], tm=min(FF_ROWS, dact.shape[0]), ncol=N_FF_TILES, name="ffn_act_b" + t)
    g["ffn_conv_w"] = jnp.concatenate([gc[0], gc[1]], axis=1)
    g["ffn_conv_b"] = jnp.concatenate([gc[2], gc[3]], axis=1)[0]
    dh2, (r_dn,) = _matmul((dug, duv), a["w_up"], "nt", F32, tm=1024, tn=1024, tk=2816, name="mm_up_dx" + t, b_blocks=True,
                           xch=_ScatterChips([dw_dn.astype(BF)]))
    dw_up, _ = _matmul(a["h2"], (dug, duv), "tn", F32, tm=512, tn=1408, tk=TK_DW // 2, name="mm_up_dw" + t, out_blocks=True)
    (dxs, dy), (gpm, gpf) = _rowwise_vjp(
        _f_post_pre, [(a["xs"], 0, D, False, F32), (a["y"], 0, D, False, BF)],
        [(sp["post_mix_norm"][l], False, True), (sp["pre_ffn_norm"][l], False, True)], [(dx1, 0, D), (dh2, 0, D)], tm=TM_ROW, name="post_mix_b" + t)
    g["post_mix_norm"], g["pre_ffn_norm"] = gpm[0], gpf[0]
    dycat, _ = _matmul(dy, a["w_out"], "nt", F32, tm=1024, tn=1024, tk=D, name="mm_out_dx" + t)
    dw_out, _ = _matmul(a["ycat"], dy, "tn", F32, tm=512, tn=1024, tk=TK_DW, name="mm_out_dw" + t)
    dw_out = dw_out.reshape(4, D // 4, D)
    dp, gm, (r_up,) = _mixers_bwd(a["p"], mp, a["saved"], dycat, tag=t, xch=_ScatterChips([dw_up.astype(BF)]))
    g.update(_mixer_param_grads(gm))
    dw_in, (r_out,) = _matmul(a["h"], dp, "tn", F32, tm=512, tn=1920, tk=TK_DW, name="mm_in_dw" + t, xch=_ScatterChips([dw_out.astype(BF)]))
    dw_in = _w_in_blocks(dw_in)
    dh, (r_in,) = _matmul(dp, a["w_in"], "nt", F32, tm=512, tn=1024, tk=NP, name="mm_in_dx" + t, xch=_ScatterChips([dw_in.astype(BF)]))
    big = {"ffn_down": (dw_dn, r_dn), "ffn_up": (dw_up, r_up), "w_out": (dw_out, r_out), "w_in": (dw_in, r_in)}
    return dxs, dh, g, big


def kernel(x, pre_mix_norm, w_in, gdn_conv_w, gdn_a_log, gdn_dt_bias, gdn_norm_w, lru_conv_w, lru_conv_b, lru_wa, lru_ba, lru_wx, lru_bx, lru_lambda, sgu_ln_w, sgu_ln_b, sgu_ws, sgu_b, sconv_w, grp_norm_w, w_out, post_mix_norm, pre_ffn_norm, ffn_up, ffn_conv_w, ffn_conv_b, ffn_down, post_ffn_norm, loss_target, m_pre_mix_norm, m_w_in, m_gdn_conv_w, m_gdn_a_log, m_gdn_dt_bias, m_gdn_norm_w, m_lru_conv_w, m_lru_conv_b, m_lru_wa, m_lru_ba, m_lru_wx, m_lru_bx, m_lru_lambda, m_sgu_ln_w, m_sgu_ln_b, m_sgu_ws, m_sgu_b, m_sconv_w, m_grp_norm_w, m_w_out, m_post_mix_norm, m_pre_ffn_norm, m_ffn_up, m_ffn_conv_w, m_ffn_conv_b, m_ffn_down, m_post_ffn_norm, v_pre_mix_norm, v_w_in, v_gdn_conv_w, v_gdn_a_log, v_gdn_dt_bias, v_gdn_norm_w, v_lru_conv_w, v_lru_conv_b, v_lru_wa, v_lru_ba, v_lru_wx, v_lru_bx, v_lru_lambda, v_sgu_ln_w, v_sgu_ln_b, v_sgu_ws, v_sgu_b, v_sconv_w, v_grp_norm_w, v_w_out, v_post_mix_norm, v_pre_ffn_norm, v_ffn_up, v_ffn_conv_w, v_ffn_conv_b, v_ffn_down, v_post_ffn_norm):
    given = dict(locals())
    me = 2 * lax.axis_index("x") + lax.axis_index("y")
    xs0, tgt = x[0], loss_target[0]

    small_sh = [given[n] for n in CHIP_SHARDED_SMALL]
    shard = {n: [given[n][l].astype(BF) for l in range(DEPTH)] for n in BIG}
    got = _exchange(_GatherChipsTwoLevel([shard["w_in"][0], _pack(small_sh)]), name="gather_first")
    full = {n: given[n] for n in WEIGHTS if n not in BIG and n not in CHIP_SHARDED_SMALL}
    per_chip = [_unpack(got[1][j], [s.shape for s in small_sh]) for j in range(4)]
    parts = [jnp.stack([per_chip[j][i] for j in range(4)]) for i in range(len(small_sh))]
    for n, pj in zip(CHIP_SHARDED_SMALL, parts):
        full[n] = pj.transpose(1, 2, 0, 3).reshape(pj.shape[1], pj.shape[2], 4 * pj.shape[3])
    sp = {n: [full[n][l:l + 1] for l in range(DEPTH)] for n in ("pre_mix_norm", "post_mix_norm", "pre_ffn_norm", "post_ffn_norm")}
    sp["ffn_conv"] = [[full["ffn_conv_w"][l][:, :D_FF], full["ffn_conv_w"][l][:, D_FF:], full["ffn_conv_b"][l:l + 1, :D_FF],
                       full["ffn_conv_b"][l:l + 1, D_FF:]] for l in range(DEPTH)]
    mps = [_mixer_params({n: full[n][l] for n in MIXER_PARAMS}) for l in range(DEPTH)]

    h, = _rowwise(_f_pre, [(xs0, 0, D, False)], [(sp["pre_mix_norm"][0], False)], [(D, BF)], tm=TM_ROW, name="pre_mix0")
    a0, (w_in1, w_out1) = _layer_fwd(0, xs0, h, _w_in_full(got[0]), None, shard, sp, mps[0], (shard["w_in"][1], shard["w_out"][1]))
    xs1, h1 = _rowwise(_f_post_pre, [(a0["x1"], 0, D, False), (a0["y2"], 0, D, False)],
                       [(sp["post_ffn_norm"][0], False), (sp["pre_mix_norm"][1], False)], [(D, F32), (D, BF)], tm=TM_ROW, name="post_ffn0")
    a1, _ = _layer_fwd(1, xs1, h1, _w_in_full(w_in1), w_out1, shard, sp, mps[1], None)
    lacc, dxo = _loss_head(a1["x1"], a1["y2"], sp["post_ffn_norm"][1], tgt, tm=TM_ROW, name="loss_head")

    gl = [None, None]
    (dx1, dy2), (gpf1,) = _rowwise_vjp(_f_post, [(a1["x1"], 0, D, False, F32), (a1["y2"], 0, D, False, BF)],
                                      [(sp["post_ffn_norm"][1], False, True)], [(dxo, 0, D)], tm=TM_ROW, name="post_ffn1_b")
    big = [None, None]
    dxs1, dh1, gl[1], big[1] = _layer_bwd(1, a1, dx1, dy2, sp, mps[1])
    gl[1]["post_ffn_norm"] = gpf1[0]
    (dx1, dy2), (gpf0, gpm1) = _rowwise_vjp(
        _f_post_pre, [(a0["x1"], 0, D, False, F32), (a0["y2"], 0, D, False, BF)],
        [(sp["post_ffn_norm"][0], False, True), (sp["pre_mix_norm"][1], False, True)], [(dxs1, 0, D), (dh1, 0, D)], tm=TM_ROW, name="post_ffn0_b")
    gl[1]["pre_mix_norm"] = gpm1[0]
    dxs0, dh0, gl[0], big[0] = _layer_bwd(0, a0, dx1, dy2, sp, mps[0])
    gl[0]["post_ffn_norm"] = gpf0[0]
    (grad_x,), (gpm0,) = _rowwise_vjp(lambda xv, w: (xv, _rms(xv, w)), [(xs0, 0, D, False, F32)], [(sp["pre_mix_norm"][0], False, True)],
                                     [(dxs0, 0, D), (dh0, 0, D)], tm=TM_ROW, name="pre_mix0_b")
    gl[0]["pre_mix_norm"] = gpm0[0]

    small = [n for n in WEIGHTS if n not in BIG]
    gfull = {n: jnp.stack([gl[0][n], gl[1][n]]) for n in small}
    small_grads = _pack([gfull[n] for n in small] + [lacc[0, 0:1]])
    sums = {}
    for n in BIG:
        for l in range(DEPTH):
            blocks, recv = big[l][n]
            own = lax.dynamic_index_in_dim(blocks, me, 0, keepdims=False)
            sums[n] = sums.get(n, []) + [_sum_parts(own, recv, tr=_row_tile(*own.shape), name="sum_grads_%s%d" % (n, l))]
    kinds = ("grad", "delta", "new_m", "new_v")
    outs = {kind: {} for kind in kinds}
    order = ("w_out", "ffn_down", "w_in", "ffn_up")
    other = _exchange(_SwapCores(sums[order[0]]), name="swap_first")
    gathered = None
    for i, n in enumerate(order):
        riders = []
        if i + 1 < len(order):
            riders.append(_SwapCores(sums[order[i + 1]]))
        if n == "ffn_down":
            riders.append(_GatherDevices(small_grads))
        shp = given[n].shape
        r4, got = _adamw(given[n], given["m_" + n], given["v_" + n], [[s, o] for s, o in zip(sums[n], other)],
                         tr=_row_tile(shp[1], shp[2]), name="adamw_" + n, xch=_Together(riders) if riders else None)
        outs_n = dict(zip(kinds, r4))
        for kind in kinds:
            outs[kind][n] = outs_n[kind]
        if i + 1 < len(order):
            other = got[:DEPTH]
        if n == "ffn_down":
            gathered = got[DEPTH]

    tot = _sum_slots(gathered, name="sum_small_grads")
    red = dict(zip(small + ["loss"], _unpack(tot, [gfull[n].shape for n in small] + [(1,)])))
    for n in CHIP_SHARDED_SMALL:
        cb = given[n].shape[-1]
        red[n] = lax.dynamic_slice_in_dim(red[n], me * cb, cb, axis=red[n].ndim - 1)
    shapes = [given[n].shape for n in small]
    res, _ = _adamw(_pack([given[n] for n in small])[None], _pack([given["m_" + n] for n in small])[None],
                    _pack([given["v_" + n] for n in small])[None], [[_pack([red[n] for n in small])]], tr=PACK_ROWS, name="adamw_small")
    for kind, r in zip(kinds, res):
        outs[kind].update(zip(small, _unpack(r[0], shapes)))

    return (red["loss"][0], grad_x[None], *[outs[k][n] for k in ("grad", "delta", "new_m", "new_v") for n in WEIGHTS])
```

```python
import functools
import math

import jax
import jax.numpy as jnp
from jax import lax
from jax.experimental import pallas as pl
from jax.experimental.pallas import tpu as pltpu

F32 = jnp.float32
BF = jnp.bfloat16
MESH_ID = pl.DeviceIdType.MESH

EPS = 1e-6
DEPTH = 2
D = 2048
DG = 512
HEADS = 4
HD = 128
CHUNK = 64
GDN_STEP = 4
LRU_C = 8.0
D_FF = 5632
N_IN = 5640
NP = 5760
OFF_Q, OFF_Z, OFF_LX, OFF_LG, OFF_UV, OFF_SB, OFF_SC, OFF_SH, OFF_BA = 0, 1536, 2048, 2560, 3072, 4096, 4608, 5120, 5632

ADAM_LR, ADAM_B1, ADAM_B2, ADAM_EPS, ADAM_WD, ADAM_STEP = 0.001, 0.9, 0.999, 1e-08, 0.01, 10

HALO = 8


def _mk_bdot(ca, cb):
    na, nb = 1 - ca, 1 - cb

    def dg(x, y, cx, cy):
        return lax.dot_general(x.astype(BF), y.astype(BF), (((cx,), (cy,)), ((), ())), preferred_element_type=F32)

    @jax.custom_vjp
    def f(a, b):
        return dg(a, b, ca, cb)

    def fwd(a, b):
        return dg(a, b, ca, cb), (a, b)

    def bwd(res, g):
        a, b = res
        da = dg(g, b, 1, nb) if ca == 1 else dg(b, g, nb, 1)
        db = dg(a, g, na, 0) if cb == 0 else dg(g, a, 0, na)
        return da.astype(a.dtype), db.astype(b.dtype)

    f.defvjp(fwd, bwd)
    return f


_bdot = _mk_bdot(1, 0)
_bdot_nt = _mk_bdot(1, 1)
_bdot_tn = _mk_bdot(0, 0)


def _sigmoid(x):
    return 1.0 / (1.0 + jnp.exp(-x))


def _silu(x):
    return x * _sigmoid(x)


def _gelu(x):
    return 0.5 * x * (1.0 + jnp.tanh(0.7978845608028654 * (x + 0.044715 * (x * x * x))))


def _log1p(z):
    u = 1.0 + z
    d = u - 1.0
    return jnp.where(d == 0.0, z, jnp.log(u) * (z / jnp.where(d == 0.0, 1.0, d)))


def _softplus(x):
    return jnp.maximum(x, 0.0) + _log1p(jnp.exp(-jnp.abs(x)))


def _neg_expm1(y):
    t = jnp.tanh(0.5 * y)
    return -2.0 * t / (1.0 - t)


def _rms(x, w):
    return x * lax.rsqrt(jnp.mean(x * x, axis=-1, keepdims=True) + EPS) * w


def _rows(x):
    return lax.broadcasted_iota(jnp.int32, x.shape, 0)


def _mk_shift():
    @functools.partial(jax.custom_vjp, nondiff_argnums=(1,))
    def shift(xx, s):
        n = xx.shape[0]
        return pltpu.roll(xx, s, axis=0)[HALO:n] if s else xx[HALO:n]

    def fwd(xx, s):
        return shift(xx, s), None

    def bwd(s, _, g):
        ext = jnp.concatenate([g, jnp.zeros((HALO, g.shape[1]), g.dtype)], axis=0)
        return (pltpu.roll(ext, HALO - s, axis=0),)

    shift.defvjp(fwd, bwd)
    return shift


_shift = _mk_shift()


def _mk_chunk_cumsum():
    def run(x, up):
        n = x.shape[0]
        pos = _rows(x) % CHUNK
        d = 1
        while d < CHUNK:
            if up:
                x = x + jnp.where(pos < CHUNK - d, pltpu.roll(x, n - d, axis=0), 0.0)
            else:
                x = x + jnp.where(pos >= d, pltpu.roll(x, d, axis=0), 0.0)
            d *= 2
        return x

    @jax.custom_vjp
    def cumsum(x):
        return run(x, False)

    cumsum.defvjp(lambda x: (run(x, False), None), lambda _, g: (run(g, True),))
    return cumsum


_chunk_cumsum = _mk_chunk_cumsum()


def _causal_conv(xx, w):
    K = w.shape[0]
    y = _shift(xx, K - 1) * w[0:1, :]
    for k in range(1, K):
        y = y + _shift(xx, K - 1 - k) * w[k:k + 1, :]
    return y


def _f_pre(x, w):
    return (_rms(x, w),)


def _f_post_pre(x, y, w_post, w_pre):
    x1 = x + _rms(y, w_post)
    return x1, _rms(x1, w_pre)


def _f_post(x, y, w_post):
    return (x + _rms(y, w_post),)


def _heads(fn, *xs):
    return jnp.concatenate([fn(*[x[:, h * HD:(h + 1) * HD] for x in xs]) for h in range(HEADS)], axis=1)


def _l2n(t):
    return t * lax.rsqrt(jnp.sum(t * t, axis=-1, keepdims=True) + EPS)


def _f_gdn_pre(qkv, ba, conv_w, alog, dtb):
    c = _silu(_causal_conv(qkv, conv_w))
    q = _heads(lambda t: _l2n(t) * (HD ** -0.5), c[:, 0:DG])
    k = _heads(_l2n, c[:, DG:2 * DG])
    v = c[:, 2 * DG:3 * DG]
    lane = lax.broadcasted_iota(jnp.int32, ba.shape, 1)
    beta = _sigmoid(ba)
    gcum = _chunk_cumsum(-jnp.exp(alog) * _softplus(ba + dtb))
    gc = jnp.where(lane < HEADS, beta, jnp.where(lane < 2 * HEADS, gcum, 0.0))
    return q, k, v, gc


def _f_gdn_chunk(q, k, v, gc, s0):
    C = q.shape[0]
    HC = HEADS * C
    sh = C.bit_length() - 1

    def stack(x):
        return jnp.concatenate([x[:, h * HD:(h + 1) * HD] for h in range(HEADS)], axis=0)

    def column(off):
        lane = lax.broadcasted_iota(jnp.int32, gc.shape, 1)
        return jnp.concatenate([jnp.sum(jnp.where(lane == off + h, gc, 0.0), axis=1, keepdims=True) for h in range(HEADS)], axis=0)

    def head(x, h):
        return x[h * C:(h + 1) * C]

    r = lax.broadcasted_iota(jnp.int32, (HC, HC), 0)
    c = lax.broadcasted_iota(jnp.int32, (HC, HC), 1)
    same = jnp.right_shift(r, sh) == jnp.right_shift(c, sh)
    causal = jnp.logical_and(same, r >= c)
    strict = jnp.logical_and(same, r > c)
    gcol, bcol = column(HEADS), column(0)
    grow = jnp.sum(jnp.where(r == c, gcol, 0.0), axis=0, keepdims=True)
    decay = jnp.where(causal, jnp.exp(jnp.where(causal, gcol - grow, 0.0)), 0.0)
    ks, qs, vs = stack(k), stack(q), stack(v)
    kb = ks * bcol
    kk = _bdot_nt(jnp.concatenate([kb, qs], axis=0), ks)
    m = jnp.where(strict, kk[0:HC] * decay, 0.0)
    attn = jnp.where(causal, kk[HC:2 * HC] * decay, 0.0)
    n = -m
    t = (r == c).astype(F32) + n
    p = n
    for _ in range(5):
        p = _bdot(p, p)
        t = t + _bdot(t, p)
    eg = jnp.exp(gcol)
    wu = _bdot(t, jnp.concatenate([kb * eg, vs * bcol], axis=1))
    w, u = wu[:, 0:HD], wu[:, HD:2 * HD]
    last = jnp.logical_and(same, jnp.bitwise_and(c, C - 1) == C - 1)
    glast = jnp.sum(jnp.where(last, grow, 0.0), axis=1, keepdims=True)
    k_g = ks * jnp.exp(glast - gcol)
    q_g = qs * eg
    ws = [_bdot(jnp.concatenate([head(w, h), head(q_g, h)], axis=0), s0[h * HD:(h + 1) * HD]) for h in range(HEADS)]
    v_new = u - jnp.concatenate([x[0:C] for x in ws], axis=0)
    o = jnp.concatenate([x[C:2 * C] for x in ws], axis=0) + _bdot(attn, v_new)
    s1 = [s0[h * HD:(h + 1) * HD] * jnp.exp(glast[h * C:h * C + 1]) + _bdot_tn(head(k_g, h), head(v_new, h)) for h in range(HEADS)]
    return jnp.concatenate([head(o, h) for h in range(HEADS)], axis=1), jnp.concatenate(s1, axis=0)


def _f_gdn_chunks(q, k, v, gc, s0):
    outs, s = [], s0
    for n in range(q.shape[0] // CHUNK):
        rs = slice(n * CHUNK, (n + 1) * CHUNK)
        o, s = _f_gdn_chunk(q[rs], k[rs], v[rs], gc[rs], s)
        outs.append(o)
    return jnp.concatenate(outs, axis=0), s


def _f_gdn_post(o, z, nw):
    return (_heads(lambda a, b: _rms(a, nw) * _silu(b), o, z),)


def _f_lru_ab(lx, conv_w, conv_b, wa, ba, wx, bx, lam):
    xc = _causal_conv(lx, conv_w) + conv_b
    r = _sigmoid(_bdot(xc, wa) + ba)
    i = _sigmoid(_bdot(xc, wx) + bx)
    log_a = -LRU_C * r * _softplus(-lam)
    a = jnp.exp(log_a)
    mult = jnp.sqrt(_neg_expm1(2.0 * log_a))
    return a, mult * (i * xc)


def _f_lru_post(hs, gate, gw):
    return (_rms(hs * _gelu(gate), gw),)


def _f_sgu(uv, ln_w, ln_b, ws, bst, gw):
    tm = uv.shape[0]
    uvf = _gelu(uv)
    u, v = uvf[:, 0:DG], uvf[:, DG:2 * DG]
    mu = jnp.mean(v, axis=-1, keepdims=True)
    vc = v - mu
    v = vc * lax.rsqrt(jnp.mean(vc * vc, axis=-1, keepdims=True) + EPS) * ln_w + ln_b
    lane = lax.broadcasted_iota(jnp.int32, bst.shape, 1)
    tril = lax.broadcasted_iota(jnp.int32, (128, 128), 0) >= lax.broadcasted_iota(jnp.int32, (128, 128), 1)
    wsm = [jnp.where(tril, ws[g * 128:(g + 1) * 128, :], 0.0) for g in range(4)]
    bias = [jnp.sum(jnp.where(lane == g, bst, 0.0), axis=1, keepdims=True) for g in range(4)]
    out = []
    for n in range(tm // 128):
        vn = v[n * 128:(n + 1) * 128, :]
        gs = [_bdot(wsm[g], vn[:, g * 128:(g + 1) * 128]) + bias[g] for g in range(4)]
        out.append(jnp.concatenate(gs, axis=1))
    vo = jnp.concatenate(out, axis=0) if len(out) > 1 else out[0]
    return (_rms(u * vo, gw),)


def _f_sconv(sb, sc, sh, conv_w, gw):
    return (_rms(sb * _causal_conv(sc * sh, conv_w), gw),)


def _f_ffn_act(ug, uv, wg, wv, bg, bv):
    return (_gelu(_causal_conv(ug, wg) + bg) * (_causal_conv(uv, wv) + bv),)


def _row_specs(rows, consts, tm, ncol, tile_of):
    specs, args = [], []
    for arr, cb, w, halo in rows:
        specs.append(pl.BlockSpec((tm, w), lambda j, i, cb=cb: (tile_of(i), cb + j)))
        args.append(arr)
        if halo:
            specs.append(pl.BlockSpec((HALO, w), lambda j, i, cb=cb: (jnp.maximum(tile_of(i) * (tm // HALO) - 1, 0), cb + j)))
            args.append(arr)
    for arr, tiled in consts:
        r, c = arr.shape
        specs.append(pl.BlockSpec((r, c // ncol), lambda j, i: (0, j)) if tiled else pl.BlockSpec((r, c), lambda j, i: (0, 0)))
        args.append(arr)
    return specs, args


def _load_rows(refs, rows, consts, tile):
    k, vals = 0, []
    for _arr, _cb, _w, halo in rows:
        t = refs[k][...].astype(F32)
        k += 1
        if halo:
            hl = jnp.where(tile > 0, refs[k][...].astype(F32), 0.0)
            k += 1
            t = jnp.concatenate([hl, t], axis=0)
        vals.append(t)
    for _ in consts:
        vals.append(refs[k][...])
        k += 1
    return vals, k


def _rowwise(fn, rows, consts, outs, *, tm, name, ncol=1, xch=None):
    n = rows[0][0].shape[0]
    nt = n // tm
    in_specs, args = _row_specs(rows, consts, tm, ncol, lambda i: i)

    def body(*refs):
        vals, k = _load_rows(refs, rows, consts, pl.program_id(1))
        for o_ref, r in zip(refs[k:], fn(*vals)):
            o_ref[...] = r.astype(o_ref.dtype)

    res, xres = _call(
        body, grid=(ncol, nt), in_specs=in_specs,
        out_specs=[pl.BlockSpec((tm, w), lambda j, i: (i, j)) for w, _ in outs],
        out_shape=[jax.ShapeDtypeStruct((n, w * ncol), dt) for w, dt in outs], scratch=[], name=name, args=args, xch=xch)
    return res if xch is None else (res, xres)


def _rowwise_vjp(fn, rows, consts, cots, *, tm, name, ncol=1):
    n = rows[0][0].shape[0]
    nt = n // tm
    rows4 = [r[:4] for r in rows]
    consts2 = [c[:2] for c in consts]
    in_specs, args = _row_specs(rows4, consts2, tm, ncol, lambda i: nt - 1 - i)
    for arr, cb, w in cots:
        in_specs.append(pl.BlockSpec((tm, w), lambda j, i, cb=cb: (nt - 1 - i, cb + j)))
        args.append(arr)
    out_specs, out_shape, scratch = [], [], []
    for arr, cb, w, halo, gdt in rows:
        if gdt is not None:
            out_specs.append(pl.BlockSpec((tm, w), lambda j, i: (nt - 1 - i, j)))
            out_shape.append(jax.ShapeDtypeStruct((n, w * ncol), gdt))
            if halo:
                scratch.append(pltpu.VMEM((HALO, w), F32))
    for arr, tiled, want in consts:
        if want:
            r, c = arr.shape
            out_specs.append(pl.BlockSpec((r, c // ncol), lambda j, i: (0, j)) if tiled else pl.BlockSpec((r, c), lambda j, i: (0, 0)))
            out_shape.append(jax.ShapeDtypeStruct((r, c), F32))
    n_in = len(in_specs)
    n_out = len(out_specs)

    def body(*refs):
        j, i = pl.program_id(0), pl.program_id(1)
        tile = nt - 1 - i
        vals, k = _load_rows(refs, rows4, consts2, tile)
        cvals = [refs[k + q][...].astype(F32) for q in range(len(cots))]
        outs = refs[n_in:n_in + n_out]
        carries = refs[n_in + n_out:]
        _, vjp = jax.vjp(fn, *vals)
        g = vjp(tuple(cvals))
        o, cidx = 0, 0
        for q, (arr, cb, w, halo, gdt) in enumerate(rows):
            if gdt is None:
                continue
            if halo:
                ge = g[q]
                main = ge[HALO:]
                carry = carries[cidx]
                cidx += 1
                tail = main[tm - HALO:] + jnp.where(i > 0, carry[...], 0.0)
                outs[o][0:tm - HALO, :] = main[0:tm - HALO].astype(gdt)
                outs[o][tm - HALO:tm, :] = tail.astype(gdt)
                carry[...] = ge[0:HALO]
            else:
                outs[o][...] = g[q].astype(gdt)
            o += 1
        for q, (arr, tiled, want) in enumerate(consts):
            if not want:
                continue
            first = (i == 0) if tiled else jnp.logical_and(i == 0, j == 0)
            acc = outs[o]
            gq = g[len(rows) + q].astype(F32)

            @pl.when(first)
            def _(acc=acc, gq=gq):
                acc[...] = gq

            @pl.when(jnp.logical_not(first))
            def _(acc=acc, gq=gq):
                acc[...] += gq

            o += 1

    res = pl.pallas_call(
        body, grid=(ncol, nt), in_specs=in_specs, out_specs=out_specs, out_shape=out_shape, scratch_shapes=scratch,
        name=name, compiler_params=pltpu.CompilerParams(dimension_semantics=("arbitrary", "arbitrary")),
    )(*args)
    nrow = sum(1 for r in rows if r[4] is not None)
    return list(res[:nrow]), list(res[nrow:])


_ANY = pl.BlockSpec(memory_space=pl.ANY)
_CHIP_REL = ((1, 0), (0, 1), (1, 1))
_DEV_REL = tuple((r >> 2 & 1, r >> 1 & 1, r & 1) for r in range(1, 8))


def _flip(v, f):
    return 1 - v if f else v


class _GatherChips:
    def __init__(self, shards):
        self.arrs = list(shards)
        n = len(self.arrs)
        self.out_shape = [jax.ShapeDtypeStruct((4,) + s.shape, s.dtype) for s in self.arrs]
        self.sems = [pltpu.SemaphoreType.DMA((3 * n,)), pltpu.SemaphoreType.DMA((3 * n,)), pltpu.SemaphoreType.DMA((n,))]

    def _copies(self, ins, outs, sems, arriving):
        send, recv, lsem = sems
        x, y, c = lax.axis_index("x"), lax.axis_index("y"), lax.axis_index("c")
        me = 2 * x + y
        if arriving:
            local = []
        else:
            local = [pltpu.make_async_copy(ins[a], outs[a].at[me], lsem.at[a]) for a in range(len(ins))]
        remote = []
        for a in range(len(ins)):
            for k, (fx, fy) in enumerate(_CHIP_REL):
                px, py = _flip(x, fx), _flip(y, fy)
                remote.append(pltpu.make_async_remote_copy(
                    src_ref=ins[a], dst_ref=outs[a].at[2 * px + py if arriving else me], send_sem=send.at[3 * a + k],
                    recv_sem=recv.at[3 * a + k], device_id=(px, py, c), device_id_type=MESH_ID))
        return local, remote

    def issue(self, ins, outs, sems):
        local, push = self._copies(ins, outs, sems, False)
        for cp in local + push:
            cp.start()

    def finish(self, ins, outs, sems):
        for cp in self._copies(ins, outs, sems, True)[1]:
            cp.wait_recv()
        local, push = self._copies(ins, outs, sems, False)
        for cp in push:
            cp.wait_send()
        for cp in local:
            cp.wait()


class _GatherChipsTwoLevel:
    def __init__(self, shards):
        self.arrs = list(shards)
        n = len(self.arrs)
        self.out_shape = [jax.ShapeDtypeStruct((4,) + s.shape, s.dtype) for s in self.arrs]
        self.sems = [pltpu.SemaphoreType.DMA((3 * n,)) for _ in range(4)] + [pltpu.SemaphoreType.DMA((n,))]

    def _far(self, ins, outs, sems, arriving):
        send, recv = sems[0], sems[1]
        x, y, c = lax.axis_index("x"), lax.axis_index("y"), lax.axis_index("c")
        me = 2 * x + y
        copies = []
        for a in range(len(ins)):
            half = ins[a].shape[0] // 2
            rows = pl.ds(c * half, half)
            for k, (fx, fy) in enumerate(_CHIP_REL):
                px, py = _flip(x, fx), _flip(y, fy)
                copies.append(pltpu.make_async_remote_copy(
                    src_ref=ins[a].at[rows], dst_ref=outs[a].at[2 * px + py if arriving else me, rows], send_sem=send.at[3 * a + k],
                    recv_sem=recv.at[3 * a + k], device_id=(px, py, c), device_id_type=MESH_ID))
        return copies

    def _near(self, ins, outs, sems, arriving):
        send, recv = sems[2], sems[3]
        x, y, c = lax.axis_index("x"), lax.axis_index("y"), lax.axis_index("c")
        copies = []
        for a in range(len(ins)):
            half = ins[a].shape[0] // 2
            rows = pl.ds((1 - c if arriving else c) * half, half)
            for k, (fx, fy) in enumerate(_CHIP_REL):
                block = outs[a].at[2 * _flip(x, fx) + _flip(y, fy), rows]
                copies.append(pltpu.make_async_remote_copy(
                    src_ref=block, dst_ref=block, send_sem=send.at[3 * a + k], recv_sem=recv.at[3 * a + k],
                    device_id=(x, y, 1 - c), device_id_type=MESH_ID))
        return copies

    def _local(self, ins, outs, sems):
        me = 2 * lax.axis_index("x") + lax.axis_index("y")
        return [pltpu.make_async_copy(ins[a], outs[a].at[me], sems[4].at[a]) for a in range(len(ins))]

    def issue(self, ins, outs, sems):
        for cp in self._local(ins, outs, sems) + self._far(ins, outs, sems, False):
            cp.start()

    def finish(self, ins, outs, sems):
        landed = self._far(ins, outs, sems, True)
        onward = self._near(ins, outs, sems, False)
        for cp, fwd in zip(landed, onward):
            cp.wait_recv()
            fwd.start()
        for cp in self._near(ins, outs, sems, True):
            cp.wait_recv()
        for cp in self._far(ins, outs, sems, False) + onward:
            cp.wait_send()
        for cp in self._local(ins, outs, sems):
            cp.wait()


class _ScatterChips:
    def __init__(self, blocks):
        self.arrs = list(blocks)
        n = len(self.arrs)
        self.out_shape = [jax.ShapeDtypeStruct((3,) + b.shape[1:], b.dtype) for b in self.arrs]
        self.sems = [pltpu.SemaphoreType.DMA((3 * n,)), pltpu.SemaphoreType.DMA((3 * n,))]

    def _copies(self, ins, outs, sems):
        send, recv = sems
        x, y, c = lax.axis_index("x"), lax.axis_index("y"), lax.axis_index("c")
        copies = []
        for a in range(len(ins)):
            for k, (fx, fy) in enumerate(_CHIP_REL):
                px, py = _flip(x, fx), _flip(y, fy)
                copies.append(pltpu.make_async_remote_copy(
                    src_ref=ins[a].at[2 * px + py], dst_ref=outs[a].at[k], send_sem=send.at[3 * a + k], recv_sem=recv.at[3 * a + k],
                    device_id=(px, py, c), device_id_type=MESH_ID))
        return copies

    def issue(self, ins, outs, sems):
        for cp in self._copies(ins, outs, sems):
            cp.start()

    def finish(self, ins, outs, sems):
        copies = self._copies(ins, outs, sems)
        for cp in copies:
            cp.wait_recv()
        for cp in copies:
            cp.wait_send()


def _exchange(xch, *, name):
    ni, no = len(xch.arrs), len(xch.out_shape)

    def body(*refs):
        ins, outs, sems = refs[:ni], refs[ni:ni + no], refs[ni + no:]
        xch.issue(ins, outs, sems)
        xch.finish(ins, outs, sems)

    return pl.pallas_call(body, in_specs=[_ANY] * ni, out_specs=[_ANY] * no, out_shape=xch.out_shape, scratch_shapes=xch.sems, name=name)(*xch.arrs)


def _call(body, *, grid, in_specs, out_specs, out_shape, scratch, name, args, xch=None):
    params = pltpu.CompilerParams(dimension_semantics=("arbitrary",) * len(grid))
    if xch is None:
        res = pl.pallas_call(body, grid=grid, in_specs=in_specs, out_specs=out_specs, out_shape=out_shape, scratch_shapes=scratch,
                             name=name, compiler_params=params)(*args)
        return list(res), []
    n_in, n_out, n_sc = len(in_specs), len(out_specs), len(scratch)
    xi, xo = len(xch.arrs), len(xch.out_shape)

    def wrapped(*refs):
        ins, refs = refs[:n_in], refs[n_in:]
        xin, refs = refs[:xi], refs[xi:]
        outs, refs = refs[:n_out], refs[n_out:]
        xout, refs = refs[:xo], refs[xo:]
        sc, sems = refs[:n_sc], refs[n_sc:]
        first = functools.reduce(jnp.logical_and, [pl.program_id(d) == 0 for d in range(len(grid))])
        last = functools.reduce(jnp.logical_and, [pl.program_id(d) == grid[d] - 1 for d in range(len(grid))])

        @pl.when(first)
        def _():
            xch.issue(xin, xout, sems)

        body(*ins, *outs, *sc)

        @pl.when(last)
        def _():
            xch.finish(xin, xout, sems)

    res = pl.pallas_call(
        wrapped, grid=grid, in_specs=list(in_specs) + [_ANY] * xi, out_specs=list(out_specs) + [_ANY] * xo,
        out_shape=list(out_shape) + xch.out_shape, scratch_shapes=list(scratch) + xch.sems, name=name, compiler_params=params,
    )(*args, *xch.arrs)
    return list(res[:n_out]), list(res[n_out:])


def _matmul(a, b, mode, out_dtype, *, tm, tn, tk, name, b_blocks=False, out_blocks=False, xch=None):
    pair = None
    if isinstance(a, tuple):
        assert mode == "nt"
        pair, a = a, jax.ShapeDtypeStruct((a[0].shape[0], 2 * a[0].shape[1]), a[0].dtype)
    if isinstance(b, tuple):
        assert mode == "tn" and not b_blocks
        pair, b = b, jax.ShapeDtypeStruct((b[0].shape[0], 2 * b[0].shape[1]), b[0].dtype)
    if b_blocks:
        _, br, bc4 = b.shape
        b2 = (br, 4 * bc4)
    else:
        b2 = b.shape

    def bspec(shape, index):
        if not b_blocks:
            return pl.BlockSpec(shape, index)
        per = bc4 // shape[1]

        def blocked(i, j, k):
            r, c = index(i, j, k)
            return (c // per, r, c % per)

        return pl.BlockSpec((None,) + shape, blocked)

    tm = min(tm, a.shape[1] if mode == "tn" else a.shape[0])
    tk = min(tk, a.shape[0] if mode == "tn" else a.shape[1])
    if mode == "tn":
        K, M = a.shape
        N = b2[1]
        a_spec = pl.BlockSpec((tk, tm), lambda i, j, k: (k, i))
        b_spec = bspec((tk, tn), lambda i, j, k: (k, j))
        dims = (((0,), (0,)), ((), ()))
    elif mode == "nt":
        M, K = a.shape
        N = b2[0]
        a_spec = pl.BlockSpec((tm, tk), lambda i, j, k: (i, k))
        b_spec = bspec((tn, tk), lambda i, j, k: (j, k))
        dims = (((1,), (1,)), ((), ()))
    else:
        M, K = a.shape
        N = b2[1]
        a_spec = pl.BlockSpec((tm, tk), lambda i, j, k: (i, k))
        b_spec = bspec((tk, tn), lambda i, j, k: (k, j))
        dims = (((1,), (0,)), ((), ()))
    assert M % tm == 0 and N % tn == 0 and K % tk == 0, (name, M, N, K, tm, tn, tk)
    nk = K // tk
    if out_blocks:
        per_o = (N // 4) // tn
        o_spec = pl.BlockSpec((None, tm, tn), lambda i, j, k: (j // per_o, i, j % per_o))
        o_shape = jax.ShapeDtypeStruct((4, M, N // 4), out_dtype)
    else:
        o_spec = pl.BlockSpec((tm, tn), lambda i, j, k: (i, j))
        o_shape = jax.ShapeDtypeStruct((M, N), out_dtype)

    def step(a_ref, b_ref, o_ref, acc_ref):
        k = pl.program_id(2)
        part = lax.dot_general(a_ref[...].astype(BF), b_ref[...].astype(BF), dims, preferred_element_type=F32)
        if nk == 1:
            o_ref[...] = part.astype(o_ref.dtype)
        else:
            @pl.when(k == 0)
            def _():
                acc_ref[...] = part

            @pl.when(k > 0)
            def _():
                acc_ref[...] += part

            @pl.when(k == nk - 1)
            def _():
                o_ref[...] = acc_ref[...].astype(o_ref.dtype)

    scratch = [pltpu.VMEM((tm, tn) if nk > 1 else (8, 128), F32)]
    grid = (M // tm, N // tn, nk)
    if pair is None:
        res, xres = _call(step, grid=grid, in_specs=[a_spec, b_spec], out_specs=[o_spec], out_shape=[o_shape], scratch=scratch,
                          name=name, args=(a, b), xch=xch)
        return res[0], xres

    axis = 2 if mode == "nt" else 1
    half = grid[axis] // 2
    assert grid[axis] % 2 == 0, (name, grid)
    which = a_spec if mode == "nt" else b_spec

    def halves(first):
        def index(i, j, k):
            pos = (i, j, k)[axis]
            used = pos < half if first else pos >= half
            local = jnp.clip(pos if first else pos - half, 0, half - 1)
            at = [local if d == axis else (v if axis == 2 else jnp.where(used, v, 0)) for d, v in enumerate((i, j, k))]
            return which.index_map(*at)
        return pl.BlockSpec(which.block_shape, index)

    def body(x1_ref, x2_ref, y_ref, o_ref, acc_ref):
        first = pl.program_id(axis) < half
        for x_ref, cond in ((x1_ref, first), (x2_ref, jnp.logical_not(first))):
            @pl.when(cond)
            def _(x_ref=x_ref):
                step(*((x_ref, y_ref) if mode == "nt" else (y_ref, x_ref)), o_ref, acc_ref)

    other = b if mode == "nt" else a
    res, xres = _call(body, grid=grid, in_specs=[halves(True), halves(False), b_spec if mode == "nt" else a_spec], out_specs=[o_spec],
                      out_shape=[o_shape], scratch=scratch, name=name, args=(pair[0], pair[1], other), xch=xch)
    return res[0], xres


def _scan_fwd(a, b, *, tm, name):
    n, c = a.shape
    nt = n // tm

    def body(a_ref, b_ref, h_ref, carry):
        i = pl.program_id(0)
        av, bv = a_ref[...], b_ref[...]
        row = _rows(av)
        d = 1
        while d < tm:
            a_s = jnp.where(row >= d, pltpu.roll(av, d, axis=0), 1.0)
            b_s = jnp.where(row >= d, pltpu.roll(bv, d, axis=0), 0.0)
            bv = av * b_s + bv
            av = av * a_s
            d *= 2
        h = bv + av * jnp.where(i > 0, carry[HALO - 1:HALO, :], 0.0)
        h_ref[...] = h
        carry[...] = h[tm - HALO:tm]

    return pl.pallas_call(
        body, grid=(nt,), in_specs=[pl.BlockSpec((tm, c), lambda i: (i, 0))] * 2,
        out_specs=pl.BlockSpec((tm, c), lambda i: (i, 0)), out_shape=jax.ShapeDtypeStruct((n, c), F32),
        scratch_shapes=[pltpu.VMEM((HALO, c), F32)], name=name,
        compiler_params=pltpu.CompilerParams(dimension_semantics=("arbitrary",)),
    )(a, b)


def _scan_bwd(a, h, dh, *, tm, name):
    n, c = a.shape
    nt = n // tm
    tb = tm // HALO

    def body(a_ref, an_ref, h_ref, hp_ref, dh_ref, da_ref, db_ref, carry):
        i = pl.program_id(0)
        tile = nt - 1 - i
        av, hv, g = a_ref[...], h_ref[...], dh_ref[...]
        row = _rows(av)
        a_next = jnp.where(tile < nt - 1, an_ref[0:1, :], 0.0)
        au = jnp.where(row < tm - 1, pltpu.roll(av, tm - 1, axis=0), a_next)
        d = 1
        while d < tm:
            a_s = jnp.where(row < tm - d, pltpu.roll(au, tm - d, axis=0), 1.0)
            g_s = jnp.where(row < tm - d, pltpu.roll(g, tm - d, axis=0), 0.0)
            g = au * g_s + g
            au = au * a_s
            d *= 2
        g = g + au * jnp.where(i > 0, carry[0:1, :], 0.0)
        h_prev = jnp.where(row >= 1, pltpu.roll(hv, 1, axis=0), jnp.where(tile > 0, hp_ref[HALO - 1:HALO, :], 0.0))
        db_ref[...] = g
        da_ref[...] = g * h_prev
        carry[...] = g[0:HALO]

    cur = pl.BlockSpec((tm, c), lambda i: (nt - 1 - i, 0))
    nxt = pl.BlockSpec((HALO, c), lambda i: (jnp.minimum((nt - i) * tb, n // HALO - 1), 0))
    prv = pl.BlockSpec((HALO, c), lambda i: (jnp.maximum((nt - 1 - i) * tb - 1, 0), 0))
    return pl.pallas_call(
        body, grid=(nt,), in_specs=[cur, nxt, cur, prv, cur], out_specs=[cur, cur],
        out_shape=[jax.ShapeDtypeStruct((n, c), F32)] * 2, scratch_shapes=[pltpu.VMEM((HALO, c), F32)], name=name,
        compiler_params=pltpu.CompilerParams(dimension_semantics=("arbitrary",)),
    )(a, a, h, h, dh)


def _gdn_fwd(q, k, v, gb, *, name, xch=None):
    n = q.shape[0]
    rows = GDN_STEP * CHUNK
    ns = n // rows

    def body(q_ref, k_ref, v_ref, gb_ref, o_ref, s_ref, state):
        @pl.when(pl.program_id(0) == 0)
        def _():
            state[...] = jnp.zeros_like(state)

        s0 = state[...]
        s_ref[0] = s0
        o, s1 = _f_gdn_chunks(q_ref[...], k_ref[...], v_ref[...], gb_ref[...], s0)
        o_ref[...] = o
        state[...] = s1

    row = pl.BlockSpec((rows, DG), lambda i: (i, 0))
    return _call(
        body, grid=(ns,), in_specs=[row, row, row, pl.BlockSpec((rows, 128), lambda i: (i, 0))],
        out_specs=[row, pl.BlockSpec((1, DG, HD), lambda i: (i, 0, 0))],
        out_shape=[jax.ShapeDtypeStruct((n, DG), F32), jax.ShapeDtypeStruct((ns, DG, HD), F32)],
        scratch=[pltpu.VMEM((DG, HD), F32)], name=name, args=(q, k, v, gb), xch=xch)


def _gdn_bwd(q, k, v, gb, s_all, do, *, name, xch=None):
    n = q.shape[0]
    rows = GDN_STEP * CHUNK
    ns = n // rows

    def body(q_ref, k_ref, v_ref, gb_ref, s_ref, do_ref, dq_ref, dk_ref, dv_ref, dgb_ref, dstate):
        @pl.when(pl.program_id(0) == 0)
        def _():
            dstate[...] = jnp.zeros_like(dstate)

        _, vjp = jax.vjp(_f_gdn_chunks, q_ref[...], k_ref[...], v_ref[...], gb_ref[...], s_ref[0])
        dq_ref[...], dk_ref[...], dv_ref[...], dgb_ref[...], dstate[...] = vjp((do_ref[...], dstate[...]))

    row = pl.BlockSpec((rows, DG), lambda i: (ns - 1 - i, 0))
    gsp = pl.BlockSpec((rows, 128), lambda i: (ns - 1 - i, 0))
    return _call(
        body, grid=(ns,), in_specs=[row, row, row, gsp, pl.BlockSpec((1, DG, HD), lambda i: (ns - 1 - i, 0, 0)), row],
        out_specs=[row, row, row, gsp],
        out_shape=[jax.ShapeDtypeStruct((n, DG), F32)] * 3 + [jax.ShapeDtypeStruct((n, 128), F32)],
        scratch=[pltpu.VMEM((DG, HD), F32)], name=name, args=(q, k, v, gb, s_all, do), xch=xch)


def _loss_head(x1, y2, w, tgt, *, tm, name):
    n, c = x1.shape

    def body(x_ref, y_ref, w_ref, t_ref, l_ref, d_ref):
        err = x_ref[...] + _rms(y_ref[...], w_ref[...]) - t_ref[...]
        part = jnp.sum(jnp.sum(err * err, axis=1, keepdims=True), axis=0, keepdims=True) * (0.5 / c)

        @pl.when(pl.program_id(0) == 0)
        def _():
            l_ref[...] = jnp.zeros_like(l_ref)

        l_ref[...] += part
        d_ref[...] = err * (1.0 / c)

    row = pl.BlockSpec((tm, c), lambda i: (i, 0))
    return pl.pallas_call(
        body, grid=(n // tm,), in_specs=[row, row, pl.BlockSpec((1, c), lambda i: (0, 0)), row],
        out_specs=[pl.BlockSpec((8, 128), lambda i: (0, 0)), row],
        out_shape=[jax.ShapeDtypeStruct((8, 128), F32), jax.ShapeDtypeStruct((n, c), F32)], name=name,
        compiler_params=pltpu.CompilerParams(dimension_semantics=("arbitrary",)),
    )(x1, y2, w, tgt)


def _sum_parts(own, recv, *, tr, name):
    r, c = own.shape
    p = recv.shape[0]

    def body(o_ref, r_ref, s_ref):
        s = o_ref[...]
        for q in range(p):
            s = s + r_ref[q].astype(F32)
        s_ref[...] = s

    return pl.pallas_call(
        body, grid=(r // tr,), in_specs=[pl.BlockSpec((tr, c), lambda i: (i, 0)), pl.BlockSpec((p, tr, c), lambda i: (0, i, 0))],
        out_specs=pl.BlockSpec((tr, c), lambda i: (i, 0)), out_shape=jax.ShapeDtypeStruct((r, c), F32), name=name,
        compiler_params=pltpu.CompilerParams(dimension_semantics=("parallel",)),
    )(own, recv)


def _sum_slots(buf, *, name):
    p, r, c = buf.shape

    def body(b_ref, s_ref):
        s = b_ref[0]
        for q in range(1, p):
            s = s + b_ref[q]
        s_ref[...] = s

    return pl.pallas_call(body, out_shape=jax.ShapeDtypeStruct((r, c), F32), name=name)(buf)


def _adamw(w, m, v, gs, *, tr, name, xch=None):
    ngrp, r, c = w.shape
    nterm = len(gs[0])
    per = r // tr
    c1 = 1.0 - ADAM_B1 ** ADAM_STEP
    c2 = 1.0 - ADAM_B2 ** ADAM_STEP

    def body(*refs):
        w_ref, m_ref, v_ref = refs[:3]
        g_refs = refs[3:3 + ngrp * nterm]
        g_ref, d_ref, nm_ref, nv_ref = refs[3 + ngrp * nterm:]
        grp = pl.program_id(0) // per
        g = None
        for q in range(ngrp):
            gq = g_refs[q * nterm][...]
            for t in range(1, nterm):
                gq = gq + g_refs[q * nterm + t][...]
            g = gq if g is None else jnp.where(grp == q, gq, g)
        nm = ADAM_B1 * m_ref[...] + (1.0 - ADAM_B1) * g
        nv = ADAM_B2 * v_ref[...] + (1.0 - ADAM_B2) * (g * g)
        g_ref[...] = g
        d_ref[...] = -ADAM_LR * ((nm / c1) / (jnp.sqrt(nv / c2) + ADAM_EPS) + ADAM_WD * w_ref[...])
        nm_ref[...] = nm
        nv_ref[...] = nv

    blk = pl.BlockSpec((None, tr, c), lambda i: (i // per, i % per, 0))
    g_specs = [pl.BlockSpec((tr, c), lambda i, q=q: (jnp.clip(i - q * per, 0, per - 1), 0)) for q in range(ngrp) for _ in range(nterm)]
    return _call(body, grid=(ngrp * per,), in_specs=[blk] * 3 + g_specs, out_specs=[blk] * 4,
                 out_shape=[jax.ShapeDtypeStruct((ngrp, r, c), F32)] * 4, scratch=[], name=name,
                 args=(w, m, v, *[t for grp in gs for t in grp]), xch=xch)


class _SwapCores:
    def __init__(self, arrs):
        self.arrs = list(arrs)
        n = len(self.arrs)
        self.out_shape = [jax.ShapeDtypeStruct(a.shape, a.dtype) for a in self.arrs]
        self.sems = [pltpu.SemaphoreType.DMA((n,)), pltpu.SemaphoreType.DMA((n,))]

    def _copies(self, ins, outs, sems):
        send, recv = sems
        sib = (lax.axis_index("x"), lax.axis_index("y"), 1 - lax.axis_index("c"))
        return [pltpu.make_async_remote_copy(src_ref=ins[a], dst_ref=outs[a], send_sem=send.at[a], recv_sem=recv.at[a],
                                             device_id=sib, device_id_type=MESH_ID) for a in range(len(ins))]

    def issue(self, ins, outs, sems):
        for cp in self._copies(ins, outs, sems):
            cp.start()

    def finish(self, ins, outs, sems):
        copies = self._copies(ins, outs, sems)
        for cp in copies:
            cp.wait_recv()
        for cp in copies:
            cp.wait_send()


class _GatherDevices:
    def __init__(self, buf):
        self.arrs = [buf]
        self.out_shape = [jax.ShapeDtypeStruct((8,) + buf.shape, buf.dtype)]
        self.sems = [pltpu.SemaphoreType.DMA((7,)), pltpu.SemaphoreType.DMA((7,)), pltpu.SemaphoreType.DMA((1,))]

    def _copies(self, ins, outs, sems, arriving):
        send, recv, lsem = sems
        x, y, c = lax.axis_index("x"), lax.axis_index("y"), lax.axis_index("c")
        me = 4 * x + 2 * y + c
        sib = (x, y, 1 - c)

        def copy(k, src, slot, to):
            return pltpu.make_async_remote_copy(src_ref=src, dst_ref=outs[0].at[slot], send_sem=send.at[k], recv_sem=recv.at[k],
                                                device_id=to, device_id_type=MESH_ID)

        far, onward = [], []
        for k, (fx, fy) in enumerate(_CHIP_REL):
            px, py = _flip(x, fx), _flip(y, fy)
            if arriving:
                far.append(copy(k, ins[0], 4 * px + 2 * py + c, (px, py, c)))
                onward.append(copy(4 + k, ins[0], 4 * px + 2 * py + 1 - c, sib))
            else:
                far.append(copy(k, ins[0], me, (px, py, c)))
                onward.append(copy(4 + k, outs[0].at[4 * px + 2 * py + c], 4 * px + 2 * py + c, sib))
        near = copy(3, ins[0], 4 * x + 2 * y + 1 - c if arriving else me, sib)
        local = None if arriving else pltpu.make_async_copy(ins[0], outs[0].at[me], lsem.at[0])
        return local, far + [near], onward

    def issue(self, ins, outs, sems):
        local, push, _ = self._copies(ins, outs, sems, False)
        for cp in [local] + push:
            cp.start()

    def finish(self, ins, outs, sems):
        _, landed, passed = self._copies(ins, outs, sems, True)
        local, push, onward = self._copies(ins, outs, sems, False)
        for cp, fwd in zip(landed[:3], onward):
            cp.wait_recv()
            fwd.start()
        for cp in landed[3:] + passed:
            cp.wait_recv()
        for cp in push + onward:
            cp.wait_send()
        local.wait()


class _Together:
    def __init__(self, parts):
        self.parts = list(parts)
        self.arrs = [a for p in self.parts for a in p.arrs]
        self.out_shape = [s for p in self.parts for s in p.out_shape]
        self.sems = [s for p in self.parts for s in p.sems]

    def _split(self, ins, outs, sems):
        i = o = s = 0
        for p in self.parts:
            ni, no, ns = len(p.arrs), len(p.out_shape), len(p.sems)
            yield p, ins[i:i + ni], outs[o:o + no], sems[s:s + ns]
            i, o, s = i + ni, o + no, s + ns

    def issue(self, ins, outs, sems):
        for p, a, b, c in self._split(ins, outs, sems):
            p.issue(a, b, c)

    def finish(self, ins, outs, sems):
        for p, a, b, c in self._split(ins, outs, sems):
            p.finish(a, b, c)


def _lanes(v, off):
    return jnp.pad(v.reshape(1, -1), ((0, 0), (off, 128 - off - v.size)))


def _block_diag(w):
    eye = jnp.eye(8, dtype=w.dtype)
    return (w[:, :, None, :] * eye[:, None, :, None]).reshape(DG, DG)


def _diag_blocks(w):
    return jnp.stack([w[h * 64:(h + 1) * 64, h * 64:(h + 1) * 64] for h in range(8)])


def _mixer_params(p):
    return dict(
        gdn_conv_w=p["gdn_conv_w"], alog=_lanes(p["gdn_a_log"], HEADS), dtb=_lanes(p["gdn_dt_bias"], HEADS),
        gdn_nw=p["gdn_norm_w"].reshape(1, HD),
        lru_conv_w=p["lru_conv_w"], lru_conv_b=p["lru_conv_b"].reshape(1, DG),
        wa=_block_diag(p["lru_wa"]), ba=p["lru_ba"].reshape(1, DG), wx=_block_diag(p["lru_wx"]), bx=p["lru_bx"].reshape(1, DG),
        lam=p["lru_lambda"].reshape(1, DG),
        ln_w=p["sgu_ln_w"].reshape(1, DG), ln_b=p["sgu_ln_b"].reshape(1, DG), ws=p["sgu_ws"].reshape(DG, 128),
        bst=jnp.pad(p["sgu_b"].T, ((0, 0), (0, 124))),
        sconv_w=p["sconv_w"], gw0=p["grp_norm_w"][0:1], gw1=p["grp_norm_w"][1:2], gw2=p["grp_norm_w"][2:3],
    )


def _mixer_param_grads(g):
    return dict(
        gdn_conv_w=g["gdn_conv_w"], gdn_a_log=g["alog"][0, HEADS:2 * HEADS], gdn_dt_bias=g["dtb"][0, HEADS:2 * HEADS],
        gdn_norm_w=g["gdn_nw"][0],
        lru_conv_w=g["lru_conv_w"], lru_conv_b=g["lru_conv_b"][0],
        lru_wa=_diag_blocks(g["wa"]), lru_ba=g["ba"].reshape(8, 64), lru_wx=_diag_blocks(g["wx"]), lru_bx=g["bx"].reshape(8, 64),
        lru_lambda=g["lam"][0],
        sgu_ln_w=g["ln_w"][0], sgu_ln_b=g["ln_b"][0], sgu_ws=g["ws"].reshape(4, 128, 128), sgu_b=g["bst"][:, 0:4].T,
        sconv_w=g["sconv_w"], grp_norm_w=jnp.concatenate([g["gw0"], g["gw1"], g["gw2"]], axis=0),
    )


TM_MIX = 256


def _mixers_fwd(p, mp, tag="", xch=None):
    c = lambda *names: [(mp[n], False) for n in names]
    q, k, v, gb = _rowwise(_f_gdn_pre, [(p, 0, 1536, True), (p, OFF_BA // 128, 128, False)], c("gdn_conv_w", "alog", "dtb"),
                           [(DG, F32)] * 3 + [(128, F32)], tm=TM_MIX, name="gdn_pre" + tag)
    (o, s_all), xres = _gdn_fwd(q, k, v, gb, name="gdn_chunks" + tag, xch=xch)
    y_a, = _rowwise(_f_gdn_post, [(o, 0, DG, False), (p, OFF_Z // DG, DG, False)], c("gdn_nw"), [(DG, BF)], tm=TM_MIX, name="gdn_post" + tag)
    a, b = _rowwise(_f_lru_ab, [(p, OFF_LX // DG, DG, True)], c("lru_conv_w", "lru_conv_b", "wa", "ba", "wx", "bx", "lam"),
                    [(DG, F32)] * 2, tm=TM_MIX, name="lru_ab" + tag)
    hs = _scan_fwd(a, b, tm=TM_MIX, name="lru_scan" + tag)
    y_b, = _rowwise(_f_lru_post, [(hs, 0, DG, False), (p, OFF_LG // DG, DG, False)], c("gw0"), [(DG, BF)], tm=TM_MIX, name="lru_post" + tag)
    y_c, = _rowwise(_f_sgu, [(p, OFF_UV // 1024, 1024, False)], c("ln_w", "ln_b", "ws", "bst", "gw1"), [(DG, BF)], tm=TM_MIX, name="sgu" + tag)
    y_d, = _rowwise(_f_sconv, [(p, OFF_SB // DG, DG, False), (p, OFF_SC // DG, DG, True), (p, OFF_SH // DG, DG, True)],
                    c("sconv_w", "gw2"), [(DG, BF)], tm=TM_MIX, name="sconv" + tag)
    return jnp.concatenate([y_a, y_b, y_c, y_d], axis=1), (q, k, v, gb, o, s_all, a, hs), xres


def _mixers_bwd(p, mp, saved, dy, tag="", xch=None):
    q, k, v, gb, o, s_all, a, hs = saved
    c = lambda *names: [(mp[n], False, True) for n in names]
    g = {}

    (do, dz), (g["gdn_nw"],) = _rowwise_vjp(
        _f_gdn_post, [(o, 0, DG, False, F32), (p, OFF_Z // DG, DG, False, BF)], c("gdn_nw"), [(dy, 0, DG)], tm=TM_MIX, name="gdn_post_b" + tag)
    (dq, dk, dv, dgb), xres = _gdn_bwd(q, k, v, gb, s_all, do, name="gdn_chunks_b" + tag, xch=xch)
    (dqkv, dba), (g["gdn_conv_w"], g["alog"], g["dtb"]) = _rowwise_vjp(
        _f_gdn_pre, [(p, 0, 1536, True, BF), (p, OFF_BA // 128, 128, False, BF)], c("gdn_conv_w", "alog", "dtb"),
        [(dq, 0, DG), (dk, 0, DG), (dv, 0, DG), (dgb, 0, 128)], tm=TM_MIX, name="gdn_pre_b" + tag)

    (dhs, dgate), (g["gw0"],) = _rowwise_vjp(
        _f_lru_post, [(hs, 0, DG, False, F32), (p, OFF_LG // DG, DG, False, BF)], c("gw0"), [(dy, 1, DG)], tm=TM_MIX, name="lru_post_b" + tag)
    da, db = _scan_bwd(a, hs, dhs, tm=TM_MIX, name="lru_scan_b" + tag)
    (dlx,), (g["lru_conv_w"], g["lru_conv_b"], g["wa"], g["ba"], g["wx"], g["bx"], g["lam"]) = _rowwise_vjp(
        _f_lru_ab, [(p, OFF_LX // DG, DG, True, BF)], c("lru_conv_w", "lru_conv_b", "wa", "ba", "wx", "bx", "lam"),
        [(da, 0, DG), (db, 0, DG)], tm=TM_MIX, name="lru_ab_b" + tag)

    (duv,), (g["ln_w"], g["ln_b"], g["ws"], g["bst"], g["gw1"]) = _rowwise_vjp(
        _f_sgu, [(p, OFF_UV // 1024, 1024, False, BF)], c("ln_w", "ln_b", "ws", "bst", "gw1"), [(dy, 2, DG)], tm=TM_MIX, name="sgu_b" + tag)

    (dsb, dsc, dsh), (g["sconv_w"], g["gw2"]) = _rowwise_vjp(
        _f_sconv, [(p, OFF_SB // DG, DG, False, BF), (p, OFF_SC // DG, DG, True, BF), (p, OFF_SH // DG, DG, True, BF)],
        c("sconv_w", "gw2"), [(dy, 3, DG)], tm=TM_MIX, name="sconv_b" + tag)

    dp = jnp.concatenate([dqkv, dz, dlx, dgate, duv, dsb, dsc, dsh, dba], axis=1)
    return dp, g, xres


WEIGHTS = ("pre_mix_norm", "w_in", "gdn_conv_w", "gdn_a_log", "gdn_dt_bias", "gdn_norm_w", "lru_conv_w", "lru_conv_b", "lru_wa",
           "lru_ba", "lru_wx", "lru_bx", "lru_lambda", "sgu_ln_w", "sgu_ln_b", "sgu_ws", "sgu_b", "sconv_w", "grp_norm_w", "w_out",
           "post_mix_norm", "pre_ffn_norm", "ffn_up", "ffn_conv_w", "ffn_conv_b", "ffn_down", "post_ffn_norm")
BIG = ("w_in", "ffn_up", "w_out", "ffn_down")
CHIP_SHARDED_SMALL = ("gdn_conv_w", "lru_conv_w", "sconv_w", "grp_norm_w", "ffn_conv_w")
MIXER_PARAMS = ("gdn_conv_w", "gdn_a_log", "gdn_dt_bias", "gdn_norm_w", "lru_conv_w", "lru_conv_b", "lru_wa", "lru_ba", "lru_wx",
                "lru_bx", "lru_lambda", "sgu_ln_w", "sgu_ln_b", "sgu_ws", "sgu_b", "sconv_w", "grp_norm_w")
TM_ROW = 256
FF_TILE = 256
FF_ROWS = 2048
N_FF_TILES = D_FF // FF_TILE
PACK_ROWS = 256
FFN_UP_CUTS = (512, 1536, 1872)
TK_DW = 4096


def _pack(arrs):
    parts = []
    for a in arrs:
        n = a.size
        parts.append(jnp.pad(a.reshape(-1), (0, -n % 1024)).reshape(-1, 128))
    rows = sum(p.shape[0] for p in parts)
    parts.append(jnp.zeros((-rows % PACK_ROWS, 128), F32))
    return jnp.concatenate(parts, axis=0)


def _unpack(buf, shapes):
    out, row = [], 0
    for s in shapes:
        n = math.prod(s)
        rows = -(-n // 1024) * 8
        out.append(buf[row:row + rows].reshape(-1)[:n].reshape(s))
        row += rows
    return out


def _row_tile(rows, cols):
    return 256 if rows % 256 == 0 and cols <= 1024 else 128


def _w_in_full(got):
    c = N_IN // 4
    z = jnp.zeros((D, NP - N_IN), got.dtype)
    return jnp.concatenate([got[0], got[1][:, :2048 - c], got[1][:, 2056 - c:], got[2], got[3], got[1][:, 2048 - c:2056 - c], z], axis=1)


def _w_in_blocks(dw):
    c = N_IN // 4
    cut = 2048 + 2 * c - 2056
    b1 = jnp.concatenate([dw[:, c:2048], dw[:, OFF_BA:OFF_BA + 8], dw[:, 2048:cut]], axis=1)
    return jnp.stack([dw[:, 0:c], b1, dw[:, cut:cut + c], dw[:, cut + c:OFF_BA]])


def _layer_fwd(l, xs, h, w_in_l, w_out_g, shard, sp, mp, nxt):
    t = str(l)
    up = shard["ffn_up"][l]
    c0, c1, c2 = FFN_UP_CUTS
    first = [up[:c0]] + ([shard["w_out"][l]] if w_out_g is None else [])
    p, got = _matmul(h, w_in_l, "nn", F32, tm=1024, tn=1920, tk=D, name="mm_in" + t, xch=_GatherChips(first))
    w_out_l = (got[1] if w_out_g is None else w_out_g).reshape(D, D)
    ycat, saved, (w_up_b,) = _mixers_fwd(p, mp, tag=t, xch=_GatherChips([up[c0:c1]]))
    y, (w_up_c,) = _matmul(ycat, w_out_l, "nn", F32, tm=1024, tn=1024, tk=D, name="mm_out" + t, xch=_GatherChips([up[c1:c2]]))
    (x1, h2), (w_up_d,) = _rowwise(
        _f_post_pre, [(xs, 0, D, False), (y, 0, D, False)], [(sp["post_mix_norm"][l], False), (sp["pre_ffn_norm"][l], False)],
        [(D, F32), (D, BF)], tm=TM_ROW, name="post_mix" + t, xch=_GatherChips([up[c2:]]))
    w_up = jnp.concatenate([got[0], w_up_b, w_up_c, w_up_d], axis=1)
    u, (w_dn_g,) = _matmul(h2, w_up, "nn", F32, tm=1024, tn=1408, tk=D, name="mm_up" + t, b_blocks=True,
                           xch=_GatherChips([shard["ffn_down"][l]]))
    ffn_rows = [(u, 0, FF_TILE, True), (u, N_FF_TILES, FF_TILE, True)]
    ffn_consts = [(c, True) for c in sp["ffn_conv"][l]]
    ffn_tm = min(FF_ROWS, u.shape[0])
    half = D // 2
    if nxt is None:
        act, = _rowwise(_f_ffn_act, ffn_rows, ffn_consts, [(FF_TILE, BF)], tm=ffn_tm, ncol=N_FF_TILES, name="ffn_act" + t)
    else:
        (act,), (n_out, n_in_a) = _rowwise(_f_ffn_act, ffn_rows, ffn_consts, [(FF_TILE, BF)], tm=ffn_tm, ncol=N_FF_TILES, name="ffn_act" + t,
                                           xch=_GatherChips([nxt[1], nxt[0][:half]]))
    w_dn_l = w_dn_g.reshape(D_FF, D)
    y2, n_in_b = _matmul(act, w_dn_l, "nn", F32, tm=1024, tn=1024, tk=2816, name="mm_down" + t,
                         xch=None if nxt is None else _GatherChips([nxt[0][half:]]))
    keep = dict(xs=xs, h=h, p=p, saved=saved, ycat=ycat, y=y, x1=x1, h2=h2, u=u, act=act, y2=y2,
                w_in=w_in_l, w_out=w_out_l, w_up=w_up, w_dn=w_dn_l)
    return keep, (None if nxt is None else (jnp.concatenate([n_in_a, n_in_b[0]], axis=1), n_out))


def _layer_bwd(l, a, dx1, dy2, sp, mp):
    t = str(l)
    g = {}
    dact, _ = _matmul(dy2, a["w_dn"], "nt", F32, tm=1024, tn=1408, tk=D, name="mm_down_dx" + t)
    dw_dn, _ = _matmul(a["act"], dy2, "tn", F32, tm=512, tn=1024, tk=TK_DW, name="mm_down_dw" + t)
    dw_dn = dw_dn.reshape(4, D_FF // 4, D)
    (dug, duv), gc = _rowwise_vjp(
        _f_ffn_act, [(a["u"], 0, FF_TILE, True, BF), (a["u"], N_FF_TILES, FF_TILE, True, BF)], [(c, True, True) for c in sp["ffn_conv"][l]],
        [(dact, 0, FF_TILE)], tm=min(FF_ROWS, dact.shape[0]), ncol=N_FF_TILES, name="ffn_act_b" + t)
    g["ffn_conv_w"] = jnp.concatenate([gc[0], gc[1]], axis=1)
    g["ffn_conv_b"] = jnp.concatenate([gc[2], gc[3]], axis=1)[0]
    dh2, (r_dn,) = _matmul((dug, duv), a["w_up"], "nt", F32, tm=1024, tn=1024, tk=2816, name="mm_up_dx" + t, b_blocks=True,
                           xch=_ScatterChips([dw_dn.astype(BF)]))
    dw_up, _ = _matmul(a["h2"], (dug, duv), "tn", F32, tm=512, tn=1408, tk=TK_DW // 2, name="mm_up_dw" + t, out_blocks=True)
    (dxs, dy), (gpm, gpf) = _rowwise_vjp(
        _f_post_pre, [(a["xs"], 0, D, False, F32), (a["y"], 0, D, False, BF)],
        [(sp["post_mix_norm"][l], False, True), (sp["pre_ffn_norm"][l], False, True)], [(dx1, 0, D), (dh2, 0, D)], tm=TM_ROW, name="post_mix_b" + t)
    g["post_mix_norm"], g["pre_ffn_norm"] = gpm[0], gpf[0]
    dycat, _ = _matmul(dy, a["w_out"], "nt", F32, tm=1024, tn=1024, tk=D, name="mm_out_dx" + t)
    dw_out, _ = _matmul(a["ycat"], dy, "tn", F32, tm=512, tn=1024, tk=TK_DW, name="mm_out_dw" + t)
    dw_out = dw_out.reshape(4, D // 4, D)
    dp, gm, (r_up,) = _mixers_bwd(a["p"], mp, a["saved"], dycat, tag=t, xch=_ScatterChips([dw_up.astype(BF)]))
    g.update(_mixer_param_grads(gm))
    dw_in, (r_out,) = _matmul(a["h"], dp, "tn", F32, tm=512, tn=1920, tk=TK_DW, name="mm_in_dw" + t, xch=_ScatterChips([dw_out.astype(BF)]))
    dw_in = _w_in_blocks(dw_in)
    dh, (r_in,) = _matmul(dp, a["w_in"], "nt", F32, tm=512, tn=1024, tk=NP, name="mm_in_dx" + t, xch=_ScatterChips([dw_in.astype(BF)]))
    big = {"ffn_down": (dw_dn, r_dn), "ffn_up": (dw_up, r_up), "w_out": (dw_out, r_out), "w_in": (dw_in, r_in)}
    return dxs, dh, g, big


def kernel(x, pre_mix_norm, w_in, gdn_conv_w, gdn_a_log, gdn_dt_bias, gdn_norm_w, lru_conv_w, lru_conv_b, lru_wa, lru_ba, lru_wx, lru_bx, lru_lambda, sgu_ln_w, sgu_ln_b, sgu_ws, sgu_b, sconv_w, grp_norm_w, w_out, post_mix_norm, pre_ffn_norm, ffn_up, ffn_conv_w, ffn_conv_b, ffn_down, post_ffn_norm, loss_target, m_pre_mix_norm, m_w_in, m_gdn_conv_w, m_gdn_a_log, m_gdn_dt_bias, m_gdn_norm_w, m_lru_conv_w, m_lru_conv_b, m_lru_wa, m_lru_ba, m_lru_wx, m_lru_bx, m_lru_lambda, m_sgu_ln_w, m_sgu_ln_b, m_sgu_ws, m_sgu_b, m_sconv_w, m_grp_norm_w, m_w_out, m_post_mix_norm, m_pre_ffn_norm, m_ffn_up, m_ffn_conv_w, m_ffn_conv_b, m_ffn_down, m_post_ffn_norm, v_pre_mix_norm, v_w_in, v_gdn_conv_w, v_gdn_a_log, v_gdn_dt_bias, v_gdn_norm_w, v_lru_conv_w, v_lru_conv_b, v_lru_wa, v_lru_ba, v_lru_wx, v_lru_bx, v_lru_lambda, v_sgu_ln_w, v_sgu_ln_b, v_sgu_ws, v_sgu_b, v_sconv_w, v_grp_norm_w, v_w_out, v_post_mix_norm, v_pre_ffn_norm, v_ffn_up, v_ffn_conv_w, v_ffn_conv_b, v_ffn_down, v_post_ffn_norm):
    given = dict(locals())
    me = 2 * lax.axis_index("x") + lax.axis_index("y")
    xs0, tgt = x[0], loss_target[0]

    small_sh = [given[n] for n in CHIP_SHARDED_SMALL]
    shard = {n: [given[n][l].astype(BF) for l in range(DEPTH)] for n in BIG}
    got = _exchange(_GatherChipsTwoLevel([shard["w_in"][0], _pack(small_sh)]), name="gather_first")
    full = {n: given[n] for n in WEIGHTS if n not in BIG and n not in CHIP_SHARDED_SMALL}
    per_chip = [_unpack(got[1][j], [s.shape for s in small_sh]) for j in range(4)]
    parts = [jnp.stack([per_chip[j][i] for j in range(4)]) for i in range(len(small_sh))]
    for n, pj in zip(CHIP_SHARDED_SMALL, parts):
        full[n] = pj.transpose(1, 2, 0, 3).reshape(pj.shape[1], pj.shape[2], 4 * pj.shape[3])
    sp = {n: [full[n][l:l + 1] for l in range(DEPTH)] for n in ("pre_mix_norm", "post_mix_norm", "pre_ffn_norm", "post_ffn_norm")}
    sp["ffn_conv"] = [[full["ffn_conv_w"][l][:, :D_FF], full["ffn_conv_w"][l][:, D_FF:], full["ffn_conv_b"][l:l + 1, :D_FF],
                       full["ffn_conv_b"][l:l + 1, D_FF:]] for l in range(DEPTH)]
    mps = [_mixer_params({n: full[n][l] for n in MIXER_PARAMS}) for l in range(DEPTH)]

    h, = _rowwise(_f_pre, [(xs0, 0, D, False)], [(sp["pre_mix_norm"][0], False)], [(D, BF)], tm=TM_ROW, name="pre_mix0")
    a0, (w_in1, w_out1) = _layer_fwd(0, xs0, h, _w_in_full(got[0]), None, shard, sp, mps[0], (shard["w_in"][1], shard["w_out"][1]))
    xs1, h1 = _rowwise(_f_post_pre, [(a0["x1"], 0, D, False), (a0["y2"], 0, D, False)],
                       [(sp["post_ffn_norm"][0], False), (sp["pre_mix_norm"][1], False)], [(D, F32), (D, BF)], tm=TM_ROW, name="post_ffn0")
    a1, _ = _layer_fwd(1, xs1, h1, _w_in_full(w_in1), w_out1, shard, sp, mps[1], None)
    lacc, dxo = _loss_head(a1["x1"], a1["y2"], sp["post_ffn_norm"][1], tgt, tm=TM_ROW, name="loss_head")

    gl = [None, None]
    (dx1, dy2), (gpf1,) = _rowwise_vjp(_f_post, [(a1["x1"], 0, D, False, F32), (a1["y2"], 0, D, False, BF)],
                                      [(sp["post_ffn_norm"][1], False, True)], [(dxo, 0, D)], tm=TM_ROW, name="post_ffn1_b")
    big = [None, None]
    dxs1, dh1, gl[1], big[1] = _layer_bwd(1, a1, dx1, dy2, sp, mps[1])
    gl[1]["post_ffn_norm"] = gpf1[0]
    (dx1, dy2), (gpf0, gpm1) = _rowwise_vjp(
        _f_post_pre, [(a0["x1"], 0, D, False, F32), (a0["y2"], 0, D, False, BF)],
        [(sp["post_ffn_norm"][0], False, True), (sp["pre_mix_norm"][1], False, True)], [(dxs1, 0, D), (dh1, 0, D)], tm=TM_ROW, name="post_ffn0_b")
    gl[1]["pre_mix_norm"] = gpm1[0]
    dxs0, dh0, gl[0], big[0] = _layer_bwd(0, a0, dx1, dy2, sp, mps[0])
    gl[0]["post_ffn_norm"] = gpf0[0]
    (grad_x,), (gpm0,) = _rowwise_vjp(lambda xv, w: (xv, _rms(xv, w)), [(xs0, 0, D, False, F32)], [(sp["pre_mix_norm"][0], False, True)],
                                     [(dxs0, 0, D), (dh0, 0, D)], tm=TM_ROW, name="pre_mix0_b")
    gl[0]["pre_mix_norm"] = gpm0[0]

    small = [n for n in WEIGHTS if n not in BIG]
    gfull = {n: jnp.stack([gl[0][n], gl[1][n]]) for n in small}
    small_grads = _pack([gfull[n] for n in small] + [lacc[0, 0:1]])
    sums = {}
    for n in BIG:
        for l in range(DEPTH):
            blocks, recv = big[l][n]
            own = lax.dynamic_index_in_dim(blocks, me, 0, keepdims=False)
            sums[n] = sums.get(n, []) + [_sum_parts(own, recv, tr=_row_tile(*own.shape), name="sum_grads_%s%d" % (n, l))]
    kinds = ("grad", "delta", "new_m", "new_v")
    outs = {kind: {} for kind in kinds}
    order = ("w_out", "ffn_down", "w_in", "ffn_up")
    other = _exchange(_SwapCores(sums[order[0]]), name="swap_first")
    gathered = None
    for i, n in enumerate(order):
        riders = []
        if i + 1 < len(order):
            riders.append(_SwapCores(sums[order[i + 1]]))
        if n == "ffn_down":
            riders.append(_GatherDevices(small_grads))
        shp = given[n].shape
        r4, got = _adamw(given[n], given["m_" + n], given["v_" + n], [[s, o] for s, o in zip(sums[n], other)],
                         tr=_row_tile(shp[1], shp[2]), name="adamw_" + n, xch=_Together(riders) if riders else None)
        outs_n = dict(zip(kinds, r4))
        for kind in kinds:
            outs[kind][n] = outs_n[kind]
        if i + 1 < len(order):
            other = got[:DEPTH]
        if n == "ffn_down":
            gathered = got[DEPTH]

    tot = _sum_slots(gathered, name="sum_small_grads")
    red = dict(zip(small + ["loss"], _unpack(tot, [gfull[n].shape for n in small] + [(1,)])))
    for n in CHIP_SHARDED_SMALL:
        cb = given[n].shape[-1]
        red[n] = lax.dynamic_slice_in_dim(red[n], me * cb, cb, axis=red[n].ndim - 1)
    shapes = [given[n].shape for n in small]
    res, _ = _adamw(_pack([given[n] for n in small])[None], _pack([given["m_" + n] for n in small])[None],
                    _pack([given["v_" + n] for n in small])[None], [[_pack([red[n] for n in small])]], tr=PACK_ROWS, name="adamw_small")
    for kind, r in zip(kinds, res):
        outs[kind].update(zip(small, _unpack(r[0], shapes)))

    return (red["loss"][0], grad_x[None], *[outs[k][n] for k in ("grad", "delta", "new_m", "new_v") for n in WEIGHTS])
```
